```python
import jax
import jax.numpy as jnp
from jax import lax
import numpy as np

D_MODEL = 1024
BATCH = 8
SEQ = 4096
DEPTH = 2

HEAD_DIM = 64
ROPE_THETA = 10000.0
LN_EPS = 1e-5
DEEPNORM_ALPHA = (2 * DEPTH) ** 0.25
DEEPNORM_BETA = (8 * DEPTH) ** -0.25
N_ADA = 6
MAX_POS_OFFSET = 4096

A_HEADS = 4
MOBA_BLOCK = 256
MOBA_TOPK = 3
MOBA_Q_CHUNK = 64

B_HEADS = 6
B_KV_GROUPS = 2
B_REP = B_HEADS // B_KV_GROUPS
CMP_BLOCK = 32
CMP_STRIDE = 16
CMP_HIDDEN = 128
SLC_BLOCK = 64
SLC_TOPK = 16
NSA_WINDOW = 512
NSA_Q_CHUNK = 64
N_NSA_BRANCHES = 3
FORCE_SCORE = 1e6

DILATED_PAIRS = ((128, 1), (512, 4), (2048, 16))
C_HEADS_PER_GROUP = 2
C_HEADS = C_HEADS_PER_GROUP * len(DILATED_PAIRS)
BAND_Q = 128

N_BRANCHES = 3

N_EXPERTS = 32
TOP_K = 4
D_EXPERT = D_MODEL
SWIGLU_ALPHA = 1.702
SWIGLU_LIMIT = 7.0
EXPERT_ROW_BLOCK = 256

IN_LAYOUT = (
    ("a_q", A_HEADS * HEAD_DIM), ("a_k", A_HEADS * HEAD_DIM), ("a_v", A_HEADS * HEAD_DIM),
    ("b_q", B_HEADS * HEAD_DIM),
    ("b_kc", B_KV_GROUPS * HEAD_DIM), ("b_vc", B_KV_GROUPS * HEAD_DIM),
    ("b_ks", B_KV_GROUPS * HEAD_DIM), ("b_vs", B_KV_GROUPS * HEAD_DIM),
    ("b_kw", B_KV_GROUPS * HEAD_DIM), ("b_vw", B_KV_GROUPS * HEAD_DIM),
    ("b_gate", B_HEADS * N_NSA_BRANCHES),
    ("c_q", C_HEADS * HEAD_DIM), ("c_k", C_HEADS * HEAD_DIM), ("c_v", C_HEADS * HEAD_DIM),
    ("merge_gate", N_BRANCHES * D_MODEL),
)
IN_WIDTH = sum(w for _, w in IN_LAYOUT)
V_COLUMNS = ("a_v", "b_vc", "b_vs", "b_vw", "c_v")

kernel_name = "hybrid_moba_nsa_dilated_moe_deepnorm"


def _layer_norm(x):
    xf = x.astype(jnp.float32)
    mu = xf.mean(-1, keepdims=True)
    var = jnp.mean(jnp.square(xf - mu), -1, keepdims=True)
    return ((xf - mu) * lax.rsqrt(var + LN_EPS)).astype(x.dtype)


def _affine_ln(x, g, b):
    return _layer_norm(x) * g + b


def _rope_tables(positions):
    inv = ROPE_THETA ** (-jnp.arange(0, HEAD_DIM, 2, dtype=jnp.float32) / HEAD_DIM)
    ang = positions.astype(jnp.float32)[..., None] * inv
    return jnp.cos(ang)[:, :, None, :], jnp.sin(ang)[:, :, None, :]


def _rope(x, cos, sin):
    x1, x2 = jnp.split(x, 2, axis=-1)
    cos = cos.astype(x.dtype)
    sin = sin.astype(x.dtype)
    return jnp.concatenate([x1 * cos - x2 * sin, x2 * cos + x1 * sin], axis=-1)


def _split_in(proj):
    offs = [int(o) for o in np.cumsum([w for _, w in IN_LAYOUT])[:-1]]
    parts = jnp.split(proj, offs, axis=-1)
    return {name: p for (name, _), p in zip(IN_LAYOUT, parts)}


def _gather_blocks(blocks, idx):
    return jax.vmap(jax.vmap(lambda a, i: a[i]))(blocks, idx)


def _banded_attention(q, k, v, window):
    b, s, g, r, dh = q.shape
    span = window - 1
    pad = ((0, 0), (span, 0), (0, 0), (0, 0))
    kp = jnp.pad(k, pad)
    vp = jnp.pad(v, pad)
    nq = s // BAND_Q
    qb = q.reshape(b, nq, BAND_Q, g, r, dh).transpose(1, 0, 3, 4, 2, 5)
    rel = jnp.arange(BAND_Q)[:, None] + span - jnp.arange(BAND_Q + span)[None, :]
    band = (rel >= 0) & (rel < window)
    scale = dh ** -0.5

    def block(args):
        q_blk, ci = args
        start = ci * BAND_Q
        kw = lax.dynamic_slice_in_dim(kp, start, BAND_Q + span, axis=1)
        vw = lax.dynamic_slice_in_dim(vp, start, BAND_Q + span, axis=1)
        key_pos = start - span + jnp.arange(BAND_Q + span)
        ok = band & (key_pos >= 0)[None, :]
        sc = jnp.einsum("bgrqd,bkgd->bgrqk", q_blk, kw).astype(jnp.float32) * scale
        sc = jnp.where(ok, sc, -jnp.inf)
        m = jnp.max(sc, -1, keepdims=True)
        e = jnp.exp(sc - m)
        den = e.sum(-1, keepdims=True)
        o = jnp.einsum("bgrqk,bkgd->bgrqd", (e / den).astype(v.dtype), vw)
        return o, (m + jnp.log(den))[..., 0]

    o, lse = lax.map(block, (qb, jnp.arange(nq)))
    o = o.transpose(1, 0, 4, 2, 3, 5).reshape(b, s, g, r, dh)
    lse = lse.transpose(1, 0, 4, 2, 3).reshape(b, s, g, r)
    return o, lse


def _moba_attention(q, k, v):
    b, s, h, dh = q.shape
    nb = -(-s // MOBA_BLOCK)
    pad = ((0, 0), (0, nb * MOBA_BLOCK - s), (0, 0), (0, 0))
    kb = jnp.pad(k, pad).reshape(b, nb, MOBA_BLOCK, h, dh).transpose(0, 3, 1, 2, 4)
    vb = jnp.pad(v, pad).reshape(b, nb, MOBA_BLOCK, h, dh).transpose(0, 3, 1, 2, 4)
    k_mean = kb.astype(jnp.float32).mean(axis=3)
    n_sel = min(MOBA_TOPK, nb - 1)
    n_chunks = s // MOBA_Q_CHUNK
    qc = q.reshape(b, n_chunks, MOBA_Q_CHUNK, h, dh).transpose(1, 0, 3, 2, 4)
    scale = dh ** -0.5
    blk_ids = jnp.arange(nb)
    key_off = jnp.arange(MOBA_BLOCK)

    def chunk(args):
        q_blk, ci = args
        t = ci * MOBA_Q_CHUNK + jnp.arange(MOBA_Q_CHUNK)
        own = (ci * MOBA_Q_CHUNK) // MOBA_BLOCK
        k_own = lax.dynamic_index_in_dim(kb, own, axis=2, keepdims=False)
        v_own = lax.dynamic_index_in_dim(vb, own, axis=2, keepdims=False)
        s_own = jnp.einsum("bhqd,bhld->bhql", q_blk, k_own).astype(jnp.float32) * scale
        s_own = jnp.where((own * MOBA_BLOCK + key_off)[None, :] <= t[:, None], s_own, -jnp.inf)
        if n_sel == 0:
            p = jax.nn.softmax(s_own, axis=-1).astype(v.dtype)
            return jnp.einsum("bhql,bhld->bhqd", p, v_own)
        gate = jnp.einsum("bhqd,bhnd->bhqn", q_blk.astype(jnp.float32), k_mean)
        gate = jnp.where(blk_ids < own, gate, -jnp.inf)
        _, sel = lax.top_k(gate, n_sel)
        valid = sel < own
        k_sel = _gather_blocks(kb, sel)
        v_sel = _gather_blocks(vb, sel)
        s_sel = jnp.einsum("bhqd,bhqnld->bhqnl", q_blk, k_sel).astype(jnp.float32) * scale
        s_sel = jnp.where(valid[..., None], s_sel, -jnp.inf)
        s_sel = s_sel.reshape(b, h, MOBA_Q_CHUNK, n_sel * MOBA_BLOCK)
        p = jax.nn.softmax(jnp.concatenate([s_sel, s_own], -1), axis=-1).astype(v.dtype)
        p_sel = p[..., : n_sel * MOBA_BLOCK].reshape(b, h, MOBA_Q_CHUNK, n_sel, MOBA_BLOCK)
        p_own = p[..., n_sel * MOBA_BLOCK:]
        return (jnp.einsum("bhqnl,bhqnld->bhqd", p_sel, v_sel)
                + jnp.einsum("bhql,bhld->bhqd", p_own, v_own))

    out = lax.map(chunk, (qc, jnp.arange(n_chunks)))
    return out.transpose(1, 0, 3, 2, 4).reshape(b, s, h, dh)


def _nsa_compress(x, w1, w2, pe):
    b, s, g, dh = x.shape
    nc = (s - CMP_BLOCK) // CMP_STRIDE + 1
    idx = jnp.arange(nc)[:, None] * CMP_STRIDE + jnp.arange(CMP_BLOCK)[None, :]
    blocks = x[:, idx] + pe[None, None, :, None, :]
    flat = blocks.transpose(0, 1, 3, 2, 4).reshape(b, nc, g, CMP_BLOCK * dh)
    return jax.nn.silu(flat @ w1) @ w2


def _nsa_attention(q, k_cmp, v_cmp, k_slc, v_slc, k_win, v_win, gate_logits):
    b, s, g, r, dh = q.shape
    nc = k_cmp.shape[1]
    ns = s // SLC_BLOCK
    n_sel = min(SLC_TOPK, ns)
    scale = dh ** -0.5
    cmp_start = jnp.arange(nc) * CMP_STRIDE
    cmp_end = cmp_start + CMP_BLOCK - 1
    slc_start = jnp.arange(ns) * SLC_BLOCK
    overlap = ((cmp_start[:, None] < slc_start[None, :] + SLC_BLOCK)
               & (cmp_start[:, None] + CMP_BLOCK > slc_start[None, :])).astype(jnp.float32)
    ksb = k_slc.reshape(b, ns, SLC_BLOCK, g, dh).transpose(0, 3, 1, 2, 4)
    vsb = v_slc.reshape(b, ns, SLC_BLOCK, g, dh).transpose(0, 3, 1, 2, 4)
    nq = s // NSA_Q_CHUNK
    qc = q.reshape(b, nq, NSA_Q_CHUNK, g, r, dh).transpose(1, 0, 3, 4, 2, 5)
    blk_ids = jnp.arange(ns)
    key_off = jnp.arange(SLC_BLOCK)

    def chunk(args):
        q_blk, ci = args
        t = ci * NSA_Q_CHUNK + jnp.arange(NSA_Q_CHUNK)
        sc = jnp.einsum("bgrqd,bngd->bgrqn", q_blk, k_cmp).astype(jnp.float32) * scale
        vis = cmp_end[None, :] <= t[:, None]
        sc = jnp.where(vis, sc, -jnp.inf)
        m = jnp.max(sc, -1, keepdims=True)
        m = jnp.where(jnp.isfinite(m), m, 0.0)
        e = jnp.where(vis, jnp.exp(sc - m), 0.0)
        p_cmp = e / jnp.maximum(e.sum(-1, keepdims=True), 1e-30)
        o_cmp = jnp.einsum("bgrqn,bngd->bgrqd", p_cmp.astype(v_cmp.dtype), v_cmp)
        imp = jnp.einsum("bgrqn,nm->bgqm", p_cmp, overlap)
        q_blk_id = (t // SLC_BLOCK)[:, None]
        forced = (blk_ids[None, :] == 0) | (blk_ids[None, :] == q_blk_id) | (blk_ids[None, :] == q_blk_id - 1)
        imp = jnp.where(forced, FORCE_SCORE, imp)
        imp = jnp.where(blk_ids[None, :] <= q_blk_id, imp, -jnp.inf)
        _, sel = lax.top_k(imp, n_sel)
        k_sel = _gather_blocks(ksb, sel)
        v_sel = _gather_blocks(vsb, sel)
        pos = sel[..., None] * SLC_BLOCK + key_off
        ok = pos <= t[None, None, :, None, None]
        ss = jnp.einsum("bgrqd,bgqnld->bgrqnl", q_blk, k_sel).astype(jnp.float32) * scale
        ss = jnp.where(ok[:, :, None], ss, -jnp.inf).reshape(b, g, r, NSA_Q_CHUNK, n_sel * SLC_BLOCK)
        p = jax.nn.softmax(ss, axis=-1).reshape(b, g, r, NSA_Q_CHUNK, n_sel, SLC_BLOCK)
        o_slc = jnp.einsum("bgrqnl,bgqnld->bgrqd", p.astype(v_slc.dtype), v_sel)
        return o_cmp, o_slc

    o_cmp, o_slc = lax.map(chunk, (qc, jnp.arange(nq)))
    o_cmp = o_cmp.transpose(1, 0, 4, 2, 3, 5).reshape(b, s, g, r, dh)
    o_slc = o_slc.transpose(1, 0, 4, 2, 3, 5).reshape(b, s, g, r, dh)
    o_win, _ = _banded_attention(q, k_win, v_win, NSA_WINDOW)
    gw = jax.nn.sigmoid(gate_logits)
    return gw[..., 0:1] * o_cmp + gw[..., 1:2] * o_slc + gw[..., 2:3] * o_win


def _dilated_attention(q, k, v):
    b, s, _, dh = q.shape
    outs, lses = [], []
    for gi, (window, dil) in enumerate(DILATED_PAIRS):
        hs = slice(gi * C_HEADS_PER_GROUP, (gi + 1) * C_HEADS_PER_GROUP)
        n_res = -(-s // dil)
        n_res = -(-n_res // BAND_Q) * BAND_Q
        pad = ((0, 0), (0, n_res * dil - s), (0, 0), (0, 0))
        qr = jnp.pad(q[:, :, hs], pad).reshape(b, n_res, dil * C_HEADS_PER_GROUP, dh)
        kr = jnp.pad(k[:, :, hs], pad).reshape(b, n_res, dil * C_HEADS_PER_GROUP, dh)
        vr = jnp.pad(v[:, :, hs], pad).reshape(b, n_res, dil * C_HEADS_PER_GROUP, dh)
        o, lse = _banded_attention(qr[:, :, :, None], kr, vr, window // dil + 1)
        outs.append(o[:, :, :, 0].reshape(b, n_res * dil, C_HEADS_PER_GROUP, dh)[:, :s])
        lses.append(lse[..., 0].reshape(b, n_res * dil, C_HEADS_PER_GROUP)[:, :s])
    wts = jax.nn.softmax(jnp.stack(lses, 0), axis=0)
    return jnp.sum(wts[..., None].astype(q.dtype) * jnp.stack(outs, 0), axis=0)


def _clamped_swiglu(gu):
    x_glu = jnp.minimum(gu[..., ::2], SWIGLU_LIMIT)
    x_lin = jnp.clip(gu[..., 1::2], -SWIGLU_LIMIT, SWIGLU_LIMIT)
    return x_glu * jax.nn.sigmoid(SWIGLU_ALPHA * x_glu) * (x_lin + 1.0)


def _moe_ffn(h, router_w, router_b, w_gu, b_gu, w_dn, b_dn):
    n, d = h.shape
    logits = (h @ router_w + router_b).astype(jnp.float32)
    top_val, top_idx = lax.top_k(logits, TOP_K)
    gate = jax.nn.softmax(top_val, axis=-1)
    nk = n * TOP_K
    flat_e = top_idx.reshape(nk)
    flat_tok = jnp.repeat(jnp.arange(n, dtype=jnp.int32), TOP_K)
    flat_gate = gate.reshape(nk)
    order = jnp.argsort(flat_e, stable=True)
    e_sorted = flat_e[order]
    counts = jnp.bincount(flat_e, length=N_EXPERTS)
    padded = -(-counts // EXPERT_ROW_BLOCK) * EXPERT_ROW_BLOCK
    start = jnp.cumsum(counts) - counts
    pend = jnp.cumsum(padded)
    pstart = pend - padded
    dest = pstart[e_sorted] + jnp.arange(nk) - start[e_sorted]
    n_blocks = -(-(nk + N_EXPERTS * (EXPERT_ROW_BLOCK - 1)) // EXPERT_ROW_BLOCK)
    rows = n_blocks * EXPERT_ROW_BLOCK
    row_tok = jnp.zeros((rows,), jnp.int32).at[dest].set(flat_tok[order])
    row_gate = jnp.zeros((rows,), jnp.float32).at[dest].set(flat_gate[order])
    block_exp = jnp.minimum(
        jnp.searchsorted(pend, jnp.arange(n_blocks) * EXPERT_ROW_BLOCK, side="right"), N_EXPERTS - 1)

    def block(args):
        tok, g, e = args
        xb = h[tok]
        y = _clamped_swiglu(xb @ w_gu[e] + b_gu[e]) @ w_dn[e] + b_dn[e]
        return y * g[:, None].astype(y.dtype)

    y_rows = lax.map(block, (row_tok.reshape(n_blocks, EXPERT_ROW_BLOCK),
                             row_gate.reshape(n_blocks, EXPERT_ROW_BLOCK), block_exp))
    return jax.ops.segment_sum(y_rows.reshape(rows, d), row_tok, num_segments=n)


def _mixer(h, cos, sin, w_in, cmp_w1_k, cmp_w2_k, cmp_pe_k, cmp_w1_v, cmp_w2_v, cmp_pe_v,
           w_branch_a, w_branch_b, w_branch_c, w_out):
    b, s, _ = h.shape
    p = _split_in(h @ w_in)

    def heads(t, nh):
        return t.reshape(b, s, nh, HEAD_DIM)

    o_a = _moba_attention(_rope(heads(p["a_q"], A_HEADS), cos, sin),
                          _rope(heads(p["a_k"], A_HEADS), cos, sin),
                          heads(p["a_v"], A_HEADS))
    q_b = _rope(heads(p["b_q"], B_HEADS), cos, sin).reshape(b, s, B_KV_GROUPS, B_REP, HEAD_DIM)
    k_c = _nsa_compress(_rope(heads(p["b_kc"], B_KV_GROUPS), cos, sin), cmp_w1_k, cmp_w2_k, cmp_pe_k)
    v_c = _nsa_compress(heads(p["b_vc"], B_KV_GROUPS), cmp_w1_v, cmp_w2_v, cmp_pe_v)
    o_b = _nsa_attention(q_b, k_c, v_c,
                         _rope(heads(p["b_ks"], B_KV_GROUPS), cos, sin), heads(p["b_vs"], B_KV_GROUPS),
                         _rope(heads(p["b_kw"], B_KV_GROUPS), cos, sin), heads(p["b_vw"], B_KV_GROUPS),
                         p["b_gate"].reshape(b, s, B_KV_GROUPS, B_REP, N_NSA_BRANCHES))
    o_c = _dilated_attention(_rope(heads(p["c_q"], C_HEADS), cos, sin),
                             _rope(heads(p["c_k"], C_HEADS), cos, sin),
                             heads(p["c_v"], C_HEADS))
    g_a, g_b, g_c = jnp.split(jax.nn.sigmoid(p["merge_gate"]), N_BRANCHES, axis=-1)
    merged = (g_a * (o_a.reshape(b, s, -1) @ w_branch_a)
              + g_b * (o_b.reshape(b, s, -1) @ w_branch_b)
              + g_c * (o_c.reshape(b, s, -1) @ w_branch_c))
    return merged @ w_out


def setup_inputs(seed: int = 0) -> dict:
    key = jax.random.key(seed)
    ks = jax.random.split(key, 26)
    f32 = jnp.float32

    def nrm(k, shape, std):
        return jax.random.normal(k, shape, f32) * std

    col_scale = jnp.asarray(np.concatenate(
        [np.full((w,), DEEPNORM_BETA if name in V_COLUMNS else 1.0, np.float32) for name, w in IN_LAYOUT]))
    cmp_in = CMP_BLOCK * HEAD_DIM
    wa = A_HEADS * HEAD_DIM
    wb = B_HEADS * HEAD_DIM
    wc = C_HEADS_PER_GROUP * HEAD_DIM
    positions = (jax.random.randint(ks[2], (BATCH, 1), 0, MAX_POS_OFFSET)
                 + jnp.arange(SEQ)[None, :]).astype(jnp.int32)
    return {
        "x": nrm(ks[0], (BATCH, SEQ, D_MODEL), 1.0),
        "c": nrm(ks[1], (BATCH, D_MODEL), 1.0),
        "positions": positions,
        "w_ada": nrm(ks[3], (DEPTH, D_MODEL, N_ADA * D_MODEL), 0.5 * D_MODEL ** -0.5),
        "b_ada": nrm(ks[4], (DEPTH, N_ADA * D_MODEL), 0.01),
        "w_in": nrm(ks[5], (DEPTH, D_MODEL, IN_WIDTH), D_MODEL ** -0.5) * col_scale,
        "cmp_w1_k": nrm(ks[6], (DEPTH, cmp_in, CMP_HIDDEN), cmp_in ** -0.5),
        "cmp_w2_k": nrm(ks[7], (DEPTH, CMP_HIDDEN, HEAD_DIM), CMP_HIDDEN ** -0.5),
        "cmp_pe_k": nrm(ks[8], (DEPTH, CMP_BLOCK, HEAD_DIM), 0.1),
        "cmp_w1_v": nrm(ks[9], (DEPTH, cmp_in, CMP_HIDDEN), cmp_in ** -0.5),
        "cmp_w2_v": nrm(ks[10], (DEPTH, CMP_HIDDEN, HEAD_DIM), CMP_HIDDEN ** -0.5),
        "cmp_pe_v": nrm(ks[11], (DEPTH, CMP_BLOCK, HEAD_DIM), 0.1),
        "w_branch_a": nrm(ks[12], (DEPTH, wa, D_MODEL), wa ** -0.5),
        "w_branch_b": nrm(ks[13], (DEPTH, wb, D_MODEL), wb ** -0.5),
        "w_branch_c": nrm(ks[14], (DEPTH, wc, D_MODEL), wc ** -0.5),
        "w_out": nrm(ks[15], (DEPTH, D_MODEL, D_MODEL), DEEPNORM_BETA * D_MODEL ** -0.5),
        "ln1_g": 1.0 + nrm(ks[16], (DEPTH, D_MODEL), 0.02),
        "ln1_b": nrm(ks[17], (DEPTH, D_MODEL), 0.02),
        "router_w": nrm(ks[18], (DEPTH, D_MODEL, N_EXPERTS), D_MODEL ** -0.5),
        "router_b": nrm(ks[19], (DEPTH, N_EXPERTS), 0.01),
        "w_gate_up": nrm(ks[20], (DEPTH, N_EXPERTS, D_MODEL, 2 * D_EXPERT), D_MODEL ** -0.5),
        "b_gate_up": nrm(ks[21], (DEPTH, N_EXPERTS, 2 * D_EXPERT), 0.01),
        "w_down": nrm(ks[22], (DEPTH, N_EXPERTS, D_EXPERT, D_MODEL), DEEPNORM_BETA * D_EXPERT ** -0.5),
        "b_down": nrm(ks[23], (DEPTH, N_EXPERTS, D_MODEL), 0.01),
        "ln2_g": 1.0 + nrm(ks[24], (DEPTH, D_MODEL), 0.02),
        "ln2_b": nrm(ks[25], (DEPTH, D_MODEL), 0.02),
    }


def reference(x, c, positions, w_ada, b_ada, w_in, cmp_w1_k, cmp_w2_k, cmp_pe_k,
              cmp_w1_v, cmp_w2_v, cmp_pe_v, w_branch_a, w_branch_b, w_branch_c, w_out,
              ln1_g, ln1_b, router_w, router_b, w_gate_up, b_gate_up, w_down, b_down,
              ln2_g, ln2_b):
    b, s, d = x.shape
    cos, sin = _rope_tables(positions)
    cond = jax.nn.silu(c)
    for l in range(DEPTH):
        mod = (cond @ w_ada[l] + b_ada[l])[:, None, :]
        sh1, sc1, g1, sh2, sc2, g2 = jnp.split(mod, N_ADA, axis=-1)
        h = _layer_norm(x) * (1.0 + sc1) + sh1
        y = _mixer(h, cos, sin, w_in[l], cmp_w1_k[l], cmp_w2_k[l], cmp_pe_k[l],
                   cmp_w1_v[l], cmp_w2_v[l], cmp_pe_v[l],
                   w_branch_a[l], w_branch_b[l], w_branch_c[l], w_out[l])
        x = _affine_ln(DEEPNORM_ALPHA * x + g1 * y, ln1_g[l], ln1_b[l])
        h = _layer_norm(x) * (1.0 + sc2) + sh2
        y = _moe_ffn(h.reshape(b * s, d), router_w[l], router_b[l], w_gate_up[l],
                     b_gate_up[l], w_down[l], b_down[l]).reshape(b, s, d)
        x = _affine_ln(DEEPNORM_ALPHA * x + g2 * y, ln2_g[l], ln2_b[l])
    return x
```

```python
import functools

import numpy as np
import jax
import jax.numpy as jnp
from jax import lax
from jax.experimental import pallas as pl
from jax.experimental.pallas import tpu as pltpu

F32 = jnp.float32
BF16 = jnp.bfloat16
I32 = jnp.int32
HIGHEST = lax.Precision.HIGHEST

D_MODEL = 1024
DEPTH = 2
HEAD_DIM = 64
ROPE_THETA = 10000.0
LN_EPS = 1e-5
DEEPNORM_ALPHA = (2 * DEPTH) ** 0.25
N_ADA = 6
A_HEADS = 4
MOBA_BLOCK = 256
MOBA_TOPK = 3
B_HEADS = 6
B_KV_GROUPS = 2
B_REP = 3
CMP_BLOCK = 32
CMP_STRIDE = 16
CMP_HIDDEN = 128
SLC_BLOCK = 64
SLC_TOPK = 16
NSA_WINDOW = 512
FORCE_SCORE = 1e6
DILATED_PAIRS = ((128, 1), (512, 4), (2048, 16))
N_EXPERTS = 32
TOP_K = 4
SWIGLU_ALPHA = 1.702
SWIGLU_LIMIT = 7.0

LANES = 128
SUBLANES = 8
HALF = LANES // 2
NEG = -1e30
ROW_CHUNKS = D_MODEL // LANES
VMEM_LIMIT = 56 * 1024 * 1024

T_MG = 0
T_ROPE0, T_ROPE1 = 24, 40
T_AQ, T_AK, T_BQ, T_BKC, T_BKS, T_BKW, T_CQ, T_CK = 24, 26, 28, 31, 32, 33, 34, 37
T_AV, T_BVC, T_BVS, T_BVW, T_CV, T_BG = 40, 42, 43, 44, 45, 48
N_PROJ_TILES = 49
PROJ_W = N_PROJ_TILES * LANES

EXPERT_ROWS = 512
TOK_TILE = 256


def _cparams(sem, **kw):
    return pltpu.CompilerParams(dimension_semantics=sem, vmem_limit_bytes=VMEM_LIMIT, **kw)


def _lane_iota(shape=(1, LANES)):
    return lax.broadcasted_iota(I32, shape, len(shape) - 1)


def _sigmoid(x):
    return 1.0 / (1.0 + jnp.exp(-x))


def _ln(x):
    mu = jnp.mean(x, axis=-1, keepdims=True)
    xc = x - mu
    var = jnp.mean(xc * xc, axis=-1, keepdims=True)
    return xc * lax.rsqrt(var + LN_EPS)


def _dot_nt(a, b, precision=None):
    return lax.dot_general(a, b, (((1,), (1,)), ((), ())), precision=precision,
                           preferred_element_type=F32)


def _ada_kernel(c_ref, w_ref, b_ref, o_ref):
    c = c_ref[...]
    cond = c * _sigmoid(c)
    o_ref[0] = jnp.dot(cond, w_ref[0], precision=HIGHEST, preferred_element_type=F32) + b_ref[0]


def _ada(c, w_ada, b_ada):
    depth, d, n = w_ada.shape
    b = c.shape[0]
    tn = 1536
    return pl.pallas_call(
        _ada_kernel,
        grid=(depth, n // tn),
        in_specs=[pl.BlockSpec((b, d), lambda l, j: (0, 0)),
                  pl.BlockSpec((1, d, tn), lambda l, j: (l, 0, j)),
                  pl.BlockSpec((1, 1, tn), lambda l, j: (l, 0, j))],
        out_specs=pl.BlockSpec((1, b, tn), lambda l, j: (l, 0, j)),
        out_shape=jax.ShapeDtypeStruct((depth, b, n), F32),
        compiler_params=_cparams(("parallel", "parallel")),
        name="ada",
    )(c, w_ada, b_ada.reshape(depth, 1, n))


def _inproj_kernel(x_ref, sc_ref, sh_ref, w_ref, cos_ref, sin_ref, o_ref, *, chunks):
    h = _ln(x_ref[...]) * (1.0 + sc_ref[0, 0]) + sh_ref[0, 0]
    hb = h.astype(BF16)
    first_half = (_lane_iota() & (HEAD_DIM - 1)) < (HEAD_DIM // 2)
    for c0, cw, rope in chunks:
        acc = jnp.dot(hb, w_ref[:, c0:c0 + cw], preferred_element_type=F32)
        if rope:
            cos = cos_ref[...]
            sin = sin_ref[...]
            for t in range(cw // LANES):
                a = acc[:, t * LANES:(t + 1) * LANES]
                rot = jnp.where(first_half, pltpu.roll(a, LANES - HEAD_DIM // 2, 1),
                                pltpu.roll(a, HEAD_DIM // 2, 1))
                o_ref[:, c0 + t * LANES:c0 + (t + 1) * LANES] = (a * cos + rot * sin).astype(o_ref.dtype)
        else:
            o_ref[:, c0:c0 + cw] = acc.astype(o_ref.dtype)


def _inproj_chunks():
    chunks = []
    for lo, hi, rope in ((0, T_ROPE0, False), (T_ROPE0, T_ROPE1, True), (T_ROPE1, N_PROJ_TILES, False)):
        c = lo * LANES
        while c < hi * LANES:
            cw = min(512, hi * LANES - c)
            chunks.append((c, cw, rope))
            c += cw
    return tuple(chunks)


def _inproj(x2, mod4, w_perm, cos_t, sin_t, seq):
    n, d = x2.shape
    tm = 256
    per_b = seq // tm
    return pl.pallas_call(
        functools.partial(_inproj_kernel, chunks=_inproj_chunks()),
        grid=(n // tm,),
        in_specs=[pl.BlockSpec((tm, d), lambda i: (i, 0)),
                  pl.BlockSpec((1, 1, 1, d), lambda i: (i // per_b, 1, 0, 0)),
                  pl.BlockSpec((1, 1, 1, d), lambda i: (i // per_b, 0, 0, 0)),
                  pl.BlockSpec((d, PROJ_W), lambda i: (0, 0)),
                  pl.BlockSpec((tm, LANES), lambda i: (i, 0)),
                  pl.BlockSpec((tm, LANES), lambda i: (i, 0))],
        out_specs=pl.BlockSpec((tm, PROJ_W), lambda i: (i, 0)),
        out_shape=jax.ShapeDtypeStruct((n, PROJ_W), BF16),
        compiler_params=_cparams(("parallel",)),
        name="inproj",
    )(x2, mod4, mod4, w_perm, cos_t, sin_t)


def _bb_attn_kernel(q_ref, k_ref, v_ref, b_ref, o_ref, qa_sc, m_sc, l_sc, acc_sc, *, blk_shift, t, nk):
    i = pl.program_id(2)
    j = pl.program_id(3)
    lane = _lane_iota()
    lo = lane < HALF

    @pl.when(j == 0)
    def _():
        q = q_ref[...]
        b = b_ref[...]
        qa_sc[0] = jnp.where(lo, q, b)
        qa_sc[1] = jnp.where(lo, b, q)
        m_sc[...] = jnp.full(m_sc.shape, -jnp.inf, F32)
        l_sc[...] = jnp.zeros(l_sc.shape, F32)
        acc_sc[...] = jnp.zeros(acc_sc.shape, F32)

    @pl.when(j <= i)
    def _():
        k = k_ref[...]
        v = v_ref[...]
        kblk = (j * t + lax.broadcasted_iota(I32, (t, 1), 0)) >> blk_shift
        onehot = jnp.where(kblk == (lane & (HALF - 1)), 1.0, 0.0).astype(BF16)
        ka = (jnp.where(lo, k, onehot), jnp.where(lo, onehot, k))
        row = i * t + lax.broadcasted_iota(I32, (t, 1), 0)
        col = j * t + lax.broadcasted_iota(I32, (1, t), 1)
        causal = col <= row
        for h in range(2):
            s = _dot_nt(qa_sc[h], ka[h])
            s = jnp.where(causal, s, NEG)
            m_prev = m_sc[h]
            m_new = jnp.maximum(m_prev, jnp.max(s, axis=1, keepdims=True))
            alpha = jnp.exp(m_prev - m_new)
            p = jnp.exp(s - m_new[:, :1])
            l_sc[h] = alpha * l_sc[h] + jnp.sum(p, axis=1, keepdims=True)
            acc_sc[h] = alpha * acc_sc[h] + jnp.dot(p.astype(BF16), v, preferred_element_type=F32)
            m_sc[h] = m_new

    @pl.when(j == nk - 1)
    def _():
        o = jnp.where(lo, acc_sc[0] / l_sc[0], acc_sc[1] / l_sc[1])
        o_ref[...] = o.astype(o_ref.dtype)


def _bb_attn(proj, bias, *, batch, seq, n_pairs, q_tile, k_tile, v_tile, bias_tile, blk):
    t = 256
    nq = seq // t
    n = batch * seq
    kern = functools.partial(_bb_attn_kernel, blk_shift=int(np.log2(blk)), t=t, nk=nq)
    return pl.pallas_call(
        kern,
        grid=(batch, n_pairs, nq, nq),
        in_specs=[pl.BlockSpec((t, LANES), lambda b, p, i, j: (b * nq + i, q_tile(p))),
                  pl.BlockSpec((t, LANES), lambda b, p, i, j: (b * nq + jnp.minimum(j, i), k_tile(p))),
                  pl.BlockSpec((t, LANES), lambda b, p, i, j: (b * nq + jnp.minimum(j, i), v_tile(p))),
                  pl.BlockSpec((t, LANES), lambda b, p, i, j: (b * nq + i, bias_tile(p)))],
        out_specs=pl.BlockSpec((t, LANES), lambda b, p, i, j: (b * nq + i, p)),
        out_shape=jax.ShapeDtypeStruct((n, n_pairs * LANES), BF16),
        scratch_shapes=[pltpu.VMEM((2, t, LANES), BF16),
                        pltpu.VMEM((2, t, LANES), F32),
                        pltpu.VMEM((2, t, LANES), F32),
                        pltpu.VMEM((2, t, LANES), F32)],
        compiler_params=_cparams(("parallel", "parallel", "parallel", "arbitrary")),
        name="bb_attn_%d" % blk,
    )(proj, proj, proj, bias)


def _band_kernel(q_ref, k_ref, v_ref, o_ref, lse_ref, m_sc, l_sc, acc_sc, *, t, window, nkv):
    i = pl.program_id(2)
    jj = pl.program_id(3)
    kt = i - (nkv - 1) + jj
    lane = _lane_iota()
    lo = lane < HALF

    @pl.when(jj == 0)
    def _():
        m_sc[...] = jnp.full(m_sc.shape, -jnp.inf, F32)
        l_sc[...] = jnp.zeros(l_sc.shape, F32)
        acc_sc[...] = jnp.zeros(acc_sc.shape, F32)

    @pl.when(kt >= 0)
    def _():
        q = q_ref[...]
        k = k_ref[...]
        v = v_ref[...]
        zero = jnp.zeros_like(q)
        qs = (jnp.where(lo, q, zero), jnp.where(lo, zero, q))
        rel = (i * t + lax.broadcasted_iota(I32, (t, 1), 0)) - (kt * t + lax.broadcasted_iota(I32, (1, t), 1))
        ok = (rel >= 0) & (rel < window)
        for h in range(2):
            s = jnp.where(ok, _dot_nt(qs[h], k), NEG)
            m_prev = m_sc[h]
            m_new = jnp.maximum(m_prev, jnp.max(s, axis=1, keepdims=True))
            alpha = jnp.exp(m_prev - m_new)
            p = jnp.exp(s - m_new[:, :1])
            l_sc[h] = alpha * l_sc[h] + jnp.sum(p, axis=1, keepdims=True)
            acc_sc[h] = alpha * acc_sc[h] + jnp.dot(p.astype(BF16), v, preferred_element_type=F32)
            m_sc[h] = m_new

    @pl.when(jj == nkv - 1)
    def _():
        o = jnp.where(lo, acc_sc[0] / l_sc[0], acc_sc[1] / l_sc[1])
        o_ref[...] = o.astype(o_ref.dtype)
        lse_ref[...] = jnp.where(lo, m_sc[0] + jnp.log(l_sc[0]), m_sc[1] + jnp.log(l_sc[1]))


def _band_attn(q_arr, k_arr, v_arr, *, batch, seq, n_pairs, q_tile, k_tile, v_tile, window, t):
    nq = seq // t
    nkv = -(-(window - 1) // t) + 1
    n = batch * seq

    def kv_row(b, i, jj):
        return b * nq + jnp.maximum(i - (nkv - 1) + jj, 0)

    kern = functools.partial(_band_kernel, t=t, window=window, nkv=nkv)
    return pl.pallas_call(
        kern,
        grid=(batch, n_pairs, nq, nkv),
        in_specs=[pl.BlockSpec((t, LANES), lambda b, p, i, jj: (b * nq + i, q_tile(p))),
                  pl.BlockSpec((t, LANES), lambda b, p, i, jj: (kv_row(b, i, jj), k_tile(p))),
                  pl.BlockSpec((t, LANES), lambda b, p, i, jj: (kv_row(b, i, jj), v_tile(p)))],
        out_specs=[pl.BlockSpec((t, LANES), lambda b, p, i, jj: (b * nq + i, p)),
                   pl.BlockSpec((t, LANES), lambda b, p, i, jj: (b * nq + i, p))],
        out_shape=[jax.ShapeDtypeStruct((n, n_pairs * LANES), BF16),
                   jax.ShapeDtypeStruct((n, n_pairs * LANES), F32)],
        scratch_shapes=[pltpu.VMEM((2, t, LANES), F32),
                        pltpu.VMEM((2, t, LANES), F32),
                        pltpu.VMEM((2, t, LANES), F32)],
        compiler_params=_cparams(("parallel", "parallel", "parallel", "arbitrary")),
        name="band_attn_%d" % window,
    )(q_arr, k_arr, v_arr)


def _kmean_kernel(k_ref, o_ref, *, nb):
    s = k_ref.shape[0]
    blk = lax.broadcasted_iota(I32, (nb, s), 1) >> int(np.log2(MOBA_BLOCK))
    avg = jnp.where(blk == lax.broadcasted_iota(I32, (nb, s), 0), 1.0 / MOBA_BLOCK, 0.0).astype(BF16)
    o_ref[0] = jnp.dot(avg, k_ref[...], preferred_element_type=F32)


def _kmean(proj, batch, seq):
    nb = seq // MOBA_BLOCK
    w = A_HEADS * HEAD_DIM
    return pl.pallas_call(
        functools.partial(_kmean_kernel, nb=nb),
        grid=(batch,),
        in_specs=[pl.BlockSpec((seq, w), lambda b: (b, T_AK * LANES // w))],
        out_specs=pl.BlockSpec((1, nb, w), lambda b: (b, 0, 0)),
        out_shape=jax.ShapeDtypeStruct((batch, nb, w), F32),
        compiler_params=_cparams(("parallel",)),
        name="moba_kmean",
    )(proj)


def _rank_desc(g, n_idx, n):
    rank = jnp.zeros(g.shape, I32)
    for m in range(n):
        c = g[m:m + 1, :]
        beats = (c > g) | ((c == g) & (n_idx > m))
        rank = rank + jnp.where(beats, 1, 0)
    return rank


def _moba_sel_kernel(q_ref, km_ref, b_ref, *, nb, n_sel):
    i = pl.program_id(2)
    t = q_ref.shape[0]
    q = q_ref[...].astype(F32)
    km = km_ref[0]
    lo = _lane_iota() < HALF
    zero = jnp.zeros_like(km)
    pad = jnp.zeros((HALF - nb, LANES), F32)
    kmt = jnp.concatenate([jnp.where(lo, zero, km), pad, jnp.where(lo, km, zero), pad], axis=0)
    gt = _dot_nt(kmt, q, precision=HIGHEST)
    n_idx = lax.broadcasted_iota(I32, (nb, 1), 0)
    valid = n_idx < i
    rows = []
    for r0 in (0, HALF):
        g = jnp.where(valid, gt[r0:r0 + nb, :], -jnp.inf)
        rank = _rank_desc(g, n_idx, nb)
        allowed = (valid & (rank < n_sel)) | (n_idx == i)
        rows.append(jnp.where(allowed, 0.0, NEG))
        rows.append(jnp.zeros((HALF - nb, t), F32))
    b_ref[...] = jnp.concatenate(rows, axis=0).T.astype(b_ref.dtype)


def _moba_select(proj, kmean, batch, seq):
    t = MOBA_BLOCK
    nq = seq // t
    nb = seq // MOBA_BLOCK
    n_sel = min(MOBA_TOPK, nb - 1)
    n_pairs = A_HEADS // 2
    return pl.pallas_call(
        functools.partial(_moba_sel_kernel, nb=nb, n_sel=n_sel),
        grid=(batch, n_pairs, nq),
        in_specs=[pl.BlockSpec((t, LANES), lambda b, p, i: (b * nq + i, T_AQ + p)),
                  pl.BlockSpec((1, nb, LANES), lambda b, p, i: (b, 0, p))],
        out_specs=pl.BlockSpec((t, LANES), lambda b, p, i: (b * nq + i, p)),
        out_shape=jax.ShapeDtypeStruct((batch * seq, n_pairs * LANES), BF16),
        compiler_params=_cparams(("parallel", "parallel", "parallel")),
        name="moba_select",
    )(proj, kmean)


def _compress_kernel(x_ref, w1_ref, pe_ref, w1f_ref, w2_ref, o_ref):
    x = x_ref[0]
    nblk = x.shape[0]
    outs = []
    for g in range(B_KV_GROUPS):
        u = jnp.dot(x, w1_ref[g, 0], preferred_element_type=F32)
        v = jnp.dot(x, w1_ref[g, 1], preferred_element_type=F32)
        pe_h = jnp.dot(pe_ref[...], w1f_ref[...], precision=HIGHEST, preferred_element_type=F32)
        hid = u + pltpu.roll(v, nblk - 1, 0) + pe_h[:1]
        hid = hid * _sigmoid(hid)
        outs.append(jnp.dot(hid, w2_ref[g], precision=HIGHEST, preferred_element_type=F32))
    o = outs[0] + outs[1]
    rows = lax.broadcasted_iota(I32, (nblk, 1), 0)
    o_ref[0] = jnp.where(rows < nblk - 1, o, 0.0).astype(o_ref.dtype)


def _compress(xt, w1, w2, pe, batch, seq):
    nblk = seq // CMP_STRIDE
    xg = xt.reshape(batch, nblk, CMP_STRIDE * LANES)
    w1r = w1.reshape(2, CMP_STRIDE, HEAD_DIM, CMP_HIDDEN)
    w1e = jnp.zeros((B_KV_GROUPS, 2, CMP_STRIDE, B_KV_GROUPS, HEAD_DIM, CMP_HIDDEN), F32)
    for g in range(B_KV_GROUPS):
        w1e = w1e.at[g, :, :, g].set(w1r)
    w1e = w1e.reshape(B_KV_GROUPS, 2, CMP_STRIDE * LANES, CMP_HIDDEN).astype(BF16)
    pe_flat = jnp.broadcast_to(pe.reshape(1, CMP_BLOCK * HEAD_DIM), (SUBLANES, CMP_BLOCK * HEAD_DIM))
    w2e = jnp.zeros((B_KV_GROUPS, CMP_HIDDEN, LANES), F32)
    for g in range(B_KV_GROUPS):
        w2e = w2e.at[g, :, g * HEAD_DIM:(g + 1) * HEAD_DIM].set(w2)
    return pl.pallas_call(
        _compress_kernel,
        grid=(batch,),
        in_specs=[pl.BlockSpec((1, nblk, CMP_STRIDE * LANES), lambda b: (b, 0, 0)),
                  pl.BlockSpec(w1e.shape, lambda b: (0, 0, 0, 0)),
                  pl.BlockSpec(pe_flat.shape, lambda b: (0, 0)),
                  pl.BlockSpec(w1.shape, lambda b: (0, 0)),
                  pl.BlockSpec(w2e.shape, lambda b: (0, 0, 0))],
        out_specs=pl.BlockSpec((1, nblk, LANES), lambda b: (b, 0, 0)),
        out_shape=jax.ShapeDtypeStruct((batch, nblk, LANES), BF16),
        compiler_params=_cparams(("parallel",)),
        name="nsa_compress",
    )(xg, w1e, pe_flat, w1, w2e)


def _nsa_cmp_kernel(q0_ref, q1_ref, q2_ref, kc_ref, vc_ref, ov_ref, o_ref, b_ref, *, n_sel, ns):
    i = pl.program_id(1)
    t = q0_ref.shape[0]
    ncp = kc_ref.shape[1]
    lane = _lane_iota()
    lo = lane < HALF
    kc = kc_ref[0]
    vc = vc_ref[0]
    tq = i * t + lax.broadcasted_iota(I32, (t, 1), 0)
    cmp_end = lax.broadcasted_iota(I32, (1, ncp), 1) * CMP_STRIDE + (CMP_BLOCK - 1)
    vis = cmp_end <= tq
    psum = [jnp.zeros((t, ncp), F32) for _ in range(B_KV_GROUPS)]
    for r, q_ref in enumerate((q0_ref, q1_ref, q2_ref)):
        q = q_ref[...]
        zero = jnp.zeros_like(q)
        outs = []
        for g in range(B_KV_GROUPS):
            qg = jnp.where(lo, q, zero) if g == 0 else jnp.where(lo, zero, q)
            s = jnp.where(vis, _dot_nt(qg, kc), NEG)
            m = jnp.max(s, axis=1, keepdims=True)
            m = jnp.where(m > 0.5 * NEG, m, 0.0)
            e = jnp.where(vis, jnp.exp(s - m), 0.0)
            p = e / jnp.maximum(jnp.sum(e, axis=1, keepdims=True), 1e-30)
            psum[g] = psum[g] + p
            outs.append(jnp.dot(p.astype(BF16), vc, preferred_element_type=F32))
        o_ref[:, r * LANES:(r + 1) * LANES] = jnp.where(lo, outs[0], outs[1]).astype(o_ref.dtype)
    n_idx = lax.broadcasted_iota(I32, (ns, 1), 0)
    tcol = i * t + lax.broadcasted_iota(I32, (1, t), 1)
    qblk = tcol >> int(np.log2(SLC_BLOCK))
    forced = (n_idx == 0) | (n_idx == qblk) | (n_idx == qblk - 1)
    valid = n_idx <= qblk
    rows = []
    for g in (1, 0):
        imp = _dot_nt(ov_ref[...], psum[g], precision=HIGHEST)
        imp = jnp.where(forced, FORCE_SCORE, imp)
        imp = jnp.where(valid, imp, -jnp.inf)
        rank = _rank_desc(imp, n_idx, ns)
        allowed = valid & (rank < n_sel)
        rows.append(jnp.where(allowed, 0.0, NEG))
        if ns < HALF:
            rows.append(jnp.zeros((HALF - ns, t), F32))
    b_ref[...] = jnp.concatenate(rows, axis=0).T.astype(b_ref.dtype)


def _nsa_cmp(proj, kc, vc, batch, seq):
    t = 256
    nq = seq // t
    ns = seq // SLC_BLOCK
    n_sel = min(SLC_TOPK, ns)
    ncp = seq // CMP_STRIDE
    cs = np.arange(ncp)[None, :] * CMP_STRIDE
    ss = np.arange(ns)[:, None] * SLC_BLOCK
    ov = ((cs < ss + SLC_BLOCK) & (cs + CMP_BLOCK > ss)).astype(np.float32)
    ov[:, ncp - 1] = 0.0
    n = batch * seq
    qspec = [pl.BlockSpec((t, LANES), (lambda b, i, r=r: (b * nq + i, T_BQ + r))) for r in range(B_REP)]
    return pl.pallas_call(
        functools.partial(_nsa_cmp_kernel, n_sel=n_sel, ns=ns),
        grid=(batch, nq),
        in_specs=qspec + [pl.BlockSpec((1, ncp, LANES), lambda b, i: (b, 0, 0)),
                          pl.BlockSpec((1, ncp, LANES), lambda b, i: (b, 0, 0)),
                          pl.BlockSpec((ns, ncp), lambda b, i: (0, 0))],
        out_specs=[pl.BlockSpec((t, B_REP * LANES), lambda b, i: (b * nq + i, 0)),
                   pl.BlockSpec((t, LANES), lambda b, i: (b * nq + i, 0))],
        out_shape=[jax.ShapeDtypeStruct((n, B_REP * LANES), BF16),
                   jax.ShapeDtypeStruct((n, LANES), BF16)],
        compiler_params=_cparams(("parallel", "parallel")),
        name="nsa_cmp_select",
    )(proj, proj, proj, kc, vc, jnp.asarray(ov))


def _merge_kernel(oa_ref, ocmp_ref, oslc_ref, owin_ref, bg_ref, oc0_ref, oc1_ref, oc2_ref,
                  l0_ref, l1_ref, l2_ref, mg_ref, x_ref, g1_ref, wa_ref, wb_ref, wc_ref, wo_ref,
                  eg_ref, lng_ref, lnb_ref, o_ref):
    d = D_MODEL
    sg = _sigmoid(bg_ref[...].astype(F32))
    ob = jnp.zeros(ocmp_ref.shape, F32)
    for br, ref in enumerate((ocmp_ref, oslc_ref, owin_ref)):
        gexp = jnp.dot(sg, eg_ref[br], precision=HIGHEST, preferred_element_type=F32)
        ob = ob + gexp * ref[...].astype(F32)
    l0, l1, l2 = l0_ref[...], l1_ref[...], l2_ref[...]
    mx = jnp.maximum(jnp.maximum(l0, l1), l2)
    e0, e1, e2 = jnp.exp(l0 - mx), jnp.exp(l1 - mx), jnp.exp(l2 - mx)
    den = e0 + e1 + e2
    oc = ((e0 / den) * oc0_ref[...].astype(F32) + (e1 / den) * oc1_ref[...].astype(F32)
          + (e2 / den) * oc2_ref[...].astype(F32))
    pa = jnp.dot(oa_ref[...], wa_ref[...], preferred_element_type=F32)
    pb = jnp.dot(ob.astype(BF16), wb_ref[...], preferred_element_type=F32)
    pc = jnp.dot(oc.astype(BF16), wc_ref[...], preferred_element_type=F32)
    merged = (_sigmoid(mg_ref[:, 0:d].astype(F32)) * pa
              + _sigmoid(mg_ref[:, d:2 * d].astype(F32)) * pb
              + _sigmoid(mg_ref[:, 2 * d:3 * d].astype(F32)) * pc)
    y = jnp.dot(merged.astype(BF16), wo_ref[...], preferred_element_type=F32)
    z = DEEPNORM_ALPHA * x_ref[...] + g1_ref[0, 0] * y
    o_ref[...] = _ln(z) * lng_ref[...] + lnb_ref[...]


def _gate_expand():
    eg = np.zeros((3, LANES, B_HEADS * HEAD_DIM), np.float32)
    for g in range(B_KV_GROUPS):
        for r in range(B_REP):
            for br in range(3):
                c0 = (r * B_KV_GROUPS + g) * HEAD_DIM
                eg[br, (g * B_REP + r) * 3 + br, c0:c0 + HEAD_DIM] = 1.0
    return jnp.asarray(eg)


def _merge(o_a, o_cmp, o_slc, o_win, proj, oc, lse, x2, mod4, wa, wb, wc, wo, lng, lnb, seq):
    n, d = x2.shape
    tm = 256
    per_b = seq // tm
    row = lambda w: pl.BlockSpec((tm, w), lambda i: (i, 0))
    full = lambda a: pl.BlockSpec(a.shape, lambda i: (0,) * a.ndim)
    eg = _gate_expand()
    return pl.pallas_call(
        _merge_kernel,
        grid=(n // tm,),
        in_specs=[row(o_a.shape[1]), row(o_cmp.shape[1]), row(o_slc.shape[1]), row(o_win.shape[1]),
                  pl.BlockSpec((tm, LANES), lambda i: (i, T_BG)),
                  row(LANES), row(LANES), row(LANES), row(LANES), row(LANES), row(LANES),
                  pl.BlockSpec((tm, 3 * d), lambda i: (i, T_MG)),
                  row(d),
                  pl.BlockSpec((1, 1, 1, d), lambda i: (i // per_b, 2, 0, 0)),
                  full(wa), full(wb), full(wc), full(wo), full(eg), full(lng), full(lnb)],
        out_specs=row(d),
        out_shape=jax.ShapeDtypeStruct((n, d), F32),
        compiler_params=_cparams(("parallel",)),
        name="merge_out",
    )(o_a, o_cmp, o_slc, o_win, proj, oc[0], oc[1], oc[2], lse[0], lse[1], lse[2],
      proj, x2, mod4, wa, wb, wc, wo, eg, lng, lnb)


def _router_kernel(x_ref, sc_ref, sh_ref, rw_ref, rb_ref, h_ref, meta_ref, gate_ref, cnt_ref, carry_sc):
    i = pl.program_id(0)
    tm = x_ref.shape[0]

    @pl.when(i == 0)
    def _():
        carry_sc[...] = jnp.zeros(carry_sc.shape, F32)

    h = _ln(x_ref[...]) * (1.0 + sc_ref[0, 0]) + sh_ref[0, 0]
    for c in range(ROW_CHUNKS):
        h_ref[pl.ds(c, tm, stride=ROW_CHUNKS), :] = h[:, c * LANES:(c + 1) * LANES]
    lg = jnp.dot(h, rw_ref[...], precision=HIGHEST, preferred_element_type=F32) + rb_ref[...]
    lane = _lane_iota()
    lane_f = lane.astype(F32)
    onehots, vals, idxs = [], [], []
    for _ in range(TOP_K):
        m = jnp.max(lg, axis=1, keepdims=True)
        idx = jnp.min(jnp.where(lg == m, lane_f, float(LANES)), axis=1, keepdims=True).astype(I32)
        oh = lane == idx
        onehots.append(oh)
        vals.append(m)
        idxs.append(idx)
        lg = jnp.where(oh, -jnp.inf, lg)
    es = [jnp.exp(v - vals[0]) for v in vals]
    den = es[0] + es[1] + es[2] + es[3]
    cnt = jnp.zeros((tm, LANES), F32)
    for oh in onehots:
        cnt = cnt + jnp.where(oh, 1.0, 0.0)
    tri = jnp.where(lax.broadcasted_iota(I32, (tm, tm), 0) > lax.broadcasted_iota(I32, (tm, tm), 1), 1.0, 0.0)
    before = jnp.dot(tri.astype(BF16), cnt.astype(BF16), preferred_element_type=F32) + carry_sc[...]
    meta = jnp.zeros((tm, LANES), I32)
    gates = jnp.zeros((tm, LANES), F32)
    for k in range(TOP_K):
        rank = jnp.sum(jnp.where(onehots[k], before, 0.0), axis=1, keepdims=True).astype(I32)
        meta = jnp.where(lane == k, idxs[k], meta)
        meta = jnp.where(lane == TOP_K + k, rank, meta)
        gates = jnp.where(lane == k, es[k] / den, gates)
    meta_ref[...] = meta
    gate_ref[...] = gates
    carry_sc[...] = carry_sc[...] + jnp.sum(cnt, axis=0, keepdims=True)
    cnt_ref[...] = jnp.broadcast_to(carry_sc[...], cnt_ref.shape).astype(I32)


def _router(x2, mod4, rw, rb, seq):
    n, d = x2.shape
    tm = TOK_TILE
    per_b = seq // tm
    rw_p = jnp.zeros((d, LANES), F32).at[:, :N_EXPERTS].set(rw)
    rb_p = jnp.full((1, LANES), NEG, F32).at[0, :N_EXPERTS].set(rb)
    return pl.pallas_call(
        _router_kernel,
        grid=(n // tm,),
        in_specs=[pl.BlockSpec((tm, d), lambda i: (i, 0)),
                  pl.BlockSpec((1, 1, 1, d), lambda i: (i // per_b, 4, 0, 0)),
                  pl.BlockSpec((1, 1, 1, d), lambda i: (i // per_b, 3, 0, 0)),
                  pl.BlockSpec((d, LANES), lambda i: (0, 0)),
                  pl.BlockSpec((1, LANES), lambda i: (0, 0))],
        out_specs=[pl.BlockSpec((tm * ROW_CHUNKS, LANES), lambda i: (i, 0)),
                   pl.BlockSpec((tm, LANES), lambda i: (i, 0)),
                   pl.BlockSpec((tm, LANES), lambda i: (i, 0)),
                   pl.BlockSpec((SUBLANES, LANES), lambda i: (0, 0))],
        out_shape=[jax.ShapeDtypeStruct((n * ROW_CHUNKS, LANES), F32),
                   jax.ShapeDtypeStruct((n, LANES), I32),
                   jax.ShapeDtypeStruct((n, LANES), F32),
                   jax.ShapeDtypeStruct((SUBLANES, LANES), I32)],
        scratch_shapes=[pltpu.VMEM((1, LANES), F32)],
        compiler_params=_cparams(("arbitrary",)),
        name="router",
    )(x2, mod4, mod4, rw_p, rb_p)


def _row_slice(ref, r):
    return ref.at[pl.ds(pl.multiple_of(r * ROW_CHUNKS, ROW_CHUNKS), ROW_CHUNKS)]


def _dispatch_kernel(dest_ref, h_hbm, xs_in_hbm, xs_hbm, sem):
    del xs_in_hbm
    i = pl.program_id(0)
    per_row = LANES // TOP_K

    def issue(t, c):
        src = _row_slice(h_hbm, i * TOK_TILE + t)
        for k in range(TOP_K):
            d = dest_ref[t // per_row, (t % per_row) * TOP_K + k]
            pltpu.make_async_copy(src, _row_slice(xs_hbm, d), sem).start()
        return c

    lax.fori_loop(0, TOK_TILE, issue, 0)

    def drain(t, c):
        pltpu.make_async_copy(_row_slice(h_hbm, 0), _row_slice(xs_hbm, 0), sem).wait()
        return c

    lax.fori_loop(0, TOK_TILE * TOP_K, drain, 0)


def _dispatch(h_rows, dest2, xs_zero, n):
    return pl.pallas_call(
        _dispatch_kernel,
        grid=(n // TOK_TILE,),
        in_specs=[pl.BlockSpec((SUBLANES, LANES), lambda i: (i, 0), memory_space=pltpu.SMEM),
                  pl.BlockSpec(memory_space=pl.ANY),
                  pl.BlockSpec(memory_space=pl.ANY)],
        out_specs=pl.BlockSpec(memory_space=pl.ANY),
        out_shape=jax.ShapeDtypeStruct(xs_zero.shape, F32),
        scratch_shapes=[pltpu.SemaphoreType.DMA],
        input_output_aliases={2: 0},
        compiler_params=_cparams(("arbitrary",)),
        name="moe_dispatch",
    )(dest2, h_rows, xs_zero)


def _expert_kernel(be_ref, xs_ref, wg_ref, bg_ref, wl_ref, bl_ref, wd_ref, bd_ref, ys_ref, x_sc):
    del be_ref
    rows = x_sc.shape[0]
    for c in range(ROW_CHUNKS):
        x_sc[:, c * LANES:(c + 1) * LANES] = xs_ref[pl.ds(c, rows, stride=ROW_CHUNKS), :].astype(BF16)
    x = x_sc[...]
    glu = jnp.dot(x, wg_ref[0], preferred_element_type=F32) + bg_ref[0]
    lin = jnp.dot(x, wl_ref[0], preferred_element_type=F32) + bl_ref[0]
    glu = jnp.minimum(glu, SWIGLU_LIMIT)
    lin = jnp.clip(lin, -SWIGLU_LIMIT, SWIGLU_LIMIT)
    act = glu * _sigmoid(SWIGLU_ALPHA * glu) * (lin + 1.0)
    y = jnp.dot(act.astype(BF16), wd_ref[0], preferred_element_type=F32) + bd_ref[0]
    for c in range(ROW_CHUNKS):
        ys_ref[pl.ds(c, rows, stride=ROW_CHUNKS), :] = y[:, c * LANES:(c + 1) * LANES]


def _experts(xs, block_exp, w_glu, b_glu, w_lin, b_lin, w_dn, b_dn):
    d = D_MODEL
    n_blocks = block_exp.shape[0]
    rb = EXPERT_ROWS
    wspec = lambda: pl.BlockSpec((1, d, d), lambda g, be: (be[g], 0, 0))
    bspec = lambda: pl.BlockSpec((1, 1, d), lambda g, be: (be[g], 0, 0))
    return pl.pallas_call(
        _expert_kernel,
        grid_spec=pltpu.PrefetchScalarGridSpec(
            num_scalar_prefetch=1,
            grid=(n_blocks,),
            in_specs=[pl.BlockSpec((rb * ROW_CHUNKS, LANES), lambda g, be: (g, 0)),
                      wspec(), bspec(), wspec(), bspec(), wspec(), bspec()],
            out_specs=pl.BlockSpec((rb * ROW_CHUNKS, LANES), lambda g, be: (g, 0)),
            scratch_shapes=[pltpu.VMEM((rb, d), BF16)]),
        out_shape=jax.ShapeDtypeStruct(xs.shape, F32),
        compiler_params=_cparams(("arbitrary",)),
        name="moe_experts",
    )(block_exp, xs, w_glu, b_glu, w_lin, b_lin, w_dn, b_dn)


def _combine_kernel(dest_ref, gate_ref, x_ref, g2_ref, lng_ref, lnb_ref, ys_hbm, o_ref, buf, sem):
    tm = TOK_TILE
    per_row = LANES // TOP_K

    def issue(t, c):
        for k in range(TOP_K):
            d = dest_ref[t // per_row, (t % per_row) * TOP_K + k]
            pltpu.make_async_copy(_row_slice(ys_hbm, d), _row_slice(buf, k * tm + t), sem).start()
        return c

    lax.fori_loop(0, tm, issue, 0)

    def drain(t, c):
        pltpu.make_async_copy(_row_slice(ys_hbm, 0), _row_slice(buf, 0), sem).wait()
        return c

    lax.fori_loop(0, tm * TOP_K, drain, 0)

    gates = gate_ref[...]
    chunks = []
    for c in range(ROW_CHUNKS):
        yc = jnp.zeros((tm, LANES), F32)
        for k in range(TOP_K):
            yc = yc + gates[:, k:k + 1] * buf[pl.ds(k * tm * ROW_CHUNKS + c, tm, stride=ROW_CHUNKS), :]
        chunks.append(yc)
    y = jnp.concatenate(chunks, axis=1)
    z = DEEPNORM_ALPHA * x_ref[...] + g2_ref[0, 0] * y
    o_ref[...] = _ln(z) * lng_ref[...] + lnb_ref[...]


def _combine(ys, dest2, gates, x2, mod4, lng, lnb, seq):
    n, d = x2.shape
    tm = TOK_TILE
    per_b = seq // tm
    return pl.pallas_call(
        _combine_kernel,
        grid=(n // tm,),
        in_specs=[pl.BlockSpec((SUBLANES, LANES), lambda i: (i, 0), memory_space=pltpu.SMEM),
                  pl.BlockSpec((tm, LANES), lambda i: (i, 0)),
                  pl.BlockSpec((tm, d), lambda i: (i, 0)),
                  pl.BlockSpec((1, 1, 1, d), lambda i: (i // per_b, 5, 0, 0)),
                  pl.BlockSpec((1, d), lambda i: (0, 0)),
                  pl.BlockSpec((1, d), lambda i: (0, 0)),
                  pl.BlockSpec(memory_space=pl.ANY)],
        out_specs=pl.BlockSpec((tm, d), lambda i: (i, 0)),
        out_shape=jax.ShapeDtypeStruct((n, d), F32),
        scratch_shapes=[pltpu.VMEM((TOP_K * tm * ROW_CHUNKS, LANES), F32), pltpu.SemaphoreType.DMA],
        compiler_params=_cparams(("arbitrary",)),
        name="moe_combine",
    )(dest2, gates, x2, mod4, lng, lnb, ys)


def _moe(x2, mod4, rw, rb, w_gu, b_gu, w_dn, b_dn, lng, lnb, seq):
    n, d = x2.shape
    h_rows, meta, gates, cnt = _router(x2, mod4, rw, rb, seq)
    counts = cnt[0, :N_EXPERTS]
    padded = -(-counts // EXPERT_ROWS) * EXPERT_ROWS
    pend = jnp.cumsum(padded)
    pstart = pend - padded
    n_blocks = -(-(n * TOP_K + N_EXPERTS * (EXPERT_ROWS - 1)) // EXPERT_ROWS)
    block_exp = jnp.minimum(
        jnp.searchsorted(pend, jnp.arange(n_blocks, dtype=I32) * EXPERT_ROWS, side="right"),
        N_EXPERTS - 1).astype(I32)
    dest = jnp.take(pstart, meta[:, :TOP_K]) + meta[:, TOP_K:2 * TOP_K]
    dest2 = dest.reshape(n * TOP_K // LANES, LANES).astype(I32)
    rows = n_blocks * EXPERT_ROWS
    xs = _dispatch(h_rows, dest2, jnp.zeros((rows * ROW_CHUNKS, LANES), F32), n)
    w_glu = w_gu[..., 0::2].astype(BF16)
    w_lin = w_gu[..., 1::2].astype(BF16)
    b_glu = b_gu[:, None, 0::2]
    b_lin = b_gu[:, None, 1::2]
    ys = _experts(xs, block_exp, w_glu, b_glu, w_lin, b_lin, w_dn.astype(BF16), b_dn[:, None, :])
    return _combine(ys, dest2, gates, x2, mod4, lng.reshape(1, d), lnb.reshape(1, d), seq)


def _permute_w_in(w):
    d = w.shape[0]
    off = {}
    o = 0
    for name, wd in (("a_q", 256), ("a_k", 256), ("a_v", 256), ("b_q", 384), ("b_kc", 128), ("b_vc", 128),
                     ("b_ks", 128), ("b_vs", 128), ("b_kw", 128), ("b_vw", 128), ("b_gate", 18),
                     ("c_q", 384), ("c_k", 384), ("c_v", 384), ("merge_gate", 3 * D_MODEL)):
        off[name] = (o, o + wd)
        o += wd
    col = lambda name: w[:, off[name][0]:off[name][1]]
    scale = HEAD_DIM ** -0.5
    b_q = col("b_q").reshape(d, B_KV_GROUPS, B_REP, HEAD_DIM).transpose(0, 2, 1, 3).reshape(d, B_HEADS * HEAD_DIM)
    gate_pad = jnp.zeros((d, LANES - B_HEADS * 3), w.dtype)
    parts = [col("merge_gate"),
             col("a_q") * scale, col("a_k"), b_q * scale, col("b_kc"), col("b_ks"), col("b_kw"),
             col("c_q") * scale, col("c_k"),
             col("a_v"), col("b_vc"), col("b_vs"), col("b_vw"), col("c_v"),
             col("b_gate"), gate_pad]
    return jnp.concatenate(parts, axis=1).astype(BF16)


def _dilate(arr, tile, batch, seq, dil):
    a = arr[:, tile * LANES:(tile + 1) * LANES].reshape(batch, seq // dil, dil, LANES)
    return a.transpose(0, 2, 1, 3).reshape(batch * seq, LANES)


def _undilate(arr, batch, seq, dil):
    a = arr.reshape(batch, dil, seq // dil, LANES)
    return a.transpose(0, 2, 1, 3).reshape(batch * seq, LANES)


def _mixer_layer(x2, mod4, cos_t, sin_t, batch, seq, w_in, cmp_w1_k, cmp_w2_k, cmp_pe_k,
                 cmp_w1_v, cmp_w2_v, cmp_pe_v, w_a, w_b, w_c, w_out, lng, lnb):
    d = D_MODEL
    proj = _inproj(x2, mod4, _permute_w_in(w_in), cos_t, sin_t, seq)

    bias_a = _moba_select(proj, _kmean(proj, batch, seq), batch, seq)
    o_a = _bb_attn(proj, bias_a, batch=batch, seq=seq, n_pairs=A_HEADS // 2,
                   q_tile=lambda p: T_AQ + p, k_tile=lambda p: T_AK + p, v_tile=lambda p: T_AV + p,
                   bias_tile=lambda p: p, blk=MOBA_BLOCK)

    kc = _compress(proj[:, T_BKC * LANES:(T_BKC + 1) * LANES], cmp_w1_k, cmp_w2_k, cmp_pe_k, batch, seq)
    vc = _compress(proj[:, T_BVC * LANES:(T_BVC + 1) * LANES], cmp_w1_v, cmp_w2_v, cmp_pe_v, batch, seq)
    o_cmp, bias_b = _nsa_cmp(proj, kc, vc, batch, seq)
    o_slc = _bb_attn(proj, bias_b, batch=batch, seq=seq, n_pairs=B_REP,
                     q_tile=lambda p: T_BQ + p, k_tile=lambda p: T_BKS, v_tile=lambda p: T_BVS,
                     bias_tile=lambda p: 0, blk=SLC_BLOCK)
    o_win, _ = _band_attn(proj, proj, proj, batch=batch, seq=seq, n_pairs=B_REP,
                          q_tile=lambda p: T_BQ + p, k_tile=lambda p: T_BKW, v_tile=lambda p: T_BVW,
                          window=NSA_WINDOW, t=256)

    oc, lse = [], []
    for gi, (window, dil) in enumerate(DILATED_PAIRS):
        wlen = window // dil + 1
        if dil == 1:
            o, l = _band_attn(proj, proj, proj, batch=batch, seq=seq, n_pairs=1,
                              q_tile=lambda p: T_CQ + gi, k_tile=lambda p: T_CK + gi,
                              v_tile=lambda p: T_CV + gi, window=wlen, t=128)
        else:
            qd = _dilate(proj, T_CQ + gi, batch, seq, dil)
            kd = _dilate(proj, T_CK + gi, batch, seq, dil)
            vd = _dilate(proj, T_CV + gi, batch, seq, dil)
            o, l = _band_attn(qd, kd, vd, batch=batch * dil, seq=seq // dil, n_pairs=1,
                              q_tile=lambda p: 0, k_tile=lambda p: 0, v_tile=lambda p: 0,
                              window=wlen, t=128)
            o, l = _undilate(o, batch, seq, dil), _undilate(l, batch, seq, dil)
        oc.append(o)
        lse.append(l)

    w_b_perm = w_b.reshape(B_KV_GROUPS, B_REP, HEAD_DIM, d).transpose(1, 0, 2, 3).reshape(B_HEADS * HEAD_DIM, d)
    return _merge(o_a, o_cmp, o_slc, o_win, proj, oc, lse, x2, mod4,
                  w_a.astype(BF16), w_b_perm.astype(BF16), w_c.astype(BF16), w_out.astype(BF16),
                  lng.reshape(1, d), lnb.reshape(1, d), seq)


def _rope_tables(positions):
    inv = ROPE_THETA ** (-jnp.arange(0, HEAD_DIM, 2, dtype=F32) / HEAD_DIM)
    ang = positions.astype(F32).reshape(-1, 1) * inv[None, :]
    cos, sin = jnp.cos(ang), jnp.sin(ang)
    return jnp.tile(cos, (1, 4)), jnp.tile(jnp.concatenate([-sin, sin], axis=1), (1, 2))


def kernel(x, c, positions, w_ada, b_ada, w_in, cmp_w1_k, cmp_w2_k, cmp_pe_k, cmp_w1_v, cmp_w2_v, cmp_pe_v, w_branch_a, w_branch_b, w_branch_c, w_out, ln1_g, ln1_b, router_w, router_b, w_gate_up, b_gate_up, w_down, b_down, ln2_g, ln2_b):
    batch, seq, d = x.shape
    depth = w_in.shape[0]
    cos_t, sin_t = _rope_tables(positions)
    mod = _ada(c, w_ada, b_ada)
    x2 = x.reshape(batch * seq, d)
    for l in range(depth):
        mod4 = mod[l].reshape(batch, N_ADA, 1, d)
        x2 = _mixer_layer(x2, mod4, cos_t, sin_t, batch, seq, w_in[l], cmp_w1_k[l], cmp_w2_k[l], cmp_pe_k[l],
                          cmp_w1_v[l], cmp_w2_v[l], cmp_pe_v[l], w_branch_a[l], w_branch_b[l],
                          w_branch_c[l], w_out[l], ln1_g[l], ln1_b[l])
        x2 = _moe(x2, mod4, router_w[l], router_b[l], w_gate_up[l], b_gate_up[l], w_down[l], b_down[l],
                  ln2_g[l], ln2_b[l], seq)
    return x2.reshape(batch, seq, d)
```

```python
import functools

import numpy as np
import jax
import jax.numpy as jnp
from jax import lax
from jax.experimental import pallas as pl
from jax.experimental.pallas import tpu as pltpu

F32 = jnp.float32
BF16 = jnp.bfloat16
I32 = jnp.int32
HIGHEST = lax.Precision.HIGHEST

D_MODEL = 1024
DEPTH = 2
HEAD_DIM = 64
ROPE_THETA = 10000.0
LN_EPS = 1e-5
DEEPNORM_ALPHA = (2 * DEPTH) ** 0.25
N_ADA = 6
A_HEADS = 4
MOBA_BLOCK = 256
MOBA_TOPK = 3
B_HEADS = 6
B_KV_GROUPS = 2
B_REP = 3
CMP_BLOCK = 32
CMP_STRIDE = 16
CMP_HIDDEN = 128
SLC_BLOCK = 64
SLC_TOPK = 16
NSA_WINDOW = 512
FORCE_SCORE = 1e6
DILATED_PAIRS = ((128, 1), (512, 4), (2048, 16))
N_EXPERTS = 32
TOP_K = 4
SWIGLU_ALPHA = 1.702
SWIGLU_LIMIT = 7.0

LANES = 128
SUBLANES = 8
HALF = LANES // 2
NEG = -1e30
ROW_CHUNKS = D_MODEL // LANES
VMEM_LIMIT = 56 * 1024 * 1024

T_MG = 0
T_ROPE0, T_ROPE1 = 24, 40
T_AQ, T_AK, T_BQ, T_BKC, T_BKS, T_BKW, T_CQ, T_CK = 24, 26, 28, 31, 32, 33, 34, 37
T_AV, T_BVC, T_BVS, T_BVW, T_CV, T_BG = 40, 42, 43, 44, 45, 48
N_PROJ_TILES = 49
PROJ_W = N_PROJ_TILES * LANES

EXPERT_ROWS = 512
TOK_TILE = 256


def _cparams(sem, **kw):
    return pltpu.CompilerParams(dimension_semantics=sem, vmem_limit_bytes=VMEM_LIMIT, **kw)


def _lane_iota(shape=(1, LANES)):
    return lax.broadcasted_iota(I32, shape, len(shape) - 1)


def _sigmoid(x):
    return 1.0 / (1.0 + jnp.exp(-x))


def _ln(x):
    mu = jnp.mean(x, axis=-1, keepdims=True)
    xc = x - mu
    var = jnp.mean(xc * xc, axis=-1, keepdims=True)
    return xc * lax.rsqrt(var + LN_EPS)


def _dot_nt(a, b, precision=None):
    return lax.dot_general(a, b, (((1,), (1,)), ((), ())), precision=precision,
                           preferred_element_type=F32)


def _ada_kernel(c_ref, w_ref, b_ref, o_ref):
    c = c_ref[...]
    cond = c * _sigmoid(c)
    o_ref[0] = jnp.dot(cond, w_ref[0], precision=HIGHEST, preferred_element_type=F32) + b_ref[0]


def _ada(c, w_ada, b_ada):
    depth, d, n = w_ada.shape
    b = c.shape[0]
    tn = 1536
    return pl.pallas_call(
        _ada_kernel,
        grid=(depth, n // tn),
        in_specs=[pl.BlockSpec((b, d), lambda l, j: (0, 0)),
                  pl.BlockSpec((1, d, tn), lambda l, j: (l, 0, j)),
                  pl.BlockSpec((1, 1, tn), lambda l, j: (l, 0, j))],
        out_specs=pl.BlockSpec((1, b, tn), lambda l, j: (l, 0, j)),
        out_shape=jax.ShapeDtypeStruct((depth, b, n), F32),
        compiler_params=_cparams(("parallel", "parallel")),
        name="ada",
    )(c, w_ada, b_ada.reshape(depth, 1, n))


def _inproj_kernel(x_ref, sc_ref, sh_ref, w_ref, cos_ref, sin_ref, o_ref, *, chunks):
    h = _ln(x_ref[...]) * (1.0 + sc_ref[0, 0]) + sh_ref[0, 0]
    hb = h.astype(BF16)
    first_half = (_lane_iota() & (HEAD_DIM - 1)) < (HEAD_DIM // 2)
    for c0, cw, rope in chunks:
        acc = jnp.dot(hb, w_ref[:, c0:c0 + cw], preferred_element_type=F32)
        if rope:
            cos = cos_ref[...]
            sin = sin_ref[...]
            for t in range(cw // LANES):
                a = acc[:, t * LANES:(t + 1) * LANES]
                rot = jnp.where(first_half, pltpu.roll(a, LANES - HEAD_DIM // 2, 1),
                                pltpu.roll(a, HEAD_DIM // 2, 1))
                o_ref[:, c0 + t * LANES:c0 + (t + 1) * LANES] = (a * cos + rot * sin).astype(o_ref.dtype)
        else:
            o_ref[:, c0:c0 + cw] = acc.astype(o_ref.dtype)


def _inproj_chunks():
    chunks = []
    for lo, hi, rope in ((0, T_ROPE0, False), (T_ROPE0, T_ROPE1, True), (T_ROPE1, N_PROJ_TILES, False)):
        c = lo * LANES
        while c < hi * LANES:
            cw = min(512, hi * LANES - c)
            chunks.append((c, cw, rope))
            c += cw
    return tuple(chunks)


def _inproj(x2, mod4, w_perm, cos_t, sin_t, seq):
    n, d = x2.shape
    tm = 256
    per_b = seq // tm
    return pl.pallas_call(
        functools.partial(_inproj_kernel, chunks=_inproj_chunks()),
        grid=(n // tm,),
        in_specs=[pl.BlockSpec((tm, d), lambda i: (i, 0)),
                  pl.BlockSpec((1, 1, 1, d), lambda i: (i // per_b, 1, 0, 0)),
                  pl.BlockSpec((1, 1, 1, d), lambda i: (i // per_b, 0, 0, 0)),
                  pl.BlockSpec((d, PROJ_W), lambda i: (0, 0)),
                  pl.BlockSpec((tm, LANES), lambda i: (i, 0)),
                  pl.BlockSpec((tm, LANES), lambda i: (i, 0))],
        out_specs=pl.BlockSpec((tm, PROJ_W), lambda i: (i, 0)),
        out_shape=jax.ShapeDtypeStruct((n, PROJ_W), BF16),
        compiler_params=_cparams(("parallel",)),
        name="inproj",
    )(x2, mod4, mod4, w_perm, cos_t, sin_t)


def _bb_attn_kernel(q_ref, k_ref, v_ref, b_ref, o_ref, qa_sc, m_sc, l_sc, acc_sc, *, blk_shift, tq):
    i = pl.program_id(2)
    lane = _lane_iota()
    lo = lane < HALF
    q = q_ref[...]
    b = b_ref[...]
    qa_sc[0] = jnp.where(lo, q, b)
    qa_sc[1] = jnp.where(lo, b, q)
    m_sc[...] = jnp.full(m_sc.shape, -jnp.inf, F32)
    l_sc[...] = jnp.zeros(l_sc.shape, F32)
    acc_sc[...] = jnp.zeros(acc_sc.shape, F32)

    def step(start, tk, diagonal):
        k = k_ref[pl.ds(start, tk), :]
        v = v_ref[pl.ds(start, tk), :]
        kblk = (start + lax.broadcasted_iota(I32, (tk, 1), 0)) >> blk_shift
        onehot = jnp.where(kblk == (lane & (HALF - 1)), 1.0, 0.0).astype(BF16)
        ka = (jnp.where(lo, k, onehot), jnp.where(lo, onehot, k))
        if diagonal:
            causal = lax.broadcasted_iota(I32, (1, tk), 1) <= lax.broadcasted_iota(I32, (tq, 1), 0)
        for h in range(2):
            s = _dot_nt(qa_sc[h], ka[h])
            if diagonal:
                s = jnp.where(causal, s, NEG)
            m_prev = m_sc[h]
            m_new = jnp.maximum(m_prev, jnp.max(s, axis=1, keepdims=True))
            alpha = jnp.exp(m_prev - m_new)
            p = jnp.exp(s - m_new[:, :1])
            l_sc[h] = alpha * l_sc[h] + jnp.sum(p, axis=1, keepdims=True)
            acc_sc[h] = alpha * acc_sc[h] + jnp.dot(p.astype(BF16), v, preferred_element_type=F32)
            m_sc[h] = m_new

    def body(t2, c):
        step(pl.multiple_of(t2 * (2 * tq), 2 * tq), 2 * tq, False)
        return c

    lax.fori_loop(0, i // 2, body, 0)

    @pl.when(i % 2 == 1)
    def _():
        step(pl.multiple_of((i - 1) * tq, tq), tq, False)

    step(pl.multiple_of(i * tq, tq), tq, True)
    o = jnp.where(lo, acc_sc[0] / l_sc[0], acc_sc[1] / l_sc[1])
    o_ref[...] = o.astype(o_ref.dtype)


def _bb_attn(proj, bias, *, batch, seq, n_pairs, q_tile, k_tile, v_tile, bias_tile, blk):
    t = 256
    nq = seq // t
    n = batch * seq
    kern = functools.partial(_bb_attn_kernel, blk_shift=int(np.log2(blk)), tq=t)
    return pl.pallas_call(
        kern,
        grid=(batch, n_pairs, nq),
        in_specs=[pl.BlockSpec((t, LANES), lambda b, p, i: (b * nq + i, q_tile(p))),
                  pl.BlockSpec((seq, LANES), lambda b, p, i: (b, k_tile(p))),
                  pl.BlockSpec((seq, LANES), lambda b, p, i: (b, v_tile(p))),
                  pl.BlockSpec((t, LANES), lambda b, p, i: (b * nq + i, bias_tile(p)))],
        out_specs=pl.BlockSpec((t, LANES), lambda b, p, i: (b * nq + i, p)),
        out_shape=jax.ShapeDtypeStruct((n, n_pairs * LANES), BF16),
        scratch_shapes=[pltpu.VMEM((2, t, LANES), BF16),
                        pltpu.VMEM((2, t, LANES), F32),
                        pltpu.VMEM((2, t, LANES), F32),
                        pltpu.VMEM((2, t, LANES), F32)],
        compiler_params=_cparams(("parallel", "parallel", "parallel")),
        name="bb_attn_%d" % blk,
    )(proj, proj, proj, bias)


def _band_kernel(q_ref, k_ref, v_ref, o_ref, lse_ref, *, tq, kw, window, seq):
    i = pl.program_id(2)
    lane = _lane_iota()
    lo = lane < HALF
    start = pl.multiple_of(jnp.clip(i * tq - (kw - tq), 0, seq - kw), LANES)
    q = q_ref[...]
    k = k_ref[pl.ds(start, kw), :]
    v = v_ref[pl.ds(start, kw), :]
    zero = jnp.zeros_like(q)
    qs = (jnp.where(lo, q, zero), jnp.where(lo, zero, q))
    rel = (i * tq + lax.broadcasted_iota(I32, (tq, 1), 0)) - (start + lax.broadcasted_iota(I32, (1, kw), 1))
    ok = (rel >= 0) & (rel < window)
    outs, lses = [], []
    for h in range(2):
        s = jnp.where(ok, _dot_nt(qs[h], k), NEG)
        m = jnp.max(s, axis=1, keepdims=True)
        p = jnp.exp(s - m)
        l = jnp.sum(p, axis=1, keepdims=True)
        outs.append(jnp.dot(p.astype(BF16), v, preferred_element_type=F32) / l)
        lses.append(m + jnp.log(l))
    o_ref[...] = jnp.where(lo, outs[0], outs[1]).astype(o_ref.dtype)
    lse_ref[...] = jnp.where(lo, lses[0], lses[1])


def _band_attn(q_arr, k_arr, v_arr, *, batch, seq, n_pairs, q_tile, k_tile, v_tile, window, t):
    t = min(t, seq)
    nq = seq // t
    kw = min(seq, t + -(-(window - 1) // LANES) * LANES)
    n = batch * seq
    kern = functools.partial(_band_kernel, tq=t, kw=kw, window=window, seq=seq)
    return pl.pallas_call(
        kern,
        grid=(batch, n_pairs, nq),
        in_specs=[pl.BlockSpec((t, LANES), lambda b, p, i: (b * nq + i, q_tile(p))),
                  pl.BlockSpec((seq, LANES), lambda b, p, i: (b, k_tile(p))),
                  pl.BlockSpec((seq, LANES), lambda b, p, i: (b, v_tile(p)))],
        out_specs=[pl.BlockSpec((t, LANES), lambda b, p, i: (b * nq + i, p)),
                   pl.BlockSpec((t, LANES), lambda b, p, i: (b * nq + i, p))],
        out_shape=[jax.ShapeDtypeStruct((n, n_pairs * LANES), BF16),
                   jax.ShapeDtypeStruct((n, n_pairs * LANES), F32)],
        compiler_params=_cparams(("parallel", "parallel", "parallel")),
        name="band_attn_%d" % window,
    )(q_arr, k_arr, v_arr)


def _kmean_kernel(k_ref, o_ref, *, nb):
    s = k_ref.shape[0]
    blk = lax.broadcasted_iota(I32, (nb, s), 1) >> int(np.log2(MOBA_BLOCK))
    avg = jnp.where(blk == lax.broadcasted_iota(I32, (nb, s), 0), 1.0 / MOBA_BLOCK, 0.0).astype(BF16)
    o_ref[0] = jnp.dot(avg, k_ref[...], preferred_element_type=F32)


def _kmean(proj, batch, seq):
    nb = seq // MOBA_BLOCK
    w = A_HEADS * HEAD_DIM
    return pl.pallas_call(
        functools.partial(_kmean_kernel, nb=nb),
        grid=(batch,),
        in_specs=[pl.BlockSpec((seq, w), lambda b: (b, T_AK * LANES // w))],
        out_specs=pl.BlockSpec((1, nb, w), lambda b: (b, 0, 0)),
        out_shape=jax.ShapeDtypeStruct((batch, nb, w), F32),
        compiler_params=_cparams(("parallel",)),
        name="moba_kmean",
    )(proj)


def _rank_desc(g, n_idx, n):
    rank = jnp.zeros(g.shape, I32)
    for m in range(n):
        c = g[m:m + 1, :]
        beats = (c > g) | ((c == g) & (n_idx > m))
        rank = rank + jnp.where(beats, 1, 0)
    return rank


def _moba_sel_kernel(q_ref, km_ref, b_ref, *, nb, n_sel):
    i = pl.program_id(2)
    t = q_ref.shape[0]
    q = q_ref[...].astype(F32)
    km = km_ref[0]
    lo = _lane_iota() < HALF
    zero = jnp.zeros_like(km)
    pad = jnp.zeros((HALF - nb, LANES), F32)
    kmt = jnp.concatenate([jnp.where(lo, zero, km), pad, jnp.where(lo, km, zero), pad], axis=0)
    gt = _dot_nt(kmt, q, precision=HIGHEST)
    n_idx = lax.broadcasted_iota(I32, (nb, 1), 0)
    valid = n_idx < i
    rows = []
    for r0 in (0, HALF):
        g = jnp.where(valid, gt[r0:r0 + nb, :], -jnp.inf)
        rank = _rank_desc(g, n_idx, nb)
        allowed = (valid & (rank < n_sel)) | (n_idx == i)
        rows.append(jnp.where(allowed, 0.0, NEG))
        rows.append(jnp.zeros((HALF - nb, t), F32))
    b_ref[...] = jnp.concatenate(rows, axis=0).T.astype(b_ref.dtype)


def _moba_select(proj, kmean, batch, seq):
    t = MOBA_BLOCK
    nq = seq // t
    nb = seq // MOBA_BLOCK
    n_sel = min(MOBA_TOPK, nb - 1)
    n_pairs = A_HEADS // 2
    return pl.pallas_call(
        functools.partial(_moba_sel_kernel, nb=nb, n_sel=n_sel),
        grid=(batch, n_pairs, nq),
        in_specs=[pl.BlockSpec((t, LANES), lambda b, p, i: (b * nq + i, T_AQ + p)),
                  pl.BlockSpec((1, nb, LANES), lambda b, p, i: (b, 0, p))],
        out_specs=pl.BlockSpec((t, LANES), lambda b, p, i: (b * nq + i, p)),
        out_shape=jax.ShapeDtypeStruct((batch * seq, n_pairs * LANES), BF16),
        compiler_params=_cparams(("parallel", "parallel", "parallel")),
        name="moba_select",
    )(proj, kmean)


def _compress_kernel(x_ref, w1_ref, pe_ref, w1f_ref, w2_ref, o_ref):
    x = x_ref[0]
    nblk = x.shape[0]
    outs = []
    for g in range(B_KV_GROUPS):
        u = jnp.dot(x, w1_ref[g, 0], preferred_element_type=F32)
        v = jnp.dot(x, w1_ref[g, 1], preferred_element_type=F32)
        pe_h = jnp.dot(pe_ref[...], w1f_ref[...], precision=HIGHEST, preferred_element_type=F32)
        hid = u + pltpu.roll(v, nblk - 1, 0) + pe_h[:1]
        hid = hid * _sigmoid(hid)
        outs.append(jnp.dot(hid, w2_ref[g], precision=HIGHEST, preferred_element_type=F32))
    o = outs[0] + outs[1]
    rows = lax.broadcasted_iota(I32, (nblk, 1), 0)
    o_ref[0] = jnp.where(rows < nblk - 1, o, 0.0).astype(o_ref.dtype)


def _compress(xt, w1, w2, pe, batch, seq):
    nblk = seq // CMP_STRIDE
    xg = xt.reshape(batch, nblk, CMP_STRIDE * LANES)
    w1r = w1.reshape(2, CMP_STRIDE, HEAD_DIM, CMP_HIDDEN)
    w1e = jnp.zeros((B_KV_GROUPS, 2, CMP_STRIDE, B_KV_GROUPS, HEAD_DIM, CMP_HIDDEN), F32)
    for g in range(B_KV_GROUPS):
        w1e = w1e.at[g, :, :, g].set(w1r)
    w1e = w1e.reshape(B_KV_GROUPS, 2, CMP_STRIDE * LANES, CMP_HIDDEN).astype(BF16)
    pe_flat = jnp.broadcast_to(pe.reshape(1, CMP_BLOCK * HEAD_DIM), (SUBLANES, CMP_BLOCK * HEAD_DIM))
    w2e = jnp.zeros((B_KV_GROUPS, CMP_HIDDEN, LANES), F32)
    for g in range(B_KV_GROUPS):
        w2e = w2e.at[g, :, g * HEAD_DIM:(g + 1) * HEAD_DIM].set(w2)
    return pl.pallas_call(
        _compress_kernel,
        grid=(batch,),
        in_specs=[pl.BlockSpec((1, nblk, CMP_STRIDE * LANES), lambda b: (b, 0, 0)),
                  pl.BlockSpec(w1e.shape, lambda b: (0, 0, 0, 0)),
                  pl.BlockSpec(pe_flat.shape, lambda b: (0, 0)),
                  pl.BlockSpec(w1.shape, lambda b: (0, 0)),
                  pl.BlockSpec(w2e.shape, lambda b: (0, 0, 0))],
        out_specs=pl.BlockSpec((1, nblk, LANES), lambda b: (b, 0, 0)),
        out_shape=jax.ShapeDtypeStruct((batch, nblk, LANES), BF16),
        compiler_params=_cparams(("parallel",)),
        name="nsa_compress",
    )(xg, w1e, pe_flat, w1, w2e)


def _nsa_cmp_kernel(q0_ref, q1_ref, q2_ref, kc_ref, vc_ref, ov_ref, o_ref, b_ref, *, n_sel, ns):
    i = pl.program_id(1)
    t = q0_ref.shape[0]
    ncp = kc_ref.shape[1]
    lane = _lane_iota()
    lo = lane < HALF
    kc = kc_ref[0]
    vc = vc_ref[0]
    tq = i * t + lax.broadcasted_iota(I32, (t, 1), 0)
    cmp_end = lax.broadcasted_iota(I32, (1, ncp), 1) * CMP_STRIDE + (CMP_BLOCK - 1)
    vis = cmp_end <= tq
    psum = [jnp.zeros((t, ncp), F32) for _ in range(B_KV_GROUPS)]
    for r, q_ref in enumerate((q0_ref, q1_ref, q2_ref)):
        q = q_ref[...]
        zero = jnp.zeros_like(q)
        outs = []
        for g in range(B_KV_GROUPS):
            qg = jnp.where(lo, q, zero) if g == 0 else jnp.where(lo, zero, q)
            s = jnp.where(vis, _dot_nt(qg, kc), NEG)
            m = jnp.max(s, axis=1, keepdims=True)
            m = jnp.where(m > 0.5 * NEG, m, 0.0)
            e = jnp.where(vis, jnp.exp(s - m), 0.0)
            p = e / jnp.maximum(jnp.sum(e, axis=1, keepdims=True), 1e-30)
            psum[g] = psum[g] + p
            outs.append(jnp.dot(p.astype(BF16), vc, preferred_element_type=F32))
        o_ref[:, r * LANES:(r + 1) * LANES] = jnp.where(lo, outs[0], outs[1]).astype(o_ref.dtype)
    n_idx = lax.broadcasted_iota(I32, (ns, 1), 0)
    tcol = i * t + lax.broadcasted_iota(I32, (1, t), 1)
    qblk = tcol >> int(np.log2(SLC_BLOCK))
    forced = (n_idx == 0) | (n_idx == qblk) | (n_idx == qblk - 1)
    valid = n_idx <= qblk
    rows = []
    for g in (1, 0):
        imp = _dot_nt(ov_ref[...], psum[g], precision=HIGHEST)
        imp = jnp.where(forced, FORCE_SCORE, imp)
        imp = jnp.where(valid, imp, -jnp.inf)
        rank = _rank_desc(imp, n_idx, ns)
        allowed = valid & (rank < n_sel)
        rows.append(jnp.where(allowed, 0.0, NEG))
        if ns < HALF:
            rows.append(jnp.zeros((HALF - ns, t), F32))
    b_ref[...] = jnp.concatenate(rows, axis=0).T.astype(b_ref.dtype)


def _nsa_cmp(proj, kc, vc, batch, seq):
    t = 256
    nq = seq // t
    ns = seq // SLC_BLOCK
    n_sel = min(SLC_TOPK, ns)
    ncp = seq // CMP_STRIDE
    cs = np.arange(ncp)[None, :] * CMP_STRIDE
    ss = np.arange(ns)[:, None] * SLC_BLOCK
    ov = ((cs < ss + SLC_BLOCK) & (cs + CMP_BLOCK > ss)).astype(np.float32)
    ov[:, ncp - 1] = 0.0
    n = batch * seq
    qspec = [pl.BlockSpec((t, LANES), (lambda b, i, r=r: (b * nq + i, T_BQ + r))) for r in range(B_REP)]
    return pl.pallas_call(
        functools.partial(_nsa_cmp_kernel, n_sel=n_sel, ns=ns),
        grid=(batch, nq),
        in_specs=qspec + [pl.BlockSpec((1, ncp, LANES), lambda b, i: (b, 0, 0)),
                          pl.BlockSpec((1, ncp, LANES), lambda b, i: (b, 0, 0)),
                          pl.BlockSpec((ns, ncp), lambda b, i: (0, 0))],
        out_specs=[pl.BlockSpec((t, B_REP * LANES), lambda b, i: (b * nq + i, 0)),
                   pl.BlockSpec((t, LANES), lambda b, i: (b * nq + i, 0))],
        out_shape=[jax.ShapeDtypeStruct((n, B_REP * LANES), BF16),
                   jax.ShapeDtypeStruct((n, LANES), BF16)],
        compiler_params=_cparams(("parallel", "parallel")),
        name="nsa_cmp_select",
    )(proj, proj, proj, kc, vc, jnp.asarray(ov))


def _merge_kernel(oa_ref, ocmp_ref, oslc_ref, owin_ref, bg_ref, oc0_ref, oc1_ref, oc2_ref,
                  l0_ref, l1_ref, l2_ref, mg_ref, x_ref, g1_ref, wa_ref, wb_ref, wc_ref, wo_ref,
                  eg_ref, lng_ref, lnb_ref, o_ref):
    d = D_MODEL
    sg = _sigmoid(bg_ref[...].astype(F32))
    ob = jnp.zeros(ocmp_ref.shape, F32)
    for br, ref in enumerate((ocmp_ref, oslc_ref, owin_ref)):
        gexp = jnp.dot(sg, eg_ref[br], precision=HIGHEST, preferred_element_type=F32)
        ob = ob + gexp * ref[...].astype(F32)
    l0, l1, l2 = l0_ref[...], l1_ref[...], l2_ref[...]
    mx = jnp.maximum(jnp.maximum(l0, l1), l2)
    e0, e1, e2 = jnp.exp(l0 - mx), jnp.exp(l1 - mx), jnp.exp(l2 - mx)
    den = e0 + e1 + e2
    oc = ((e0 / den) * oc0_ref[...].astype(F32) + (e1 / den) * oc1_ref[...].astype(F32)
          + (e2 / den) * oc2_ref[...].astype(F32))
    pa = jnp.dot(oa_ref[...], wa_ref[...], preferred_element_type=F32)
    pb = jnp.dot(ob.astype(BF16), wb_ref[...], preferred_element_type=F32)
    pc = jnp.dot(oc.astype(BF16), wc_ref[...], preferred_element_type=F32)
    merged = (_sigmoid(mg_ref[:, 0:d].astype(F32)) * pa
              + _sigmoid(mg_ref[:, d:2 * d].astype(F32)) * pb
              + _sigmoid(mg_ref[:, 2 * d:3 * d].astype(F32)) * pc)
    y = jnp.dot(merged.astype(BF16), wo_ref[...], preferred_element_type=F32)
    z = DEEPNORM_ALPHA * x_ref[...] + g1_ref[0, 0] * y
    o_ref[...] = _ln(z) * lng_ref[...] + lnb_ref[...]


def _gate_expand():
    eg = np.zeros((3, LANES, B_HEADS * HEAD_DIM), np.float32)
    for g in range(B_KV_GROUPS):
        for r in range(B_REP):
            for br in range(3):
                c0 = (r * B_KV_GROUPS + g) * HEAD_DIM
                eg[br, (g * B_REP + r) * 3 + br, c0:c0 + HEAD_DIM] = 1.0
    return jnp.asarray(eg)


def _merge(o_a, o_cmp, o_slc, o_win, proj, oc, lse, x2, mod4, wa, wb, wc, wo, lng, lnb, seq):
    n, d = x2.shape
    tm = 256
    per_b = seq // tm
    row = lambda w: pl.BlockSpec((tm, w), lambda i: (i, 0))
    full = lambda a: pl.BlockSpec(a.shape, lambda i: (0,) * a.ndim)
    eg = _gate_expand()
    return pl.pallas_call(
        _merge_kernel,
        grid=(n // tm,),
        in_specs=[row(o_a.shape[1]), row(o_cmp.shape[1]), row(o_slc.shape[1]), row(o_win.shape[1]),
                  pl.BlockSpec((tm, LANES), lambda i: (i, T_BG)),
                  row(LANES), row(LANES), row(LANES), row(LANES), row(LANES), row(LANES),
                  pl.BlockSpec((tm, 3 * d), lambda i: (i, T_MG)),
                  row(d),
                  pl.BlockSpec((1, 1, 1, d), lambda i: (i // per_b, 2, 0, 0)),
                  full(wa), full(wb), full(wc), full(wo), full(eg), full(lng), full(lnb)],
        out_specs=row(d),
        out_shape=jax.ShapeDtypeStruct((n, d), F32),
        compiler_params=_cparams(("parallel",)),
        name="merge_out",
    )(o_a, o_cmp, o_slc, o_win, proj, oc[0], oc[1], oc[2], lse[0], lse[1], lse[2],
      proj, x2, mod4, wa, wb, wc, wo, eg, lng, lnb)


def _router_kernel(x_ref, sc_ref, sh_ref, rw_ref, rb_ref, h_ref, meta_ref, gate_ref, cnt_ref, carry_sc):
    i = pl.program_id(0)
    tm = x_ref.shape[0]

    @pl.when(i == 0)
    def _():
        carry_sc[...] = jnp.zeros(carry_sc.shape, F32)

    h = _ln(x_ref[...]) * (1.0 + sc_ref[0, 0]) + sh_ref[0, 0]
    for c in range(ROW_CHUNKS):
        h_ref[pl.ds(c, tm, stride=ROW_CHUNKS), :] = h[:, c * LANES:(c + 1) * LANES]
    lg = jnp.dot(h, rw_ref[...], precision=HIGHEST, preferred_element_type=F32) + rb_ref[...]
    lane = _lane_iota()
    lane_f = lane.astype(F32)
    onehots, vals, idxs = [], [], []
    for _ in range(TOP_K):
        m = jnp.max(lg, axis=1, keepdims=True)
        idx = jnp.min(jnp.where(lg == m, lane_f, float(LANES)), axis=1, keepdims=True).astype(I32)
        oh = lane == idx
        onehots.append(oh)
        vals.append(m)
        idxs.append(idx)
        lg = jnp.where(oh, -jnp.inf, lg)
    es = [jnp.exp(v - vals[0]) for v in vals]
    den = es[0] + es[1] + es[2] + es[3]
    cnt = jnp.zeros((tm, LANES), F32)
    for oh in onehots:
        cnt = cnt + jnp.where(oh, 1.0, 0.0)
    tri = jnp.where(lax.broadcasted_iota(I32, (tm, tm), 0) > lax.broadcasted_iota(I32, (tm, tm), 1), 1.0, 0.0)
    before = jnp.dot(tri.astype(BF16), cnt.astype(BF16), preferred_element_type=F32) + carry_sc[...]
    meta = jnp.zeros((tm, LANES), I32)
    gates = jnp.zeros((tm, LANES), F32)
    for k in range(TOP_K):
        rank = jnp.sum(jnp.where(onehots[k], before, 0.0), axis=1, keepdims=True).astype(I32)
        meta = jnp.where(lane == k, idxs[k], meta)
        meta = jnp.where(lane == TOP_K + k, rank, meta)
        gates = jnp.where(lane == k, es[k] / den, gates)
    meta_ref[...] = meta
    gate_ref[...] = gates
    carry_sc[...] = carry_sc[...] + jnp.sum(cnt, axis=0, keepdims=True)
    cnt_ref[...] = jnp.broadcast_to(carry_sc[...], cnt_ref.shape).astype(I32)


def _router(x2, mod4, rw, rb, seq):
    n, d = x2.shape
    tm = TOK_TILE
    per_b = seq // tm
    rw_p = jnp.zeros((d, LANES), F32).at[:, :N_EXPERTS].set(rw)
    rb_p = jnp.full((1, LANES), NEG, F32).at[0, :N_EXPERTS].set(rb)
    return pl.pallas_call(
        _router_kernel,
        grid=(n // tm,),
        in_specs=[pl.BlockSpec((tm, d), lambda i: (i, 0)),
                  pl.BlockSpec((1, 1, 1, d), lambda i: (i // per_b, 4, 0, 0)),
                  pl.BlockSpec((1, 1, 1, d), lambda i: (i // per_b, 3, 0, 0)),
                  pl.BlockSpec((d, LANES), lambda i: (0, 0)),
                  pl.BlockSpec((1, LANES), lambda i: (0, 0))],
        out_specs=[pl.BlockSpec((tm * ROW_CHUNKS, LANES), lambda i: (i, 0)),
                   pl.BlockSpec((tm, LANES), lambda i: (i, 0)),
                   pl.BlockSpec((tm, LANES), lambda i: (i, 0)),
                   pl.BlockSpec((SUBLANES, LANES), lambda i: (0, 0))],
        out_shape=[jax.ShapeDtypeStruct((n * ROW_CHUNKS, LANES), F32),
                   jax.ShapeDtypeStruct((n, LANES), I32),
                   jax.ShapeDtypeStruct((n, LANES), F32),
                   jax.ShapeDtypeStruct((SUBLANES, LANES), I32)],
        scratch_shapes=[pltpu.VMEM((1, LANES), F32)],
        compiler_params=_cparams(("arbitrary",)),
        name="router",
    )(x2, mod4, mod4, rw_p, rb_p)


def _row_slice(ref, r):
    return ref.at[pl.ds(pl.multiple_of(r * ROW_CHUNKS, ROW_CHUNKS), ROW_CHUNKS)]


def _wait_rows(src_like, dst_like, sem, n_rows):
    span = pl.ds(0, n_rows * ROW_CHUNKS)
    pltpu.make_async_copy(src_like.at[span], dst_like.at[span], sem).wait()


def _dispatch_kernel(dest_ref, h_ref, xs_in_hbm, xs_hbm, sem):
    del xs_in_hbm
    per_row = LANES // TOP_K

    def issue(t, c):
        src = _row_slice(h_ref, t)
        for k in range(TOP_K):
            d = dest_ref[t // per_row, (t % per_row) * TOP_K + k]
            pltpu.make_async_copy(src, _row_slice(xs_hbm, d), sem).start()
        return c

    lax.fori_loop(0, TOK_TILE, issue, 0)
    _wait_rows(xs_hbm, xs_hbm, sem, TOK_TILE * TOP_K)


def _dispatch(h_rows, dest2, xs_zero, n):
    return pl.pallas_call(
        _dispatch_kernel,
        grid=(n // TOK_TILE,),
        in_specs=[pl.BlockSpec((SUBLANES, LANES), lambda i: (i, 0), memory_space=pltpu.SMEM),
                  pl.BlockSpec((TOK_TILE * ROW_CHUNKS, LANES), lambda i: (i, 0)),
                  pl.BlockSpec(memory_space=pl.ANY)],
        out_specs=pl.BlockSpec(memory_space=pl.ANY),
        out_shape=jax.ShapeDtypeStruct(xs_zero.shape, F32),
        scratch_shapes=[pltpu.SemaphoreType.DMA],
        input_output_aliases={2: 0},
        compiler_params=_cparams(("arbitrary",)),
        name="moe_dispatch",
    )(dest2, h_rows, xs_zero)


def _expert_kernel(be_ref, first_ref, xs_ref, wgu_ref, bgu_ref, wd_ref, bd_ref, ys_ref, x_sc, wgu_sc, wd_sc):
    del be_ref
    g = pl.program_id(0)
    rows = x_sc.shape[0]
    grp = 2 * LANES

    @pl.when(first_ref[g] == 1)
    def _():
        r = lax.broadcasted_iota(I32, (grp, grp), 0)
        c = lax.broadcasted_iota(I32, (grp, grp), 1)
        src = jnp.where(c < LANES, 2 * c, 2 * (c - LANES) + 1)
        sel = jnp.where(r == src, 1.0, 0.0).astype(BF16)
        for j in range(wgu_sc.shape[1] // grp):
            blk = wgu_ref[0, :, j * grp:(j + 1) * grp].astype(BF16)
            wgu_sc[:, j * grp:(j + 1) * grp] = jnp.dot(blk, sel, preferred_element_type=F32).astype(BF16)
        wd_sc[...] = wd_ref[0].astype(BF16)

    for c in range(ROW_CHUNKS):
        x_sc[:, c * LANES:(c + 1) * LANES] = xs_ref[pl.ds(c, rows, stride=ROW_CHUNKS), :].astype(BF16)
    gu = jnp.dot(x_sc[...], wgu_sc[...], preferred_element_type=F32) + bgu_ref[0]
    acts = []
    for j in range(gu.shape[1] // grp):
        glu = jnp.minimum(gu[:, j * grp:j * grp + LANES], SWIGLU_LIMIT)
        lin = jnp.clip(gu[:, j * grp + LANES:(j + 1) * grp], -SWIGLU_LIMIT, SWIGLU_LIMIT)
        acts.append((glu * _sigmoid(SWIGLU_ALPHA * glu) * (lin + 1.0)).astype(BF16))
    act = jnp.concatenate(acts, axis=1)
    y = jnp.dot(act, wd_sc[...], preferred_element_type=F32) + bd_ref[0]
    for c in range(ROW_CHUNKS):
        ys_ref[pl.ds(c, rows, stride=ROW_CHUNKS), :] = y[:, c * LANES:(c + 1) * LANES]


def _experts(xs, block_exp, first, w_gu, b_gu_grouped, w_dn, b_dn):
    d = D_MODEL
    n_blocks = block_exp.shape[0]
    rb = EXPERT_ROWS
    return pl.pallas_call(
        _expert_kernel,
        grid_spec=pltpu.PrefetchScalarGridSpec(
            num_scalar_prefetch=2,
            grid=(n_blocks,),
            in_specs=[pl.BlockSpec((rb * ROW_CHUNKS, LANES), lambda g, be, fi: (g, 0)),
                      pl.BlockSpec((1, d, 2 * d), lambda g, be, fi: (be[g], 0, 0)),
                      pl.BlockSpec((1, 1, 2 * d), lambda g, be, fi: (be[g], 0, 0)),
                      pl.BlockSpec((1, d, d), lambda g, be, fi: (be[g], 0, 0)),
                      pl.BlockSpec((1, 1, d), lambda g, be, fi: (be[g], 0, 0))],
            out_specs=pl.BlockSpec((rb * ROW_CHUNKS, LANES), lambda g, be, fi: (g, 0)),
            scratch_shapes=[pltpu.VMEM((rb, d), BF16),
                            pltpu.VMEM((d, 2 * d), BF16),
                            pltpu.VMEM((d, d), BF16)]),
        out_shape=jax.ShapeDtypeStruct(xs.shape, F32),
        compiler_params=_cparams(("arbitrary",)),
        name="moe_experts",
    )(block_exp, first, xs, w_gu, b_gu_grouped, w_dn, b_dn)


def _combine_kernel(dest_ref, dest_next_ref, gate_ref, x_ref, g2_ref, lng_ref, lnb_ref, ys_hbm, o_ref,
                    buf0, buf1, sem):
    tm = TOK_TILE
    per_row = LANES // TOP_K
    i = pl.program_id(0)
    n_steps = pl.num_programs(0)
    bufs = (buf0, buf1)

    def issue(d_ref, slot):
        def body(t, c):
            for k in range(TOP_K):
                d = d_ref[t // per_row, (t % per_row) * TOP_K + k]
                pltpu.make_async_copy(_row_slice(ys_hbm, d), _row_slice(bufs[slot], k * tm + t),
                                      sem.at[slot]).start()
            return c
        lax.fori_loop(0, tm, body, 0)

    def finish(slot):
        buf = bufs[slot]
        _wait_rows(ys_hbm, buf, sem.at[slot], tm * TOP_K)
        gates = gate_ref[...]
        chunks = []
        for c in range(ROW_CHUNKS):
            yc = jnp.zeros((tm, LANES), F32)
            for k in range(TOP_K):
                yc = yc + gates[:, k:k + 1] * buf[pl.ds(k * tm * ROW_CHUNKS + c, tm, stride=ROW_CHUNKS), :]
            chunks.append(yc)
        y = jnp.concatenate(chunks, axis=1)
        z = DEEPNORM_ALPHA * x_ref[...] + g2_ref[0, 0] * y
        o_ref[...] = _ln(z) * lng_ref[...] + lnb_ref[...]

    @pl.when(i == 0)
    def _():
        issue(dest_ref, 0)

    for slot in range(2):
        @pl.when(i % 2 == slot)
        def _(slot=slot):
            @pl.when(i + 1 < n_steps)
            def _():
                issue(dest_next_ref, 1 - slot)
            finish(slot)


def _combine(ys, dest2, gates, x2, mod4, lng, lnb, seq):
    n, d = x2.shape
    tm = TOK_TILE
    per_b = seq // tm
    last = n // tm - 1
    return pl.pallas_call(
        _combine_kernel,
        grid=(n // tm,),
        in_specs=[pl.BlockSpec((SUBLANES, LANES), lambda i: (i, 0), memory_space=pltpu.SMEM),
                  pl.BlockSpec((SUBLANES, LANES), lambda i: (jnp.minimum(i + 1, last), 0),
                               memory_space=pltpu.SMEM),
                  pl.BlockSpec((tm, LANES), lambda i: (i, 0)),
                  pl.BlockSpec((tm, d), lambda i: (i, 0)),
                  pl.BlockSpec((1, 1, 1, d), lambda i: (i // per_b, 5, 0, 0)),
                  pl.BlockSpec((1, d), lambda i: (0, 0)),
                  pl.BlockSpec((1, d), lambda i: (0, 0)),
                  pl.BlockSpec(memory_space=pl.ANY)],
        out_specs=pl.BlockSpec((tm, d), lambda i: (i, 0)),
        out_shape=jax.ShapeDtypeStruct((n, d), F32),
        scratch_shapes=[pltpu.VMEM((TOP_K * tm * ROW_CHUNKS, LANES), F32),
                        pltpu.VMEM((TOP_K * tm * ROW_CHUNKS, LANES), F32),
                        pltpu.SemaphoreType.DMA((2,))],
        compiler_params=_cparams(("arbitrary",)),
        name="moe_combine",
    )(dest2, dest2, gates, x2, mod4, lng, lnb, ys)


def _moe(x2, mod4, rw, rb, w_gu, b_gu, w_dn, b_dn, lng, lnb, seq):
    n, d = x2.shape
    h_rows, meta, gates, cnt = _router(x2, mod4, rw, rb, seq)
    counts = cnt[0, :N_EXPERTS]
    padded = -(-counts // EXPERT_ROWS) * EXPERT_ROWS
    pend = jnp.cumsum(padded)
    pstart = pend - padded
    n_blocks = -(-(n * TOP_K + N_EXPERTS * (EXPERT_ROWS - 1)) // EXPERT_ROWS)
    block_row0 = jnp.arange(n_blocks, dtype=I32) * EXPERT_ROWS
    block_exp = jnp.minimum(jnp.sum((pend[None, :] <= block_row0[:, None]).astype(I32), axis=1), N_EXPERTS - 1)
    first = jnp.concatenate([jnp.ones((1,), I32), (block_exp[1:] != block_exp[:-1]).astype(I32)])
    dest = jnp.take(pstart, meta[:, :TOP_K]) + meta[:, TOP_K:2 * TOP_K]
    dest2 = dest.reshape(n * TOP_K // LANES, LANES).astype(I32)
    rows = n_blocks * EXPERT_ROWS
    xs = _dispatch(h_rows, dest2, jnp.zeros((rows * ROW_CHUNKS, LANES), F32), n)
    e = b_gu.shape[0]
    b_grouped = b_gu.reshape(e, d // LANES, LANES, 2).transpose(0, 1, 3, 2).reshape(e, 1, 2 * d)
    ys = _experts(xs, block_exp, first, w_gu, b_grouped, w_dn, b_dn[:, None, :])
    return _combine(ys, dest2, gates, x2, mod4, lng.reshape(1, d), lnb.reshape(1, d), seq)


def _permute_w_in(w):
    d = w.shape[0]
    off = {}
    o = 0
    for name, wd in (("a_q", 256), ("a_k", 256), ("a_v", 256), ("b_q", 384), ("b_kc", 128), ("b_vc", 128),
                     ("b_ks", 128), ("b_vs", 128), ("b_kw", 128), ("b_vw", 128), ("b_gate", 18),
                     ("c_q", 384), ("c_k", 384), ("c_v", 384), ("merge_gate", 3 * D_MODEL)):
        off[name] = (o, o + wd)
        o += wd
    col = lambda name: w[:, off[name][0]:off[name][1]]
    scale = HEAD_DIM ** -0.5
    b_q = col("b_q").reshape(d, B_KV_GROUPS, B_REP, HEAD_DIM).transpose(0, 2, 1, 3).reshape(d, B_HEADS * HEAD_DIM)
    gate_pad = jnp.zeros((d, LANES - B_HEADS * 3), w.dtype)
    parts = [col("merge_gate"),
             col("a_q") * scale, col("a_k"), b_q * scale, col("b_kc"), col("b_ks"), col("b_kw"),
             col("c_q") * scale, col("c_k"),
             col("a_v"), col("b_vc"), col("b_vs"), col("b_vw"), col("c_v"),
             col("b_gate"), gate_pad]
    return jnp.concatenate(parts, axis=1).astype(BF16)


def _dilate(arr, tile, batch, seq, dil):
    a = arr[:, tile * LANES:(tile + 1) * LANES].reshape(batch, seq // dil, dil, LANES)
    return a.transpose(0, 2, 1, 3).reshape(batch * seq, LANES)


def _undilate(arr, batch, seq, dil):
    a = arr.reshape(batch, dil, seq // dil, LANES)
    return a.transpose(0, 2, 1, 3).reshape(batch * seq, LANES)


def _mixer_layer(x2, mod4, cos_t, sin_t, batch, seq, w_in, cmp_w1_k, cmp_w2_k, cmp_pe_k,
                 cmp_w1_v, cmp_w2_v, cmp_pe_v, w_a, w_b, w_c, w_out, lng, lnb):
    d = D_MODEL
    proj = _inproj(x2, mod4, _permute_w_in(w_in), cos_t, sin_t, seq)

    bias_a = _moba_select(proj, _kmean(proj, batch, seq), batch, seq)
    o_a = _bb_attn(proj, bias_a, batch=batch, seq=seq, n_pairs=A_HEADS // 2,
                   q_tile=lambda p: T_AQ + p, k_tile=lambda p: T_AK + p, v_tile=lambda p: T_AV + p,
                   bias_tile=lambda p: p, blk=MOBA_BLOCK)

    kc = _compress(proj[:, T_BKC * LANES:(T_BKC + 1) * LANES], cmp_w1_k, cmp_w2_k, cmp_pe_k, batch, seq)
    vc = _compress(proj[:, T_BVC * LANES:(T_BVC + 1) * LANES], cmp_w1_v, cmp_w2_v, cmp_pe_v, batch, seq)
    o_cmp, bias_b = _nsa_cmp(proj, kc, vc, batch, seq)
    o_slc = _bb_attn(proj, bias_b, batch=batch, seq=seq, n_pairs=B_REP,
                     q_tile=lambda p: T_BQ + p, k_tile=lambda p: T_BKS, v_tile=lambda p: T_BVS,
                     bias_tile=lambda p: 0, blk=SLC_BLOCK)
    o_win, _ = _band_attn(proj, proj, proj, batch=batch, seq=seq, n_pairs=B_REP,
                          q_tile=lambda p: T_BQ + p, k_tile=lambda p: T_BKW, v_tile=lambda p: T_BVW,
                          window=NSA_WINDOW, t=256)

    oc, lse = [], []
    for gi, (window, dil) in enumerate(DILATED_PAIRS):
        wlen = window // dil + 1
        if dil == 1:
            o, l = _band_attn(proj, proj, proj, batch=batch, seq=seq, n_pairs=1,
                              q_tile=lambda p: T_CQ + gi, k_tile=lambda p: T_CK + gi,
                              v_tile=lambda p: T_CV + gi, window=wlen, t=256)
        else:
            qd = _dilate(proj, T_CQ + gi, batch, seq, dil)
            kd = _dilate(proj, T_CK + gi, batch, seq, dil)
            vd = _dilate(proj, T_CV + gi, batch, seq, dil)
            o, l = _band_attn(qd, kd, vd, batch=batch * dil, seq=seq // dil, n_pairs=1,
                              q_tile=lambda p: 0, k_tile=lambda p: 0, v_tile=lambda p: 0,
                              window=wlen, t=256)
            o, l = _undilate(o, batch, seq, dil), _undilate(l, batch, seq, dil)
        oc.append(o)
        lse.append(l)

    w_b_perm = w_b.reshape(B_KV_GROUPS, B_REP, HEAD_DIM, d).transpose(1, 0, 2, 3).reshape(B_HEADS * HEAD_DIM, d)
    return _merge(o_a, o_cmp, o_slc, o_win, proj, oc, lse, x2, mod4,
                  w_a.astype(BF16), w_b_perm.astype(BF16), w_c.astype(BF16), w_out.astype(BF16),
                  lng.reshape(1, d), lnb.reshape(1, d), seq)


def _rope_tables(positions):
    inv = ROPE_THETA ** (-jnp.arange(0, HEAD_DIM, 2, dtype=F32) / HEAD_DIM)
    ang = positions.astype(F32).reshape(-1, 1) * inv[None, :]
    cos, sin = jnp.cos(ang), jnp.sin(ang)
    return jnp.tile(cos, (1, 4)), jnp.tile(jnp.concatenate([-sin, sin], axis=1), (1, 2))


def kernel(x, c, positions, w_ada, b_ada, w_in, cmp_w1_k, cmp_w2_k, cmp_pe_k, cmp_w1_v, cmp_w2_v, cmp_pe_v, w_branch_a, w_branch_b, w_branch_c, w_out, ln1_g, ln1_b, router_w, router_b, w_gate_up, b_gate_up, w_down, b_down, ln2_g, ln2_b):
    batch, seq, d = x.shape
    depth = w_in.shape[0]
    cos_t, sin_t = _rope_tables(positions)
    mod = _ada(c, w_ada, b_ada)
    x2 = x.reshape(batch * seq, d)
    for l in range(depth):
        mod4 = mod[l].reshape(batch, N_ADA, 1, d)
        x2 = _mixer_layer(x2, mod4, cos_t, sin_t, batch, seq, w_in[l], cmp_w1_k[l], cmp_w2_k[l], cmp_pe_k[l],
                          cmp_w1_v[l], cmp_w2_v[l], cmp_pe_v[l], w_branch_a[l], w_branch_b[l],
                          w_branch_c[l], w_out[l], ln1_g[l], ln1_b[l])
        x2 = _moe(x2, mod4, router_w[l], router_b[l], w_gate_up[l], b_gate_up[l], w_down[l], b_down[l],
                  ln2_g[l], ln2_b[l], seq)
    return x2.reshape(batch, seq, d)
```

```python
import functools

import numpy as np
import jax
import jax.numpy as jnp
from jax import lax
from jax.experimental import pallas as pl
from jax.experimental.pallas import tpu as pltpu

F32 = jnp.float32
BF16 = jnp.bfloat16
I32 = jnp.int32
HIGHEST = lax.Precision.HIGHEST

D_MODEL = 1024
DEPTH = 2
HEAD_DIM = 64
ROPE_THETA = 10000.0
LN_EPS = 1e-5
DEEPNORM_ALPHA = (2 * DEPTH) ** 0.25
N_ADA = 6
A_HEADS = 4
MOBA_BLOCK = 256
MOBA_TOPK = 3
B_HEADS = 6
B_KV_GROUPS = 2
B_REP = 3
CMP_BLOCK = 32
CMP_STRIDE = 16
CMP_HIDDEN = 128
SLC_BLOCK = 64
SLC_TOPK = 16
NSA_WINDOW = 512
FORCE_SCORE = 1e6
DILATED_PAIRS = ((128, 1), (512, 4), (2048, 16))
N_EXPERTS = 32
TOP_K = 4
SWIGLU_ALPHA = 1.702
SWIGLU_LIMIT = 7.0

LANES = 128
SUBLANES = 8
HALF = LANES // 2
NEG = -1e30
ROW_CHUNKS = D_MODEL // LANES
VMEM_LIMIT = 56 * 1024 * 1024

T_MG = 0
T_ROPE0, T_ROPE1 = 24, 40
T_AQ, T_AK, T_BQ, T_BKC, T_BKS, T_BKW, T_CQ, T_CK = 24, 26, 28, 31, 32, 33, 34, 37
T_AV, T_BVC, T_BVS, T_BVW, T_CV, T_BG = 40, 42, 43, 44, 45, 48
N_PROJ_TILES = 49
PROJ_W = N_PROJ_TILES * LANES

EXPERT_ROWS = 256
TOK_TILE = 256


def _cparams(sem, **kw):
    return pltpu.CompilerParams(dimension_semantics=sem, vmem_limit_bytes=VMEM_LIMIT, **kw)


def _lane_iota(shape=(1, LANES)):
    return lax.broadcasted_iota(I32, shape, len(shape) - 1)


def _sigmoid(x):
    return 1.0 / (1.0 + jnp.exp(-x))


def _ln(x):
    mu = jnp.mean(x, axis=-1, keepdims=True)
    xc = x - mu
    var = jnp.mean(xc * xc, axis=-1, keepdims=True)
    return xc * lax.rsqrt(var + LN_EPS)


def _dot_nt(a, b, precision=None):
    return lax.dot_general(a, b, (((1,), (1,)), ((), ())), precision=precision,
                           preferred_element_type=F32)


def _ada_kernel(c_ref, w_ref, b_ref, o_ref):
    c = c_ref[...]
    cond = c * _sigmoid(c)
    o_ref[0] = jnp.dot(cond, w_ref[0], precision=HIGHEST, preferred_element_type=F32) + b_ref[0]


def _ada(c, w_ada, b_ada):
    depth, d, n = w_ada.shape
    b = c.shape[0]
    tn = 1536
    return pl.pallas_call(
        _ada_kernel,
        grid=(depth, n // tn),
        in_specs=[pl.BlockSpec((b, d), lambda l, j: (0, 0)),
                  pl.BlockSpec((1, d, tn), lambda l, j: (l, 0, j)),
                  pl.BlockSpec((1, 1, tn), lambda l, j: (l, 0, j))],
        out_specs=pl.BlockSpec((1, b, tn), lambda l, j: (l, 0, j)),
        out_shape=jax.ShapeDtypeStruct((depth, b, n), F32),
        compiler_params=_cparams(("parallel", "parallel")),
        name="ada",
    )(c, w_ada, b_ada.reshape(depth, 1, n))


def _inproj_kernel(x_ref, sc_ref, sh_ref, w_ref, cos_ref, sin_ref, o_ref, *, chunks):
    h = _ln(x_ref[...]) * (1.0 + sc_ref[0, 0]) + sh_ref[0, 0]
    hb = h.astype(BF16)
    first_half = (_lane_iota() & (HEAD_DIM - 1)) < (HEAD_DIM // 2)
    for c0, cw, rope in chunks:
        acc = jnp.dot(hb, w_ref[:, c0:c0 + cw], preferred_element_type=F32)
        if rope:
            cos = cos_ref[...]
            sin = sin_ref[...]
            for t in range(cw // LANES):
                a = acc[:, t * LANES:(t + 1) * LANES]
                rot = jnp.where(first_half, pltpu.roll(a, LANES - HEAD_DIM // 2, 1),
                                pltpu.roll(a, HEAD_DIM // 2, 1))
                o_ref[:, c0 + t * LANES:c0 + (t + 1) * LANES] = (a * cos + rot * sin).astype(o_ref.dtype)
        else:
            o_ref[:, c0:c0 + cw] = acc.astype(o_ref.dtype)


def _inproj_chunks():
    chunks = []
    for lo, hi, rope in ((0, T_ROPE0, False), (T_ROPE0, T_ROPE1, True), (T_ROPE1, N_PROJ_TILES, False)):
        c = lo * LANES
        while c < hi * LANES:
            cw = min(512, hi * LANES - c)
            chunks.append((c, cw, rope))
            c += cw
    return tuple(chunks)


def _inproj(x2, mod4, w_perm, cos_t, sin_t, seq):
    n, d = x2.shape
    tm = 256
    per_b = seq // tm
    return pl.pallas_call(
        functools.partial(_inproj_kernel, chunks=_inproj_chunks()),
        grid=(n // tm,),
        in_specs=[pl.BlockSpec((tm, d), lambda i: (i, 0)),
                  pl.BlockSpec((1, 1, 1, d), lambda i: (i // per_b, 1, 0, 0)),
                  pl.BlockSpec((1, 1, 1, d), lambda i: (i // per_b, 0, 0, 0)),
                  pl.BlockSpec((d, PROJ_W), lambda i: (0, 0)),
                  pl.BlockSpec((tm, LANES), lambda i: (i, 0)),
                  pl.BlockSpec((tm, LANES), lambda i: (i, 0))],
        out_specs=pl.BlockSpec((tm, PROJ_W), lambda i: (i, 0)),
        out_shape=jax.ShapeDtypeStruct((n, PROJ_W), BF16),
        compiler_params=_cparams(("parallel",)),
        name="inproj",
    )(x2, mod4, mod4, w_perm, cos_t, sin_t)


def _bb_attn_kernel(q_ref, k_ref, v_ref, b_ref, o_ref, qa_sc, s_sc, mp_sc, m_sc, acc_sc,
                    *, blk_shift, tq, big):
    i = pl.program_id(2)
    lane = _lane_iota()
    lo = lane < HALF
    q = q_ref[...]
    b = b_ref[...]
    qa_sc[0] = jnp.where(lo, q, b)
    qa_sc[1] = jnp.where(lo, b, q)
    mp_sc[...] = jnp.full(mp_sc.shape, -jnp.inf, F32)
    acc_sc[...] = jnp.zeros(acc_sc.shape, F32)

    def scores(u0, n_units, diagonal_last):
        tk = n_units * tq
        start = pl.multiple_of(u0 * tq, tq)
        k = k_ref[pl.ds(start, tk), :]
        kblk = (start + lax.broadcasted_iota(I32, (tk, 1), 0)) >> blk_shift
        onehot = jnp.where(kblk == (lane & (HALF - 1)), 1.0, 0.0).astype(BF16)
        ka = (jnp.where(lo, k, onehot), jnp.where(lo, onehot, k))
        causal = lax.broadcasted_iota(I32, (1, tq), 1) <= lax.broadcasted_iota(I32, (tq, 1), 0)
        for h in range(2):
            s = _dot_nt(qa_sc[h], ka[h])
            mp = mp_sc[h]
            for u in range(n_units):
                su = s[:, u * tq:(u + 1) * tq]
                if diagonal_last and u == n_units - 1:
                    su = jnp.where(causal, su, NEG)
                s_sc[h, u0 + u] = su
                for c in range(tq // LANES):
                    mp = jnp.maximum(mp, su[:, c * LANES:(c + 1) * LANES])
            mp_sc[h] = mp

    def weighted(u0, n_units):
        tk = n_units * tq
        start = pl.multiple_of(u0 * tq, tq)
        v = v_ref[pl.ds(start, tk), :]
        one = jnp.ones_like(v)
        vs = (jnp.where(lo, v, one), jnp.where(lo, one, v))
        for h in range(2):
            m_row = m_sc[h]
            ps = []
            for u in range(n_units):
                su = s_sc[h, u0 + u]
                for c in range(tq // LANES):
                    ps.append(jnp.exp2(su[:, c * LANES:(c + 1) * LANES] - m_row).astype(BF16))
            acc_sc[h] = acc_sc[h] + jnp.dot(jnp.concatenate(ps, axis=1), vs[h], preferred_element_type=F32)

    n_big = i // big
    n_tail = i - n_big * big + 1

    def run(fn):
        def big_body(t, c):
            fn(t * big, big, False) if fn is scores else fn(t * big, big)
            return c

        lax.fori_loop(0, n_big, big_body, 0)
        for n in range(1, big + 1):
            @pl.when(n_tail == n)
            def _(n=n):
                fn(n_big * big, n, True) if fn is scores else fn(n_big * big, n)

    run(scores)
    for h in range(2):
        m_sc[h] = jnp.broadcast_to(jnp.max(mp_sc[h], axis=1, keepdims=True), (tq, LANES))
    run(weighted)
    outs = [acc_sc[h] / pltpu.roll(acc_sc[h], HALF, 1) for h in range(2)]
    o_ref[...] = jnp.where(lo, outs[0], outs[1]).astype(o_ref.dtype)


def _bb_attn(proj, bias, *, batch, seq, n_pairs, q_tile, k_tile, v_tile, bias_tile, blk):
    t = 256
    nq = seq // t
    n = batch * seq
    kern = functools.partial(_bb_attn_kernel, blk_shift=int(np.log2(blk)), tq=t, big=4)
    return pl.pallas_call(
        kern,
        grid=(batch, n_pairs, nq),
        in_specs=[pl.BlockSpec((t, LANES), lambda b, p, i: (b * nq + i, q_tile(p))),
                  pl.BlockSpec((seq, LANES), lambda b, p, i: (b, k_tile(p))),
                  pl.BlockSpec((seq, LANES), lambda b, p, i: (b, v_tile(p))),
                  pl.BlockSpec((t, LANES), lambda b, p, i: (b * nq + i, bias_tile(p)))],
        out_specs=pl.BlockSpec((t, LANES), lambda b, p, i: (b * nq + i, p)),
        out_shape=jax.ShapeDtypeStruct((n, n_pairs * LANES), BF16),
        scratch_shapes=[pltpu.VMEM((2, t, LANES), BF16),
                        pltpu.VMEM((2, nq, t, t), F32),
                        pltpu.VMEM((2, t, LANES), F32),
                        pltpu.VMEM((2, t, LANES), F32),
                        pltpu.VMEM((2, t, LANES), F32)],
        compiler_params=_cparams(("parallel", "parallel", "parallel")),
        name="bb_attn_%d" % blk,
    )(proj, proj, proj, bias)


def _band_kernel(q_ref, k_ref, v_ref, o_ref, lse_ref, *, tq, kw, window, seq):
    i = pl.program_id(2)
    lane = _lane_iota()
    lo = lane < HALF
    start = pl.multiple_of(jnp.clip(i * tq - (kw - tq), 0, seq - kw), LANES)
    q = q_ref[...]
    k = k_ref[pl.ds(start, kw), :]
    v = v_ref[pl.ds(start, kw), :]
    zero = jnp.zeros_like(q)
    qs = (jnp.where(lo, q, zero), jnp.where(lo, zero, q))
    rel = (i * tq + lax.broadcasted_iota(I32, (tq, 1), 0)) - (start + lax.broadcasted_iota(I32, (1, kw), 1))
    ok = (rel >= 0) & (rel < window)
    outs, lses = [], []
    for h in range(2):
        s = jnp.where(ok, _dot_nt(qs[h], k), NEG)
        m = jnp.max(s, axis=1, keepdims=True)
        p = jnp.exp2(s - m)
        l = jnp.sum(p, axis=1, keepdims=True)
        outs.append(jnp.dot(p.astype(BF16), v, preferred_element_type=F32) / l)
        lses.append(m + jnp.log2(l))
    o_ref[...] = jnp.where(lo, outs[0], outs[1]).astype(o_ref.dtype)
    lse_ref[...] = jnp.where(lo, lses[0], lses[1])


def _band_attn(q_arr, k_arr, v_arr, *, batch, seq, n_pairs, q_tile, k_tile, v_tile, window, t):
    t = min(t, seq)
    nq = seq // t
    kw = min(seq, t + -(-(window - 1) // LANES) * LANES)
    n = batch * seq
    kern = functools.partial(_band_kernel, tq=t, kw=kw, window=window, seq=seq)
    return pl.pallas_call(
        kern,
        grid=(batch, n_pairs, nq),
        in_specs=[pl.BlockSpec((t, LANES), lambda b, p, i: (b * nq + i, q_tile(p))),
                  pl.BlockSpec((seq, LANES), lambda b, p, i: (b, k_tile(p))),
                  pl.BlockSpec((seq, LANES), lambda b, p, i: (b, v_tile(p)))],
        out_specs=[pl.BlockSpec((t, LANES), lambda b, p, i: (b * nq + i, p)),
                   pl.BlockSpec((t, LANES), lambda b, p, i: (b * nq + i, p))],
        out_shape=[jax.ShapeDtypeStruct((n, n_pairs * LANES), BF16),
                   jax.ShapeDtypeStruct((n, n_pairs * LANES), F32)],
        compiler_params=_cparams(("parallel", "parallel", "parallel")),
        name="band_attn_%d" % window,
    )(q_arr, k_arr, v_arr)


def _kmean_kernel(k_ref, o_ref, *, nb):
    s = k_ref.shape[0]
    blk = lax.broadcasted_iota(I32, (nb, s), 1) >> int(np.log2(MOBA_BLOCK))
    avg = jnp.where(blk == lax.broadcasted_iota(I32, (nb, s), 0), 1.0 / MOBA_BLOCK, 0.0).astype(BF16)
    o_ref[0] = jnp.dot(avg, k_ref[...], preferred_element_type=F32)


def _kmean(proj, batch, seq):
    nb = seq // MOBA_BLOCK
    w = A_HEADS * HEAD_DIM
    return pl.pallas_call(
        functools.partial(_kmean_kernel, nb=nb),
        grid=(batch,),
        in_specs=[pl.BlockSpec((seq, w), lambda b: (b, T_AK * LANES // w))],
        out_specs=pl.BlockSpec((1, nb, w), lambda b: (b, 0, 0)),
        out_shape=jax.ShapeDtypeStruct((batch, nb, w), F32),
        compiler_params=_cparams(("parallel",)),
        name="moba_kmean",
    )(proj)


def _rank_desc(g, n_idx, n):
    rank = jnp.zeros(g.shape, I32)
    for m in range(n):
        c = g[m:m + 1, :]
        beats = (c > g) | ((c == g) & (n_idx > m))
        rank = rank + jnp.where(beats, 1, 0)
    return rank


def _moba_sel_kernel(q_ref, km_ref, b_ref, *, nb, n_sel):
    i = pl.program_id(2)
    t = q_ref.shape[0]
    q = q_ref[...].astype(F32)
    km = km_ref[0]
    lo = _lane_iota() < HALF
    zero = jnp.zeros_like(km)
    pad = jnp.zeros((HALF - nb, LANES), F32)
    kmt = jnp.concatenate([jnp.where(lo, zero, km), pad, jnp.where(lo, km, zero), pad], axis=0)
    gt = _dot_nt(kmt, q, precision=HIGHEST)
    n_idx = lax.broadcasted_iota(I32, (nb, 1), 0)
    valid = n_idx < i
    rows = []
    for r0 in (0, HALF):
        g = jnp.where(valid, gt[r0:r0 + nb, :], -jnp.inf)
        rank = _rank_desc(g, n_idx, nb)
        allowed = (valid & (rank < n_sel)) | (n_idx == i)
        rows.append(jnp.where(allowed, 0.0, NEG))
        rows.append(jnp.zeros((HALF - nb, t), F32))
    b_ref[...] = jnp.concatenate(rows, axis=0).T.astype(b_ref.dtype)


def _moba_select(proj, kmean, batch, seq):
    t = MOBA_BLOCK
    nq = seq // t
    nb = seq // MOBA_BLOCK
    n_sel = min(MOBA_TOPK, nb - 1)
    n_pairs = A_HEADS // 2
    return pl.pallas_call(
        functools.partial(_moba_sel_kernel, nb=nb, n_sel=n_sel),
        grid=(batch, n_pairs, nq),
        in_specs=[pl.BlockSpec((t, LANES), lambda b, p, i: (b * nq + i, T_AQ + p)),
                  pl.BlockSpec((1, nb, LANES), lambda b, p, i: (b, 0, p))],
        out_specs=pl.BlockSpec((t, LANES), lambda b, p, i: (b * nq + i, p)),
        out_shape=jax.ShapeDtypeStruct((batch * seq, n_pairs * LANES), BF16),
        compiler_params=_cparams(("parallel", "parallel", "parallel")),
        name="moba_select",
    )(proj, kmean)


def _compress_kernel(x_ref, w1_ref, pe_ref, w1f_ref, w2_ref, o_ref):
    x = x_ref[0]
    nblk = x.shape[0]
    outs = []
    for g in range(B_KV_GROUPS):
        u = jnp.dot(x, w1_ref[g, 0], preferred_element_type=F32)
        v = jnp.dot(x, w1_ref[g, 1], preferred_element_type=F32)
        pe_h = jnp.dot(pe_ref[...], w1f_ref[...], precision=HIGHEST, preferred_element_type=F32)
        hid = u + pltpu.roll(v, nblk - 1, 0) + pe_h[:1]
        hid = hid * _sigmoid(hid)
        outs.append(jnp.dot(hid, w2_ref[g], precision=HIGHEST, preferred_element_type=F32))
    o = outs[0] + outs[1]
    rows = lax.broadcasted_iota(I32, (nblk, 1), 0)
    o_ref[0] = jnp.where(rows < nblk - 1, o, 0.0).astype(o_ref.dtype)


def _compress(xt, w1, w2, pe, batch, seq):
    nblk = seq // CMP_STRIDE
    xg = xt.reshape(batch, nblk, CMP_STRIDE * LANES)
    w1r = w1.reshape(2, CMP_STRIDE, HEAD_DIM, CMP_HIDDEN)
    w1e = jnp.zeros((B_KV_GROUPS, 2, CMP_STRIDE, B_KV_GROUPS, HEAD_DIM, CMP_HIDDEN), F32)
    for g in range(B_KV_GROUPS):
        w1e = w1e.at[g, :, :, g].set(w1r)
    w1e = w1e.reshape(B_KV_GROUPS, 2, CMP_STRIDE * LANES, CMP_HIDDEN).astype(BF16)
    pe_flat = jnp.broadcast_to(pe.reshape(1, CMP_BLOCK * HEAD_DIM), (SUBLANES, CMP_BLOCK * HEAD_DIM))
    w2e = jnp.zeros((B_KV_GROUPS, CMP_HIDDEN, LANES), F32)
    for g in range(B_KV_GROUPS):
        w2e = w2e.at[g, :, g * HEAD_DIM:(g + 1) * HEAD_DIM].set(w2)
    return pl.pallas_call(
        _compress_kernel,
        grid=(batch,),
        in_specs=[pl.BlockSpec((1, nblk, CMP_STRIDE * LANES), lambda b: (b, 0, 0)),
                  pl.BlockSpec(w1e.shape, lambda b: (0, 0, 0, 0)),
                  pl.BlockSpec(pe_flat.shape, lambda b: (0, 0)),
                  pl.BlockSpec(w1.shape, lambda b: (0, 0)),
                  pl.BlockSpec(w2e.shape, lambda b: (0, 0, 0))],
        out_specs=pl.BlockSpec((1, nblk, LANES), lambda b: (b, 0, 0)),
        out_shape=jax.ShapeDtypeStruct((batch, nblk, LANES), BF16),
        compiler_params=_cparams(("parallel",)),
        name="nsa_compress",
    )(xg, w1e, pe_flat, w1, w2e)


def _nsa_cmp_kernel(q0_ref, q1_ref, q2_ref, kc_ref, vc_ref, ov_ref, o_ref, b_ref, *, n_sel, ns):
    i = pl.program_id(1)
    t = q0_ref.shape[0]
    ncp = kc_ref.shape[1]
    lane = _lane_iota()
    lo = lane < HALF
    kc = kc_ref[0]
    vc = vc_ref[0]
    tq = i * t + lax.broadcasted_iota(I32, (t, 1), 0)
    cmp_end = lax.broadcasted_iota(I32, (1, ncp), 1) * CMP_STRIDE + (CMP_BLOCK - 1)
    vis = cmp_end <= tq
    psum = [jnp.zeros((t, ncp), F32) for _ in range(B_KV_GROUPS)]
    for r, q_ref in enumerate((q0_ref, q1_ref, q2_ref)):
        q = q_ref[...]
        zero = jnp.zeros_like(q)
        outs = []
        for g in range(B_KV_GROUPS):
            qg = jnp.where(lo, q, zero) if g == 0 else jnp.where(lo, zero, q)
            s = jnp.where(vis, _dot_nt(qg, kc), NEG)
            m = jnp.max(s, axis=1, keepdims=True)
            m = jnp.where(m > 0.5 * NEG, m, 0.0)
            e = jnp.where(vis, jnp.exp2(s - m), 0.0)
            p = e / jnp.maximum(jnp.sum(e, axis=1, keepdims=True), 1e-30)
            psum[g] = psum[g] + p
            outs.append(jnp.dot(p.astype(BF16), vc, preferred_element_type=F32))
        o_ref[:, r * LANES:(r + 1) * LANES] = jnp.where(lo, outs[0], outs[1]).astype(o_ref.dtype)
    n_idx = lax.broadcasted_iota(I32, (ns, 1), 0)
    tcol = i * t + lax.broadcasted_iota(I32, (1, t), 1)
    qblk = tcol >> int(np.log2(SLC_BLOCK))
    forced = (n_idx == 0) | (n_idx == qblk) | (n_idx == qblk - 1)
    valid = n_idx <= qblk
    rows = []
    for g in (1, 0):
        imp = _dot_nt(ov_ref[...], psum[g], precision=HIGHEST)
        imp = jnp.where(forced, FORCE_SCORE, imp)
        imp = jnp.where(valid, imp, -jnp.inf)
        rank = _rank_desc(imp, n_idx, ns)
        allowed = valid & (rank < n_sel)
        rows.append(jnp.where(allowed, 0.0, NEG))
        if ns < HALF:
            rows.append(jnp.zeros((HALF - ns, t), F32))
    b_ref[...] = jnp.concatenate(rows, axis=0).T.astype(b_ref.dtype)


def _nsa_cmp(proj, kc, vc, batch, seq):
    t = 256
    nq = seq // t
    ns = seq // SLC_BLOCK
    n_sel = min(SLC_TOPK, ns)
    ncp = seq // CMP_STRIDE
    cs = np.arange(ncp)[None, :] * CMP_STRIDE
    ss = np.arange(ns)[:, None] * SLC_BLOCK
    ov = ((cs < ss + SLC_BLOCK) & (cs + CMP_BLOCK > ss)).astype(np.float32)
    ov[:, ncp - 1] = 0.0
    n = batch * seq
    qspec = [pl.BlockSpec((t, LANES), (lambda b, i, r=r: (b * nq + i, T_BQ + r))) for r in range(B_REP)]
    return pl.pallas_call(
        functools.partial(_nsa_cmp_kernel, n_sel=n_sel, ns=ns),
        grid=(batch, nq),
        in_specs=qspec + [pl.BlockSpec((1, ncp, LANES), lambda b, i: (b, 0, 0)),
                          pl.BlockSpec((1, ncp, LANES), lambda b, i: (b, 0, 0)),
                          pl.BlockSpec((ns, ncp), lambda b, i: (0, 0))],
        out_specs=[pl.BlockSpec((t, B_REP * LANES), lambda b, i: (b * nq + i, 0)),
                   pl.BlockSpec((t, LANES), lambda b, i: (b * nq + i, 0))],
        out_shape=[jax.ShapeDtypeStruct((n, B_REP * LANES), BF16),
                   jax.ShapeDtypeStruct((n, LANES), BF16)],
        compiler_params=_cparams(("parallel", "parallel")),
        name="nsa_cmp_select",
    )(proj, proj, proj, kc, vc, jnp.asarray(ov))


def _merge_kernel(oa_ref, ocmp_ref, oslc_ref, owin_ref, bg_ref, oc0_ref, oc1_ref, oc2_ref,
                  l0_ref, l1_ref, l2_ref, mg_ref, x_ref, g1_ref, wa_ref, wb_ref, wc_ref, wo_ref,
                  eg_ref, lng_ref, lnb_ref, o_ref):
    d = D_MODEL
    sg = _sigmoid(bg_ref[...].astype(F32))
    ob = jnp.zeros(ocmp_ref.shape, F32)
    for br, ref in enumerate((ocmp_ref, oslc_ref, owin_ref)):
        gexp = jnp.dot(sg, eg_ref[br], precision=HIGHEST, preferred_element_type=F32)
        ob = ob + gexp * ref[...].astype(F32)
    l0, l1, l2 = l0_ref[...], l1_ref[...], l2_ref[...]
    mx = jnp.maximum(jnp.maximum(l0, l1), l2)
    e0, e1, e2 = jnp.exp2(l0 - mx), jnp.exp2(l1 - mx), jnp.exp2(l2 - mx)
    den = e0 + e1 + e2
    oc = ((e0 / den) * oc0_ref[...].astype(F32) + (e1 / den) * oc1_ref[...].astype(F32)
          + (e2 / den) * oc2_ref[...].astype(F32))
    pa = jnp.dot(oa_ref[...], wa_ref[...], preferred_element_type=F32)
    pb = jnp.dot(ob.astype(BF16), wb_ref[...], preferred_element_type=F32)
    pc = jnp.dot(oc.astype(BF16), wc_ref[...], preferred_element_type=F32)
    merged = (_sigmoid(mg_ref[:, 0:d].astype(F32)) * pa
              + _sigmoid(mg_ref[:, d:2 * d].astype(F32)) * pb
              + _sigmoid(mg_ref[:, 2 * d:3 * d].astype(F32)) * pc)
    y = jnp.dot(merged.astype(BF16), wo_ref[...], preferred_element_type=F32)
    z = DEEPNORM_ALPHA * x_ref[...] + g1_ref[0, 0] * y
    o_ref[...] = _ln(z) * lng_ref[...] + lnb_ref[...]


def _gate_expand():
    eg = np.zeros((3, LANES, B_HEADS * HEAD_DIM), np.float32)
    for g in range(B_KV_GROUPS):
        for r in range(B_REP):
            for br in range(3):
                c0 = (r * B_KV_GROUPS + g) * HEAD_DIM
                eg[br, (g * B_REP + r) * 3 + br, c0:c0 + HEAD_DIM] = 1.0
    return jnp.asarray(eg)


def _merge(o_a, o_cmp, o_slc, o_win, proj, oc, lse, x2, mod4, wa, wb, wc, wo, lng, lnb, seq):
    n, d = x2.shape
    tm = 256
    per_b = seq // tm
    row = lambda w: pl.BlockSpec((tm, w), lambda i: (i, 0))
    full = lambda a: pl.BlockSpec(a.shape, lambda i: (0,) * a.ndim)
    eg = _gate_expand()
    return pl.pallas_call(
        _merge_kernel,
        grid=(n // tm,),
        in_specs=[row(o_a.shape[1]), row(o_cmp.shape[1]), row(o_slc.shape[1]), row(o_win.shape[1]),
                  pl.BlockSpec((tm, LANES), lambda i: (i, T_BG)),
                  row(LANES), row(LANES), row(LANES), row(LANES), row(LANES), row(LANES),
                  pl.BlockSpec((tm, 3 * d), lambda i: (i, T_MG)),
                  row(d),
                  pl.BlockSpec((1, 1, 1, d), lambda i: (i // per_b, 2, 0, 0)),
                  full(wa), full(wb), full(wc), full(wo), full(eg), full(lng), full(lnb)],
        out_specs=row(d),
        out_shape=jax.ShapeDtypeStruct((n, d), F32),
        compiler_params=_cparams(("parallel",)),
        name="merge_out",
    )(o_a, o_cmp, o_slc, o_win, proj, oc[0], oc[1], oc[2], lse[0], lse[1], lse[2],
      proj, x2, mod4, wa, wb, wc, wo, eg, lng, lnb)


def _router_kernel(x_ref, sc_ref, sh_ref, rw_ref, rb_ref, h_ref, meta_ref, gate_ref, cnt_ref, carry_sc):
    i = pl.program_id(0)
    tm = x_ref.shape[0]

    @pl.when(i == 0)
    def _():
        carry_sc[...] = jnp.zeros(carry_sc.shape, F32)

    h = _ln(x_ref[...]) * (1.0 + sc_ref[0, 0]) + sh_ref[0, 0]
    for c in range(ROW_CHUNKS):
        h_ref[pl.ds(c, tm, stride=ROW_CHUNKS), :] = h[:, c * LANES:(c + 1) * LANES]
    lg = jnp.dot(h, rw_ref[...], precision=HIGHEST, preferred_element_type=F32) + rb_ref[...]
    lane = _lane_iota()
    lane_f = lane.astype(F32)
    onehots, vals, idxs = [], [], []
    for _ in range(TOP_K):
        m = jnp.max(lg, axis=1, keepdims=True)
        idx = jnp.min(jnp.where(lg == m, lane_f, float(LANES)), axis=1, keepdims=True).astype(I32)
        oh = lane == idx
        onehots.append(oh)
        vals.append(m)
        idxs.append(idx)
        lg = jnp.where(oh, -jnp.inf, lg)
    es = [jnp.exp(v - vals[0]) for v in vals]
    den = es[0] + es[1] + es[2] + es[3]
    cnt = jnp.zeros((tm, LANES), F32)
    for oh in onehots:
        cnt = cnt + jnp.where(oh, 1.0, 0.0)
    tri = jnp.where(lax.broadcasted_iota(I32, (tm, tm), 0) > lax.broadcasted_iota(I32, (tm, tm), 1), 1.0, 0.0)
    before = jnp.dot(tri.astype(BF16), cnt.astype(BF16), preferred_element_type=F32) + carry_sc[...]
    meta = jnp.zeros((tm, LANES), I32)
    gates = jnp.zeros((tm, LANES), F32)
    for k in range(TOP_K):
        rank = jnp.sum(jnp.where(onehots[k], before, 0.0), axis=1, keepdims=True).astype(I32)
        meta = jnp.where(lane == k, idxs[k], meta)
        meta = jnp.where(lane == TOP_K + k, rank, meta)
        gates = jnp.where(lane == k, es[k] / den, gates)
    meta_ref[...] = meta
    gate_ref[...] = gates
    carry_sc[...] = carry_sc[...] + jnp.sum(cnt, axis=0, keepdims=True)
    cnt_ref[...] = jnp.broadcast_to(carry_sc[...], cnt_ref.shape).astype(I32)


def _router(x2, mod4, rw, rb, seq):
    n, d = x2.shape
    tm = TOK_TILE
    per_b = seq // tm
    rw_p = jnp.zeros((d, LANES), F32).at[:, :N_EXPERTS].set(rw)
    rb_p = jnp.full((1, LANES), NEG, F32).at[0, :N_EXPERTS].set(rb)
    return pl.pallas_call(
        _router_kernel,
        grid=(n // tm,),
        in_specs=[pl.BlockSpec((tm, d), lambda i: (i, 0)),
                  pl.BlockSpec((1, 1, 1, d), lambda i: (i // per_b, 4, 0, 0)),
                  pl.BlockSpec((1, 1, 1, d), lambda i: (i // per_b, 3, 0, 0)),
                  pl.BlockSpec((d, LANES), lambda i: (0, 0)),
                  pl.BlockSpec((1, LANES), lambda i: (0, 0))],
        out_specs=[pl.BlockSpec((tm * ROW_CHUNKS, LANES), lambda i: (i, 0)),
                   pl.BlockSpec((tm, LANES), lambda i: (i, 0)),
                   pl.BlockSpec((tm, LANES), lambda i: (i, 0)),
                   pl.BlockSpec((SUBLANES, LANES), lambda i: (0, 0))],
        out_shape=[jax.ShapeDtypeStruct((n * ROW_CHUNKS, LANES), F32),
                   jax.ShapeDtypeStruct((n, LANES), I32),
                   jax.ShapeDtypeStruct((n, LANES), F32),
                   jax.ShapeDtypeStruct((SUBLANES, LANES), I32)],
        scratch_shapes=[pltpu.VMEM((1, LANES), F32)],
        compiler_params=_cparams(("arbitrary",)),
        name="router",
    )(x2, mod4, mod4, rw_p, rb_p)


def _row_slice(ref, r):
    return ref.at[pl.ds(pl.multiple_of(r * ROW_CHUNKS, ROW_CHUNKS), ROW_CHUNKS)]


def _wait_rows(src_like, dst_like, sem, n_rows):
    span = pl.ds(0, n_rows * ROW_CHUNKS)
    pltpu.make_async_copy(src_like.at[span], dst_like.at[span], sem).wait()


def _dispatch_kernel(dest_ref, h_ref, xs_in_hbm, xs_hbm, sem):
    del xs_in_hbm
    per_row = LANES // TOP_K

    def issue(t, c):
        src = _row_slice(h_ref, t)
        for k in range(TOP_K):
            d = dest_ref[t // per_row, (t % per_row) * TOP_K + k]
            pltpu.make_async_copy(src, _row_slice(xs_hbm, d), sem).start()
        return c

    lax.fori_loop(0, TOK_TILE, issue, 0)
    _wait_rows(xs_hbm, xs_hbm, sem, TOK_TILE * TOP_K)


def _dispatch(h_rows, dest2, xs_zero, n):
    return pl.pallas_call(
        _dispatch_kernel,
        grid=(n // TOK_TILE,),
        in_specs=[pl.BlockSpec((SUBLANES, LANES), lambda i: (i, 0), memory_space=pltpu.SMEM),
                  pl.BlockSpec((TOK_TILE * ROW_CHUNKS, LANES), lambda i: (i, 0)),
                  pl.BlockSpec(memory_space=pl.ANY)],
        out_specs=pl.BlockSpec(memory_space=pl.ANY),
        out_shape=jax.ShapeDtypeStruct(xs_zero.shape, F32),
        scratch_shapes=[pltpu.SemaphoreType.DMA],
        input_output_aliases={2: 0},
        compiler_params=_cparams(("arbitrary",)),
        name="moe_dispatch",
    )(dest2, h_rows, xs_zero)


def _expert_kernel(be_ref, first_ref, used_ref, xs_ref, wgu_ref, bgu_ref, wd_ref, bd_ref, ys_ref,
                   x_sc, wgu_sc, wd_sc):
    del be_ref
    g = pl.program_id(0)

    @pl.when(g < used_ref[0])
    def _():
        _expert_block(g, first_ref, xs_ref, wgu_ref, bgu_ref, wd_ref, bd_ref, ys_ref, x_sc, wgu_sc, wd_sc)

    @pl.when(g >= used_ref[0])
    def _():
        ys_ref[...] = jnp.zeros(ys_ref.shape, ys_ref.dtype)


def _expert_block(g, first_ref, xs_ref, wgu_ref, bgu_ref, wd_ref, bd_ref, ys_ref, x_sc, wgu_sc, wd_sc):
    rows = x_sc.shape[0]
    grp = 2 * LANES

    @pl.when(first_ref[g] == 1)
    def _():
        r = lax.broadcasted_iota(I32, (grp, grp), 0)
        c = lax.broadcasted_iota(I32, (grp, grp), 1)
        src = jnp.where(c < LANES, 2 * c, 2 * (c - LANES) + 1)
        sel = jnp.where(r == src, 1.0, 0.0).astype(BF16)
        for j in range(wgu_sc.shape[1] // grp):
            blk = wgu_ref[0, :, j * grp:(j + 1) * grp].astype(BF16)
            wgu_sc[:, j * grp:(j + 1) * grp] = jnp.dot(blk, sel, preferred_element_type=F32).astype(BF16)
        wd_sc[...] = wd_ref[0].astype(BF16)

    for c in range(ROW_CHUNKS):
        x_sc[:, c * LANES:(c + 1) * LANES] = xs_ref[pl.ds(c, rows, stride=ROW_CHUNKS), :].astype(BF16)
    gu = jnp.dot(x_sc[...], wgu_sc[...], preferred_element_type=F32) + bgu_ref[0]
    acts = []
    for j in range(gu.shape[1] // grp):
        glu = jnp.minimum(gu[:, j * grp:j * grp + LANES], SWIGLU_LIMIT)
        lin = jnp.clip(gu[:, j * grp + LANES:(j + 1) * grp], -SWIGLU_LIMIT, SWIGLU_LIMIT)
        acts.append((glu * _sigmoid(SWIGLU_ALPHA * glu) * (lin + 1.0)).astype(BF16))
    act = jnp.concatenate(acts, axis=1)
    y = jnp.dot(act, wd_sc[...], preferred_element_type=F32) + bd_ref[0]
    for c in range(ROW_CHUNKS):
        ys_ref[pl.ds(c, rows, stride=ROW_CHUNKS), :] = y[:, c * LANES:(c + 1) * LANES]


def _experts(xs, block_exp, first, n_used, w_gu, b_gu_grouped, w_dn, b_dn):
    d = D_MODEL
    n_blocks = block_exp.shape[0]
    rb = EXPERT_ROWS
    blk = lambda g, be, fi, nu: (jnp.minimum(g, nu[0] - 1), 0)
    exp3 = lambda g, be, fi, nu: (be[jnp.minimum(g, nu[0] - 1)], 0, 0)
    return pl.pallas_call(
        _expert_kernel,
        grid_spec=pltpu.PrefetchScalarGridSpec(
            num_scalar_prefetch=3,
            grid=(n_blocks,),
            in_specs=[pl.BlockSpec((rb * ROW_CHUNKS, LANES), blk),
                      pl.BlockSpec((1, d, 2 * d), exp3),
                      pl.BlockSpec((1, 1, 2 * d), exp3),
                      pl.BlockSpec((1, d, d), exp3),
                      pl.BlockSpec((1, 1, d), exp3)],
            out_specs=pl.BlockSpec((rb * ROW_CHUNKS, LANES), lambda g, be, fi, nu: (g, 0)),
            scratch_shapes=[pltpu.VMEM((rb, d), BF16),
                            pltpu.VMEM((d, 2 * d), BF16),
                            pltpu.VMEM((d, d), BF16)]),
        out_shape=jax.ShapeDtypeStruct(xs.shape, F32),
        compiler_params=_cparams(("arbitrary",)),
        name="moe_experts",
    )(block_exp, first, n_used, xs, w_gu, b_gu_grouped, w_dn, b_dn)


def _combine_kernel(dest_ref, dest_next_ref, gate_ref, x_ref, g2_ref, lng_ref, lnb_ref, ys_hbm, o_ref,
                    buf0, buf1, sem):
    tm = TOK_TILE
    per_row = LANES // TOP_K
    i = pl.program_id(0)
    n_steps = pl.num_programs(0)
    bufs = (buf0, buf1)

    def issue(d_ref, slot):
        def body(t, c):
            for k in range(TOP_K):
                d = d_ref[t // per_row, (t % per_row) * TOP_K + k]
                pltpu.make_async_copy(_row_slice(ys_hbm, d), _row_slice(bufs[slot], k * tm + t),
                                      sem.at[slot]).start()
            return c
        lax.fori_loop(0, tm, body, 0)

    def finish(slot):
        buf = bufs[slot]
        _wait_rows(ys_hbm, buf, sem.at[slot], tm * TOP_K)
        gates = gate_ref[...]
        chunks = []
        for c in range(ROW_CHUNKS):
            yc = jnp.zeros((tm, LANES), F32)
            for k in range(TOP_K):
                yc = yc + gates[:, k:k + 1] * buf[pl.ds(k * tm * ROW_CHUNKS + c, tm, stride=ROW_CHUNKS), :]
            chunks.append(yc)
        y = jnp.concatenate(chunks, axis=1)
        z = DEEPNORM_ALPHA * x_ref[...] + g2_ref[0, 0] * y
        o_ref[...] = _ln(z) * lng_ref[...] + lnb_ref[...]

    @pl.when(i == 0)
    def _():
        issue(dest_ref, 0)

    for slot in range(2):
        @pl.when(i % 2 == slot)
        def _(slot=slot):
            @pl.when(i + 1 < n_steps)
            def _():
                issue(dest_next_ref, 1 - slot)
            finish(slot)


def _combine(ys, dest2, gates, x2, mod4, lng, lnb, seq):
    n, d = x2.shape
    tm = TOK_TILE
    per_b = seq // tm
    last = n // tm - 1
    return pl.pallas_call(
        _combine_kernel,
        grid=(n // tm,),
        in_specs=[pl.BlockSpec((SUBLANES, LANES), lambda i: (i, 0), memory_space=pltpu.SMEM),
                  pl.BlockSpec((SUBLANES, LANES), lambda i: (jnp.minimum(i + 1, last), 0),
                               memory_space=pltpu.SMEM),
                  pl.BlockSpec((tm, LANES), lambda i: (i, 0)),
                  pl.BlockSpec((tm, d), lambda i: (i, 0)),
                  pl.BlockSpec((1, 1, 1, d), lambda i: (i // per_b, 5, 0, 0)),
                  pl.BlockSpec((1, d), lambda i: (0, 0)),
                  pl.BlockSpec((1, d), lambda i: (0, 0)),
                  pl.BlockSpec(memory_space=pl.ANY)],
        out_specs=pl.BlockSpec((tm, d), lambda i: (i, 0)),
        out_shape=jax.ShapeDtypeStruct((n, d), F32),
        scratch_shapes=[pltpu.VMEM((TOP_K * tm * ROW_CHUNKS, LANES), F32),
                        pltpu.VMEM((TOP_K * tm * ROW_CHUNKS, LANES), F32),
                        pltpu.SemaphoreType.DMA((2,))],
        compiler_params=_cparams(("arbitrary",)),
        name="moe_combine",
    )(dest2, dest2, gates, x2, mod4, lng, lnb, ys)


def _moe(x2, mod4, rw, rb, w_gu, b_gu, w_dn, b_dn, lng, lnb, seq):
    n, d = x2.shape
    h_rows, meta, gates, cnt = _router(x2, mod4, rw, rb, seq)
    counts = cnt[0, :N_EXPERTS]
    padded = -(-counts // EXPERT_ROWS) * EXPERT_ROWS
    pend = jnp.cumsum(padded)
    pstart = pend - padded
    n_blocks = -(-(n * TOP_K + N_EXPERTS * (EXPERT_ROWS - 1)) // EXPERT_ROWS)
    block_row0 = jnp.arange(n_blocks, dtype=I32) * EXPERT_ROWS
    block_exp = jnp.minimum(jnp.sum((pend[None, :] <= block_row0[:, None]).astype(I32), axis=1), N_EXPERTS - 1)
    first = jnp.concatenate([jnp.ones((1,), I32), (block_exp[1:] != block_exp[:-1]).astype(I32)])
    dest = jnp.take(pstart, meta[:, :TOP_K]) + meta[:, TOP_K:2 * TOP_K]
    dest2 = dest.reshape(n * TOP_K // LANES, LANES).astype(I32)
    rows = n_blocks * EXPERT_ROWS
    xs = _dispatch(h_rows, dest2, jnp.zeros((rows * ROW_CHUNKS, LANES), F32), n)
    e = b_gu.shape[0]
    b_grouped = b_gu.reshape(e, d // LANES, LANES, 2).transpose(0, 1, 3, 2).reshape(e, 1, 2 * d)
    n_used = (pend[-1:] // EXPERT_ROWS).astype(I32)
    ys = _experts(xs, block_exp, first, n_used, w_gu, b_grouped, w_dn, b_dn[:, None, :])
    return _combine(ys, dest2, gates, x2, mod4, lng.reshape(1, d), lnb.reshape(1, d), seq)


def _permute_w_in(w):
    d = w.shape[0]
    off = {}
    o = 0
    for name, wd in (("a_q", 256), ("a_k", 256), ("a_v", 256), ("b_q", 384), ("b_kc", 128), ("b_vc", 128),
                     ("b_ks", 128), ("b_vs", 128), ("b_kw", 128), ("b_vw", 128), ("b_gate", 18),
                     ("c_q", 384), ("c_k", 384), ("c_v", 384), ("merge_gate", 3 * D_MODEL)):
        off[name] = (o, o + wd)
        o += wd
    col = lambda name: w[:, off[name][0]:off[name][1]]
    scale = HEAD_DIM ** -0.5 * float(np.log2(np.e))
    b_q = col("b_q").reshape(d, B_KV_GROUPS, B_REP, HEAD_DIM).transpose(0, 2, 1, 3).reshape(d, B_HEADS * HEAD_DIM)
    gate_pad = jnp.zeros((d, LANES - B_HEADS * 3), w.dtype)
    parts = [col("merge_gate"),
             col("a_q") * scale, col("a_k"), b_q * scale, col("b_kc"), col("b_ks"), col("b_kw"),
             col("c_q") * scale, col("c_k"),
             col("a_v"), col("b_vc"), col("b_vs"), col("b_vw"), col("c_v"),
             col("b_gate"), gate_pad]
    return jnp.concatenate(parts, axis=1).astype(BF16)


def _dilate(arr, tile, batch, seq, dil):
    a = arr[:, tile * LANES:(tile + 1) * LANES].reshape(batch, seq // dil, dil, LANES)
    return a.transpose(0, 2, 1, 3).reshape(batch * seq, LANES)


def _undilate(arr, batch, seq, dil):
    a = arr.reshape(batch, dil, seq // dil, LANES)
    return a.transpose(0, 2, 1, 3).reshape(batch * seq, LANES)


def _mixer_layer(x2, mod4, cos_t, sin_t, batch, seq, w_in, cmp_w1_k, cmp_w2_k, cmp_pe_k,
                 cmp_w1_v, cmp_w2_v, cmp_pe_v, w_a, w_b, w_c, w_out, lng, lnb):
    d = D_MODEL
    proj = _inproj(x2, mod4, _permute_w_in(w_in), cos_t, sin_t, seq)

    bias_a = _moba_select(proj, _kmean(proj, batch, seq), batch, seq)
    o_a = _bb_attn(proj, bias_a, batch=batch, seq=seq, n_pairs=A_HEADS // 2,
                   q_tile=lambda p: T_AQ + p, k_tile=lambda p: T_AK + p, v_tile=lambda p: T_AV + p,
                   bias_tile=lambda p: p, blk=MOBA_BLOCK)

    kc = _compress(proj[:, T_BKC * LANES:(T_BKC + 1) * LANES], cmp_w1_k, cmp_w2_k, cmp_pe_k, batch, seq)
    vc = _compress(proj[:, T_BVC * LANES:(T_BVC + 1) * LANES], cmp_w1_v, cmp_w2_v, cmp_pe_v, batch, seq)
    o_cmp, bias_b = _nsa_cmp(proj, kc, vc, batch, seq)
    o_slc = _bb_attn(proj, bias_b, batch=batch, seq=seq, n_pairs=B_REP,
                     q_tile=lambda p: T_BQ + p, k_tile=lambda p: T_BKS, v_tile=lambda p: T_BVS,
                     bias_tile=lambda p: 0, blk=SLC_BLOCK)
    o_win, _ = _band_attn(proj, proj, proj, batch=batch, seq=seq, n_pairs=B_REP,
                          q_tile=lambda p: T_BQ + p, k_tile=lambda p: T_BKW, v_tile=lambda p: T_BVW,
                          window=NSA_WINDOW, t=256)

    oc, lse = [], []
    for gi, (window, dil) in enumerate(DILATED_PAIRS):
        wlen = window // dil + 1
        if dil == 1:
            o, l = _band_attn(proj, proj, proj, batch=batch, seq=seq, n_pairs=1,
                              q_tile=lambda p: T_CQ + gi, k_tile=lambda p: T_CK + gi,
                              v_tile=lambda p: T_CV + gi, window=wlen, t=256)
        else:
            qd = _dilate(proj, T_CQ + gi, batch, seq, dil)
            kd = _dilate(proj, T_CK + gi, batch, seq, dil)
            vd = _dilate(proj, T_CV + gi, batch, seq, dil)
            o, l = _band_attn(qd, kd, vd, batch=batch * dil, seq=seq // dil, n_pairs=1,
                              q_tile=lambda p: 0, k_tile=lambda p: 0, v_tile=lambda p: 0,
                              window=wlen, t=256)
            o, l = _undilate(o, batch, seq, dil), _undilate(l, batch, seq, dil)
        oc.append(o)
        lse.append(l)

    w_b_perm = w_b.reshape(B_KV_GROUPS, B_REP, HEAD_DIM, d).transpose(1, 0, 2, 3).reshape(B_HEADS * HEAD_DIM, d)
    return _merge(o_a, o_cmp, o_slc, o_win, proj, oc, lse, x2, mod4,
                  w_a.astype(BF16), w_b_perm.astype(BF16), w_c.astype(BF16), w_out.astype(BF16),
                  lng.reshape(1, d), lnb.reshape(1, d), seq)


def _rope_tables(positions):
    inv = ROPE_THETA ** (-jnp.arange(0, HEAD_DIM, 2, dtype=F32) / HEAD_DIM)
    ang = positions.astype(F32).reshape(-1, 1) * inv[None, :]
    cos, sin = jnp.cos(ang), jnp.sin(ang)
    return jnp.tile(cos, (1, 4)), jnp.tile(jnp.concatenate([-sin, sin], axis=1), (1, 2))


def kernel(x, c, positions, w_ada, b_ada, w_in, cmp_w1_k, cmp_w2_k, cmp_pe_k, cmp_w1_v, cmp_w2_v, cmp_pe_v, w_branch_a, w_branch_b, w_branch_c, w_out, ln1_g, ln1_b, router_w, router_b, w_gate_up, b_gate_up, w_down, b_down, ln2_g, ln2_b):
    batch, seq, d = x.shape
    depth = w_in.shape[0]
    cos_t, sin_t = _rope_tables(positions)
    mod = _ada(c, w_ada, b_ada)
    x2 = x.reshape(batch * seq, d)
    for l in range(depth):
        mod4 = mod[l].reshape(batch, N_ADA, 1, d)
        x2 = _mixer_layer(x2, mod4, cos_t, sin_t, batch, seq, w_in[l], cmp_w1_k[l], cmp_w2_k[l], cmp_pe_k[l],
                          cmp_w1_v[l], cmp_w2_v[l], cmp_pe_v[l], w_branch_a[l], w_branch_b[l],
                          w_branch_c[l], w_out[l], ln1_g[l], ln1_b[l])
        x2 = _moe(x2, mod4, router_w[l], router_b[l], w_gate_up[l], b_gate_up[l], w_down[l], b_down[l],
                  ln2_g[l], ln2_b[l], seq)
    return x2.reshape(batch, seq, d)
```

```python
import functools

import numpy as np
import jax
import jax.numpy as jnp
from jax import lax
from jax.experimental import pallas as pl
from jax.experimental.pallas import tpu as pltpu

F32 = jnp.float32
BF16 = jnp.bfloat16
I32 = jnp.int32
HIGHEST = lax.Precision.HIGHEST

D_MODEL = 1024
DEPTH = 2
HEAD_DIM = 64
ROPE_THETA = 10000.0
LN_EPS = 1e-5
DEEPNORM_ALPHA = (2 * DEPTH) ** 0.25
N_ADA = 6
A_HEADS = 4
MOBA_BLOCK = 256
MOBA_TOPK = 3
B_HEADS = 6
B_KV_GROUPS = 2
B_REP = 3
CMP_BLOCK = 32
CMP_STRIDE = 16
CMP_HIDDEN = 128
SLC_BLOCK = 64
SLC_TOPK = 16
NSA_WINDOW = 512
FORCE_SCORE = 1e6
DILATED_PAIRS = ((128, 1), (512, 4), (2048, 16))
N_EXPERTS = 32
TOP_K = 4
SWIGLU_ALPHA = 1.702
SWIGLU_LIMIT = 7.0

LANES = 128
SUBLANES = 8
HALF = LANES // 2
NEG = -1e30
ROW_CHUNKS = D_MODEL // LANES
VMEM_LIMIT = 56 * 1024 * 1024

T_MG = 0
T_ROPE0, T_ROPE1 = 24, 40
T_AQ, T_AK, T_BQ, T_BKC, T_BKS, T_BKW, T_CQ, T_CK = 24, 26, 28, 31, 32, 33, 34, 37
T_AV, T_BVC, T_BVS, T_BVW, T_CV, T_BG = 40, 42, 43, 44, 45, 48
N_PROJ_TILES = 49
PROJ_W = N_PROJ_TILES * LANES

EXPERT_ROWS = 512
TOK_TILE = 256


def _cparams(sem, **kw):
    return pltpu.CompilerParams(dimension_semantics=sem, vmem_limit_bytes=VMEM_LIMIT, **kw)


def _lane_iota(shape=(1, LANES)):
    return lax.broadcasted_iota(I32, shape, len(shape) - 1)


def _sigmoid(x):
    return 1.0 / (1.0 + jnp.exp(-x))


def _ln(x):
    mu = jnp.mean(x, axis=-1, keepdims=True)
    xc = x - mu
    var = jnp.mean(xc * xc, axis=-1, keepdims=True)
    return xc * lax.rsqrt(var + LN_EPS)


def _dot_nt(a, b, precision=None):
    return lax.dot_general(a, b, (((1,), (1,)), ((), ())), precision=precision,
                           preferred_element_type=F32)


def _ada_kernel(c_ref, w_ref, b_ref, o_ref):
    c = c_ref[...]
    cond = c * _sigmoid(c)
    o_ref[0] = jnp.dot(cond, w_ref[0], precision=HIGHEST, preferred_element_type=F32) + b_ref[0]


def _ada(c, w_ada, b_ada):
    depth, d, n = w_ada.shape
    b = c.shape[0]
    tn = 1536
    return pl.pallas_call(
        _ada_kernel,
        grid=(depth, n // tn),
        in_specs=[pl.BlockSpec((b, d), lambda l, j: (0, 0)),
                  pl.BlockSpec((1, d, tn), lambda l, j: (l, 0, j)),
                  pl.BlockSpec((1, 1, tn), lambda l, j: (l, 0, j))],
        out_specs=pl.BlockSpec((1, b, tn), lambda l, j: (l, 0, j)),
        out_shape=jax.ShapeDtypeStruct((depth, b, n), F32),
        compiler_params=_cparams(("parallel", "parallel")),
        name="ada",
    )(c, w_ada, b_ada.reshape(depth, 1, n))


def _inproj_kernel(x_ref, sc_ref, sh_ref, w_ref, cos_ref, sin_ref, o_ref, *, chunks):
    h = _ln(x_ref[...]) * (1.0 + sc_ref[0, 0]) + sh_ref[0, 0]
    hb = h.astype(BF16)
    first_half = (_lane_iota() & (HEAD_DIM - 1)) < (HEAD_DIM // 2)
    for c0, cw, rope in chunks:
        acc = jnp.dot(hb, w_ref[:, c0:c0 + cw], preferred_element_type=F32)
        if rope:
            cos = cos_ref[...]
            sin = sin_ref[...]
            for t in range(cw // LANES):
                a = acc[:, t * LANES:(t + 1) * LANES]
                rot = jnp.where(first_half, pltpu.roll(a, LANES - HEAD_DIM // 2, 1),
                                pltpu.roll(a, HEAD_DIM // 2, 1))
                o_ref[:, c0 + t * LANES:c0 + (t + 1) * LANES] = (a * cos + rot * sin).astype(o_ref.dtype)
        else:
            o_ref[:, c0:c0 + cw] = acc.astype(o_ref.dtype)


def _inproj_chunks():
    chunks = []
    for lo, hi, rope in ((0, T_ROPE0, False), (T_ROPE0, T_ROPE1, True), (T_ROPE1, N_PROJ_TILES, False)):
        c = lo * LANES
        while c < hi * LANES:
            cw = min(512, hi * LANES - c)
            chunks.append((c, cw, rope))
            c += cw
    return tuple(chunks)


def _inproj(x2, mod4, w_perm, cos_t, sin_t, seq):
    n, d = x2.shape
    tm = 256
    per_b = seq // tm
    return pl.pallas_call(
        functools.partial(_inproj_kernel, chunks=_inproj_chunks()),
        grid=(n // tm,),
        in_specs=[pl.BlockSpec((tm, d), lambda i: (i, 0)),
                  pl.BlockSpec((1, 1, 1, d), lambda i: (i // per_b, 1, 0, 0)),
                  pl.BlockSpec((1, 1, 1, d), lambda i: (i // per_b, 0, 0, 0)),
                  pl.BlockSpec((d, PROJ_W), lambda i: (0, 0)),
                  pl.BlockSpec((tm, LANES), lambda i: (i, 0)),
                  pl.BlockSpec((tm, LANES), lambda i: (i, 0))],
        out_specs=pl.BlockSpec((tm, PROJ_W), lambda i: (i, 0)),
        out_shape=jax.ShapeDtypeStruct((n, PROJ_W), BF16),
        compiler_params=_cparams(("parallel",)),
        name="inproj",
    )(x2, mod4, mod4, w_perm, cos_t, sin_t)


def _bb_attn_kernel(q_ref, k_ref, v_ref, b_ref, o_ref, qa_sc, s_sc, mp_sc, m_sc, acc_sc,
                    *, blk_shift, tq, big):
    i = pl.program_id(2)
    lane = _lane_iota()
    lo = lane < HALF
    q = q_ref[...]
    b = b_ref[...]
    qa_sc[0] = jnp.where(lo, q, b)
    qa_sc[1] = jnp.where(lo, b, q)
    mp_sc[...] = jnp.full(mp_sc.shape, -jnp.inf, F32)
    acc_sc[...] = jnp.zeros(acc_sc.shape, F32)

    def scores(u0, n_units, diagonal_last):
        tk = n_units * tq
        start = pl.multiple_of(u0 * tq, tq)
        k = k_ref[pl.ds(start, tk), :]
        kblk = (start + lax.broadcasted_iota(I32, (tk, 1), 0)) >> blk_shift
        onehot = jnp.where(kblk == (lane & (HALF - 1)), 1.0, 0.0).astype(BF16)
        ka = (jnp.where(lo, k, onehot), jnp.where(lo, onehot, k))
        causal = lax.broadcasted_iota(I32, (1, tq), 1) <= lax.broadcasted_iota(I32, (tq, 1), 0)
        for h in range(2):
            s = _dot_nt(qa_sc[h], ka[h])
            mp = mp_sc[h]
            for u in range(n_units):
                su = s[:, u * tq:(u + 1) * tq]
                if diagonal_last and u == n_units - 1:
                    su = jnp.where(causal, su, NEG)
                s_sc[h, u0 + u] = su
                for c in range(tq // LANES):
                    mp = jnp.maximum(mp, su[:, c * LANES:(c + 1) * LANES])
            mp_sc[h] = mp

    def weighted(u0, n_units):
        tk = n_units * tq
        start = pl.multiple_of(u0 * tq, tq)
        v = v_ref[pl.ds(start, tk), :]
        one = jnp.ones_like(v)
        vs = (jnp.where(lo, v, one), jnp.where(lo, one, v))
        for h in range(2):
            m_row = m_sc[h]
            ps = []
            for u in range(n_units):
                su = s_sc[h, u0 + u]
                for c in range(tq // LANES):
                    ps.append(jnp.exp2(su[:, c * LANES:(c + 1) * LANES] - m_row).astype(BF16))
            acc_sc[h] = acc_sc[h] + jnp.dot(jnp.concatenate(ps, axis=1), vs[h], preferred_element_type=F32)

    n_big = i // big
    n_tail = i - n_big * big + 1

    def run(fn):
        def big_body(t, c):
            fn(t * big, big, False) if fn is scores else fn(t * big, big)
            return c

        lax.fori_loop(0, n_big, big_body, 0)
        for n in range(1, big + 1):
            @pl.when(n_tail == n)
            def _(n=n):
                fn(n_big * big, n, True) if fn is scores else fn(n_big * big, n)

    run(scores)
    for h in range(2):
        m_sc[h] = jnp.broadcast_to(jnp.max(mp_sc[h], axis=1, keepdims=True), (tq, LANES))
    run(weighted)
    outs = [acc_sc[h] / pltpu.roll(acc_sc[h], HALF, 1) for h in range(2)]
    o_ref[...] = jnp.where(lo, outs[0], outs[1]).astype(o_ref.dtype)


def _bb_attn(proj, bias, *, batch, seq, n_pairs, q_tile, k_tile, v_tile, bias_tile, blk):
    t = 256
    nq = seq // t
    n = batch * seq
    kern = functools.partial(_bb_attn_kernel, blk_shift=int(np.log2(blk)), tq=t, big=4)
    return pl.pallas_call(
        kern,
        grid=(batch, n_pairs, nq),
        in_specs=[pl.BlockSpec((t, LANES), lambda b, p, i: (b * nq + i, q_tile(p))),
                  pl.BlockSpec((seq, LANES), lambda b, p, i: (b, k_tile(p))),
                  pl.BlockSpec((seq, LANES), lambda b, p, i: (b, v_tile(p))),
                  pl.BlockSpec((t, LANES), lambda b, p, i: (b * nq + i, bias_tile(p)))],
        out_specs=pl.BlockSpec((t, LANES), lambda b, p, i: (b * nq + i, p)),
        out_shape=jax.ShapeDtypeStruct((n, n_pairs * LANES), BF16),
        scratch_shapes=[pltpu.VMEM((2, t, LANES), BF16),
                        pltpu.VMEM((2, nq, t, t), F32),
                        pltpu.VMEM((2, t, LANES), F32),
                        pltpu.VMEM((2, t, LANES), F32),
                        pltpu.VMEM((2, t, LANES), F32)],
        compiler_params=_cparams(("parallel", "parallel", "parallel")),
        name="bb_attn_%d" % blk,
    )(proj, proj, proj, bias)


def _band_kernel(q_ref, k_ref, v_ref, o_ref, lse_ref, *, tq, kw, window, seq):
    i = pl.program_id(2)
    lane = _lane_iota()
    lo = lane < HALF
    start = pl.multiple_of(jnp.clip(i * tq - (kw - tq), 0, seq - kw), LANES)
    q = q_ref[...]
    k = k_ref[pl.ds(start, kw), :]
    v = v_ref[pl.ds(start, kw), :]
    zero = jnp.zeros_like(q)
    qs = (jnp.where(lo, q, zero), jnp.where(lo, zero, q))
    rel = (i * tq + lax.broadcasted_iota(I32, (tq, 1), 0)) - (start + lax.broadcasted_iota(I32, (1, kw), 1))
    ok = (rel >= 0) & (rel < window)
    outs, lses = [], []
    for h in range(2):
        s = jnp.where(ok, _dot_nt(qs[h], k), NEG)
        m = jnp.max(s, axis=1, keepdims=True)
        p = jnp.exp2(s - m)
        l = jnp.sum(p, axis=1, keepdims=True)
        outs.append(jnp.dot(p.astype(BF16), v, preferred_element_type=F32) / l)
        lses.append(m + jnp.log2(l))
    o_ref[...] = jnp.where(lo, outs[0], outs[1]).astype(o_ref.dtype)
    lse_ref[...] = jnp.where(lo, lses[0], lses[1])


def _band_attn(q_arr, k_arr, v_arr, *, batch, seq, n_pairs, q_tile, k_tile, v_tile, window, t):
    t = min(t, seq)
    nq = seq // t
    kw = min(seq, t + -(-(window - 1) // LANES) * LANES)
    n = batch * seq
    kern = functools.partial(_band_kernel, tq=t, kw=kw, window=window, seq=seq)
    return pl.pallas_call(
        kern,
        grid=(batch, n_pairs, nq),
        in_specs=[pl.BlockSpec((t, LANES), lambda b, p, i: (b * nq + i, q_tile(p))),
                  pl.BlockSpec((seq, LANES), lambda b, p, i: (b, k_tile(p))),
                  pl.BlockSpec((seq, LANES), lambda b, p, i: (b, v_tile(p)))],
        out_specs=[pl.BlockSpec((t, LANES), lambda b, p, i: (b * nq + i, p)),
                   pl.BlockSpec((t, LANES), lambda b, p, i: (b * nq + i, p))],
        out_shape=[jax.ShapeDtypeStruct((n, n_pairs * LANES), BF16),
                   jax.ShapeDtypeStruct((n, n_pairs * LANES), F32)],
        compiler_params=_cparams(("parallel", "parallel", "parallel")),
        name="band_attn_%d" % window,
    )(q_arr, k_arr, v_arr)


def _kmean_kernel(k_ref, o_ref, *, nb):
    s = k_ref.shape[0]
    blk = lax.broadcasted_iota(I32, (nb, s), 1) >> int(np.log2(MOBA_BLOCK))
    avg = jnp.where(blk == lax.broadcasted_iota(I32, (nb, s), 0), 1.0 / MOBA_BLOCK, 0.0).astype(BF16)
    o_ref[0] = jnp.dot(avg, k_ref[...], preferred_element_type=F32)


def _kmean(proj, batch, seq):
    nb = seq // MOBA_BLOCK
    w = A_HEADS * HEAD_DIM
    return pl.pallas_call(
        functools.partial(_kmean_kernel, nb=nb),
        grid=(batch,),
        in_specs=[pl.BlockSpec((seq, w), lambda b: (b, T_AK * LANES // w))],
        out_specs=pl.BlockSpec((1, nb, w), lambda b: (b, 0, 0)),
        out_shape=jax.ShapeDtypeStruct((batch, nb, w), F32),
        compiler_params=_cparams(("parallel",)),
        name="moba_kmean",
    )(proj)


def _rank_desc(g, n_idx, n):
    rank = jnp.zeros(g.shape, I32)
    for m in range(n):
        c = g[m:m + 1, :]
        beats = (c > g) | ((c == g) & (n_idx > m))
        rank = rank + jnp.where(beats, 1, 0)
    return rank


def _moba_sel_kernel(q_ref, km_ref, b_ref, *, nb, n_sel):
    i = pl.program_id(2)
    t = q_ref.shape[0]
    q = q_ref[...].astype(F32)
    km = km_ref[0]
    lo = _lane_iota() < HALF
    zero = jnp.zeros_like(km)
    pad = jnp.zeros((HALF - nb, LANES), F32)
    kmt = jnp.concatenate([jnp.where(lo, zero, km), pad, jnp.where(lo, km, zero), pad], axis=0)
    gt = _dot_nt(kmt, q, precision=HIGHEST)
    n_idx = lax.broadcasted_iota(I32, (nb, 1), 0)
    valid = n_idx < i
    rows = []
    for r0 in (0, HALF):
        g = jnp.where(valid, gt[r0:r0 + nb, :], -jnp.inf)
        rank = _rank_desc(g, n_idx, nb)
        allowed = (valid & (rank < n_sel)) | (n_idx == i)
        rows.append(jnp.where(allowed, 0.0, NEG))
        rows.append(jnp.zeros((HALF - nb, t), F32))
    b_ref[...] = jnp.concatenate(rows, axis=0).T.astype(b_ref.dtype)


def _moba_select(proj, kmean, batch, seq):
    t = MOBA_BLOCK
    nq = seq // t
    nb = seq // MOBA_BLOCK
    n_sel = min(MOBA_TOPK, nb - 1)
    n_pairs = A_HEADS // 2
    return pl.pallas_call(
        functools.partial(_moba_sel_kernel, nb=nb, n_sel=n_sel),
        grid=(batch, n_pairs, nq),
        in_specs=[pl.BlockSpec((t, LANES), lambda b, p, i: (b * nq + i, T_AQ + p)),
                  pl.BlockSpec((1, nb, LANES), lambda b, p, i: (b, 0, p))],
        out_specs=pl.BlockSpec((t, LANES), lambda b, p, i: (b * nq + i, p)),
        out_shape=jax.ShapeDtypeStruct((batch * seq, n_pairs * LANES), BF16),
        compiler_params=_cparams(("parallel", "parallel", "parallel")),
        name="moba_select",
    )(proj, kmean)


def _compress_kernel(x_ref, w1_ref, pe_ref, w1f_ref, w2_ref, o_ref):
    x = x_ref[0]
    nblk = x.shape[0]
    outs = []
    for g in range(B_KV_GROUPS):
        u = jnp.dot(x, w1_ref[g, 0], preferred_element_type=F32)
        v = jnp.dot(x, w1_ref[g, 1], preferred_element_type=F32)
        pe_h = jnp.dot(pe_ref[...], w1f_ref[...], precision=HIGHEST, preferred_element_type=F32)
        hid = u + pltpu.roll(v, nblk - 1, 0) + pe_h[:1]
        hid = hid * _sigmoid(hid)
        outs.append(jnp.dot(hid, w2_ref[g], precision=HIGHEST, preferred_element_type=F32))
    o = outs[0] + outs[1]
    rows = lax.broadcasted_iota(I32, (nblk, 1), 0)
    o_ref[0] = jnp.where(rows < nblk - 1, o, 0.0).astype(o_ref.dtype)


def _compress(xt, w1, w2, pe, batch, seq):
    nblk = seq // CMP_STRIDE
    xg = xt.reshape(batch, nblk, CMP_STRIDE * LANES)
    w1r = w1.reshape(2, CMP_STRIDE, HEAD_DIM, CMP_HIDDEN)
    w1e = jnp.zeros((B_KV_GROUPS, 2, CMP_STRIDE, B_KV_GROUPS, HEAD_DIM, CMP_HIDDEN), F32)
    for g in range(B_KV_GROUPS):
        w1e = w1e.at[g, :, :, g].set(w1r)
    w1e = w1e.reshape(B_KV_GROUPS, 2, CMP_STRIDE * LANES, CMP_HIDDEN).astype(BF16)
    pe_flat = jnp.broadcast_to(pe.reshape(1, CMP_BLOCK * HEAD_DIM), (SUBLANES, CMP_BLOCK * HEAD_DIM))
    w2e = jnp.zeros((B_KV_GROUPS, CMP_HIDDEN, LANES), F32)
    for g in range(B_KV_GROUPS):
        w2e = w2e.at[g, :, g * HEAD_DIM:(g + 1) * HEAD_DIM].set(w2)
    return pl.pallas_call(
        _compress_kernel,
        grid=(batch,),
        in_specs=[pl.BlockSpec((1, nblk, CMP_STRIDE * LANES), lambda b: (b, 0, 0)),
                  pl.BlockSpec(w1e.shape, lambda b: (0, 0, 0, 0)),
                  pl.BlockSpec(pe_flat.shape, lambda b: (0, 0)),
                  pl.BlockSpec(w1.shape, lambda b: (0, 0)),
                  pl.BlockSpec(w2e.shape, lambda b: (0, 0, 0))],
        out_specs=pl.BlockSpec((1, nblk, LANES), lambda b: (b, 0, 0)),
        out_shape=jax.ShapeDtypeStruct((batch, nblk, LANES), BF16),
        compiler_params=_cparams(("parallel",)),
        name="nsa_compress",
    )(xg, w1e, pe_flat, w1, w2e)


def _nsa_cmp_kernel(q0_ref, q1_ref, q2_ref, kc_ref, vc_ref, ov_ref, o_ref, b_ref, *, n_sel, ns):
    i = pl.program_id(1)
    t = q0_ref.shape[0]
    ncp = kc_ref.shape[1]
    lane = _lane_iota()
    lo = lane < HALF
    kc = kc_ref[0]
    vc = vc_ref[0]
    tq = i * t + lax.broadcasted_iota(I32, (t, 1), 0)
    cmp_end = lax.broadcasted_iota(I32, (1, ncp), 1) * CMP_STRIDE + (CMP_BLOCK - 1)
    vis = cmp_end <= tq
    psum = [jnp.zeros((t, ncp), F32) for _ in range(B_KV_GROUPS)]
    for r, q_ref in enumerate((q0_ref, q1_ref, q2_ref)):
        q = q_ref[...]
        zero = jnp.zeros_like(q)
        outs = []
        for g in range(B_KV_GROUPS):
            qg = jnp.where(lo, q, zero) if g == 0 else jnp.where(lo, zero, q)
            s = jnp.where(vis, _dot_nt(qg, kc), NEG)
            m = jnp.max(s, axis=1, keepdims=True)
            m = jnp.where(m > 0.5 * NEG, m, 0.0)
            e = jnp.where(vis, jnp.exp2(s - m), 0.0)
            p = e / jnp.maximum(jnp.sum(e, axis=1, keepdims=True), 1e-30)
            psum[g] = psum[g] + p
            outs.append(jnp.dot(p.astype(BF16), vc, preferred_element_type=F32))
        o_ref[:, r * LANES:(r + 1) * LANES] = jnp.where(lo, outs[0], outs[1]).astype(o_ref.dtype)
    n_idx = lax.broadcasted_iota(I32, (ns, 1), 0)
    tcol = i * t + lax.broadcasted_iota(I32, (1, t), 1)
    qblk = tcol >> int(np.log2(SLC_BLOCK))
    forced = (n_idx == 0) | (n_idx == qblk) | (n_idx == qblk - 1)
    valid = n_idx <= qblk
    rows = []
    for g in (1, 0):
        imp = _dot_nt(ov_ref[...], psum[g], precision=HIGHEST)
        imp = jnp.where(forced, FORCE_SCORE, imp)
        imp = jnp.where(valid, imp, -jnp.inf)
        rank = _rank_desc(imp, n_idx, ns)
        allowed = valid & (rank < n_sel)
        rows.append(jnp.where(allowed, 0.0, NEG))
        if ns < HALF:
            rows.append(jnp.zeros((HALF - ns, t), F32))
    b_ref[...] = jnp.concatenate(rows, axis=0).T.astype(b_ref.dtype)


def _nsa_cmp(proj, kc, vc, batch, seq):
    t = 256
    nq = seq // t
    ns = seq // SLC_BLOCK
    n_sel = min(SLC_TOPK, ns)
    ncp = seq // CMP_STRIDE
    cs = np.arange(ncp)[None, :] * CMP_STRIDE
    ss = np.arange(ns)[:, None] * SLC_BLOCK
    ov = ((cs < ss + SLC_BLOCK) & (cs + CMP_BLOCK > ss)).astype(np.float32)
    ov[:, ncp - 1] = 0.0
    n = batch * seq
    qspec = [pl.BlockSpec((t, LANES), (lambda b, i, r=r: (b * nq + i, T_BQ + r))) for r in range(B_REP)]
    return pl.pallas_call(
        functools.partial(_nsa_cmp_kernel, n_sel=n_sel, ns=ns),
        grid=(batch, nq),
        in_specs=qspec + [pl.BlockSpec((1, ncp, LANES), lambda b, i: (b, 0, 0)),
                          pl.BlockSpec((1, ncp, LANES), lambda b, i: (b, 0, 0)),
                          pl.BlockSpec((ns, ncp), lambda b, i: (0, 0))],
        out_specs=[pl.BlockSpec((t, B_REP * LANES), lambda b, i: (b * nq + i, 0)),
                   pl.BlockSpec((t, LANES), lambda b, i: (b * nq + i, 0))],
        out_shape=[jax.ShapeDtypeStruct((n, B_REP * LANES), BF16),
                   jax.ShapeDtypeStruct((n, LANES), BF16)],
        compiler_params=_cparams(("parallel", "parallel")),
        name="nsa_cmp_select",
    )(proj, proj, proj, kc, vc, jnp.asarray(ov))


def _merge_kernel(oa_ref, ocmp_ref, oslc_ref, owin_ref, bg_ref, oc0_ref, oc1_ref, oc2_ref,
                  l0_ref, l1_ref, l2_ref, mg_ref, x_ref, g1_ref, wa_ref, wb_ref, wc_ref, wo_ref,
                  eg_ref, lng_ref, lnb_ref, o_ref):
    d = D_MODEL
    sg = _sigmoid(bg_ref[...].astype(F32))
    ob = jnp.zeros(ocmp_ref.shape, F32)
    for br, ref in enumerate((ocmp_ref, oslc_ref, owin_ref)):
        gexp = jnp.dot(sg, eg_ref[br], precision=HIGHEST, preferred_element_type=F32)
        ob = ob + gexp * ref[...].astype(F32)
    l0, l1, l2 = l0_ref[...], l1_ref[...], l2_ref[...]
    mx = jnp.maximum(jnp.maximum(l0, l1), l2)
    e0, e1, e2 = jnp.exp2(l0 - mx), jnp.exp2(l1 - mx), jnp.exp2(l2 - mx)
    den = e0 + e1 + e2
    oc = ((e0 / den) * oc0_ref[...].astype(F32) + (e1 / den) * oc1_ref[...].astype(F32)
          + (e2 / den) * oc2_ref[...].astype(F32))
    pa = jnp.dot(oa_ref[...], wa_ref[...], preferred_element_type=F32)
    pb = jnp.dot(ob.astype(BF16), wb_ref[...], preferred_element_type=F32)
    pc = jnp.dot(oc.astype(BF16), wc_ref[...], preferred_element_type=F32)
    merged = (_sigmoid(mg_ref[:, 0:d].astype(F32)) * pa
              + _sigmoid(mg_ref[:, d:2 * d].astype(F32)) * pb
              + _sigmoid(mg_ref[:, 2 * d:3 * d].astype(F32)) * pc)
    y = jnp.dot(merged.astype(BF16), wo_ref[...], preferred_element_type=F32)
    z = DEEPNORM_ALPHA * x_ref[...] + g1_ref[0, 0] * y
    o_ref[...] = _ln(z) * lng_ref[...] + lnb_ref[...]


def _gate_expand():
    eg = np.zeros((3, LANES, B_HEADS * HEAD_DIM), np.float32)
    for g in range(B_KV_GROUPS):
        for r in range(B_REP):
            for br in range(3):
                c0 = (r * B_KV_GROUPS + g) * HEAD_DIM
                eg[br, (g * B_REP + r) * 3 + br, c0:c0 + HEAD_DIM] = 1.0
    return jnp.asarray(eg)


def _merge(o_a, o_cmp, o_slc, o_win, proj, oc, lse, x2, mod4, wa, wb, wc, wo, lng, lnb, seq):
    n, d = x2.shape
    tm = 256
    per_b = seq // tm
    row = lambda w: pl.BlockSpec((tm, w), lambda i: (i, 0))
    full = lambda a: pl.BlockSpec(a.shape, lambda i: (0,) * a.ndim)
    eg = _gate_expand()
    return pl.pallas_call(
        _merge_kernel,
        grid=(n // tm,),
        in_specs=[row(o_a.shape[1]), row(o_cmp.shape[1]), row(o_slc.shape[1]), row(o_win.shape[1]),
                  pl.BlockSpec((tm, LANES), lambda i: (i, T_BG)),
                  row(LANES), row(LANES), row(LANES), row(LANES), row(LANES), row(LANES),
                  pl.BlockSpec((tm, 3 * d), lambda i: (i, T_MG)),
                  row(d),
                  pl.BlockSpec((1, 1, 1, d), lambda i: (i // per_b, 2, 0, 0)),
                  full(wa), full(wb), full(wc), full(wo), full(eg), full(lng), full(lnb)],
        out_specs=row(d),
        out_shape=jax.ShapeDtypeStruct((n, d), F32),
        compiler_params=_cparams(("parallel",)),
        name="merge_out",
    )(o_a, o_cmp, o_slc, o_win, proj, oc[0], oc[1], oc[2], lse[0], lse[1], lse[2],
      proj, x2, mod4, wa, wb, wc, wo, eg, lng, lnb)


def _router_kernel(x_ref, sc_ref, sh_ref, rw_ref, rb_ref, h_ref, meta_ref, gate_ref, cnt_ref, carry_sc):
    i = pl.program_id(0)
    tm = x_ref.shape[0]

    @pl.when(i == 0)
    def _():
        carry_sc[...] = jnp.zeros(carry_sc.shape, F32)

    h = _ln(x_ref[...]) * (1.0 + sc_ref[0, 0]) + sh_ref[0, 0]
    for c in range(ROW_CHUNKS):
        h_ref[pl.ds(c, tm, stride=ROW_CHUNKS), :] = h[:, c * LANES:(c + 1) * LANES]
    lg = jnp.dot(h, rw_ref[...], precision=HIGHEST, preferred_element_type=F32) + rb_ref[...]
    lane = _lane_iota()
    lane_f = lane.astype(F32)
    onehots, vals, idxs = [], [], []
    for _ in range(TOP_K):
        m = jnp.max(lg, axis=1, keepdims=True)
        idx = jnp.min(jnp.where(lg == m, lane_f, float(LANES)), axis=1, keepdims=True).astype(I32)
        oh = lane == idx
        onehots.append(oh)
        vals.append(m)
        idxs.append(idx)
        lg = jnp.where(oh, -jnp.inf, lg)
    es = [jnp.exp(v - vals[0]) for v in vals]
    den = es[0] + es[1] + es[2] + es[3]
    cnt = jnp.zeros((tm, LANES), F32)
    for oh in onehots:
        cnt = cnt + jnp.where(oh, 1.0, 0.0)
    tri = jnp.where(lax.broadcasted_iota(I32, (tm, tm), 0) > lax.broadcasted_iota(I32, (tm, tm), 1), 1.0, 0.0)
    before = jnp.dot(tri.astype(BF16), cnt.astype(BF16), preferred_element_type=F32) + carry_sc[...]
    meta = jnp.zeros((tm, LANES), I32)
    gates = jnp.zeros((tm, LANES), F32)
    for k in range(TOP_K):
        rank = jnp.sum(jnp.where(onehots[k], before, 0.0), axis=1, keepdims=True).astype(I32)
        meta = jnp.where(lane == k, idxs[k], meta)
        meta = jnp.where(lane == TOP_K + k, rank, meta)
        gates = jnp.where(lane == k, es[k] / den, gates)
    meta_ref[...] = meta
    gate_ref[...] = gates
    carry_sc[...] = carry_sc[...] + jnp.sum(cnt, axis=0, keepdims=True)
    cnt_ref[...] = jnp.broadcast_to(carry_sc[...], cnt_ref.shape).astype(I32)


def _router(x2, mod4, rw, rb, seq):
    n, d = x2.shape
    tm = TOK_TILE
    per_b = seq // tm
    rw_p = jnp.zeros((d, LANES), F32).at[:, :N_EXPERTS].set(rw)
    rb_p = jnp.full((1, LANES), NEG, F32).at[0, :N_EXPERTS].set(rb)
    return pl.pallas_call(
        _router_kernel,
        grid=(n // tm,),
        in_specs=[pl.BlockSpec((tm, d), lambda i: (i, 0)),
                  pl.BlockSpec((1, 1, 1, d), lambda i: (i // per_b, 4, 0, 0)),
                  pl.BlockSpec((1, 1, 1, d), lambda i: (i // per_b, 3, 0, 0)),
                  pl.BlockSpec((d, LANES), lambda i: (0, 0)),
                  pl.BlockSpec((1, LANES), lambda i: (0, 0))],
        out_specs=[pl.BlockSpec((tm * ROW_CHUNKS, LANES), lambda i: (i, 0)),
                   pl.BlockSpec((tm, LANES), lambda i: (i, 0)),
                   pl.BlockSpec((tm, LANES), lambda i: (i, 0)),
                   pl.BlockSpec((SUBLANES, LANES), lambda i: (0, 0))],
        out_shape=[jax.ShapeDtypeStruct((n * ROW_CHUNKS, LANES), F32),
                   jax.ShapeDtypeStruct((n, LANES), I32),
                   jax.ShapeDtypeStruct((n, LANES), F32),
                   jax.ShapeDtypeStruct((SUBLANES, LANES), I32)],
        scratch_shapes=[pltpu.VMEM((1, LANES), F32)],
        compiler_params=_cparams(("arbitrary",)),
        name="router",
    )(x2, mod4, mod4, rw_p, rb_p)


def _row_slice(ref, r):
    return ref.at[pl.ds(pl.multiple_of(r * ROW_CHUNKS, ROW_CHUNKS), ROW_CHUNKS)]


def _wait_rows(src_like, dst_like, sem, n_rows):
    span = pl.ds(0, n_rows * ROW_CHUNKS)
    pltpu.make_async_copy(src_like.at[span], dst_like.at[span], sem).wait()


def _expert_kernel(be_ref, first_ref, used_ref, idx_prev_ref, idx_next_ref, h_hbm, wgu_ref, bgu_ref, wd_ref,
                   bd_ref, yk_hbm, xb0, xb1, yb0, yb1, x_sc, wgu_sc, wd_sc, gsem, ssem):
    del be_ref
    s = pl.program_id(0)
    n_used = used_ref[0]
    rows = x_sc.shape[0]
    per_blk = rows // LANES
    xbufs, ybufs = (xb0, xb1), (yb0, yb1)

    def gather_one(i_ref, slot, j):
        tok = i_ref[j // LANES, j % LANES]
        pltpu.make_async_copy(_row_slice(h_hbm, tok), _row_slice(xbufs[slot], j), gsem.at[slot]).start()

    def scatter_one(slot, j):
        dst = idx_prev_ref[per_blk + j // LANES, j % LANES]
        pltpu.make_async_copy(_row_slice(ybufs[slot], j), _row_slice(yk_hbm, dst), ssem.at[slot]).start()

    def rolled(fn):
        def body(j, c):
            fn(j)
            return c
        lax.fori_loop(0, rows, body, 0)

    @pl.when(s == 0)
    def _():
        rolled(lambda j: gather_one(idx_prev_ref, 0, j))

    for p in range(2):
        @pl.when((s < n_used) & (s % 2 == p))
        def _(p=p):
            q = 1 - p

            @pl.when(s == 0)
            def _():
                yb1[...] = jnp.zeros(yb1.shape, F32)
                even_scratch = yk_hbm.at[pl.ds(yk_hbm.shape[0] - 2 * rows * ROW_CHUNKS, rows * ROW_CHUNKS)]
                fill = pltpu.make_async_copy(yb1, even_scratch, ssem.at[0])
                fill.start()
                fill.wait()

            _wait_rows(h_hbm, xbufs[p], gsem.at[p], rows)

            @pl.when(s >= 1)
            def _():
                _wait_rows(ybufs[p], yk_hbm, ssem.at[p], rows)

            def side_work():
                for j in range(rows):
                    gather_one(idx_next_ref, q, j)
                    scatter_one(q, j)

            _expert_block(s, first_ref, xbufs[p], wgu_ref, bgu_ref, wd_ref, bd_ref, ybufs[p],
                          x_sc, wgu_sc, wd_sc, side_work=side_work)

        @pl.when((s == n_used) & (s % 2 == p))
        def _(p=p):
            q = 1 - p
            _wait_rows(h_hbm, xbufs[p], gsem.at[p], rows)
            _wait_rows(ybufs[p], yk_hbm, ssem.at[p], rows)
            rolled(lambda j: scatter_one(q, j))
            _wait_rows(ybufs[q], yk_hbm, ssem.at[q], rows)


def _expert_block(g, first_ref, xs_ref, wgu_ref, bgu_ref, wd_ref, bd_ref, ys_ref, x_sc, wgu_sc, wd_sc,
                  side_work=None):
    rows = x_sc.shape[0]
    grp = 2 * LANES

    @pl.when(first_ref[g] == 1)
    def _():
        r = lax.broadcasted_iota(I32, (grp, grp), 0)
        c = lax.broadcasted_iota(I32, (grp, grp), 1)
        src = jnp.where(c < LANES, 2 * c, 2 * (c - LANES) + 1)
        sel = jnp.where(r == src, 1.0, 0.0).astype(BF16)
        for j in range(wgu_sc.shape[1] // grp):
            blk = wgu_ref[0, :, j * grp:(j + 1) * grp].astype(BF16)
            wgu_sc[:, j * grp:(j + 1) * grp] = jnp.dot(blk, sel, preferred_element_type=F32).astype(BF16)
        wd_sc[...] = wd_ref[0].astype(BF16)

    if side_work is not None:
        side_work()
    for c in range(ROW_CHUNKS):
        x_sc[:, c * LANES:(c + 1) * LANES] = xs_ref[pl.ds(c, rows, stride=ROW_CHUNKS), :].astype(BF16)
    gu = jnp.dot(x_sc[...], wgu_sc[...], preferred_element_type=F32) + bgu_ref[0]
    acts = []
    for j in range(gu.shape[1] // grp):
        glu = jnp.minimum(gu[:, j * grp:j * grp + LANES], SWIGLU_LIMIT)
        lin = jnp.clip(gu[:, j * grp + LANES:(j + 1) * grp], -SWIGLU_LIMIT, SWIGLU_LIMIT)
        acts.append((glu * _sigmoid(SWIGLU_ALPHA * glu) * (lin + 1.0)).astype(BF16))
    act = jnp.concatenate(acts, axis=1)
    y = jnp.dot(act, wd_sc[...], preferred_element_type=F32) + bd_ref[0]
    for c in range(ROW_CHUNKS):
        ys_ref[pl.ds(c, rows, stride=ROW_CHUNKS), :] = y[:, c * LANES:(c + 1) * LANES]


def _experts(h_rows, row_idx, block_exp, first, n_used, w_gu, b_gu_grouped, w_dn, b_dn, n_slots):
    d = D_MODEL
    n_blocks = block_exp.shape[0]
    rb = EXPERT_ROWS
    assert 2 * rb == SUBLANES * LANES
    last = lambda g, nu: jnp.minimum(g, nu[0] - 1)
    exp3 = lambda g, be, fi, nu: (be[last(g, nu)], 0, 0)
    return pl.pallas_call(
        _expert_kernel,
        grid_spec=pltpu.PrefetchScalarGridSpec(
            num_scalar_prefetch=3,
            grid=(n_blocks + 1,),
            in_specs=[pl.BlockSpec((SUBLANES, LANES), lambda g, be, fi, nu: (jnp.minimum(g, nu[0]), 0),
                                   memory_space=pltpu.SMEM),
                      pl.BlockSpec((SUBLANES, LANES), lambda g, be, fi, nu: (last(g + 1, nu) + 1, 0),
                                   memory_space=pltpu.SMEM),
                      pl.BlockSpec(memory_space=pl.ANY),
                      pl.BlockSpec((1, d, 2 * d), exp3),
                      pl.BlockSpec((1, 1, 2 * d), exp3),
                      pl.BlockSpec((1, d, d), exp3),
                      pl.BlockSpec((1, 1, d), exp3)],
            out_specs=pl.BlockSpec(memory_space=pl.ANY),
            scratch_shapes=[pltpu.VMEM((rb * ROW_CHUNKS, LANES), F32),
                            pltpu.VMEM((rb * ROW_CHUNKS, LANES), F32),
                            pltpu.VMEM((rb * ROW_CHUNKS, LANES), F32),
                            pltpu.VMEM((rb * ROW_CHUNKS, LANES), F32),
                            pltpu.VMEM((rb, d), BF16),
                            pltpu.VMEM((d, 2 * d), BF16),
                            pltpu.VMEM((d, d), BF16),
                            pltpu.SemaphoreType.DMA((2,)),
                            pltpu.SemaphoreType.DMA((2,))]),
        out_shape=jax.ShapeDtypeStruct((n_slots * ROW_CHUNKS, LANES), F32),
        compiler_params=_cparams(("arbitrary",)),
        name="moe_experts",
    )(block_exp, first, n_used, row_idx, row_idx, h_rows, w_gu, b_gu_grouped, w_dn, b_dn)


def _combine_kernel(gate_ref, x_ref, g2_ref, lng_ref, lnb_ref, yk_ref, o_ref):
    tm = x_ref.shape[0]
    gates = gate_ref[...]
    chunks = []
    for c in range(ROW_CHUNKS):
        yc = jnp.zeros((tm, LANES), F32)
        for k in range(TOP_K):
            yc = yc + gates[:, k:k + 1] * yk_ref[pl.ds(k * ROW_CHUNKS + c, tm, stride=TOP_K * ROW_CHUNKS), :]
        chunks.append(yc)
    y = jnp.concatenate(chunks, axis=1)
    z = DEEPNORM_ALPHA * x_ref[...] + g2_ref[0, 0] * y
    o_ref[...] = _ln(z) * lng_ref[...] + lnb_ref[...]


def _combine(yk, gates, x2, mod4, lng, lnb, seq):
    n, d = x2.shape
    tm = TOK_TILE
    per_b = seq // tm
    return pl.pallas_call(
        _combine_kernel,
        grid=(n // tm,),
        in_specs=[pl.BlockSpec((tm, LANES), lambda i: (i, 0)),
                  pl.BlockSpec((tm, d), lambda i: (i, 0)),
                  pl.BlockSpec((1, 1, 1, d), lambda i: (i // per_b, 5, 0, 0)),
                  pl.BlockSpec((1, d), lambda i: (0, 0)),
                  pl.BlockSpec((1, d), lambda i: (0, 0)),
                  pl.BlockSpec((tm * TOP_K * ROW_CHUNKS, LANES), lambda i: (i, 0))],
        out_specs=pl.BlockSpec((tm, d), lambda i: (i, 0)),
        out_shape=jax.ShapeDtypeStruct((n, d), F32),
        compiler_params=_cparams(("parallel",)),
        name="moe_combine",
    )(gates, x2, mod4, lng, lnb, yk)


def _moe(x2, mod4, rw, rb, w_gu, b_gu, w_dn, b_dn, lng, lnb, seq):
    n, d = x2.shape
    h_rows, meta, gates, cnt = _router(x2, mod4, rw, rb, seq)
    counts = cnt[0, :N_EXPERTS]
    padded = -(-counts // EXPERT_ROWS) * EXPERT_ROWS
    pend = jnp.cumsum(padded)
    pstart = pend - padded
    n_blocks = -(-(n * TOP_K + N_EXPERTS * (EXPERT_ROWS - 1)) // EXPERT_ROWS)
    block_row0 = jnp.arange(n_blocks, dtype=I32) * EXPERT_ROWS
    block_exp = jnp.minimum(jnp.sum((pend[None, :] <= block_row0[:, None]).astype(I32), axis=1), N_EXPERTS - 1)
    first = jnp.concatenate([jnp.ones((1,), I32), (block_exp[1:] != block_exp[:-1]).astype(I32)])
    dest = jnp.take(pstart, meta[:, :TOP_K]) + meta[:, TOP_K:2 * TOP_K]
    rows = n_blocks * EXPERT_ROWS
    n_pairs = n * TOP_K
    r = jnp.arange(rows, dtype=I32)
    scratch_slot = n_pairs + ((r // EXPERT_ROWS) % 2) * EXPERT_ROWS + r % EXPERT_ROWS
    row_slot = scratch_slot.at[dest.reshape(-1)].set(jnp.arange(n_pairs, dtype=I32), unique_indices=True)
    row_tok = jnp.minimum(row_slot // TOP_K, n - 1)
    per_blk = EXPERT_ROWS // LANES
    toks = row_tok.reshape(n_blocks, per_blk, LANES)
    slots = row_slot.reshape(n_blocks, per_blk, LANES)
    dummy_slots = (n_pairs + EXPERT_ROWS + jnp.arange(EXPERT_ROWS, dtype=I32)).reshape(1, per_blk, LANES)
    row_idx = jnp.concatenate([jnp.concatenate([toks[:1], toks], axis=0),
                               jnp.concatenate([dummy_slots, slots], axis=0)], axis=1)
    row_idx = row_idx.reshape((n_blocks + 1) * 2 * per_blk, LANES)
    e = b_gu.shape[0]
    b_grouped = b_gu.reshape(e, d // LANES, LANES, 2).transpose(0, 1, 3, 2).reshape(e, 1, 2 * d)
    n_used = (pend[-1:] // EXPERT_ROWS).astype(I32)
    yk = _experts(h_rows, row_idx, block_exp, first, n_used, w_gu, b_grouped, w_dn, b_dn[:, None, :],
                  n_pairs + 2 * EXPERT_ROWS)
    return _combine(yk, gates, x2, mod4, lng.reshape(1, d), lnb.reshape(1, d), seq)


def _permute_w_in(w):
    d = w.shape[0]
    off = {}
    o = 0
    for name, wd in (("a_q", 256), ("a_k", 256), ("a_v", 256), ("b_q", 384), ("b_kc", 128), ("b_vc", 128),
                     ("b_ks", 128), ("b_vs", 128), ("b_kw", 128), ("b_vw", 128), ("b_gate", 18),
                     ("c_q", 384), ("c_k", 384), ("c_v", 384), ("merge_gate", 3 * D_MODEL)):
        off[name] = (o, o + wd)
        o += wd
    col = lambda name: w[:, off[name][0]:off[name][1]]
    scale = HEAD_DIM ** -0.5 * float(np.log2(np.e))
    b_q = col("b_q").reshape(d, B_KV_GROUPS, B_REP, HEAD_DIM).transpose(0, 2, 1, 3).reshape(d, B_HEADS * HEAD_DIM)
    gate_pad = jnp.zeros((d, LANES - B_HEADS * 3), w.dtype)
    parts = [col("merge_gate"),
             col("a_q") * scale, col("a_k"), b_q * scale, col("b_kc"), col("b_ks"), col("b_kw"),
             col("c_q") * scale, col("c_k"),
             col("a_v"), col("b_vc"), col("b_vs"), col("b_vw"), col("c_v"),
             col("b_gate"), gate_pad]
    return jnp.concatenate(parts, axis=1).astype(BF16)


def _dilate(arr, tile, batch, seq, dil):
    a = arr[:, tile * LANES:(tile + 1) * LANES].reshape(batch, seq // dil, dil, LANES)
    return a.transpose(0, 2, 1, 3).reshape(batch * seq, LANES)


def _undilate(arr, batch, seq, dil):
    a = arr.reshape(batch, dil, seq // dil, LANES)
    return a.transpose(0, 2, 1, 3).reshape(batch * seq, LANES)


def _mixer_layer(x2, mod4, cos_t, sin_t, batch, seq, w_in, cmp_w1_k, cmp_w2_k, cmp_pe_k,
                 cmp_w1_v, cmp_w2_v, cmp_pe_v, w_a, w_b, w_c, w_out, lng, lnb):
    d = D_MODEL
    proj = _inproj(x2, mod4, _permute_w_in(w_in), cos_t, sin_t, seq)

    bias_a = _moba_select(proj, _kmean(proj, batch, seq), batch, seq)
    o_a = _bb_attn(proj, bias_a, batch=batch, seq=seq, n_pairs=A_HEADS // 2,
                   q_tile=lambda p: T_AQ + p, k_tile=lambda p: T_AK + p, v_tile=lambda p: T_AV + p,
                   bias_tile=lambda p: p, blk=MOBA_BLOCK)

    kc = _compress(proj[:, T_BKC * LANES:(T_BKC + 1) * LANES], cmp_w1_k, cmp_w2_k, cmp_pe_k, batch, seq)
    vc = _compress(proj[:, T_BVC * LANES:(T_BVC + 1) * LANES], cmp_w1_v, cmp_w2_v, cmp_pe_v, batch, seq)
    o_cmp, bias_b = _nsa_cmp(proj, kc, vc, batch, seq)
    o_slc = _bb_attn(proj, bias_b, batch=batch, seq=seq, n_pairs=B_REP,
                     q_tile=lambda p: T_BQ + p, k_tile=lambda p: T_BKS, v_tile=lambda p: T_BVS,
                     bias_tile=lambda p: 0, blk=SLC_BLOCK)
    o_win, _ = _band_attn(proj, proj, proj, batch=batch, seq=seq, n_pairs=B_REP,
                          q_tile=lambda p: T_BQ + p, k_tile=lambda p: T_BKW, v_tile=lambda p: T_BVW,
                          window=NSA_WINDOW, t=256)

    oc, lse = [], []
    for gi, (window, dil) in enumerate(DILATED_PAIRS):
        wlen = window // dil + 1
        if dil == 1:
            o, l = _band_attn(proj, proj, proj, batch=batch, seq=seq, n_pairs=1,
                              q_tile=lambda p: T_CQ + gi, k_tile=lambda p: T_CK + gi,
                              v_tile=lambda p: T_CV + gi, window=wlen, t=256)
        else:
            qd = _dilate(proj, T_CQ + gi, batch, seq, dil)
            kd = _dilate(proj, T_CK + gi, batch, seq, dil)
            vd = _dilate(proj, T_CV + gi, batch, seq, dil)
            o, l = _band_attn(qd, kd, vd, batch=batch * dil, seq=seq // dil, n_pairs=1,
                              q_tile=lambda p: 0, k_tile=lambda p: 0, v_tile=lambda p: 0,
                              window=wlen, t=256)
            o, l = _undilate(o, batch, seq, dil), _undilate(l, batch, seq, dil)
        oc.append(o)
        lse.append(l)

    w_b_perm = w_b.reshape(B_KV_GROUPS, B_REP, HEAD_DIM, d).transpose(1, 0, 2, 3).reshape(B_HEADS * HEAD_DIM, d)
    return _merge(o_a, o_cmp, o_slc, o_win, proj, oc, lse, x2, mod4,
                  w_a.astype(BF16), w_b_perm.astype(BF16), w_c.astype(BF16), w_out.astype(BF16),
                  lng.reshape(1, d), lnb.reshape(1, d), seq)


def _rope_tables(positions):
    inv = ROPE_THETA ** (-jnp.arange(0, HEAD_DIM, 2, dtype=F32) / HEAD_DIM)
    ang = positions.astype(F32).reshape(-1, 1) * inv[None, :]
    cos, sin = jnp.cos(ang), jnp.sin(ang)
    return jnp.tile(cos, (1, 4)), jnp.tile(jnp.concatenate([-sin, sin], axis=1), (1, 2))


def kernel(x, c, positions, w_ada, b_ada, w_in, cmp_w1_k, cmp_w2_k, cmp_pe_k, cmp_w1_v, cmp_w2_v, cmp_pe_v, w_branch_a, w_branch_b, w_branch_c, w_out, ln1_g, ln1_b, router_w, router_b, w_gate_up, b_gate_up, w_down, b_down, ln2_g, ln2_b):
    batch, seq, d = x.shape
    depth = w_in.shape[0]
    cos_t, sin_t = _rope_tables(positions)
    mod = _ada(c, w_ada, b_ada)
    x2 = x.reshape(batch * seq, d)
    for l in range(depth):
        mod4 = mod[l].reshape(batch, N_ADA, 1, d)
        x2 = _mixer_layer(x2, mod4, cos_t, sin_t, batch, seq, w_in[l], cmp_w1_k[l], cmp_w2_k[l], cmp_pe_k[l],
                          cmp_w1_v[l], cmp_w2_v[l], cmp_pe_v[l], w_branch_a[l], w_branch_b[l],
                          w_branch_c[l], w_out[l], ln1_g[l], ln1_b[l])
        x2 = _moe(x2, mod4, router_w[l], router_b[l], w_gate_up[l], b_gate_up[l], w_down[l], b_down[l],
                  ln2_g[l], ln2_b[l], seq)
    return x2.reshape(batch, seq, d)
```

```python
import functools

import numpy as np
import jax
import jax.numpy as jnp
from jax import lax
from jax.experimental import pallas as pl
from jax.experimental.pallas import tpu as pltpu

F32 = jnp.float32
BF16 = jnp.bfloat16
I32 = jnp.int32
HIGHEST = lax.Precision.HIGHEST

D_MODEL = 1024
DEPTH = 2
HEAD_DIM = 64
ROPE_THETA = 10000.0
LN_EPS = 1e-5
DEEPNORM_ALPHA = (2 * DEPTH) ** 0.25
N_ADA = 6
A_HEADS = 4
MOBA_BLOCK = 256
MOBA_TOPK = 3
B_HEADS = 6
B_KV_GROUPS = 2
B_REP = 3
CMP_BLOCK = 32
CMP_STRIDE = 16
CMP_HIDDEN = 128
SLC_BLOCK = 64
SLC_TOPK = 16
NSA_WINDOW = 512
FORCE_SCORE = 1e6
DILATED_PAIRS = ((128, 1), (512, 4), (2048, 16))
N_EXPERTS = 32
TOP_K = 4
SWIGLU_ALPHA = 1.702
SWIGLU_LIMIT = 7.0

LANES = 128
SUBLANES = 8
HALF = LANES // 2
NEG = -1e30
ROW_CHUNKS = D_MODEL // LANES
VMEM_LIMIT = 56 * 1024 * 1024

T_MG = 0
T_ROPE0, T_ROPE1 = 24, 40
T_AQ, T_AK, T_BQ, T_BKC, T_BKS, T_BKW, T_CQ, T_CK = 24, 26, 28, 31, 32, 33, 34, 37
T_AV, T_BVC, T_BVS, T_BVW, T_CV, T_BG = 40, 42, 43, 44, 45, 48
N_PROJ_TILES = 49
PROJ_W = N_PROJ_TILES * LANES

EXPERT_ROWS = 512
TOK_TILE = 256


def _cparams(sem, **kw):
    return pltpu.CompilerParams(dimension_semantics=sem, vmem_limit_bytes=VMEM_LIMIT, **kw)


def _lane_iota(shape=(1, LANES)):
    return lax.broadcasted_iota(I32, shape, len(shape) - 1)


def _sigmoid(x):
    return 1.0 / (1.0 + jnp.exp(-x))


def _ln(x):
    mu = jnp.mean(x, axis=-1, keepdims=True)
    xc = x - mu
    var = jnp.mean(xc * xc, axis=-1, keepdims=True)
    return xc * lax.rsqrt(var + LN_EPS)


def _dot_nt(a, b, precision=None):
    return lax.dot_general(a, b, (((1,), (1,)), ((), ())), precision=precision,
                           preferred_element_type=F32)


def _ada_kernel(c_ref, w_ref, b_ref, o_ref):
    c = c_ref[...]
    cond = c * _sigmoid(c)
    o_ref[0] = jnp.dot(cond, w_ref[0], precision=HIGHEST, preferred_element_type=F32) + b_ref[0]


def _ada(c, w_ada, b_ada):
    depth, d, n = w_ada.shape
    b = c.shape[0]
    tn = 1536
    return pl.pallas_call(
        _ada_kernel,
        grid=(depth, n // tn),
        in_specs=[pl.BlockSpec((b, d), lambda l, j: (0, 0)),
                  pl.BlockSpec((1, d, tn), lambda l, j: (l, 0, j)),
                  pl.BlockSpec((1, 1, tn), lambda l, j: (l, 0, j))],
        out_specs=pl.BlockSpec((1, b, tn), lambda l, j: (l, 0, j)),
        out_shape=jax.ShapeDtypeStruct((depth, b, n), F32),
        compiler_params=_cparams(("parallel", "parallel")),
        name="ada",
    )(c, w_ada, b_ada.reshape(depth, 1, n))


def _inproj_kernel(x_ref, sc_ref, sh_ref, w_ref, cos_ref, sin_ref, o_ref, *, chunks):
    h = _ln(x_ref[...]) * (1.0 + sc_ref[0, 0]) + sh_ref[0, 0]
    hb = h.astype(BF16)
    first_half = (_lane_iota() & (HEAD_DIM - 1)) < (HEAD_DIM // 2)
    for c0, cw, rope in chunks:
        acc = jnp.dot(hb, w_ref[:, c0:c0 + cw], preferred_element_type=F32)
        if rope:
            cos = cos_ref[...]
            sin = sin_ref[...]
            for t in range(cw // LANES):
                a = acc[:, t * LANES:(t + 1) * LANES]
                rot = jnp.where(first_half, pltpu.roll(a, LANES - HEAD_DIM // 2, 1),
                                pltpu.roll(a, HEAD_DIM // 2, 1))
                o_ref[:, c0 + t * LANES:c0 + (t + 1) * LANES] = (a * cos + rot * sin).astype(o_ref.dtype)
        else:
            o_ref[:, c0:c0 + cw] = acc.astype(o_ref.dtype)


def _inproj_chunks():
    chunks = []
    for lo, hi, rope in ((0, T_ROPE0, False), (T_ROPE0, T_ROPE1, True), (T_ROPE1, N_PROJ_TILES, False)):
        c = lo * LANES
        while c < hi * LANES:
            cw = min(512, hi * LANES - c)
            chunks.append((c, cw, rope))
            c += cw
    return tuple(chunks)


def _inproj(x2, mod4, w_perm, cos_t, sin_t, seq):
    n, d = x2.shape
    tm = 256
    per_b = seq // tm
    return pl.pallas_call(
        functools.partial(_inproj_kernel, chunks=_inproj_chunks()),
        grid=(n // tm,),
        in_specs=[pl.BlockSpec((tm, d), lambda i: (i, 0)),
                  pl.BlockSpec((1, 1, 1, d), lambda i: (i // per_b, 1, 0, 0)),
                  pl.BlockSpec((1, 1, 1, d), lambda i: (i // per_b, 0, 0, 0)),
                  pl.BlockSpec((d, PROJ_W), lambda i: (0, 0)),
                  pl.BlockSpec((tm, LANES), lambda i: (i, 0)),
                  pl.BlockSpec((tm, LANES), lambda i: (i, 0))],
        out_specs=pl.BlockSpec((tm, PROJ_W), lambda i: (i, 0)),
        out_shape=jax.ShapeDtypeStruct((n, PROJ_W), BF16),
        compiler_params=_cparams(("parallel",)),
        name="inproj",
    )(x2, mod4, mod4, w_perm, cos_t, sin_t)


def _bb_attn_kernel(q_ref, k_ref, v_ref, b_ref, o_ref, qa_sc, s_sc, mp_sc, m_sc, acc_sc,
                    *, blk_shift, tq, big):
    i = pl.program_id(2)
    lane = _lane_iota()
    lo = lane < HALF
    q = q_ref[...]
    b = b_ref[...]
    qa_sc[0] = jnp.where(lo, q, b)
    qa_sc[1] = jnp.where(lo, b, q)
    mp_sc[...] = jnp.full(mp_sc.shape, -jnp.inf, F32)
    acc_sc[...] = jnp.zeros(acc_sc.shape, F32)

    def scores(u0, n_units, diagonal_last):
        tk = n_units * tq
        start = pl.multiple_of(u0 * tq, tq)
        k = k_ref[pl.ds(start, tk), :]
        kblk = (start + lax.broadcasted_iota(I32, (tk, 1), 0)) >> blk_shift
        onehot = jnp.where(kblk == (lane & (HALF - 1)), 1.0, 0.0).astype(BF16)
        ka = (jnp.where(lo, k, onehot), jnp.where(lo, onehot, k))
        causal = lax.broadcasted_iota(I32, (1, tq), 1) <= lax.broadcasted_iota(I32, (tq, 1), 0)
        for h in range(2):
            s = _dot_nt(qa_sc[h], ka[h])
            mp = mp_sc[h]
            for u in range(n_units):
                su = s[:, u * tq:(u + 1) * tq]
                if diagonal_last and u == n_units - 1:
                    su = jnp.where(causal, su, NEG)
                s_sc[h, u0 + u] = su
                for c in range(tq // LANES):
                    mp = jnp.maximum(mp, su[:, c * LANES:(c + 1) * LANES])
            mp_sc[h] = mp

    def weighted(u0, n_units):
        tk = n_units * tq
        start = pl.multiple_of(u0 * tq, tq)
        v = v_ref[pl.ds(start, tk), :]
        one = jnp.ones_like(v)
        vs = (jnp.where(lo, v, one), jnp.where(lo, one, v))
        for h in range(2):
            m_row = m_sc[h]
            ps = []
            for u in range(n_units):
                su = s_sc[h, u0 + u]
                for c in range(tq // LANES):
                    ps.append(jnp.exp2(su[:, c * LANES:(c + 1) * LANES] - m_row).astype(BF16))
            acc_sc[h] = acc_sc[h] + jnp.dot(jnp.concatenate(ps, axis=1), vs[h], preferred_element_type=F32)

    n_big = i // big
    n_tail = i - n_big * big + 1

    def run(fn):
        def big_body(t, c):
            fn(t * big, big, False) if fn is scores else fn(t * big, big)
            return c

        lax.fori_loop(0, n_big, big_body, 0)
        for n in range(1, big + 1):
            @pl.when(n_tail == n)
            def _(n=n):
                fn(n_big * big, n, True) if fn is scores else fn(n_big * big, n)

    run(scores)
    for h in range(2):
        m_sc[h] = jnp.broadcast_to(jnp.max(mp_sc[h], axis=1, keepdims=True), (tq, LANES))
    run(weighted)
    outs = [acc_sc[h] / pltpu.roll(acc_sc[h], HALF, 1) for h in range(2)]
    o_ref[...] = jnp.where(lo, outs[0], outs[1]).astype(o_ref.dtype)


def _bb_attn(proj, bias, *, batch, seq, n_pairs, q_tile, k_tile, v_tile, bias_tile, blk):
    t = 256
    nq = seq // t
    n = batch * seq
    kern = functools.partial(_bb_attn_kernel, blk_shift=int(np.log2(blk)), tq=t, big=4)
    return pl.pallas_call(
        kern,
        grid=(batch, n_pairs, nq),
        in_specs=[pl.BlockSpec((t, LANES), lambda b, p, i: (b * nq + i, q_tile(p))),
                  pl.BlockSpec((seq, LANES), lambda b, p, i: (b, k_tile(p))),
                  pl.BlockSpec((seq, LANES), lambda b, p, i: (b, v_tile(p))),
                  pl.BlockSpec((t, LANES), lambda b, p, i: (b * nq + i, bias_tile(p)))],
        out_specs=pl.BlockSpec((t, LANES), lambda b, p, i: (b * nq + i, p)),
        out_shape=jax.ShapeDtypeStruct((n, n_pairs * LANES), BF16),
        scratch_shapes=[pltpu.VMEM((2, t, LANES), BF16),
                        pltpu.VMEM((2, nq, t, t), F32),
                        pltpu.VMEM((2, t, LANES), F32),
                        pltpu.VMEM((2, t, LANES), F32),
                        pltpu.VMEM((2, t, LANES), F32)],
        compiler_params=_cparams(("parallel", "parallel", "parallel")),
        name="bb_attn_%d" % blk,
    )(proj, proj, proj, bias)


def _band_kernel(q_ref, k_ref, v_ref, o_ref, lse_ref, *, tq, kw, window, seq):
    i = pl.program_id(2)
    lane = _lane_iota()
    lo = lane < HALF
    start = pl.multiple_of(jnp.clip(i * tq - (kw - tq), 0, seq - kw), LANES)
    q = q_ref[...]
    k = k_ref[pl.ds(start, kw), :]
    v = v_ref[pl.ds(start, kw), :]
    zero = jnp.zeros_like(q)
    qs = (jnp.where(lo, q, zero), jnp.where(lo, zero, q))
    rel = (i * tq + lax.broadcasted_iota(I32, (tq, 1), 0)) - (start + lax.broadcasted_iota(I32, (1, kw), 1))
    ok = (rel >= 0) & (rel < window)
    outs, lses = [], []
    for h in range(2):
        s = jnp.where(ok, _dot_nt(qs[h], k), NEG)
        m = jnp.max(s, axis=1, keepdims=True)
        p = jnp.exp2(s - m)
        l = jnp.sum(p, axis=1, keepdims=True)
        outs.append(jnp.dot(p.astype(BF16), v, preferred_element_type=F32) / l)
        lses.append(m + jnp.log2(l))
    o_ref[...] = jnp.where(lo, outs[0], outs[1]).astype(o_ref.dtype)
    lse_ref[...] = jnp.where(lo, lses[0], lses[1])


def _band_attn(q_arr, k_arr, v_arr, *, batch, seq, n_pairs, q_tile, k_tile, v_tile, window, t):
    t = min(t, seq)
    nq = seq // t
    kw = min(seq, t + -(-(window - 1) // LANES) * LANES)
    n = batch * seq
    kern = functools.partial(_band_kernel, tq=t, kw=kw, window=window, seq=seq)
    return pl.pallas_call(
        kern,
        grid=(batch, n_pairs, nq),
        in_specs=[pl.BlockSpec((t, LANES), lambda b, p, i: (b * nq + i, q_tile(p))),
                  pl.BlockSpec((seq, LANES), lambda b, p, i: (b, k_tile(p))),
                  pl.BlockSpec((seq, LANES), lambda b, p, i: (b, v_tile(p)))],
        out_specs=[pl.BlockSpec((t, LANES), lambda b, p, i: (b * nq + i, p)),
                   pl.BlockSpec((t, LANES), lambda b, p, i: (b * nq + i, p))],
        out_shape=[jax.ShapeDtypeStruct((n, n_pairs * LANES), BF16),
                   jax.ShapeDtypeStruct((n, n_pairs * LANES), F32)],
        compiler_params=_cparams(("parallel", "parallel", "parallel")),
        name="band_attn_%d" % window,
    )(q_arr, k_arr, v_arr)


def _kmean_kernel(k_ref, o_ref, *, nb):
    s = k_ref.shape[0]
    blk = lax.broadcasted_iota(I32, (nb, s), 1) >> int(np.log2(MOBA_BLOCK))
    avg = jnp.where(blk == lax.broadcasted_iota(I32, (nb, s), 0), 1.0 / MOBA_BLOCK, 0.0).astype(BF16)
    o_ref[0] = jnp.dot(avg, k_ref[...], preferred_element_type=F32)


def _kmean(proj, batch, seq):
    nb = seq // MOBA_BLOCK
    w = A_HEADS * HEAD_DIM
    return pl.pallas_call(
        functools.partial(_kmean_kernel, nb=nb),
        grid=(batch,),
        in_specs=[pl.BlockSpec((seq, w), lambda b: (b, T_AK * LANES // w))],
        out_specs=pl.BlockSpec((1, nb, w), lambda b: (b, 0, 0)),
        out_shape=jax.ShapeDtypeStruct((batch, nb, w), F32),
        compiler_params=_cparams(("parallel",)),
        name="moba_kmean",
    )(proj)


def _rank_desc(g, n_idx, n):
    rank = jnp.zeros(g.shape, I32)
    for m in range(n):
        c = g[m:m + 1, :]
        beats = (c > g) | ((c == g) & (n_idx > m))
        rank = rank + jnp.where(beats, 1, 0)
    return rank


def _moba_sel_kernel(q_ref, km_ref, b_ref, *, nb, n_sel):
    i = pl.program_id(2)
    t = q_ref.shape[0]
    q = q_ref[...].astype(F32)
    km = km_ref[0]
    lo = _lane_iota() < HALF
    zero = jnp.zeros_like(km)
    pad = jnp.zeros((HALF - nb, LANES), F32)
    kmt = jnp.concatenate([jnp.where(lo, zero, km), pad, jnp.where(lo, km, zero), pad], axis=0)
    gt = _dot_nt(kmt, q, precision=HIGHEST)
    n_idx = lax.broadcasted_iota(I32, (nb, 1), 0)
    valid = n_idx < i
    rows = []
    for r0 in (0, HALF):
        g = jnp.where(valid, gt[r0:r0 + nb, :], -jnp.inf)
        rank = _rank_desc(g, n_idx, nb)
        allowed = (valid & (rank < n_sel)) | (n_idx == i)
        rows.append(jnp.where(allowed, 0.0, NEG))
        rows.append(jnp.zeros((HALF - nb, t), F32))
    b_ref[...] = jnp.concatenate(rows, axis=0).T.astype(b_ref.dtype)


def _moba_select(proj, kmean, batch, seq):
    t = MOBA_BLOCK
    nq = seq // t
    nb = seq // MOBA_BLOCK
    n_sel = min(MOBA_TOPK, nb - 1)
    n_pairs = A_HEADS // 2
    return pl.pallas_call(
        functools.partial(_moba_sel_kernel, nb=nb, n_sel=n_sel),
        grid=(batch, n_pairs, nq),
        in_specs=[pl.BlockSpec((t, LANES), lambda b, p, i: (b * nq + i, T_AQ + p)),
                  pl.BlockSpec((1, nb, LANES), lambda b, p, i: (b, 0, p))],
        out_specs=pl.BlockSpec((t, LANES), lambda b, p, i: (b * nq + i, p)),
        out_shape=jax.ShapeDtypeStruct((batch * seq, n_pairs * LANES), BF16),
        compiler_params=_cparams(("parallel", "parallel", "parallel")),
        name="moba_select",
    )(proj, kmean)


def _compress_kernel(x_ref, w1_ref, pe_ref, w1f_ref, w2_ref, o_ref):
    x = x_ref[0]
    nblk = x.shape[0]
    outs = []
    for g in range(B_KV_GROUPS):
        u = jnp.dot(x, w1_ref[g, 0], preferred_element_type=F32)
        v = jnp.dot(x, w1_ref[g, 1], preferred_element_type=F32)
        pe_h = jnp.dot(pe_ref[...], w1f_ref[...], precision=HIGHEST, preferred_element_type=F32)
        hid = u + pltpu.roll(v, nblk - 1, 0) + pe_h[:1]
        hid = hid * _sigmoid(hid)
        outs.append(jnp.dot(hid, w2_ref[g], precision=HIGHEST, preferred_element_type=F32))
    o = outs[0] + outs[1]
    rows = lax.broadcasted_iota(I32, (nblk, 1), 0)
    o_ref[0] = jnp.where(rows < nblk - 1, o, 0.0).astype(o_ref.dtype)


def _compress(xt, w1, w2, pe, batch, seq):
    nblk = seq // CMP_STRIDE
    xg = xt.reshape(batch, nblk, CMP_STRIDE * LANES)
    w1r = w1.reshape(2, CMP_STRIDE, HEAD_DIM, CMP_HIDDEN)
    w1e = jnp.zeros((B_KV_GROUPS, 2, CMP_STRIDE, B_KV_GROUPS, HEAD_DIM, CMP_HIDDEN), F32)
    for g in range(B_KV_GROUPS):
        w1e = w1e.at[g, :, :, g].set(w1r)
    w1e = w1e.reshape(B_KV_GROUPS, 2, CMP_STRIDE * LANES, CMP_HIDDEN).astype(BF16)
    pe_flat = jnp.broadcast_to(pe.reshape(1, CMP_BLOCK * HEAD_DIM), (SUBLANES, CMP_BLOCK * HEAD_DIM))
    w2e = jnp.zeros((B_KV_GROUPS, CMP_HIDDEN, LANES), F32)
    for g in range(B_KV_GROUPS):
        w2e = w2e.at[g, :, g * HEAD_DIM:(g + 1) * HEAD_DIM].set(w2)
    return pl.pallas_call(
        _compress_kernel,
        grid=(batch,),
        in_specs=[pl.BlockSpec((1, nblk, CMP_STRIDE * LANES), lambda b: (b, 0, 0)),
                  pl.BlockSpec(w1e.shape, lambda b: (0, 0, 0, 0)),
                  pl.BlockSpec(pe_flat.shape, lambda b: (0, 0)),
                  pl.BlockSpec(w1.shape, lambda b: (0, 0)),
                  pl.BlockSpec(w2e.shape, lambda b: (0, 0, 0))],
        out_specs=pl.BlockSpec((1, nblk, LANES), lambda b: (b, 0, 0)),
        out_shape=jax.ShapeDtypeStruct((batch, nblk, LANES), BF16),
        compiler_params=_cparams(("parallel",)),
        name="nsa_compress",
    )(xg, w1e, pe_flat, w1, w2e)


def _nsa_cmp_kernel(q0_ref, q1_ref, q2_ref, kc_ref, vc_ref, ov_ref, o_ref, b_ref, *, n_sel, ns):
    i = pl.program_id(1)
    t = q0_ref.shape[0]
    ncp = kc_ref.shape[1]
    lane = _lane_iota()
    lo = lane < HALF
    kc = kc_ref[0]
    vc = vc_ref[0]
    tq = i * t + lax.broadcasted_iota(I32, (t, 1), 0)
    cmp_end = lax.broadcasted_iota(I32, (1, ncp), 1) * CMP_STRIDE + (CMP_BLOCK - 1)
    vis = cmp_end <= tq
    psum = [jnp.zeros((t, ncp), F32) for _ in range(B_KV_GROUPS)]
    for r, q_ref in enumerate((q0_ref, q1_ref, q2_ref)):
        q = q_ref[...]
        zero = jnp.zeros_like(q)
        outs = []
        for g in range(B_KV_GROUPS):
            qg = jnp.where(lo, q, zero) if g == 0 else jnp.where(lo, zero, q)
            s = jnp.where(vis, _dot_nt(qg, kc), NEG)
            m = jnp.max(s, axis=1, keepdims=True)
            m = jnp.where(m > 0.5 * NEG, m, 0.0)
            e = jnp.where(vis, jnp.exp2(s - m), 0.0)
            p = e / jnp.maximum(jnp.sum(e, axis=1, keepdims=True), 1e-30)
            psum[g] = psum[g] + p
            outs.append(jnp.dot(p.astype(BF16), vc, preferred_element_type=F32))
        o_ref[:, r * LANES:(r + 1) * LANES] = jnp.where(lo, outs[0], outs[1]).astype(o_ref.dtype)
    n_idx = lax.broadcasted_iota(I32, (ns, 1), 0)
    tcol = i * t + lax.broadcasted_iota(I32, (1, t), 1)
    qblk = tcol >> int(np.log2(SLC_BLOCK))
    forced = (n_idx == 0) | (n_idx == qblk) | (n_idx == qblk - 1)
    valid = n_idx <= qblk
    rows = []
    for g in (1, 0):
        imp = _dot_nt(ov_ref[...], psum[g], precision=HIGHEST)
        imp = jnp.where(forced, FORCE_SCORE, imp)
        imp = jnp.where(valid, imp, -jnp.inf)
        rank = _rank_desc(imp, n_idx, ns)
        allowed = valid & (rank < n_sel)
        rows.append(jnp.where(allowed, 0.0, NEG))
        if ns < HALF:
            rows.append(jnp.zeros((HALF - ns, t), F32))
    b_ref[...] = jnp.concatenate(rows, axis=0).T.astype(b_ref.dtype)


def _nsa_cmp(proj, kc, vc, batch, seq):
    t = 256
    nq = seq // t
    ns = seq // SLC_BLOCK
    n_sel = min(SLC_TOPK, ns)
    ncp = seq // CMP_STRIDE
    cs = np.arange(ncp)[None, :] * CMP_STRIDE
    ss = np.arange(ns)[:, None] * SLC_BLOCK
    ov = ((cs < ss + SLC_BLOCK) & (cs + CMP_BLOCK > ss)).astype(np.float32)
    ov[:, ncp - 1] = 0.0
    n = batch * seq
    qspec = [pl.BlockSpec((t, LANES), (lambda b, i, r=r: (b * nq + i, T_BQ + r))) for r in range(B_REP)]
    return pl.pallas_call(
        functools.partial(_nsa_cmp_kernel, n_sel=n_sel, ns=ns),
        grid=(batch, nq),
        in_specs=qspec + [pl.BlockSpec((1, ncp, LANES), lambda b, i: (b, 0, 0)),
                          pl.BlockSpec((1, ncp, LANES), lambda b, i: (b, 0, 0)),
                          pl.BlockSpec((ns, ncp), lambda b, i: (0, 0))],
        out_specs=[pl.BlockSpec((t, B_REP * LANES), lambda b, i: (b * nq + i, 0)),
                   pl.BlockSpec((t, LANES), lambda b, i: (b * nq + i, 0))],
        out_shape=[jax.ShapeDtypeStruct((n, B_REP * LANES), BF16),
                   jax.ShapeDtypeStruct((n, LANES), BF16)],
        compiler_params=_cparams(("parallel", "parallel")),
        name="nsa_cmp_select",
    )(proj, proj, proj, kc, vc, jnp.asarray(ov))


def _merge_kernel(oa_ref, ocmp_ref, oslc_ref, owin_ref, bg_ref, oc0_ref, oc1_ref, oc2_ref,
                  l0_ref, l1_ref, l2_ref, mg_ref, x_ref, g1_ref, wa_ref, wb_ref, wc_ref, wo_ref,
                  eg_ref, lng_ref, lnb_ref, o_ref):
    d = D_MODEL
    sg = _sigmoid(bg_ref[...].astype(F32))
    ob = jnp.zeros(ocmp_ref.shape, F32)
    for br, ref in enumerate((ocmp_ref, oslc_ref, owin_ref)):
        gexp = jnp.dot(sg, eg_ref[br], precision=HIGHEST, preferred_element_type=F32)
        ob = ob + gexp * ref[...].astype(F32)
    l0, l1, l2 = l0_ref[...], l1_ref[...], l2_ref[...]
    mx = jnp.maximum(jnp.maximum(l0, l1), l2)
    e0, e1, e2 = jnp.exp2(l0 - mx), jnp.exp2(l1 - mx), jnp.exp2(l2 - mx)
    den = e0 + e1 + e2
    oc = ((e0 / den) * oc0_ref[...].astype(F32) + (e1 / den) * oc1_ref[...].astype(F32)
          + (e2 / den) * oc2_ref[...].astype(F32))
    pa = jnp.dot(oa_ref[...], wa_ref[...], preferred_element_type=F32)
    pb = jnp.dot(ob.astype(BF16), wb_ref[...], preferred_element_type=F32)
    pc = jnp.dot(oc.astype(BF16), wc_ref[...], preferred_element_type=F32)
    merged = (_sigmoid(mg_ref[:, 0:d].astype(F32)) * pa
              + _sigmoid(mg_ref[:, d:2 * d].astype(F32)) * pb
              + _sigmoid(mg_ref[:, 2 * d:3 * d].astype(F32)) * pc)
    y = jnp.dot(merged.astype(BF16), wo_ref[...], preferred_element_type=F32)
    z = DEEPNORM_ALPHA * x_ref[...] + g1_ref[0, 0] * y
    o_ref[...] = _ln(z) * lng_ref[...] + lnb_ref[...]


def _gate_expand():
    eg = np.zeros((3, LANES, B_HEADS * HEAD_DIM), np.float32)
    for g in range(B_KV_GROUPS):
        for r in range(B_REP):
            for br in range(3):
                c0 = (r * B_KV_GROUPS + g) * HEAD_DIM
                eg[br, (g * B_REP + r) * 3 + br, c0:c0 + HEAD_DIM] = 1.0
    return jnp.asarray(eg)


def _merge(o_a, o_cmp, o_slc, o_win, proj, oc, lse, x2, mod4, wa, wb, wc, wo, lng, lnb, seq):
    n, d = x2.shape
    tm = 256
    per_b = seq // tm
    row = lambda w: pl.BlockSpec((tm, w), lambda i: (i, 0))
    full = lambda a: pl.BlockSpec(a.shape, lambda i: (0,) * a.ndim)
    eg = _gate_expand()
    return pl.pallas_call(
        _merge_kernel,
        grid=(n // tm,),
        in_specs=[row(o_a.shape[1]), row(o_cmp.shape[1]), row(o_slc.shape[1]), row(o_win.shape[1]),
                  pl.BlockSpec((tm, LANES), lambda i: (i, T_BG)),
                  row(LANES), row(LANES), row(LANES), row(LANES), row(LANES), row(LANES),
                  pl.BlockSpec((tm, 3 * d), lambda i: (i, T_MG)),
                  row(d),
                  pl.BlockSpec((1, 1, 1, d), lambda i: (i // per_b, 2, 0, 0)),
                  full(wa), full(wb), full(wc), full(wo), full(eg), full(lng), full(lnb)],
        out_specs=row(d),
        out_shape=jax.ShapeDtypeStruct((n, d), F32),
        compiler_params=_cparams(("parallel",)),
        name="merge_out",
    )(o_a, o_cmp, o_slc, o_win, proj, oc[0], oc[1], oc[2], lse[0], lse[1], lse[2],
      proj, x2, mod4, wa, wb, wc, wo, eg, lng, lnb)


def _router_kernel(x_ref, sc_ref, sh_ref, rw_ref, rb_ref, h_ref, meta_ref, gate_ref, cnt_ref, carry_sc):
    i = pl.program_id(0)
    tm = x_ref.shape[0]

    @pl.when(i == 0)
    def _():
        carry_sc[...] = jnp.zeros(carry_sc.shape, F32)

    h = _ln(x_ref[...]) * (1.0 + sc_ref[0, 0]) + sh_ref[0, 0]
    for c in range(ROW_CHUNKS):
        h_ref[pl.ds(c, tm, stride=ROW_CHUNKS), :] = h[:, c * LANES:(c + 1) * LANES]
    lg = jnp.dot(h, rw_ref[...], precision=HIGHEST, preferred_element_type=F32) + rb_ref[...]
    lane = _lane_iota()
    lane_f = lane.astype(F32)
    onehots, vals, idxs = [], [], []
    for _ in range(TOP_K):
        m = jnp.max(lg, axis=1, keepdims=True)
        idx = jnp.min(jnp.where(lg == m, lane_f, float(LANES)), axis=1, keepdims=True).astype(I32)
        oh = lane == idx
        onehots.append(oh)
        vals.append(m)
        idxs.append(idx)
        lg = jnp.where(oh, -jnp.inf, lg)
    es = [jnp.exp(v - vals[0]) for v in vals]
    den = es[0] + es[1] + es[2] + es[3]
    cnt = jnp.zeros((tm, LANES), F32)
    for oh in onehots:
        cnt = cnt + jnp.where(oh, 1.0, 0.0)
    tri = jnp.where(lax.broadcasted_iota(I32, (tm, tm), 0) > lax.broadcasted_iota(I32, (tm, tm), 1), 1.0, 0.0)
    before = jnp.dot(tri.astype(BF16), cnt.astype(BF16), preferred_element_type=F32) + carry_sc[...]
    meta = jnp.zeros((tm, LANES), I32)
    gates = jnp.zeros((tm, LANES), F32)
    for k in range(TOP_K):
        rank = jnp.sum(jnp.where(onehots[k], before, 0.0), axis=1, keepdims=True).astype(I32)
        meta = jnp.where(lane == k, idxs[k], meta)
        meta = jnp.where(lane == TOP_K + k, rank, meta)
        gates = jnp.where(lane == k, es[k] / den, gates)
    meta_ref[...] = meta
    gate_ref[...] = gates
    carry_sc[...] = carry_sc[...] + jnp.sum(cnt, axis=0, keepdims=True)
    cnt_ref[...] = jnp.broadcast_to(carry_sc[...], cnt_ref.shape).astype(I32)


def _router(x2, mod4, rw, rb, seq):
    n, d = x2.shape
    tm = TOK_TILE
    per_b = seq // tm
    rw_p = jnp.zeros((d, LANES), F32).at[:, :N_EXPERTS].set(rw)
    rb_p = jnp.full((1, LANES), NEG, F32).at[0, :N_EXPERTS].set(rb)
    return pl.pallas_call(
        _router_kernel,
        grid=(n // tm,),
        in_specs=[pl.BlockSpec((tm, d), lambda i: (i, 0)),
                  pl.BlockSpec((1, 1, 1, d), lambda i: (i // per_b, 4, 0, 0)),
                  pl.BlockSpec((1, 1, 1, d), lambda i: (i // per_b, 3, 0, 0)),
                  pl.BlockSpec((d, LANES), lambda i: (0, 0)),
                  pl.BlockSpec((1, LANES), lambda i: (0, 0))],
        out_specs=[pl.BlockSpec((tm * ROW_CHUNKS, LANES), lambda i: (i, 0)),
                   pl.BlockSpec((tm, LANES), lambda i: (i, 0)),
                   pl.BlockSpec((tm, LANES), lambda i: (i, 0)),
                   pl.BlockSpec((SUBLANES, LANES), lambda i: (0, 0))],
        out_shape=[jax.ShapeDtypeStruct((n * ROW_CHUNKS, LANES), F32),
                   jax.ShapeDtypeStruct((n, LANES), I32),
                   jax.ShapeDtypeStruct((n, LANES), F32),
                   jax.ShapeDtypeStruct((SUBLANES, LANES), I32)],
        scratch_shapes=[pltpu.VMEM((1, LANES), F32)],
        compiler_params=_cparams(("arbitrary",)),
        name="router",
    )(x2, mod4, mod4, rw_p, rb_p)


def _row_slice(ref, r):
    return ref.at[pl.ds(pl.multiple_of(r * ROW_CHUNKS, ROW_CHUNKS), ROW_CHUNKS)]


def _wait_rows(src_like, dst_like, sem, n_rows):
    span = pl.ds(0, n_rows * ROW_CHUNKS)
    pltpu.make_async_copy(src_like.at[span], dst_like.at[span], sem).wait()


def _expert_kernel(be_ref, first_ref, used_ref, idx_prev_ref, idx_next_ref, h_hbm, wgu_ref, bgu_ref, wd_ref,
                   bd_ref, yk_hbm, xb0, xb1, yb0, yb1, x_sc, wgu_sc, wd_sc, gsem, ssem):
    del be_ref
    s = pl.program_id(0)
    n_used = used_ref[0]
    rows = x_sc.shape[0]
    per_blk = rows // LANES
    xbufs, ybufs = (xb0, xb1), (yb0, yb1)

    def gather_one(i_ref, slot, j, priority=0):
        tok = i_ref[j // LANES, j % LANES]
        pltpu.make_async_copy(_row_slice(h_hbm, tok), _row_slice(xbufs[slot], j),
                              gsem.at[slot]).start(priority=priority)

    def scatter_one(slot, j, priority=0):
        dst = idx_prev_ref[per_blk + j // LANES, j % LANES]
        pltpu.make_async_copy(_row_slice(ybufs[slot], j), _row_slice(yk_hbm, dst),
                              ssem.at[slot]).start(priority=priority)

    def rolled(fn):
        def body(j, c):
            fn(j)
            return c
        lax.fori_loop(0, rows, body, 0)

    @pl.when(s == 0)
    def _():
        rolled(lambda j: gather_one(idx_prev_ref, 0, j))

    for p in range(2):
        @pl.when((s < n_used) & (s % 2 == p))
        def _(p=p):
            q = 1 - p

            @pl.when(s == 0)
            def _():
                yb1[...] = jnp.zeros(yb1.shape, F32)
                even_scratch = yk_hbm.at[pl.ds(yk_hbm.shape[0] - 2 * rows * ROW_CHUNKS, rows * ROW_CHUNKS)]
                fill = pltpu.make_async_copy(yb1, even_scratch, ssem.at[0])
                fill.start()
                fill.wait()

            _wait_rows(h_hbm, xbufs[p], gsem.at[p], rows)

            @pl.when(s >= 1)
            def _():
                _wait_rows(ybufs[p], yk_hbm, ssem.at[p], rows)

            def side_work():
                for j in range(rows):
                    gather_one(idx_next_ref, q, j, priority=j % 2)
                    scatter_one(q, j, priority=(j + 1) % 2)

            _expert_block(s, first_ref, xbufs[p], wgu_ref, bgu_ref, wd_ref, bd_ref, ybufs[p],
                          x_sc, wgu_sc, wd_sc, side_work=side_work)

        @pl.when((s == n_used) & (s % 2 == p))
        def _(p=p):
            q = 1 - p
            _wait_rows(h_hbm, xbufs[p], gsem.at[p], rows)
            _wait_rows(ybufs[p], yk_hbm, ssem.at[p], rows)
            rolled(lambda j: scatter_one(q, j))
            _wait_rows(ybufs[q], yk_hbm, ssem.at[q], rows)


def _expert_block(g, first_ref, xs_ref, wgu_ref, bgu_ref, wd_ref, bd_ref, ys_ref, x_sc, wgu_sc, wd_sc,
                  side_work=None):
    rows = x_sc.shape[0]
    grp = 2 * LANES

    @pl.when(first_ref[g] == 1)
    def _():
        r = lax.broadcasted_iota(I32, (grp, grp), 0)
        c = lax.broadcasted_iota(I32, (grp, grp), 1)
        src = jnp.where(c < LANES, 2 * c, 2 * (c - LANES) + 1)
        sel = jnp.where(r == src, 1.0, 0.0).astype(BF16)
        for j in range(wgu_sc.shape[1] // grp):
            blk = wgu_ref[0, :, j * grp:(j + 1) * grp].astype(BF16)
            wgu_sc[:, j * grp:(j + 1) * grp] = jnp.dot(blk, sel, preferred_element_type=F32).astype(BF16)
        wd_sc[...] = wd_ref[0].astype(BF16)

    if side_work is not None:
        side_work()
    for c in range(ROW_CHUNKS):
        x_sc[:, c * LANES:(c + 1) * LANES] = xs_ref[pl.ds(c, rows, stride=ROW_CHUNKS), :].astype(BF16)
    gu = jnp.dot(x_sc[...], wgu_sc[...], preferred_element_type=F32) + bgu_ref[0]
    acts = []
    for j in range(gu.shape[1] // grp):
        glu = jnp.minimum(gu[:, j * grp:j * grp + LANES], SWIGLU_LIMIT)
        lin = jnp.clip(gu[:, j * grp + LANES:(j + 1) * grp], -SWIGLU_LIMIT, SWIGLU_LIMIT)
        acts.append((glu * _sigmoid(SWIGLU_ALPHA * glu) * (lin + 1.0)).astype(BF16))
    act = jnp.concatenate(acts, axis=1)
    y = jnp.dot(act, wd_sc[...], preferred_element_type=F32) + bd_ref[0]
    for c in range(ROW_CHUNKS):
        ys_ref[pl.ds(c, rows, stride=ROW_CHUNKS), :] = y[:, c * LANES:(c + 1) * LANES]


def _experts(h_rows, row_idx, block_exp, first, n_used, w_gu, b_gu_grouped, w_dn, b_dn, n_slots):
    d = D_MODEL
    n_blocks = block_exp.shape[0]
    rb = EXPERT_ROWS
    assert 2 * rb == SUBLANES * LANES
    last = lambda g, nu: jnp.minimum(g, nu[0] - 1)
    exp3 = lambda g, be, fi, nu: (be[last(g, nu)], 0, 0)
    return pl.pallas_call(
        _expert_kernel,
        grid_spec=pltpu.PrefetchScalarGridSpec(
            num_scalar_prefetch=3,
            grid=(n_blocks + 1,),
            in_specs=[pl.BlockSpec((SUBLANES, LANES), lambda g, be, fi, nu: (jnp.minimum(g, nu[0]), 0),
                                   memory_space=pltpu.SMEM),
                      pl.BlockSpec((SUBLANES, LANES), lambda g, be, fi, nu: (last(g + 1, nu) + 1, 0),
                                   memory_space=pltpu.SMEM),
                      pl.BlockSpec(memory_space=pl.ANY),
                      pl.BlockSpec((1, d, 2 * d), exp3),
                      pl.BlockSpec((1, 1, 2 * d), exp3),
                      pl.BlockSpec((1, d, d), exp3),
                      pl.BlockSpec((1, 1, d), exp3)],
            out_specs=pl.BlockSpec(memory_space=pl.ANY),
            scratch_shapes=[pltpu.VMEM((rb * ROW_CHUNKS, LANES), F32),
                            pltpu.VMEM((rb * ROW_CHUNKS, LANES), F32),
                            pltpu.VMEM((rb * ROW_CHUNKS, LANES), F32),
                            pltpu.VMEM((rb * ROW_CHUNKS, LANES), F32),
                            pltpu.VMEM((rb, d), BF16),
                            pltpu.VMEM((d, 2 * d), BF16),
                            pltpu.VMEM((d, d), BF16),
                            pltpu.SemaphoreType.DMA((2,)),
                            pltpu.SemaphoreType.DMA((2,))]),
        out_shape=jax.ShapeDtypeStruct((n_slots * ROW_CHUNKS, LANES), F32),
        compiler_params=_cparams(("arbitrary",)),
        name="moe_experts",
    )(block_exp, first, n_used, row_idx, row_idx, h_rows, w_gu, b_gu_grouped, w_dn, b_dn)


def _combine_kernel(gate_ref, x_ref, g2_ref, lng_ref, lnb_ref, yk_ref, o_ref):
    tm = x_ref.shape[0]
    gates = gate_ref[...]
    chunks = []
    for c in range(ROW_CHUNKS):
        yc = jnp.zeros((tm, LANES), F32)
        for k in range(TOP_K):
            yc = yc + gates[:, k:k + 1] * yk_ref[pl.ds(k * ROW_CHUNKS + c, tm, stride=TOP_K * ROW_CHUNKS), :]
        chunks.append(yc)
    y = jnp.concatenate(chunks, axis=1)
    z = DEEPNORM_ALPHA * x_ref[...] + g2_ref[0, 0] * y
    o_ref[...] = _ln(z) * lng_ref[...] + lnb_ref[...]


def _combine(yk, gates, x2, mod4, lng, lnb, seq):
    n, d = x2.shape
    tm = TOK_TILE
    per_b = seq // tm
    return pl.pallas_call(
        _combine_kernel,
        grid=(n // tm,),
        in_specs=[pl.BlockSpec((tm, LANES), lambda i: (i, 0)),
                  pl.BlockSpec((tm, d), lambda i: (i, 0)),
                  pl.BlockSpec((1, 1, 1, d), lambda i: (i // per_b, 5, 0, 0)),
                  pl.BlockSpec((1, d), lambda i: (0, 0)),
                  pl.BlockSpec((1, d), lambda i: (0, 0)),
                  pl.BlockSpec((tm * TOP_K * ROW_CHUNKS, LANES), lambda i: (i, 0))],
        out_specs=pl.BlockSpec((tm, d), lambda i: (i, 0)),
        out_shape=jax.ShapeDtypeStruct((n, d), F32),
        compiler_params=_cparams(("parallel",)),
        name="moe_combine",
    )(gates, x2, mod4, lng, lnb, yk)


def _moe(x2, mod4, rw, rb, w_gu, b_gu, w_dn, b_dn, lng, lnb, seq):
    n, d = x2.shape
    h_rows, meta, gates, cnt = _router(x2, mod4, rw, rb, seq)
    counts = cnt[0, :N_EXPERTS]
    padded = -(-counts // EXPERT_ROWS) * EXPERT_ROWS
    pend = jnp.cumsum(padded)
    pstart = pend - padded
    n_blocks = -(-(n * TOP_K + N_EXPERTS * (EXPERT_ROWS - 1)) // EXPERT_ROWS)
    block_row0 = jnp.arange(n_blocks, dtype=I32) * EXPERT_ROWS
    block_exp = jnp.minimum(jnp.sum((pend[None, :] <= block_row0[:, None]).astype(I32), axis=1), N_EXPERTS - 1)
    first = jnp.concatenate([jnp.ones((1,), I32), (block_exp[1:] != block_exp[:-1]).astype(I32)])
    dest = jnp.take(pstart, meta[:, :TOP_K]) + meta[:, TOP_K:2 * TOP_K]
    rows = n_blocks * EXPERT_ROWS
    n_pairs = n * TOP_K
    r = jnp.arange(rows, dtype=I32)
    scratch_slot = n_pairs + ((r // EXPERT_ROWS) % 2) * EXPERT_ROWS + r % EXPERT_ROWS
    row_slot = scratch_slot.at[dest.reshape(-1)].set(jnp.arange(n_pairs, dtype=I32), unique_indices=True)
    row_tok = jnp.minimum(row_slot // TOP_K, n - 1)
    per_blk = EXPERT_ROWS // LANES
    toks = row_tok.reshape(n_blocks, per_blk, LANES)
    slots = row_slot.reshape(n_blocks, per_blk, LANES)
    dummy_slots = (n_pairs + EXPERT_ROWS + jnp.arange(EXPERT_ROWS, dtype=I32)).reshape(1, per_blk, LANES)
    row_idx = jnp.concatenate([jnp.concatenate([toks[:1], toks], axis=0),
                               jnp.concatenate([dummy_slots, slots], axis=0)], axis=1)
    row_idx = row_idx.reshape((n_blocks + 1) * 2 * per_blk, LANES)
    e = b_gu.shape[0]
    b_grouped = b_gu.reshape(e, d // LANES, LANES, 2).transpose(0, 1, 3, 2).reshape(e, 1, 2 * d)
    n_used = (pend[-1:] // EXPERT_ROWS).astype(I32)
    yk = _experts(h_rows, row_idx, block_exp, first, n_used, w_gu, b_grouped, w_dn, b_dn[:, None, :],
                  n_pairs + 2 * EXPERT_ROWS)
    return _combine(yk, gates, x2, mod4, lng.reshape(1, d), lnb.reshape(1, d), seq)


def _permute_w_in(w):
    d = w.shape[0]
    off = {}
    o = 0
    for name, wd in (("a_q", 256), ("a_k", 256), ("a_v", 256), ("b_q", 384), ("b_kc", 128), ("b_vc", 128),
                     ("b_ks", 128), ("b_vs", 128), ("b_kw", 128), ("b_vw", 128), ("b_gate", 18),
                     ("c_q", 384), ("c_k", 384), ("c_v", 384), ("merge_gate", 3 * D_MODEL)):
        off[name] = (o, o + wd)
        o += wd
    col = lambda name: w[:, off[name][0]:off[name][1]]
    scale = HEAD_DIM ** -0.5 * float(np.log2(np.e))
    b_q = col("b_q").reshape(d, B_KV_GROUPS, B_REP, HEAD_DIM).transpose(0, 2, 1, 3).reshape(d, B_HEADS * HEAD_DIM)
    gate_pad = jnp.zeros((d, LANES - B_HEADS * 3), w.dtype)
    parts = [col("merge_gate"),
             col("a_q") * scale, col("a_k"), b_q * scale, col("b_kc"), col("b_ks"), col("b_kw"),
             col("c_q") * scale, col("c_k"),
             col("a_v"), col("b_vc"), col("b_vs"), col("b_vw"), col("c_v"),
             col("b_gate"), gate_pad]
    return jnp.concatenate(parts, axis=1).astype(BF16)


def _dilate(arr, tile, batch, seq, dil):
    a = arr[:, tile * LANES:(tile + 1) * LANES].reshape(batch, seq // dil, dil, LANES)
    return a.transpose(0, 2, 1, 3).reshape(batch * seq, LANES)


def _undilate(arr, batch, seq, dil):
    a = arr.reshape(batch, dil, seq // dil, LANES)
    return a.transpose(0, 2, 1, 3).reshape(batch * seq, LANES)


def _mixer_layer(x2, mod4, cos_t, sin_t, batch, seq, w_in, cmp_w1_k, cmp_w2_k, cmp_pe_k,
                 cmp_w1_v, cmp_w2_v, cmp_pe_v, w_a, w_b, w_c, w_out, lng, lnb):
    d = D_MODEL
    proj = _inproj(x2, mod4, _permute_w_in(w_in), cos_t, sin_t, seq)

    bias_a = _moba_select(proj, _kmean(proj, batch, seq), batch, seq)
    o_a = _bb_attn(proj, bias_a, batch=batch, seq=seq, n_pairs=A_HEADS // 2,
                   q_tile=lambda p: T_AQ + p, k_tile=lambda p: T_AK + p, v_tile=lambda p: T_AV + p,
                   bias_tile=lambda p: p, blk=MOBA_BLOCK)

    kc = _compress(proj[:, T_BKC * LANES:(T_BKC + 1) * LANES], cmp_w1_k, cmp_w2_k, cmp_pe_k, batch, seq)
    vc = _compress(proj[:, T_BVC * LANES:(T_BVC + 1) * LANES], cmp_w1_v, cmp_w2_v, cmp_pe_v, batch, seq)
    o_cmp, bias_b = _nsa_cmp(proj, kc, vc, batch, seq)
    o_slc = _bb_attn(proj, bias_b, batch=batch, seq=seq, n_pairs=B_REP,
                     q_tile=lambda p: T_BQ + p, k_tile=lambda p: T_BKS, v_tile=lambda p: T_BVS,
                     bias_tile=lambda p: 0, blk=SLC_BLOCK)
    o_win, _ = _band_attn(proj, proj, proj, batch=batch, seq=seq, n_pairs=B_REP,
                          q_tile=lambda p: T_BQ + p, k_tile=lambda p: T_BKW, v_tile=lambda p: T_BVW,
                          window=NSA_WINDOW, t=256)

    oc, lse = [], []
    for gi, (window, dil) in enumerate(DILATED_PAIRS):
        wlen = window // dil + 1
        if dil == 1:
            o, l = _band_attn(proj, proj, proj, batch=batch, seq=seq, n_pairs=1,
                              q_tile=lambda p: T_CQ + gi, k_tile=lambda p: T_CK + gi,
                              v_tile=lambda p: T_CV + gi, window=wlen, t=256)
        else:
            qd = _dilate(proj, T_CQ + gi, batch, seq, dil)
            kd = _dilate(proj, T_CK + gi, batch, seq, dil)
            vd = _dilate(proj, T_CV + gi, batch, seq, dil)
            o, l = _band_attn(qd, kd, vd, batch=batch * dil, seq=seq // dil, n_pairs=1,
                              q_tile=lambda p: 0, k_tile=lambda p: 0, v_tile=lambda p: 0,
                              window=wlen, t=256)
            o, l = _undilate(o, batch, seq, dil), _undilate(l, batch, seq, dil)
        oc.append(o)
        lse.append(l)

    w_b_perm = w_b.reshape(B_KV_GROUPS, B_REP, HEAD_DIM, d).transpose(1, 0, 2, 3).reshape(B_HEADS * HEAD_DIM, d)
    return _merge(o_a, o_cmp, o_slc, o_win, proj, oc, lse, x2, mod4,
                  w_a.astype(BF16), w_b_perm.astype(BF16), w_c.astype(BF16), w_out.astype(BF16),
                  lng.reshape(1, d), lnb.reshape(1, d), seq)


def _rope_tables(positions):
    inv = ROPE_THETA ** (-jnp.arange(0, HEAD_DIM, 2, dtype=F32) / HEAD_DIM)
    ang = positions.astype(F32).reshape(-1, 1) * inv[None, :]
    cos, sin = jnp.cos(ang), jnp.sin(ang)
    return jnp.tile(cos, (1, 4)), jnp.tile(jnp.concatenate([-sin, sin], axis=1), (1, 2))


def kernel(x, c, positions, w_ada, b_ada, w_in, cmp_w1_k, cmp_w2_k, cmp_pe_k, cmp_w1_v, cmp_w2_v, cmp_pe_v, w_branch_a, w_branch_b, w_branch_c, w_out, ln1_g, ln1_b, router_w, router_b, w_gate_up, b_gate_up, w_down, b_down, ln2_g, ln2_b):
    batch, seq, d = x.shape
    depth = w_in.shape[0]
    cos_t, sin_t = _rope_tables(positions)
    mod = _ada(c, w_ada, b_ada)
    x2 = x.reshape(batch * seq, d)
    for l in range(depth):
        mod4 = mod[l].reshape(batch, N_ADA, 1, d)
        x2 = _mixer_layer(x2, mod4, cos_t, sin_t, batch, seq, w_in[l], cmp_w1_k[l], cmp_w2_k[l], cmp_pe_k[l],
                          cmp_w1_v[l], cmp_w2_v[l], cmp_pe_v[l], w_branch_a[l], w_branch_b[l],
                          w_branch_c[l], w_out[l], ln1_g[l], ln1_b[l])
        x2 = _moe(x2, mod4, router_w[l], router_b[l], w_gate_up[l], b_gate_up[l], w_down[l], b_down[l],
                  ln2_g[l], ln2_b[l], seq)
    return x2.reshape(batch, seq, d)
```

```python
import functools

import numpy as np
import jax
import jax.numpy as jnp
from jax import lax
from jax.experimental import pallas as pl
from jax.experimental.pallas import tpu as pltpu

F32 = jnp.float32
BF16 = jnp.bfloat16
I32 = jnp.int32
HIGHEST = lax.Precision.HIGHEST

D_MODEL = 1024
DEPTH = 2
HEAD_DIM = 64
ROPE_THETA = 10000.0
LN_EPS = 1e-5
DEEPNORM_ALPHA = (2 * DEPTH) ** 0.25
N_ADA = 6
A_HEADS = 4
MOBA_BLOCK = 256
MOBA_TOPK = 3
B_HEADS = 6
B_KV_GROUPS = 2
B_REP = 3
CMP_BLOCK = 32
CMP_STRIDE = 16
CMP_HIDDEN = 128
SLC_BLOCK = 64
SLC_TOPK = 16
NSA_WINDOW = 512
FORCE_SCORE = 1e6
DILATED_PAIRS = ((128, 1), (512, 4), (2048, 16))
N_EXPERTS = 32
TOP_K = 4
SWIGLU_ALPHA = 1.702
SWIGLU_LIMIT = 7.0

LANES = 128
SUBLANES = 8
HALF = LANES // 2
NEG = -1e30
ROW_CHUNKS = D_MODEL // LANES
VMEM_LIMIT = 56 * 1024 * 1024

T_MG = 0
T_ROPE0, T_ROPE1 = 24, 40
T_AQ, T_AK, T_BQ, T_BKC, T_BKS, T_BKW, T_CQ, T_CK = 24, 26, 28, 31, 32, 33, 34, 37
T_AV, T_BVC, T_BVS, T_BVW, T_CV, T_BG = 40, 42, 43, 44, 45, 48
N_PROJ_TILES = 49
PROJ_W = N_PROJ_TILES * LANES

EXPERT_ROWS = 512
TOK_TILE = 256
N_FILL = 2 * N_EXPERTS + 2


def _cparams(sem, **kw):
    return pltpu.CompilerParams(dimension_semantics=sem, vmem_limit_bytes=VMEM_LIMIT, **kw)


def _lane_iota(shape=(1, LANES)):
    return lax.broadcasted_iota(I32, shape, len(shape) - 1)


def _sigmoid(x):
    return 1.0 / (1.0 + jnp.exp(-x))


def _ln(x):
    mu = jnp.mean(x, axis=-1, keepdims=True)
    xc = x - mu
    var = jnp.mean(xc * xc, axis=-1, keepdims=True)
    return xc * lax.rsqrt(var + LN_EPS)


def _dot_nt(a, b, precision=None):
    return lax.dot_general(a, b, (((1,), (1,)), ((), ())), precision=precision,
                           preferred_element_type=F32)


def _ada_kernel(c_ref, w_ref, b_ref, o_ref):
    c = c_ref[...]
    cond = c * _sigmoid(c)
    o_ref[0] = jnp.dot(cond, w_ref[0], precision=HIGHEST, preferred_element_type=F32) + b_ref[0]


def _ada(c, w_ada, b_ada):
    depth, d, n = w_ada.shape
    b = c.shape[0]
    tn = 1536
    return pl.pallas_call(
        _ada_kernel,
        grid=(depth, n // tn),
        in_specs=[pl.BlockSpec((b, d), lambda l, j: (0, 0)),
                  pl.BlockSpec((1, d, tn), lambda l, j: (l, 0, j)),
                  pl.BlockSpec((1, 1, tn), lambda l, j: (l, 0, j))],
        out_specs=pl.BlockSpec((1, b, tn), lambda l, j: (l, 0, j)),
        out_shape=jax.ShapeDtypeStruct((depth, b, n), F32),
        compiler_params=_cparams(("parallel", "parallel")),
        name="ada",
    )(c, w_ada, b_ada.reshape(depth, 1, n))


def _inproj_kernel(x_ref, sc_ref, sh_ref, w_ref, cos_ref, sin_ref, o_ref, *, chunks):
    h = _ln(x_ref[...]) * (1.0 + sc_ref[0, 0]) + sh_ref[0, 0]
    hb = h.astype(BF16)
    first_half = (_lane_iota() & (HEAD_DIM - 1)) < (HEAD_DIM // 2)
    for c0, cw, rope in chunks:
        acc = jnp.dot(hb, w_ref[:, c0:c0 + cw], preferred_element_type=F32)
        if rope:
            cos = cos_ref[...]
            sin = sin_ref[...]
            for t in range(cw // LANES):
                a = acc[:, t * LANES:(t + 1) * LANES]
                rot = jnp.where(first_half, pltpu.roll(a, LANES - HEAD_DIM // 2, 1),
                                pltpu.roll(a, HEAD_DIM // 2, 1))
                o_ref[:, c0 + t * LANES:c0 + (t + 1) * LANES] = (a * cos + rot * sin).astype(o_ref.dtype)
        else:
            o_ref[:, c0:c0 + cw] = acc.astype(o_ref.dtype)


def _inproj_chunks():
    chunks = []
    for lo, hi, rope in ((0, T_ROPE0, False), (T_ROPE0, T_ROPE1, True), (T_ROPE1, N_PROJ_TILES, False)):
        c = lo * LANES
        while c < hi * LANES:
            cw = min(512, hi * LANES - c)
            chunks.append((c, cw, rope))
            c += cw
    return tuple(chunks)


def _inproj(x2, mod4, w_perm, cos_t, sin_t, seq):
    n, d = x2.shape
    tm = 256
    per_b = seq // tm
    return pl.pallas_call(
        functools.partial(_inproj_kernel, chunks=_inproj_chunks()),
        grid=(n // tm,),
        in_specs=[pl.BlockSpec((tm, d), lambda i: (i, 0)),
                  pl.BlockSpec((1, 1, 1, d), lambda i: (i // per_b, 1, 0, 0)),
                  pl.BlockSpec((1, 1, 1, d), lambda i: (i // per_b, 0, 0, 0)),
                  pl.BlockSpec((d, PROJ_W), lambda i: (0, 0)),
                  pl.BlockSpec((tm, LANES), lambda i: (i, 0)),
                  pl.BlockSpec((tm, LANES), lambda i: (i, 0))],
        out_specs=pl.BlockSpec((tm, PROJ_W), lambda i: (i, 0)),
        out_shape=jax.ShapeDtypeStruct((n, PROJ_W), BF16),
        compiler_params=_cparams(("parallel",)),
        name="inproj",
    )(x2, mod4, mod4, w_perm, cos_t, sin_t)


def _bb_attn_kernel(q_ref, k_ref, v_ref, b_ref, o_ref, qa_sc, s_sc, mp_sc, m_sc, acc_sc,
                    *, blk_shift, tq, big):
    i = pl.program_id(2)
    lane = _lane_iota()
    lo = lane < HALF
    q = q_ref[...]
    b = b_ref[...]
    qa_sc[0] = jnp.where(lo, q, b)
    qa_sc[1] = jnp.where(lo, b, q)
    mp_sc[...] = jnp.full(mp_sc.shape, -jnp.inf, F32)
    acc_sc[...] = jnp.zeros(acc_sc.shape, F32)

    def scores(u0, n_units, diagonal_last):
        tk = n_units * tq
        start = pl.multiple_of(u0 * tq, tq)
        k = k_ref[pl.ds(start, tk), :]
        kblk = (start + lax.broadcasted_iota(I32, (tk, 1), 0)) >> blk_shift
        onehot = jnp.where(kblk == (lane & (HALF - 1)), 1.0, 0.0).astype(BF16)
        ka = (jnp.where(lo, k, onehot), jnp.where(lo, onehot, k))
        causal = lax.broadcasted_iota(I32, (1, tq), 1) <= lax.broadcasted_iota(I32, (tq, 1), 0)
        for h in range(2):
            s = _dot_nt(qa_sc[h], ka[h])
            mp = mp_sc[h]
            for u in range(n_units):
                su = s[:, u * tq:(u + 1) * tq]
                if diagonal_last and u == n_units - 1:
                    su = jnp.where(causal, su, NEG)
                s_sc[h, u0 + u] = su
                for c in range(tq // LANES):
                    mp = jnp.maximum(mp, su[:, c * LANES:(c + 1) * LANES])
            mp_sc[h] = mp

    def weighted(u0, n_units):
        tk = n_units * tq
        start = pl.multiple_of(u0 * tq, tq)
        v = v_ref[pl.ds(start, tk), :]
        one = jnp.ones_like(v)
        vs = (jnp.where(lo, v, one), jnp.where(lo, one, v))
        for h in range(2):
            m_row = m_sc[h]
            ps = []
            for u in range(n_units):
                su = s_sc[h, u0 + u]
                for c in range(tq // LANES):
                    ps.append(jnp.exp2(su[:, c * LANES:(c + 1) * LANES] - m_row).astype(BF16))
            acc_sc[h] = acc_sc[h] + jnp.dot(jnp.concatenate(ps, axis=1), vs[h], preferred_element_type=F32)

    n_big = i // big
    n_tail = i - n_big * big + 1

    def run(fn):
        def big_body(t, c):
            fn(t * big, big, False) if fn is scores else fn(t * big, big)
            return c

        lax.fori_loop(0, n_big, big_body, 0)
        for n in range(1, big + 1):
            @pl.when(n_tail == n)
            def _(n=n):
                fn(n_big * big, n, True) if fn is scores else fn(n_big * big, n)

    run(scores)
    for h in range(2):
        m_sc[h] = jnp.broadcast_to(jnp.max(mp_sc[h], axis=1, keepdims=True), (tq, LANES))
    run(weighted)
    outs = [acc_sc[h] / pltpu.roll(acc_sc[h], HALF, 1) for h in range(2)]
    o_ref[...] = jnp.where(lo, outs[0], outs[1]).astype(o_ref.dtype)


def _bb_attn(proj, bias, *, batch, seq, n_pairs, q_tile, k_tile, v_tile, bias_tile, blk):
    t = 256
    nq = seq // t
    n = batch * seq
    kern = functools.partial(_bb_attn_kernel, blk_shift=int(np.log2(blk)), tq=t, big=4)
    return pl.pallas_call(
        kern,
        grid=(batch, n_pairs, nq),
        in_specs=[pl.BlockSpec((t, LANES), lambda b, p, i: (b * nq + i, q_tile(p))),
                  pl.BlockSpec((seq, LANES), lambda b, p, i: (b, k_tile(p))),
                  pl.BlockSpec((seq, LANES), lambda b, p, i: (b, v_tile(p))),
                  pl.BlockSpec((t, LANES), lambda b, p, i: (b * nq + i, bias_tile(p)))],
        out_specs=pl.BlockSpec((t, LANES), lambda b, p, i: (b * nq + i, p)),
        out_shape=jax.ShapeDtypeStruct((n, n_pairs * LANES), BF16),
        scratch_shapes=[pltpu.VMEM((2, t, LANES), BF16),
                        pltpu.VMEM((2, nq, t, t), F32),
                        pltpu.VMEM((2, t, LANES), F32),
                        pltpu.VMEM((2, t, LANES), F32),
                        pltpu.VMEM((2, t, LANES), F32)],
        compiler_params=_cparams(("parallel", "parallel", "parallel")),
        name="bb_attn_%d" % blk,
    )(proj, proj, proj, bias)


def _band_kernel(q_ref, k_ref, v_ref, o_ref, lse_ref, *, tq, kw, window, seq):
    i = pl.program_id(2)
    lane = _lane_iota()
    lo = lane < HALF
    start = pl.multiple_of(jnp.clip(i * tq - (kw - tq), 0, seq - kw), LANES)
    q = q_ref[...]
    k = k_ref[pl.ds(start, kw), :]
    v = v_ref[pl.ds(start, kw), :]
    zero = jnp.zeros_like(q)
    qs = (jnp.where(lo, q, zero), jnp.where(lo, zero, q))
    rel = (i * tq + lax.broadcasted_iota(I32, (tq, 1), 0)) - (start + lax.broadcasted_iota(I32, (1, kw), 1))
    ok = (rel >= 0) & (rel < window)
    outs, lses = [], []
    for h in range(2):
        s = jnp.where(ok, _dot_nt(qs[h], k), NEG)
        m = jnp.max(s, axis=1, keepdims=True)
        p = jnp.exp2(s - m)
        l = jnp.sum(p, axis=1, keepdims=True)
        outs.append(jnp.dot(p.astype(BF16), v, preferred_element_type=F32) / l)
        lses.append(m + jnp.log2(l))
    o_ref[...] = jnp.where(lo, outs[0], outs[1]).astype(o_ref.dtype)
    lse_ref[...] = jnp.where(lo, lses[0], lses[1])


def _band_attn(q_arr, k_arr, v_arr, *, batch, seq, n_pairs, q_tile, k_tile, v_tile, window, t):
    t = min(t, seq)
    nq = seq // t
    kw = min(seq, t + -(-(window - 1) // LANES) * LANES)
    n = batch * seq
    kern = functools.partial(_band_kernel, tq=t, kw=kw, window=window, seq=seq)
    return pl.pallas_call(
        kern,
        grid=(batch, n_pairs, nq),
        in_specs=[pl.BlockSpec((t, LANES), lambda b, p, i: (b * nq + i, q_tile(p))),
                  pl.BlockSpec((seq, LANES), lambda b, p, i: (b, k_tile(p))),
                  pl.BlockSpec((seq, LANES), lambda b, p, i: (b, v_tile(p)))],
        out_specs=[pl.BlockSpec((t, LANES), lambda b, p, i: (b * nq + i, p)),
                   pl.BlockSpec((t, LANES), lambda b, p, i: (b * nq + i, p))],
        out_shape=[jax.ShapeDtypeStruct((n, n_pairs * LANES), BF16),
                   jax.ShapeDtypeStruct((n, n_pairs * LANES), F32)],
        compiler_params=_cparams(("parallel", "parallel", "parallel")),
        name="band_attn_%d" % window,
    )(q_arr, k_arr, v_arr)


def _kmean_kernel(k_ref, o_ref, *, nb):
    s = k_ref.shape[0]
    blk = lax.broadcasted_iota(I32, (nb, s), 1) >> int(np.log2(MOBA_BLOCK))
    avg = jnp.where(blk == lax.broadcasted_iota(I32, (nb, s), 0), 1.0 / MOBA_BLOCK, 0.0).astype(BF16)
    o_ref[0] = jnp.dot(avg, k_ref[...], preferred_element_type=F32)


def _kmean(proj, batch, seq):
    nb = seq // MOBA_BLOCK
    w = A_HEADS * HEAD_DIM
    return pl.pallas_call(
        functools.partial(_kmean_kernel, nb=nb),
        grid=(batch,),
        in_specs=[pl.BlockSpec((seq, w), lambda b: (b, T_AK * LANES // w))],
        out_specs=pl.BlockSpec((1, nb, w), lambda b: (b, 0, 0)),
        out_shape=jax.ShapeDtypeStruct((batch, nb, w), F32),
        compiler_params=_cparams(("parallel",)),
        name="moba_kmean",
    )(proj)


def _rank_desc(g, n_idx, n):
    rank = jnp.zeros(g.shape, I32)
    for m in range(n):
        c = g[m:m + 1, :]
        beats = (c > g) | ((c == g) & (n_idx > m))
        rank = rank + jnp.where(beats, 1, 0)
    return rank


def _moba_sel_kernel(q_ref, km_ref, b_ref, *, nb, n_sel):
    i = pl.program_id(2)
    t = q_ref.shape[0]
    q = q_ref[...].astype(F32)
    km = km_ref[0]
    lo = _lane_iota() < HALF
    zero = jnp.zeros_like(km)
    pad = jnp.zeros((HALF - nb, LANES), F32)
    kmt = jnp.concatenate([jnp.where(lo, zero, km), pad, jnp.where(lo, km, zero), pad], axis=0)
    gt = _dot_nt(kmt, q, precision=HIGHEST)
    n_idx = lax.broadcasted_iota(I32, (nb, 1), 0)
    valid = n_idx < i
    rows = []
    for r0 in (0, HALF):
        g = jnp.where(valid, gt[r0:r0 + nb, :], -jnp.inf)
        rank = _rank_desc(g, n_idx, nb)
        allowed = (valid & (rank < n_sel)) | (n_idx == i)
        rows.append(jnp.where(allowed, 0.0, NEG))
        rows.append(jnp.zeros((HALF - nb, t), F32))
    b_ref[...] = jnp.concatenate(rows, axis=0).T.astype(b_ref.dtype)


def _moba_select(proj, kmean, batch, seq):
    t = MOBA_BLOCK
    nq = seq // t
    nb = seq // MOBA_BLOCK
    n_sel = min(MOBA_TOPK, nb - 1)
    n_pairs = A_HEADS // 2
    return pl.pallas_call(
        functools.partial(_moba_sel_kernel, nb=nb, n_sel=n_sel),
        grid=(batch, n_pairs, nq),
        in_specs=[pl.BlockSpec((t, LANES), lambda b, p, i: (b * nq + i, T_AQ + p)),
                  pl.BlockSpec((1, nb, LANES), lambda b, p, i: (b, 0, p))],
        out_specs=pl.BlockSpec((t, LANES), lambda b, p, i: (b * nq + i, p)),
        out_shape=jax.ShapeDtypeStruct((batch * seq, n_pairs * LANES), BF16),
        compiler_params=_cparams(("parallel", "parallel", "parallel")),
        name="moba_select",
    )(proj, kmean)


def _compress_kernel(x_ref, w1_ref, pe_ref, w1f_ref, w2_ref, o_ref):
    x = x_ref[0]
    nblk = x.shape[0]
    outs = []
    for g in range(B_KV_GROUPS):
        u = jnp.dot(x, w1_ref[g, 0], preferred_element_type=F32)
        v = jnp.dot(x, w1_ref[g, 1], preferred_element_type=F32)
        pe_h = jnp.dot(pe_ref[...], w1f_ref[...], precision=HIGHEST, preferred_element_type=F32)
        hid = u + pltpu.roll(v, nblk - 1, 0) + pe_h[:1]
        hid = hid * _sigmoid(hid)
        outs.append(jnp.dot(hid, w2_ref[g], precision=HIGHEST, preferred_element_type=F32))
    o = outs[0] + outs[1]
    rows = lax.broadcasted_iota(I32, (nblk, 1), 0)
    o_ref[0] = jnp.where(rows < nblk - 1, o, 0.0).astype(o_ref.dtype)


def _compress(xt, w1, w2, pe, batch, seq):
    nblk = seq // CMP_STRIDE
    xg = xt.reshape(batch, nblk, CMP_STRIDE * LANES)
    w1r = w1.reshape(2, CMP_STRIDE, HEAD_DIM, CMP_HIDDEN)
    w1e = jnp.zeros((B_KV_GROUPS, 2, CMP_STRIDE, B_KV_GROUPS, HEAD_DIM, CMP_HIDDEN), F32)
    for g in range(B_KV_GROUPS):
        w1e = w1e.at[g, :, :, g].set(w1r)
    w1e = w1e.reshape(B_KV_GROUPS, 2, CMP_STRIDE * LANES, CMP_HIDDEN).astype(BF16)
    pe_flat = jnp.broadcast_to(pe.reshape(1, CMP_BLOCK * HEAD_DIM), (SUBLANES, CMP_BLOCK * HEAD_DIM))
    w2e = jnp.zeros((B_KV_GROUPS, CMP_HIDDEN, LANES), F32)
    for g in range(B_KV_GROUPS):
        w2e = w2e.at[g, :, g * HEAD_DIM:(g + 1) * HEAD_DIM].set(w2)
    return pl.pallas_call(
        _compress_kernel,
        grid=(batch,),
        in_specs=[pl.BlockSpec((1, nblk, CMP_STRIDE * LANES), lambda b: (b, 0, 0)),
                  pl.BlockSpec(w1e.shape, lambda b: (0, 0, 0, 0)),
                  pl.BlockSpec(pe_flat.shape, lambda b: (0, 0)),
                  pl.BlockSpec(w1.shape, lambda b: (0, 0)),
                  pl.BlockSpec(w2e.shape, lambda b: (0, 0, 0))],
        out_specs=pl.BlockSpec((1, nblk, LANES), lambda b: (b, 0, 0)),
        out_shape=jax.ShapeDtypeStruct((batch, nblk, LANES), BF16),
        compiler_params=_cparams(("parallel",)),
        name="nsa_compress",
    )(xg, w1e, pe_flat, w1, w2e)


def _nsa_cmp_kernel(q0_ref, q1_ref, q2_ref, kc_ref, vc_ref, ov_ref, o_ref, b_ref, *, n_sel, ns):
    i = pl.program_id(1)
    t = q0_ref.shape[0]
    ncp = kc_ref.shape[1]
    lane = _lane_iota()
    lo = lane < HALF
    kc = kc_ref[0]
    vc = vc_ref[0]
    tq = i * t + lax.broadcasted_iota(I32, (t, 1), 0)
    cmp_end = lax.broadcasted_iota(I32, (1, ncp), 1) * CMP_STRIDE + (CMP_BLOCK - 1)
    vis = cmp_end <= tq
    psum = [jnp.zeros((t, ncp), F32) for _ in range(B_KV_GROUPS)]
    for r, q_ref in enumerate((q0_ref, q1_ref, q2_ref)):
        q = q_ref[...]
        zero = jnp.zeros_like(q)
        outs = []
        for g in range(B_KV_GROUPS):
            qg = jnp.where(lo, q, zero) if g == 0 else jnp.where(lo, zero, q)
            s = jnp.where(vis, _dot_nt(qg, kc), NEG)
            m = jnp.max(s, axis=1, keepdims=True)
            m = jnp.where(m > 0.5 * NEG, m, 0.0)
            e = jnp.where(vis, jnp.exp2(s - m), 0.0)
            p = e / jnp.maximum(jnp.sum(e, axis=1, keepdims=True), 1e-30)
            psum[g] = psum[g] + p
            outs.append(jnp.dot(p.astype(BF16), vc, preferred_element_type=F32))
        o_ref[:, r * LANES:(r + 1) * LANES] = jnp.where(lo, outs[0], outs[1]).astype(o_ref.dtype)
    n_idx = lax.broadcasted_iota(I32, (ns, 1), 0)
    tcol = i * t + lax.broadcasted_iota(I32, (1, t), 1)
    qblk = tcol >> int(np.log2(SLC_BLOCK))
    forced = (n_idx == 0) | (n_idx == qblk) | (n_idx == qblk - 1)
    valid = n_idx <= qblk
    rows = []
    for g in (1, 0):
        imp = _dot_nt(ov_ref[...], psum[g], precision=HIGHEST)
        imp = jnp.where(forced, FORCE_SCORE, imp)
        imp = jnp.where(valid, imp, -jnp.inf)
        rank = _rank_desc(imp, n_idx, ns)
        allowed = valid & (rank < n_sel)
        rows.append(jnp.where(allowed, 0.0, NEG))
        if ns < HALF:
            rows.append(jnp.zeros((HALF - ns, t), F32))
    b_ref[...] = jnp.concatenate(rows, axis=0).T.astype(b_ref.dtype)


def _nsa_cmp(proj, kc, vc, batch, seq):
    t = 256
    nq = seq // t
    ns = seq // SLC_BLOCK
    n_sel = min(SLC_TOPK, ns)
    ncp = seq // CMP_STRIDE
    cs = np.arange(ncp)[None, :] * CMP_STRIDE
    ss = np.arange(ns)[:, None] * SLC_BLOCK
    ov = ((cs < ss + SLC_BLOCK) & (cs + CMP_BLOCK > ss)).astype(np.float32)
    ov[:, ncp - 1] = 0.0
    n = batch * seq
    qspec = [pl.BlockSpec((t, LANES), (lambda b, i, r=r: (b * nq + i, T_BQ + r))) for r in range(B_REP)]
    return pl.pallas_call(
        functools.partial(_nsa_cmp_kernel, n_sel=n_sel, ns=ns),
        grid=(batch, nq),
        in_specs=qspec + [pl.BlockSpec((1, ncp, LANES), lambda b, i: (b, 0, 0)),
                          pl.BlockSpec((1, ncp, LANES), lambda b, i: (b, 0, 0)),
                          pl.BlockSpec((ns, ncp), lambda b, i: (0, 0))],
        out_specs=[pl.BlockSpec((t, B_REP * LANES), lambda b, i: (b * nq + i, 0)),
                   pl.BlockSpec((t, LANES), lambda b, i: (b * nq + i, 0))],
        out_shape=[jax.ShapeDtypeStruct((n, B_REP * LANES), BF16),
                   jax.ShapeDtypeStruct((n, LANES), BF16)],
        compiler_params=_cparams(("parallel", "parallel")),
        name="nsa_cmp_select",
    )(proj, proj, proj, kc, vc, jnp.asarray(ov))


def _merge_kernel(oa_ref, ocmp_ref, oslc_ref, owin_ref, bg_ref, oc0_ref, oc1_ref, oc2_ref,
                  l0_ref, l1_ref, l2_ref, mg_ref, x_ref, g1_ref, wa_ref, wb_ref, wc_ref, wo_ref,
                  eg_ref, lng_ref, lnb_ref, o_ref):
    d = D_MODEL
    sg = _sigmoid(bg_ref[...].astype(F32))
    ob = jnp.zeros(ocmp_ref.shape, F32)
    for br, ref in enumerate((ocmp_ref, oslc_ref, owin_ref)):
        gexp = jnp.dot(sg, eg_ref[br], precision=HIGHEST, preferred_element_type=F32)
        ob = ob + gexp * ref[...].astype(F32)
    l0, l1, l2 = l0_ref[...], l1_ref[...], l2_ref[...]
    mx = jnp.maximum(jnp.maximum(l0, l1), l2)
    e0, e1, e2 = jnp.exp2(l0 - mx), jnp.exp2(l1 - mx), jnp.exp2(l2 - mx)
    den = e0 + e1 + e2
    oc = ((e0 / den) * oc0_ref[...].astype(F32) + (e1 / den) * oc1_ref[...].astype(F32)
          + (e2 / den) * oc2_ref[...].astype(F32))
    pa = jnp.dot(oa_ref[...], wa_ref[...], preferred_element_type=F32)
    pb = jnp.dot(ob.astype(BF16), wb_ref[...], preferred_element_type=F32)
    pc = jnp.dot(oc.astype(BF16), wc_ref[...], preferred_element_type=F32)
    merged = (_sigmoid(mg_ref[:, 0:d].astype(F32)) * pa
              + _sigmoid(mg_ref[:, d:2 * d].astype(F32)) * pb
              + _sigmoid(mg_ref[:, 2 * d:3 * d].astype(F32)) * pc)
    y = jnp.dot(merged.astype(BF16), wo_ref[...], preferred_element_type=F32)
    z = DEEPNORM_ALPHA * x_ref[...] + g1_ref[0, 0] * y
    o_ref[...] = _ln(z) * lng_ref[...] + lnb_ref[...]


def _gate_expand():
    eg = np.zeros((3, LANES, B_HEADS * HEAD_DIM), np.float32)
    for g in range(B_KV_GROUPS):
        for r in range(B_REP):
            for br in range(3):
                c0 = (r * B_KV_GROUPS + g) * HEAD_DIM
                eg[br, (g * B_REP + r) * 3 + br, c0:c0 + HEAD_DIM] = 1.0
    return jnp.asarray(eg)


def _merge(o_a, o_cmp, o_slc, o_win, proj, oc, lse, x2, mod4, wa, wb, wc, wo, lng, lnb, seq):
    n, d = x2.shape
    tm = 256
    per_b = seq // tm
    row = lambda w: pl.BlockSpec((tm, w), lambda i: (i, 0))
    full = lambda a: pl.BlockSpec(a.shape, lambda i: (0,) * a.ndim)
    eg = _gate_expand()
    return pl.pallas_call(
        _merge_kernel,
        grid=(n // tm,),
        in_specs=[row(o_a.shape[1]), row(o_cmp.shape[1]), row(o_slc.shape[1]), row(o_win.shape[1]),
                  pl.BlockSpec((tm, LANES), lambda i: (i, T_BG)),
                  row(LANES), row(LANES), row(LANES), row(LANES), row(LANES), row(LANES),
                  pl.BlockSpec((tm, 3 * d), lambda i: (i, T_MG)),
                  row(d),
                  pl.BlockSpec((1, 1, 1, d), lambda i: (i // per_b, 2, 0, 0)),
                  full(wa), full(wb), full(wc), full(wo), full(eg), full(lng), full(lnb)],
        out_specs=row(d),
        out_shape=jax.ShapeDtypeStruct((n, d), F32),
        compiler_params=_cparams(("parallel",)),
        name="merge_out",
    )(o_a, o_cmp, o_slc, o_win, proj, oc[0], oc[1], oc[2], lse[0], lse[1], lse[2],
      proj, x2, mod4, wa, wb, wc, wo, eg, lng, lnb)


def _router_kernel(x_ref, sc_ref, sh_ref, rw_ref, rb_ref, meta_ref, gate_ref, tile_ref, cnt_ref, carry_sc):
    i = pl.program_id(0)
    tm = x_ref.shape[0]

    @pl.when(i == 0)
    def _():
        carry_sc[...] = jnp.zeros(carry_sc.shape, F32)

    h = _ln(x_ref[...]) * (1.0 + sc_ref[0, 0]) + sh_ref[0, 0]
    lg = jnp.dot(h, rw_ref[...], precision=HIGHEST, preferred_element_type=F32) + rb_ref[...]
    lane = _lane_iota()
    lane_f = lane.astype(F32)
    onehots, vals, idxs = [], [], []
    for _ in range(TOP_K):
        m = jnp.max(lg, axis=1, keepdims=True)
        idx = jnp.min(jnp.where(lg == m, lane_f, float(LANES)), axis=1, keepdims=True).astype(I32)
        oh = lane == idx
        onehots.append(oh)
        vals.append(m)
        idxs.append(idx)
        lg = jnp.where(oh, -jnp.inf, lg)
    es = [jnp.exp(v - vals[0]) for v in vals]
    den = es[0] + es[1] + es[2] + es[3]
    cnt = jnp.zeros((tm, LANES), F32)
    for oh in onehots:
        cnt = cnt + jnp.where(oh, 1.0, 0.0)
    tri = jnp.where(lax.broadcasted_iota(I32, (tm, tm), 0) > lax.broadcasted_iota(I32, (tm, tm), 1), 1.0, 0.0)
    earlier = jnp.dot(tri.astype(BF16), cnt.astype(BF16), preferred_element_type=F32)
    tile_cnt = jnp.sum(cnt, axis=0, keepdims=True)
    upper = jnp.where(lax.broadcasted_iota(I32, (LANES, LANES), 0) < lax.broadcasted_iota(I32, (LANES, LANES), 1),
                      1.0, 0.0)
    run_off = jnp.dot(jnp.broadcast_to(tile_cnt, (SUBLANES, LANES)).astype(BF16), upper.astype(BF16),
                      preferred_element_type=F32)[:1]
    pos = earlier + run_off
    meta = jnp.zeros((tm, LANES), I32)
    gates = jnp.zeros((tm, LANES), F32)
    for k in range(TOP_K):
        local = jnp.sum(jnp.where(onehots[k], pos, 0.0), axis=1, keepdims=True).astype(I32)
        meta = jnp.where(lane == k, idxs[k], meta)
        meta = jnp.where(lane == TOP_K + k, local, meta)
        gates = jnp.where(lane == k, es[k] / den, gates)
    meta_ref[...] = meta
    gate_ref[...] = gates
    row = lax.broadcasted_iota(I32, (SUBLANES, LANES), 0)
    tile_ref[...] = jnp.where(row == 0, carry_sc[...], jnp.where(row == 1, tile_cnt, run_off)).astype(I32)
    carry_sc[...] = carry_sc[...] + tile_cnt
    cnt_ref[...] = jnp.broadcast_to(carry_sc[...], cnt_ref.shape).astype(I32)


def _router(x2, mod4, rw, rb, seq):
    n, d = x2.shape
    tm = TOK_TILE
    per_b = seq // tm
    rw_p = jnp.zeros((d, LANES), F32).at[:, :N_EXPERTS].set(rw)
    rb_p = jnp.full((1, LANES), NEG, F32).at[0, :N_EXPERTS].set(rb)
    return pl.pallas_call(
        _router_kernel,
        grid=(n // tm,),
        in_specs=[pl.BlockSpec((tm, d), lambda i: (i, 0)),
                  pl.BlockSpec((1, 1, 1, d), lambda i: (i // per_b, 4, 0, 0)),
                  pl.BlockSpec((1, 1, 1, d), lambda i: (i // per_b, 3, 0, 0)),
                  pl.BlockSpec((d, LANES), lambda i: (0, 0)),
                  pl.BlockSpec((1, LANES), lambda i: (0, 0))],
        out_specs=[pl.BlockSpec((tm, LANES), lambda i: (i, 0)),
                   pl.BlockSpec((tm, LANES), lambda i: (i, 0)),
                   pl.BlockSpec((SUBLANES, LANES), lambda i: (i, 0)),
                   pl.BlockSpec((SUBLANES, LANES), lambda i: (0, 0))],
        out_shape=[jax.ShapeDtypeStruct((n, LANES), I32),
                   jax.ShapeDtypeStruct((n, LANES), F32),
                   jax.ShapeDtypeStruct((n // tm * SUBLANES, LANES), I32),
                   jax.ShapeDtypeStruct((SUBLANES, LANES), I32)],
        scratch_shapes=[pltpu.VMEM((1, LANES), F32)],
        compiler_params=_cparams(("arbitrary",)),
        name="router",
    )(x2, mod4, mod4, rw_p, rb_p)


def _row_slice(ref, r):
    return ref.at[pl.ds(pl.multiple_of(r * ROW_CHUNKS, ROW_CHUNKS), ROW_CHUNKS)]


def _wait_rows(src_like, dst_like, sem, n_rows):
    span = pl.ds(0, n_rows * ROW_CHUNKS)
    pltpu.make_async_copy(src_like.at[span], dst_like.at[span], sem).wait()


def _span_rows(ref, row, n):
    return ref.at[pl.ds(pl.multiple_of(row * ROW_CHUNKS, ROW_CHUNKS), n * ROW_CHUNKS)]


def _for_each_run(tab_ref, fn):
    for e in range(N_EXPERTS):
        n = tab_ref[1, e]

        @pl.when(n > 0)
        def _(e=e, n=n):
            fn(tab_ref[0, e], tab_ref[2, e], n)


def _dispatch_kernel(tab_ref, fill_ref, x_ref, sc_ref, sh_ref, meta_ref, xs_hbm, xb0, xb1, zb, sem, fsem):
    i = pl.program_id(0)
    n_steps = pl.num_programs(0)
    tm = x_ref.shape[0]
    nrow = tm * TOP_K
    bufs = (xb0, xb1)

    @pl.when(i == 0)
    def _():
        zb[...] = jnp.zeros(zb.shape, F32)
        for j in range(N_FILL):
            n = fill_ref[1, j]

            @pl.when(n > 0)
            def _(j=j, n=n):
                fill = pltpu.make_async_copy(_span_rows(zb, 0, n), _span_rows(xs_hbm, fill_ref[0, j], n), fsem)
                fill.start()
                fill.wait()

    h = (_ln(x_ref[...]) * (1.0 + sc_ref[0, 0]) + sh_ref[0, 0]).astype(BF16)
    pos_t = meta_ref[...].astype(F32).T
    r = lax.broadcasted_iota(I32, (nrow, 1), 0)
    sel = jnp.zeros((nrow, tm), F32)
    for k in range(TOP_K):
        sel = sel + jnp.where(r == pos_t[TOP_K + k:TOP_K + k + 1, :].astype(I32), 1.0, 0.0)
    xt = jnp.dot(sel.astype(BF16), h, preferred_element_type=F32)

    for slot in range(2):
        @pl.when(i % 2 == slot)
        def _(slot=slot):
            buf = bufs[slot]

            @pl.when(i >= 2)
            def _():
                _wait_rows(buf, xs_hbm, sem.at[slot], nrow)

            for c in range(ROW_CHUNKS):
                buf[pl.ds(c, nrow, stride=ROW_CHUNKS), :] = xt[:, c * LANES:(c + 1) * LANES]
            _for_each_run(tab_ref, lambda srow, trow, n: pltpu.make_async_copy(
                _span_rows(buf, trow, n), _span_rows(xs_hbm, srow, n), sem.at[slot]).start())

            @pl.when(i == n_steps - 1)
            def _():
                _wait_rows(buf, xs_hbm, sem.at[slot], nrow)

                @pl.when(i >= 1)
                def _():
                    _wait_rows(bufs[1 - slot], xs_hbm, sem.at[1 - slot], nrow)


def _dispatch(x2, mod4, meta, tab, fill_tab, n_rows, seq):
    n, d = x2.shape
    tm = TOK_TILE
    per_b = seq // tm
    return pl.pallas_call(
        _dispatch_kernel,
        grid=(n // tm,),
        in_specs=[pl.BlockSpec((SUBLANES, LANES), lambda i: (i, 0), memory_space=pltpu.SMEM),
                  pl.BlockSpec((SUBLANES, LANES), lambda i: (0, 0), memory_space=pltpu.SMEM),
                  pl.BlockSpec((tm, d), lambda i: (i, 0)),
                  pl.BlockSpec((1, 1, 1, d), lambda i: (i // per_b, 4, 0, 0)),
                  pl.BlockSpec((1, 1, 1, d), lambda i: (i // per_b, 3, 0, 0)),
                  pl.BlockSpec((tm, LANES), lambda i: (i, 0))],
        out_specs=pl.BlockSpec(memory_space=pl.ANY),
        out_shape=jax.ShapeDtypeStruct((n_rows * ROW_CHUNKS, LANES), F32),
        scratch_shapes=[pltpu.VMEM((tm * TOP_K * ROW_CHUNKS, LANES), F32),
                        pltpu.VMEM((tm * TOP_K * ROW_CHUNKS, LANES), F32),
                        pltpu.VMEM((EXPERT_ROWS * ROW_CHUNKS, LANES), F32),
                        pltpu.SemaphoreType.DMA((2,)),
                        pltpu.SemaphoreType.DMA],
        compiler_params=_cparams(("arbitrary",)),
        name="moe_dispatch",
    )(tab, fill_tab, x2, mod4, mod4, meta)


def _expert_kernel(be_ref, first_ref, used_ref, xs_ref, wgu_ref, bgu_ref, wd_ref, bd_ref, ys_ref,
                   x_sc, wgu_sc, wd_sc):
    del be_ref
    g = pl.program_id(0)

    @pl.when(g < used_ref[0])
    def _():
        _expert_block(g, first_ref, xs_ref, wgu_ref, bgu_ref, wd_ref, bd_ref, ys_ref, x_sc, wgu_sc, wd_sc)

    @pl.when(g >= used_ref[0])
    def _():
        ys_ref[...] = jnp.zeros(ys_ref.shape, ys_ref.dtype)


def _expert_block(g, first_ref, xs_ref, wgu_ref, bgu_ref, wd_ref, bd_ref, ys_ref, x_sc, wgu_sc, wd_sc):
    rows = x_sc.shape[0]
    grp = 2 * LANES

    @pl.when(first_ref[g] == 1)
    def _():
        r = lax.broadcasted_iota(I32, (grp, grp), 0)
        c = lax.broadcasted_iota(I32, (grp, grp), 1)
        src = jnp.where(c < LANES, 2 * c, 2 * (c - LANES) + 1)
        sel = jnp.where(r == src, 1.0, 0.0).astype(BF16)
        for j in range(wgu_sc.shape[1] // grp):
            blk = wgu_ref[0, :, j * grp:(j + 1) * grp].astype(BF16)
            wgu_sc[:, j * grp:(j + 1) * grp] = jnp.dot(blk, sel, preferred_element_type=F32).astype(BF16)
        wd_sc[...] = wd_ref[0].astype(BF16)

    for c in range(ROW_CHUNKS):
        x_sc[:, c * LANES:(c + 1) * LANES] = xs_ref[pl.ds(c, rows, stride=ROW_CHUNKS), :].astype(BF16)
    gu = jnp.dot(x_sc[...], wgu_sc[...], preferred_element_type=F32) + bgu_ref[0]
    acts = []
    for j in range(gu.shape[1] // grp):
        glu = jnp.minimum(gu[:, j * grp:j * grp + LANES], SWIGLU_LIMIT)
        lin = jnp.clip(gu[:, j * grp + LANES:(j + 1) * grp], -SWIGLU_LIMIT, SWIGLU_LIMIT)
        acts.append((glu * _sigmoid(SWIGLU_ALPHA * glu) * (lin + 1.0)).astype(BF16))
    act = jnp.concatenate(acts, axis=1)
    y = jnp.dot(act, wd_sc[...], preferred_element_type=F32) + bd_ref[0]
    for c in range(ROW_CHUNKS):
        ys_ref[pl.ds(c, rows, stride=ROW_CHUNKS), :] = y[:, c * LANES:(c + 1) * LANES]


def _experts(xs, block_exp, first, n_used, layer, w_gu, b_gu_grouped, w_dn, b_dn):
    d = D_MODEL
    n_blocks = block_exp.shape[0]
    rb = EXPERT_ROWS
    last = lambda g, nu: jnp.minimum(g, nu[0] - 1)
    wmap = lambda g, be, fi, nu: (layer, be[last(g, nu)], 0, 0)
    bmap = lambda g, be, fi, nu: (be[last(g, nu)], 0, 0)
    return pl.pallas_call(
        _expert_kernel,
        grid_spec=pltpu.PrefetchScalarGridSpec(
            num_scalar_prefetch=3,
            grid=(n_blocks,),
            in_specs=[pl.BlockSpec((rb * ROW_CHUNKS, LANES), lambda g, be, fi, nu: (last(g, nu), 0)),
                      pl.BlockSpec((None, 1, d, 2 * d), wmap),
                      pl.BlockSpec((1, 1, 2 * d), bmap),
                      pl.BlockSpec((None, 1, d, d), wmap),
                      pl.BlockSpec((1, 1, d), bmap)],
            out_specs=pl.BlockSpec((rb * ROW_CHUNKS, LANES), lambda g, be, fi, nu: (g, 0)),
            scratch_shapes=[pltpu.VMEM((rb, d), BF16),
                            pltpu.VMEM((d, 2 * d), BF16),
                            pltpu.VMEM((d, d), BF16)]),
        out_shape=jax.ShapeDtypeStruct(xs.shape, F32),
        compiler_params=_cparams(("arbitrary",)),
        name="moe_experts",
    )(block_exp, first, n_used, xs, w_gu, b_gu_grouped, w_dn, b_dn)


def _combine_kernel(tab_ref, tab_next_ref, meta_ref, gate_ref, x_ref, g2_ref, lng_ref, lnb_ref, ys_hbm, o_ref,
                    yb0, yb1, sem):
    i = pl.program_id(0)
    n_steps = pl.num_programs(0)
    tm = x_ref.shape[0]
    nrow = tm * TOP_K
    bufs = (yb0, yb1)

    def fetch(t_ref, slot):
        _for_each_run(t_ref, lambda srow, trow, n: pltpu.make_async_copy(
            _span_rows(ys_hbm, srow, n), _span_rows(bufs[slot], trow, n), sem.at[slot]).start())

    @pl.when(i == 0)
    def _():
        fetch(tab_ref, 0)

    for slot in range(2):
        @pl.when(i % 2 == slot)
        def _(slot=slot):
            buf = bufs[slot]

            @pl.when(i + 1 < n_steps)
            def _():
                fetch(tab_next_ref, 1 - slot)

            _wait_rows(ys_hbm, buf, sem.at[slot], nrow)
            yt = jnp.concatenate([buf[pl.ds(c, nrow, stride=ROW_CHUNKS), :] for c in range(ROW_CHUNKS)],
                                 axis=1).astype(BF16)
            meta = meta_ref[...]
            gates = gate_ref[...]
            col = lax.broadcasted_iota(I32, (1, nrow), 1)
            w = jnp.zeros((tm, nrow), F32)
            for k in range(TOP_K):
                w = w + jnp.where(col == meta[:, TOP_K + k:TOP_K + k + 1], gates[:, k:k + 1], 0.0)
            w_hi = w.astype(BF16)
            w_lo = (w - w_hi.astype(F32)).astype(BF16)
            y = (jnp.dot(w_hi, yt, preferred_element_type=F32) + jnp.dot(w_lo, yt, preferred_element_type=F32))
            z = DEEPNORM_ALPHA * x_ref[...] + g2_ref[0, 0] * y
            o_ref[...] = _ln(z) * lng_ref[...] + lnb_ref[...]


def _combine(ys, tab, meta, gates, x2, mod4, lng, lnb, seq):
    n, d = x2.shape
    tm = TOK_TILE
    per_b = seq // tm
    last = n // tm - 1
    return pl.pallas_call(
        _combine_kernel,
        grid=(n // tm,),
        in_specs=[pl.BlockSpec((SUBLANES, LANES), lambda i: (i, 0), memory_space=pltpu.SMEM),
                  pl.BlockSpec((SUBLANES, LANES), lambda i: (jnp.minimum(i + 1, last), 0),
                               memory_space=pltpu.SMEM),
                  pl.BlockSpec((tm, LANES), lambda i: (i, 0)),
                  pl.BlockSpec((tm, LANES), lambda i: (i, 0)),
                  pl.BlockSpec((tm, d), lambda i: (i, 0)),
                  pl.BlockSpec((1, 1, 1, d), lambda i: (i // per_b, 5, 0, 0)),
                  pl.BlockSpec((1, d), lambda i: (0, 0)),
                  pl.BlockSpec((1, d), lambda i: (0, 0)),
                  pl.BlockSpec(memory_space=pl.ANY)],
        out_specs=pl.BlockSpec((tm, d), lambda i: (i, 0)),
        out_shape=jax.ShapeDtypeStruct((n, d), F32),
        scratch_shapes=[pltpu.VMEM((tm * TOP_K * ROW_CHUNKS, LANES), F32),
                        pltpu.VMEM((tm * TOP_K * ROW_CHUNKS, LANES), F32),
                        pltpu.SemaphoreType.DMA((2,))],
        compiler_params=_cparams(("arbitrary",)),
        name="moe_combine",
    )(tab, tab, meta, gates, x2, mod4, lng, lnb, ys)


def _moe(x2, mod4, layer, rw, rb, w_gu, b_gu, w_dn, b_dn, lng, lnb, seq):
    n, d = x2.shape
    meta, gates, tile_meta, cnt = _router(x2, mod4, rw, rb, seq)
    counts = cnt[0, :N_EXPERTS]
    padded = -(-counts // EXPERT_ROWS) * EXPERT_ROWS
    pend = jnp.cumsum(padded)
    pstart = pend - padded
    n_blocks = -(-(n * TOP_K + N_EXPERTS * (EXPERT_ROWS - 1)) // EXPERT_ROWS)
    n_rows = n_blocks * EXPERT_ROWS
    block_row0 = jnp.arange(n_blocks, dtype=I32) * EXPERT_ROWS
    block_exp = jnp.minimum(jnp.sum((pend[None, :] <= block_row0[:, None]).astype(I32), axis=1), N_EXPERTS - 1)
    first = jnp.concatenate([jnp.ones((1,), I32), (block_exp[1:] != block_exp[:-1]).astype(I32)])
    n_used = (pend[-1:] // EXPERT_ROWS).astype(I32)
    n_tiles = n // TOK_TILE
    start_row = jnp.zeros((1, 1, LANES), I32).at[0, 0, :N_EXPERTS].set(pstart)
    sel_row0 = (jnp.arange(SUBLANES) == 0).astype(I32)[None, :, None]
    tab = (tile_meta.reshape(n_tiles, SUBLANES, LANES) + sel_row0 * start_row).reshape(n_tiles * SUBLANES, LANES)
    tail0 = pend[-1] + jnp.arange(N_FILL - N_EXPERTS, dtype=I32) * EXPERT_ROWS
    fill_start = jnp.concatenate([pstart + counts, tail0])
    fill_len = jnp.concatenate([padded - counts, jnp.clip(n_rows - tail0, 0, EXPERT_ROWS)])
    fill_tab = jnp.zeros((SUBLANES, LANES), I32).at[0, :N_FILL].set(fill_start).at[1, :N_FILL].set(fill_len)
    xs = _dispatch(x2, mod4, meta, tab, fill_tab, n_rows, seq)
    e = b_gu.shape[0]
    b_grouped = b_gu.reshape(e, d // LANES, LANES, 2).transpose(0, 1, 3, 2).reshape(e, 1, 2 * d)
    ys = _experts(xs, block_exp, first, n_used, layer, w_gu, b_grouped, w_dn, b_dn[:, None, :])
    return _combine(ys, tab, meta, gates, x2, mod4, lng.reshape(1, d), lnb.reshape(1, d), seq)


def _permute_w_in(w):
    d = w.shape[0]
    off = {}
    o = 0
    for name, wd in (("a_q", 256), ("a_k", 256), ("a_v", 256), ("b_q", 384), ("b_kc", 128), ("b_vc", 128),
                     ("b_ks", 128), ("b_vs", 128), ("b_kw", 128), ("b_vw", 128), ("b_gate", 18),
                     ("c_q", 384), ("c_k", 384), ("c_v", 384), ("merge_gate", 3 * D_MODEL)):
        off[name] = (o, o + wd)
        o += wd
    col = lambda name: w[:, off[name][0]:off[name][1]]
    scale = HEAD_DIM ** -0.5 * float(np.log2(np.e))
    b_q = col("b_q").reshape(d, B_KV_GROUPS, B_REP, HEAD_DIM).transpose(0, 2, 1, 3).reshape(d, B_HEADS * HEAD_DIM)
    gate_pad = jnp.zeros((d, LANES - B_HEADS * 3), w.dtype)
    parts = [col("merge_gate"),
             col("a_q") * scale, col("a_k"), b_q * scale, col("b_kc"), col("b_ks"), col("b_kw"),
             col("c_q") * scale, col("c_k"),
             col("a_v"), col("b_vc"), col("b_vs"), col("b_vw"), col("c_v"),
             col("b_gate"), gate_pad]
    return jnp.concatenate(parts, axis=1).astype(BF16)


def _dilate(arr, tile, batch, seq, dil):
    a = arr[:, tile * LANES:(tile + 1) * LANES].reshape(batch, seq // dil, dil, LANES)
    return a.transpose(0, 2, 1, 3).reshape(batch * seq, LANES)


def _undilate(arr, batch, seq, dil):
    a = arr.reshape(batch, dil, seq // dil, LANES)
    return a.transpose(0, 2, 1, 3).reshape(batch * seq, LANES)


def _mixer_layer(x2, mod4, cos_t, sin_t, batch, seq, w_in, cmp_w1_k, cmp_w2_k, cmp_pe_k,
                 cmp_w1_v, cmp_w2_v, cmp_pe_v, w_a, w_b, w_c, w_out, lng, lnb):
    d = D_MODEL
    proj = _inproj(x2, mod4, _permute_w_in(w_in), cos_t, sin_t, seq)

    bias_a = _moba_select(proj, _kmean(proj, batch, seq), batch, seq)
    o_a = _bb_attn(proj, bias_a, batch=batch, seq=seq, n_pairs=A_HEADS // 2,
                   q_tile=lambda p: T_AQ + p, k_tile=lambda p: T_AK + p, v_tile=lambda p: T_AV + p,
                   bias_tile=lambda p: p, blk=MOBA_BLOCK)

    kc = _compress(proj[:, T_BKC * LANES:(T_BKC + 1) * LANES], cmp_w1_k, cmp_w2_k, cmp_pe_k, batch, seq)
    vc = _compress(proj[:, T_BVC * LANES:(T_BVC + 1) * LANES], cmp_w1_v, cmp_w2_v, cmp_pe_v, batch, seq)
    o_cmp, bias_b = _nsa_cmp(proj, kc, vc, batch, seq)
    o_slc = _bb_attn(proj, bias_b, batch=batch, seq=seq, n_pairs=B_REP,
                     q_tile=lambda p: T_BQ + p, k_tile=lambda p: T_BKS, v_tile=lambda p: T_BVS,
                     bias_tile=lambda p: 0, blk=SLC_BLOCK)
    o_win, _ = _band_attn(proj, proj, proj, batch=batch, seq=seq, n_pairs=B_REP,
                          q_tile=lambda p: T_BQ + p, k_tile=lambda p: T_BKW, v_tile=lambda p: T_BVW,
                          window=NSA_WINDOW, t=256)

    oc, lse = [], []
    for gi, (window, dil) in enumerate(DILATED_PAIRS):
        wlen = window // dil + 1
        if dil == 1:
            o, l = _band_attn(proj, proj, proj, batch=batch, seq=seq, n_pairs=1,
                              q_tile=lambda p: T_CQ + gi, k_tile=lambda p: T_CK + gi,
                              v_tile=lambda p: T_CV + gi, window=wlen, t=256)
        else:
            qd = _dilate(proj, T_CQ + gi, batch, seq, dil)
            kd = _dilate(proj, T_CK + gi, batch, seq, dil)
            vd = _dilate(proj, T_CV + gi, batch, seq, dil)
            o, l = _band_attn(qd, kd, vd, batch=batch * dil, seq=seq // dil, n_pairs=1,
                              q_tile=lambda p: 0, k_tile=lambda p: 0, v_tile=lambda p: 0,
                              window=wlen, t=256)
            o, l = _undilate(o, batch, seq, dil), _undilate(l, batch, seq, dil)
        oc.append(o)
        lse.append(l)

    w_b_perm = w_b.reshape(B_KV_GROUPS, B_REP, HEAD_DIM, d).transpose(1, 0, 2, 3).reshape(B_HEADS * HEAD_DIM, d)
    return _merge(o_a, o_cmp, o_slc, o_win, proj, oc, lse, x2, mod4,
                  w_a.astype(BF16), w_b_perm.astype(BF16), w_c.astype(BF16), w_out.astype(BF16),
                  lng.reshape(1, d), lnb.reshape(1, d), seq)


def _rope_tables(positions):
    inv = ROPE_THETA ** (-jnp.arange(0, HEAD_DIM, 2, dtype=F32) / HEAD_DIM)
    ang = positions.astype(F32).reshape(-1, 1) * inv[None, :]
    cos, sin = jnp.cos(ang), jnp.sin(ang)
    return jnp.tile(cos, (1, 4)), jnp.tile(jnp.concatenate([-sin, sin], axis=1), (1, 2))


def kernel(x, c, positions, w_ada, b_ada, w_in, cmp_w1_k, cmp_w2_k, cmp_pe_k, cmp_w1_v, cmp_w2_v, cmp_pe_v, w_branch_a, w_branch_b, w_branch_c, w_out, ln1_g, ln1_b, router_w, router_b, w_gate_up, b_gate_up, w_down, b_down, ln2_g, ln2_b):
    batch, seq, d = x.shape
    depth = w_in.shape[0]
    cos_t, sin_t = _rope_tables(positions)
    mod = _ada(c, w_ada, b_ada)
    x2 = x.reshape(batch * seq, d)
    for l in range(depth):
        mod4 = mod[l].reshape(batch, N_ADA, 1, d)
        x2 = _mixer_layer(x2, mod4, cos_t, sin_t, batch, seq, w_in[l], cmp_w1_k[l], cmp_w2_k[l], cmp_pe_k[l],
                          cmp_w1_v[l], cmp_w2_v[l], cmp_pe_v[l], w_branch_a[l], w_branch_b[l],
                          w_branch_c[l], w_out[l], ln1_g[l], ln1_b[l])
        x2 = _moe(x2, mod4, l, router_w[l], router_b[l], w_gate_up, b_gate_up[l], w_down, b_down[l],
                  ln2_g[l], ln2_b[l], seq)
    return x2.reshape(batch, seq, d)
```

```python
import functools

import numpy as np
import jax
import jax.numpy as jnp
from jax import lax
from jax.experimental import pallas as pl
from jax.experimental.pallas import tpu as pltpu

F32 = jnp.float32
BF16 = jnp.bfloat16
I32 = jnp.int32
HIGHEST = lax.Precision.HIGHEST

D_MODEL = 1024
DEPTH = 2
HEAD_DIM = 64
ROPE_THETA = 10000.0
LN_EPS = 1e-5
DEEPNORM_ALPHA = (2 * DEPTH) ** 0.25
N_ADA = 6
A_HEADS = 4
MOBA_BLOCK = 256
MOBA_TOPK = 3
B_HEADS = 6
B_KV_GROUPS = 2
B_REP = 3
CMP_BLOCK = 32
CMP_STRIDE = 16
CMP_HIDDEN = 128
SLC_BLOCK = 64
SLC_TOPK = 16
NSA_WINDOW = 512
FORCE_SCORE = 1e6
DILATED_PAIRS = ((128, 1), (512, 4), (2048, 16))
N_EXPERTS = 32
TOP_K = 4
SWIGLU_ALPHA = 1.702
SWIGLU_LIMIT = 7.0

LANES = 128
SUBLANES = 8
HALF = LANES // 2
NEG = -1e30
ROW_CHUNKS = D_MODEL // LANES
VMEM_LIMIT = 56 * 1024 * 1024

T_MG = 0
T_ROPE0, T_ROPE1 = 24, 40
T_AQ, T_AK, T_BQ, T_BKC, T_BKS, T_BKW, T_CQ, T_CK = 24, 26, 28, 31, 32, 33, 34, 37
T_AV, T_BVC, T_BVS, T_BVW, T_CV, T_BG = 40, 42, 43, 44, 45, 48
N_PROJ_TILES = 49
PROJ_W = N_PROJ_TILES * LANES

EXPERT_ROWS = 512
TOK_TILE = 256
N_FILL = 2 * N_EXPERTS + 2


def _cparams(sem, **kw):
    return pltpu.CompilerParams(dimension_semantics=sem, vmem_limit_bytes=VMEM_LIMIT, **kw)


def _lane_iota(shape=(1, LANES)):
    return lax.broadcasted_iota(I32, shape, len(shape) - 1)


def _sigmoid(x):
    return 1.0 / (1.0 + jnp.exp(-x))


def _ln(x):
    mu = jnp.mean(x, axis=-1, keepdims=True)
    xc = x - mu
    var = jnp.mean(xc * xc, axis=-1, keepdims=True)
    return xc * lax.rsqrt(var + LN_EPS)


def _dot_nt(a, b, precision=None):
    return lax.dot_general(a, b, (((1,), (1,)), ((), ())), precision=precision,
                           preferred_element_type=F32)


def _ada_kernel(c_ref, w_ref, b_ref, o_ref):
    c = c_ref[...]
    cond = c * _sigmoid(c)
    o_ref[0] = jnp.dot(cond, w_ref[0], precision=HIGHEST, preferred_element_type=F32) + b_ref[0]


def _ada(c, w_ada, b_ada):
    depth, d, n = w_ada.shape
    b = c.shape[0]
    tn = 1536
    return pl.pallas_call(
        _ada_kernel,
        grid=(depth, n // tn),
        in_specs=[pl.BlockSpec((b, d), lambda l, j: (0, 0)),
                  pl.BlockSpec((1, d, tn), lambda l, j: (l, 0, j)),
                  pl.BlockSpec((1, 1, tn), lambda l, j: (l, 0, j))],
        out_specs=pl.BlockSpec((1, b, tn), lambda l, j: (l, 0, j)),
        out_shape=jax.ShapeDtypeStruct((depth, b, n), F32),
        compiler_params=_cparams(("parallel", "parallel")),
        name="ada",
    )(c, w_ada, b_ada.reshape(depth, 1, n))


def _inproj_kernel(x_ref, sc_ref, sh_ref, w_ref, cos_ref, sin_ref, o_ref, *, chunks):
    h = _ln(x_ref[...]) * (1.0 + sc_ref[0, 0]) + sh_ref[0, 0]
    hb = h.astype(BF16)
    first_half = (_lane_iota() & (HEAD_DIM - 1)) < (HEAD_DIM // 2)
    for c0, cw, rope in chunks:
        acc = jnp.dot(hb, w_ref[:, c0:c0 + cw], preferred_element_type=F32)
        if rope:
            cos = cos_ref[...]
            sin = sin_ref[...]
            for t in range(cw // LANES):
                a = acc[:, t * LANES:(t + 1) * LANES]
                rot = jnp.where(first_half, pltpu.roll(a, LANES - HEAD_DIM // 2, 1),
                                pltpu.roll(a, HEAD_DIM // 2, 1))
                o_ref[:, c0 + t * LANES:c0 + (t + 1) * LANES] = (a * cos + rot * sin).astype(o_ref.dtype)
        else:
            o_ref[:, c0:c0 + cw] = acc.astype(o_ref.dtype)


def _inproj_chunks():
    chunks = []
    for lo, hi, rope in ((0, T_ROPE0, False), (T_ROPE0, T_ROPE1, True), (T_ROPE1, N_PROJ_TILES, False)):
        c = lo * LANES
        while c < hi * LANES:
            cw = min(512, hi * LANES - c)
            chunks.append((c, cw, rope))
            c += cw
    return tuple(chunks)


def _inproj(x2, mod4, w_perm, cos_t, sin_t, seq):
    n, d = x2.shape
    tm = 256
    per_b = seq // tm
    return pl.pallas_call(
        functools.partial(_inproj_kernel, chunks=_inproj_chunks()),
        grid=(n // tm,),
        in_specs=[pl.BlockSpec((tm, d), lambda i: (i, 0)),
                  pl.BlockSpec((1, 1, 1, d), lambda i: (i // per_b, 1, 0, 0)),
                  pl.BlockSpec((1, 1, 1, d), lambda i: (i // per_b, 0, 0, 0)),
                  pl.BlockSpec((d, PROJ_W), lambda i: (0, 0)),
                  pl.BlockSpec((tm, LANES), lambda i: (i, 0)),
                  pl.BlockSpec((tm, LANES), lambda i: (i, 0))],
        out_specs=pl.BlockSpec((tm, PROJ_W), lambda i: (i, 0)),
        out_shape=jax.ShapeDtypeStruct((n, PROJ_W), BF16),
        compiler_params=_cparams(("parallel",)),
        name="inproj",
    )(x2, mod4, mod4, w_perm, cos_t, sin_t)


def _bb_attn_kernel(q_ref, k_ref, v_ref, b_ref, o_ref, qa_sc, s_sc, mp_sc, m_sc, acc_sc,
                    *, blk_shift, tq, tu, big):
    i = pl.program_id(2)
    n_diag = tq // tu
    lane = _lane_iota()
    lo = lane < HALF
    q = q_ref[...]
    b = b_ref[...]
    qa_sc[0] = jnp.where(lo, q, b)
    qa_sc[1] = jnp.where(lo, b, q)
    mp_sc[...] = jnp.full(mp_sc.shape, -jnp.inf, F32)
    acc_sc[...] = jnp.zeros(acc_sc.shape, F32)

    def scores(u0, n_units, diagonal_last):
        tk = n_units * tu
        start = pl.multiple_of(u0 * tu, tu)
        k = k_ref[pl.ds(start, tk), :]
        kblk = (start + lax.broadcasted_iota(I32, (tk, 1), 0)) >> blk_shift
        onehot = jnp.where(kblk == (lane & (HALF - 1)), 1.0, 0.0).astype(BF16)
        ka = (jnp.where(lo, k, onehot), jnp.where(lo, onehot, k))
        row = lax.broadcasted_iota(I32, (tq, 1), 0)
        col = lax.broadcasted_iota(I32, (1, tu), 1)
        for h in range(2):
            s = _dot_nt(qa_sc[h], ka[h])
            mp = mp_sc[h]
            for u in range(n_units):
                su = s[:, u * tu:(u + 1) * tu]
                d = u - (n_units - n_diag)
                if diagonal_last and d >= 0:
                    su = jnp.where(col + d * tu <= row, su, NEG)
                s_sc[h, u0 + u] = su
                for c in range(tu // LANES):
                    mp = jnp.maximum(mp, su[:, c * LANES:(c + 1) * LANES])
            mp_sc[h] = mp

    def weighted(u0, n_units):
        tk = n_units * tu
        start = pl.multiple_of(u0 * tu, tu)
        v = v_ref[pl.ds(start, tk), :]
        one = jnp.ones_like(v)
        vs = (jnp.where(lo, v, one), jnp.where(lo, one, v))
        for h in range(2):
            m_row = m_sc[h]
            ps = []
            for u in range(n_units):
                su = s_sc[h, u0 + u]
                for c in range(tu // LANES):
                    ps.append(jnp.exp2(su[:, c * LANES:(c + 1) * LANES] - m_row).astype(BF16))
            acc_sc[h] = acc_sc[h] + jnp.dot(jnp.concatenate(ps, axis=1), vs[h], preferred_element_type=F32)

    n_full = i * n_diag
    n_big = n_full // big
    n_tail = n_full - n_big * big + n_diag

    def run(fn):
        def big_body(t, c):
            fn(t * big, big, False) if fn is scores else fn(t * big, big)
            return c

        lax.fori_loop(0, n_big, big_body, 0)
        for n in range(n_diag, big + n_diag, n_diag):
            @pl.when(n_tail == n)
            def _(n=n):
                fn(n_big * big, n, True) if fn is scores else fn(n_big * big, n)

    run(scores)
    for h in range(2):
        m_sc[h] = jnp.broadcast_to(jnp.max(mp_sc[h], axis=1, keepdims=True), (tq, LANES))
    run(weighted)
    outs = [acc_sc[h] / pltpu.roll(acc_sc[h], HALF, 1) for h in range(2)]
    o_ref[...] = jnp.where(lo, outs[0], outs[1]).astype(o_ref.dtype)


def _bb_attn(proj, bias, *, batch, seq, n_pairs, q_tile, k_tile, v_tile, bias_tile, blk):
    t, tu = 512, 256
    nq = seq // t
    n = batch * seq
    kern = functools.partial(_bb_attn_kernel, blk_shift=int(np.log2(blk)), tq=t, tu=tu, big=4)
    return pl.pallas_call(
        kern,
        grid=(batch, n_pairs, nq),
        in_specs=[pl.BlockSpec((t, LANES), lambda b, p, i: (b * nq + i, q_tile(p))),
                  pl.BlockSpec((seq, LANES), lambda b, p, i: (b, k_tile(p))),
                  pl.BlockSpec((seq, LANES), lambda b, p, i: (b, v_tile(p))),
                  pl.BlockSpec((t, LANES), lambda b, p, i: (b * nq + i, bias_tile(p)))],
        out_specs=pl.BlockSpec((t, LANES), lambda b, p, i: (b * nq + i, p)),
        out_shape=jax.ShapeDtypeStruct((n, n_pairs * LANES), BF16),
        scratch_shapes=[pltpu.VMEM((2, t, LANES), BF16),
                        pltpu.VMEM((2, seq // tu, t, tu), F32),
                        pltpu.VMEM((2, t, LANES), F32),
                        pltpu.VMEM((2, t, LANES), F32),
                        pltpu.VMEM((2, t, LANES), F32)],
        compiler_params=_cparams(("parallel", "parallel", "parallel")),
        name="bb_attn_%d" % blk,
    )(proj, proj, proj, bias)


def _band_kernel(q_ref, k_ref, v_ref, o_ref, lse_ref, *, tq, kw, window, seq):
    i = pl.program_id(2)
    lane = _lane_iota()
    lo = lane < HALF
    start = pl.multiple_of(jnp.clip(i * tq - (kw - tq), 0, seq - kw), LANES)
    q = q_ref[...]
    k = k_ref[pl.ds(start, kw), :]
    v = v_ref[pl.ds(start, kw), :]
    zero = jnp.zeros_like(q)
    qs = (jnp.where(lo, q, zero), jnp.where(lo, zero, q))
    rel = (i * tq + lax.broadcasted_iota(I32, (tq, 1), 0)) - (start + lax.broadcasted_iota(I32, (1, kw), 1))
    ok = (rel >= 0) & (rel < window)
    one = jnp.ones_like(v)
    vs = (jnp.where(lo, v, one), jnp.where(lo, one, v))
    outs, lses = [], []
    for h in range(2):
        s = jnp.where(ok, _dot_nt(qs[h], k), NEG)
        m = jnp.max(s, axis=1, keepdims=True)
        acc = jnp.dot(jnp.exp2(s - m).astype(BF16), vs[h], preferred_element_type=F32)
        l = pltpu.roll(acc, HALF, 1)
        outs.append(acc / l)
        lses.append(m + jnp.log2(l))
    o_ref[...] = jnp.where(lo, outs[0], outs[1]).astype(o_ref.dtype)
    lse_ref[...] = jnp.where(lo, lses[0], lses[1])


def _band_attn(q_arr, k_arr, v_arr, *, batch, seq, n_pairs, q_tile, k_tile, v_tile, window, t):
    t = min(t, seq)
    nq = seq // t
    kw = min(seq, t + -(-(window - 1) // LANES) * LANES)
    n = batch * seq
    kern = functools.partial(_band_kernel, tq=t, kw=kw, window=window, seq=seq)
    return pl.pallas_call(
        kern,
        grid=(batch, n_pairs, nq),
        in_specs=[pl.BlockSpec((t, LANES), lambda b, p, i: (b * nq + i, q_tile(p))),
                  pl.BlockSpec((seq, LANES), lambda b, p, i: (b, k_tile(p))),
                  pl.BlockSpec((seq, LANES), lambda b, p, i: (b, v_tile(p)))],
        out_specs=[pl.BlockSpec((t, LANES), lambda b, p, i: (b * nq + i, p)),
                   pl.BlockSpec((t, LANES), lambda b, p, i: (b * nq + i, p))],
        out_shape=[jax.ShapeDtypeStruct((n, n_pairs * LANES), BF16),
                   jax.ShapeDtypeStruct((n, n_pairs * LANES), F32)],
        compiler_params=_cparams(("parallel", "parallel", "parallel")),
        name="band_attn_%d" % window,
    )(q_arr, k_arr, v_arr)


def _kmean_kernel(k_ref, o_ref, *, nb):
    s = k_ref.shape[0]
    blk = lax.broadcasted_iota(I32, (nb, s), 1) >> int(np.log2(MOBA_BLOCK))
    avg = jnp.where(blk == lax.broadcasted_iota(I32, (nb, s), 0), 1.0 / MOBA_BLOCK, 0.0).astype(BF16)
    o_ref[0] = jnp.dot(avg, k_ref[...], preferred_element_type=F32)


def _kmean(proj, batch, seq):
    nb = seq // MOBA_BLOCK
    w = A_HEADS * HEAD_DIM
    return pl.pallas_call(
        functools.partial(_kmean_kernel, nb=nb),
        grid=(batch,),
        in_specs=[pl.BlockSpec((seq, w), lambda b: (b, T_AK * LANES // w))],
        out_specs=pl.BlockSpec((1, nb, w), lambda b: (b, 0, 0)),
        out_shape=jax.ShapeDtypeStruct((batch, nb, w), F32),
        compiler_params=_cparams(("parallel",)),
        name="moba_kmean",
    )(proj)


def _rank_desc(g, n_idx, n):
    rank = jnp.zeros(g.shape, I32)
    for m in range(n):
        c = g[m:m + 1, :]
        beats = (c > g) | ((c == g) & (n_idx > m))
        rank = rank + jnp.where(beats, 1, 0)
    return rank


def _moba_sel_kernel(q_ref, km_ref, b_ref, *, nb, n_sel):
    i = pl.program_id(2)
    t = q_ref.shape[0]
    q = q_ref[...].astype(F32)
    km = km_ref[0]
    lo = _lane_iota() < HALF
    zero = jnp.zeros_like(km)
    pad = jnp.zeros((HALF - nb, LANES), F32)
    kmt = jnp.concatenate([jnp.where(lo, zero, km), pad, jnp.where(lo, km, zero), pad], axis=0)
    gt = _dot_nt(kmt, q, precision=HIGHEST)
    n_idx = lax.broadcasted_iota(I32, (nb, 1), 0)
    valid = n_idx < i
    rows = []
    for r0 in (0, HALF):
        g = jnp.where(valid, gt[r0:r0 + nb, :], -jnp.inf)
        rank = _rank_desc(g, n_idx, nb)
        allowed = (valid & (rank < n_sel)) | (n_idx == i)
        rows.append(jnp.where(allowed, 0.0, NEG))
        rows.append(jnp.zeros((HALF - nb, t), F32))
    b_ref[...] = jnp.concatenate(rows, axis=0).T.astype(b_ref.dtype)


def _moba_select(proj, kmean, batch, seq):
    t = MOBA_BLOCK
    nq = seq // t
    nb = seq // MOBA_BLOCK
    n_sel = min(MOBA_TOPK, nb - 1)
    n_pairs = A_HEADS // 2
    return pl.pallas_call(
        functools.partial(_moba_sel_kernel, nb=nb, n_sel=n_sel),
        grid=(batch, n_pairs, nq),
        in_specs=[pl.BlockSpec((t, LANES), lambda b, p, i: (b * nq + i, T_AQ + p)),
                  pl.BlockSpec((1, nb, LANES), lambda b, p, i: (b, 0, p))],
        out_specs=pl.BlockSpec((t, LANES), lambda b, p, i: (b * nq + i, p)),
        out_shape=jax.ShapeDtypeStruct((batch * seq, n_pairs * LANES), BF16),
        compiler_params=_cparams(("parallel", "parallel", "parallel")),
        name="moba_select",
    )(proj, kmean)


def _compress_kernel(x_ref, w1_ref, pe_ref, w1f_ref, w2_ref, o_ref):
    x = x_ref[0]
    nblk = x.shape[0]
    outs = []
    for g in range(B_KV_GROUPS):
        u = jnp.dot(x, w1_ref[g, 0], preferred_element_type=F32)
        v = jnp.dot(x, w1_ref[g, 1], preferred_element_type=F32)
        pe_h = jnp.dot(pe_ref[...], w1f_ref[...], precision=HIGHEST, preferred_element_type=F32)
        hid = u + pltpu.roll(v, nblk - 1, 0) + pe_h[:1]
        hid = hid * _sigmoid(hid)
        outs.append(jnp.dot(hid, w2_ref[g], precision=HIGHEST, preferred_element_type=F32))
    o = outs[0] + outs[1]
    rows = lax.broadcasted_iota(I32, (nblk, 1), 0)
    o_ref[0] = jnp.where(rows < nblk - 1, o, 0.0).astype(o_ref.dtype)


def _compress(xt, w1, w2, pe, batch, seq):
    nblk = seq // CMP_STRIDE
    xg = xt.reshape(batch, nblk, CMP_STRIDE * LANES)
    w1r = w1.reshape(2, CMP_STRIDE, HEAD_DIM, CMP_HIDDEN)
    w1e = jnp.zeros((B_KV_GROUPS, 2, CMP_STRIDE, B_KV_GROUPS, HEAD_DIM, CMP_HIDDEN), F32)
    for g in range(B_KV_GROUPS):
        w1e = w1e.at[g, :, :, g].set(w1r)
    w1e = w1e.reshape(B_KV_GROUPS, 2, CMP_STRIDE * LANES, CMP_HIDDEN).astype(BF16)
    pe_flat = jnp.broadcast_to(pe.reshape(1, CMP_BLOCK * HEAD_DIM), (SUBLANES, CMP_BLOCK * HEAD_DIM))
    w2e = jnp.zeros((B_KV_GROUPS, CMP_HIDDEN, LANES), F32)
    for g in range(B_KV_GROUPS):
        w2e = w2e.at[g, :, g * HEAD_DIM:(g + 1) * HEAD_DIM].set(w2)
    return pl.pallas_call(
        _compress_kernel,
        grid=(batch,),
        in_specs=[pl.BlockSpec((1, nblk, CMP_STRIDE * LANES), lambda b: (b, 0, 0)),
                  pl.BlockSpec(w1e.shape, lambda b: (0, 0, 0, 0)),
                  pl.BlockSpec(pe_flat.shape, lambda b: (0, 0)),
                  pl.BlockSpec(w1.shape, lambda b: (0, 0)),
                  pl.BlockSpec(w2e.shape, lambda b: (0, 0, 0))],
        out_specs=pl.BlockSpec((1, nblk, LANES), lambda b: (b, 0, 0)),
        out_shape=jax.ShapeDtypeStruct((batch, nblk, LANES), BF16),
        compiler_params=_cparams(("parallel",)),
        name="nsa_compress",
    )(xg, w1e, pe_flat, w1, w2e)


def _nsa_cmp_kernel(q0_ref, q1_ref, q2_ref, kc_ref, vc_ref, ov_ref, o_ref, b_ref, *, n_sel, ns):
    i = pl.program_id(1)
    t = q0_ref.shape[0]
    ncp = kc_ref.shape[1]
    lane = _lane_iota()
    lo = lane < HALF
    kc = kc_ref[0]
    vc = vc_ref[0]
    tq = i * t + lax.broadcasted_iota(I32, (t, 1), 0)
    cmp_end = lax.broadcasted_iota(I32, (1, ncp), 1) * CMP_STRIDE + (CMP_BLOCK - 1)
    vis = cmp_end <= tq
    psum = [jnp.zeros((t, ncp), F32) for _ in range(B_KV_GROUPS)]
    for r, q_ref in enumerate((q0_ref, q1_ref, q2_ref)):
        q = q_ref[...]
        zero = jnp.zeros_like(q)
        outs = []
        for g in range(B_KV_GROUPS):
            qg = jnp.where(lo, q, zero) if g == 0 else jnp.where(lo, zero, q)
            s = jnp.where(vis, _dot_nt(qg, kc), NEG)
            m = jnp.max(s, axis=1, keepdims=True)
            m = jnp.where(m > 0.5 * NEG, m, 0.0)
            e = jnp.where(vis, jnp.exp2(s - m), 0.0)
            p = e / jnp.maximum(jnp.sum(e, axis=1, keepdims=True), 1e-30)
            psum[g] = psum[g] + p
            outs.append(jnp.dot(p.astype(BF16), vc, preferred_element_type=F32))
        o_ref[:, r * LANES:(r + 1) * LANES] = jnp.where(lo, outs[0], outs[1]).astype(o_ref.dtype)
    n_idx = lax.broadcasted_iota(I32, (ns, 1), 0)
    tcol = i * t + lax.broadcasted_iota(I32, (1, t), 1)
    qblk = tcol >> int(np.log2(SLC_BLOCK))
    forced = (n_idx == 0) | (n_idx == qblk) | (n_idx == qblk - 1)
    valid = n_idx <= qblk
    rows = []
    for g in (1, 0):
        imp = _dot_nt(ov_ref[...], psum[g], precision=HIGHEST)
        imp = jnp.where(forced, FORCE_SCORE, imp)
        imp = jnp.where(valid, imp, -jnp.inf)
        rank = _rank_desc(imp, n_idx, ns)
        allowed = valid & (rank < n_sel)
        rows.append(jnp.where(allowed, 0.0, NEG))
        if ns < HALF:
            rows.append(jnp.zeros((HALF - ns, t), F32))
    b_ref[...] = jnp.concatenate(rows, axis=0).T.astype(b_ref.dtype)


def _nsa_cmp(proj, kc, vc, batch, seq):
    t = 256
    nq = seq // t
    ns = seq // SLC_BLOCK
    n_sel = min(SLC_TOPK, ns)
    ncp = seq // CMP_STRIDE
    cs = np.arange(ncp)[None, :] * CMP_STRIDE
    ss = np.arange(ns)[:, None] * SLC_BLOCK
    ov = ((cs < ss + SLC_BLOCK) & (cs + CMP_BLOCK > ss)).astype(np.float32)
    ov[:, ncp - 1] = 0.0
    n = batch * seq
    qspec = [pl.BlockSpec((t, LANES), (lambda b, i, r=r: (b * nq + i, T_BQ + r))) for r in range(B_REP)]
    return pl.pallas_call(
        functools.partial(_nsa_cmp_kernel, n_sel=n_sel, ns=ns),
        grid=(batch, nq),
        in_specs=qspec + [pl.BlockSpec((1, ncp, LANES), lambda b, i: (b, 0, 0)),
                          pl.BlockSpec((1, ncp, LANES), lambda b, i: (b, 0, 0)),
                          pl.BlockSpec((ns, ncp), lambda b, i: (0, 0))],
        out_specs=[pl.BlockSpec((t, B_REP * LANES), lambda b, i: (b * nq + i, 0)),
                   pl.BlockSpec((t, LANES), lambda b, i: (b * nq + i, 0))],
        out_shape=[jax.ShapeDtypeStruct((n, B_REP * LANES), BF16),
                   jax.ShapeDtypeStruct((n, LANES), BF16)],
        compiler_params=_cparams(("parallel", "parallel")),
        name="nsa_cmp_select",
    )(proj, proj, proj, kc, vc, jnp.asarray(ov))


def _merge_kernel(oa_ref, ocmp_ref, oslc_ref, owin_ref, bg_ref, oc0_ref, oc1_ref, oc2_ref,
                  l0_ref, l1_ref, l2_ref, mg_ref, x_ref, g1_ref, wa_ref, wb_ref, wc_ref, wo_ref,
                  eg_ref, lng_ref, lnb_ref, o_ref):
    d = D_MODEL
    sg = _sigmoid(bg_ref[...].astype(F32))
    sg_hi = sg.astype(BF16)
    sg_lo = (sg - sg_hi.astype(F32)).astype(BF16)
    ob = jnp.zeros(ocmp_ref.shape, F32)
    for br, ref in enumerate((ocmp_ref, oslc_ref, owin_ref)):
        gexp = (jnp.dot(sg_hi, eg_ref[br], preferred_element_type=F32)
                + jnp.dot(sg_lo, eg_ref[br], preferred_element_type=F32))
        ob = ob + gexp * ref[...].astype(F32)
    l0, l1, l2 = l0_ref[...], l1_ref[...], l2_ref[...]
    mx = jnp.maximum(jnp.maximum(l0, l1), l2)
    e0, e1, e2 = jnp.exp2(l0 - mx), jnp.exp2(l1 - mx), jnp.exp2(l2 - mx)
    den = e0 + e1 + e2
    oc = ((e0 / den) * oc0_ref[...].astype(F32) + (e1 / den) * oc1_ref[...].astype(F32)
          + (e2 / den) * oc2_ref[...].astype(F32))
    pa = jnp.dot(oa_ref[...], wa_ref[...], preferred_element_type=F32)
    pb = jnp.dot(ob.astype(BF16), wb_ref[...], preferred_element_type=F32)
    pc = jnp.dot(oc.astype(BF16), wc_ref[...], preferred_element_type=F32)
    merged = (_sigmoid(mg_ref[:, 0:d].astype(F32)) * pa
              + _sigmoid(mg_ref[:, d:2 * d].astype(F32)) * pb
              + _sigmoid(mg_ref[:, 2 * d:3 * d].astype(F32)) * pc)
    y = jnp.dot(merged.astype(BF16), wo_ref[...], preferred_element_type=F32)
    z = DEEPNORM_ALPHA * x_ref[...] + g1_ref[0, 0] * y
    o_ref[...] = _ln(z) * lng_ref[...] + lnb_ref[...]


def _gate_expand():
    eg = np.zeros((3, LANES, B_HEADS * HEAD_DIM), np.float32)
    for g in range(B_KV_GROUPS):
        for r in range(B_REP):
            for br in range(3):
                c0 = (r * B_KV_GROUPS + g) * HEAD_DIM
                eg[br, (g * B_REP + r) * 3 + br, c0:c0 + HEAD_DIM] = 1.0
    return jnp.asarray(eg)


def _merge(o_a, o_cmp, o_slc, o_win, proj, oc, lse, x2, mod4, wa, wb, wc, wo, lng, lnb, seq):
    n, d = x2.shape
    tm = 256
    per_b = seq // tm
    row = lambda w: pl.BlockSpec((tm, w), lambda i: (i, 0))
    full = lambda a: pl.BlockSpec(a.shape, lambda i: (0,) * a.ndim)
    eg = _gate_expand().astype(BF16)
    return pl.pallas_call(
        _merge_kernel,
        grid=(n // tm,),
        in_specs=[row(o_a.shape[1]), row(o_cmp.shape[1]), row(o_slc.shape[1]), row(o_win.shape[1]),
                  pl.BlockSpec((tm, LANES), lambda i: (i, T_BG)),
                  row(LANES), row(LANES), row(LANES), row(LANES), row(LANES), row(LANES),
                  pl.BlockSpec((tm, 3 * d), lambda i: (i, T_MG)),
                  row(d),
                  pl.BlockSpec((1, 1, 1, d), lambda i: (i // per_b, 2, 0, 0)),
                  full(wa), full(wb), full(wc), full(wo), full(eg), full(lng), full(lnb)],
        out_specs=row(d),
        out_shape=jax.ShapeDtypeStruct((n, d), F32),
        compiler_params=_cparams(("parallel",)),
        name="merge_out",
    )(o_a, o_cmp, o_slc, o_win, proj, oc[0], oc[1], oc[2], lse[0], lse[1], lse[2],
      proj, x2, mod4, wa, wb, wc, wo, eg, lng, lnb)


def _router_kernel(x_ref, sc_ref, sh_ref, rw_ref, rb_ref, meta_ref, gate_ref, tile_ref, cnt_ref, carry_sc):
    i = pl.program_id(0)
    tm = TOK_TILE

    @pl.when(i == 0)
    def _():
        carry_sc[...] = jnp.zeros(carry_sc.shape, F32)

    for j in range(x_ref.shape[0] // tm):
        rows = pl.ds(j * tm, tm)
        _router_tile(x_ref[rows, :], sc_ref, sh_ref, rw_ref, rb_ref, meta_ref.at[rows], gate_ref.at[rows],
                     tile_ref.at[pl.ds(j * SUBLANES, SUBLANES)], carry_sc)
    cnt_ref[...] = jnp.broadcast_to(carry_sc[...], cnt_ref.shape).astype(I32)


def _router_tile(x, sc_ref, sh_ref, rw_ref, rb_ref, meta_ref, gate_ref, tile_ref, carry_sc):
    tm = x.shape[0]
    h = _ln(x) * (1.0 + sc_ref[0, 0]) + sh_ref[0, 0]
    lg = jnp.dot(h, rw_ref[...], precision=HIGHEST, preferred_element_type=F32) + rb_ref[...]
    lane = _lane_iota()
    lane_f = lane.astype(F32)
    onehots, vals, idxs = [], [], []
    for _ in range(TOP_K):
        m = jnp.max(lg, axis=1, keepdims=True)
        idx = jnp.min(jnp.where(lg == m, lane_f, float(LANES)), axis=1, keepdims=True).astype(I32)
        oh = lane == idx
        onehots.append(oh)
        vals.append(m)
        idxs.append(idx)
        lg = jnp.where(oh, -jnp.inf, lg)
    es = [jnp.exp(v - vals[0]) for v in vals]
    den = es[0] + es[1] + es[2] + es[3]
    cnt = jnp.zeros((tm, LANES), F32)
    for oh in onehots:
        cnt = cnt + jnp.where(oh, 1.0, 0.0)
    tri = jnp.where(lax.broadcasted_iota(I32, (tm, tm), 0) > lax.broadcasted_iota(I32, (tm, tm), 1), 1.0, 0.0)
    earlier = jnp.dot(tri.astype(BF16), cnt.astype(BF16), preferred_element_type=F32)
    tile_cnt = jnp.sum(cnt, axis=0, keepdims=True)
    upper = jnp.where(lax.broadcasted_iota(I32, (LANES, LANES), 0) < lax.broadcasted_iota(I32, (LANES, LANES), 1),
                      1.0, 0.0)
    run_off = jnp.dot(jnp.broadcast_to(tile_cnt, (SUBLANES, LANES)).astype(BF16), upper.astype(BF16),
                      preferred_element_type=F32)[:1]
    pos = earlier + run_off
    meta = jnp.zeros((tm, LANES), I32)
    gates = jnp.zeros((tm, LANES), F32)
    for k in range(TOP_K):
        local = jnp.sum(jnp.where(onehots[k], pos, 0.0), axis=1, keepdims=True).astype(I32)
        meta = jnp.where(lane == k, idxs[k], meta)
        meta = jnp.where(lane == TOP_K + k, local, meta)
        gates = jnp.where(lane == k, es[k] / den, gates)
    meta_ref[...] = meta
    gate_ref[...] = gates
    row = lax.broadcasted_iota(I32, (SUBLANES, LANES), 0)
    tile_ref[...] = jnp.where(row == 0, carry_sc[...], jnp.where(row == 1, tile_cnt, run_off)).astype(I32)
    carry_sc[...] = carry_sc[...] + tile_cnt


def _router(x2, mod4, rw, rb, seq):
    n, d = x2.shape
    sub = 1
    tm = sub * TOK_TILE
    per_b = seq // tm
    rw_p = jnp.zeros((d, LANES), F32).at[:, :N_EXPERTS].set(rw)
    rb_p = jnp.full((1, LANES), NEG, F32).at[0, :N_EXPERTS].set(rb)
    return pl.pallas_call(
        _router_kernel,
        grid=(n // tm,),
        in_specs=[pl.BlockSpec((tm, d), lambda i: (i, 0)),
                  pl.BlockSpec((1, 1, 1, d), lambda i: (i // per_b, 4, 0, 0)),
                  pl.BlockSpec((1, 1, 1, d), lambda i: (i // per_b, 3, 0, 0)),
                  pl.BlockSpec((d, LANES), lambda i: (0, 0)),
                  pl.BlockSpec((1, LANES), lambda i: (0, 0))],
        out_specs=[pl.BlockSpec((tm, LANES), lambda i: (i, 0)),
                   pl.BlockSpec((tm, LANES), lambda i: (i, 0)),
                   pl.BlockSpec((sub * SUBLANES, LANES), lambda i: (i, 0)),
                   pl.BlockSpec((SUBLANES, LANES), lambda i: (0, 0))],
        out_shape=[jax.ShapeDtypeStruct((n, LANES), I32),
                   jax.ShapeDtypeStruct((n, LANES), F32),
                   jax.ShapeDtypeStruct((n // TOK_TILE * SUBLANES, LANES), I32),
                   jax.ShapeDtypeStruct((SUBLANES, LANES), I32)],
        scratch_shapes=[pltpu.VMEM((1, LANES), F32)],
        compiler_params=_cparams(("arbitrary",)),
        name="router",
    )(x2, mod4, mod4, rw_p, rb_p)


def _row_slice(ref, r):
    return ref.at[pl.ds(pl.multiple_of(r * ROW_CHUNKS, ROW_CHUNKS), ROW_CHUNKS)]


def _wait_rows(src_like, dst_like, sem, n_rows):
    span = pl.ds(0, n_rows * ROW_CHUNKS)
    pltpu.make_async_copy(src_like.at[span], dst_like.at[span], sem).wait()


def _span_rows(ref, row, n):
    return ref.at[pl.ds(pl.multiple_of(row * ROW_CHUNKS, ROW_CHUNKS), n * ROW_CHUNKS)]


def _for_each_run(tab_ref, fn):
    for e in range(N_EXPERTS):
        n = tab_ref[1, e]

        @pl.when(n > 0)
        def _(e=e, n=n):
            fn(tab_ref[0, e], tab_ref[2, e], n)


def _dispatch_kernel(tab_ref, fill_ref, x_ref, sc_ref, sh_ref, meta_ref, xs_hbm, xb0, xb1, zb, sem, fsem):
    i = pl.program_id(0)
    n_steps = pl.num_programs(0)
    tm = x_ref.shape[0]
    nrow = tm * TOP_K
    bufs = (xb0, xb1)

    @pl.when(i == 0)
    def _():
        zb[...] = jnp.zeros(zb.shape, F32)
        for j in range(N_FILL):
            n = fill_ref[1, j]

            @pl.when(n > 0)
            def _(j=j, n=n):
                fill = pltpu.make_async_copy(_span_rows(zb, 0, n), _span_rows(xs_hbm, fill_ref[0, j], n), fsem)
                fill.start()
                fill.wait()

    h = (_ln(x_ref[...]) * (1.0 + sc_ref[0, 0]) + sh_ref[0, 0]).astype(BF16)
    pos_t = meta_ref[...].astype(F32).T
    r = lax.broadcasted_iota(I32, (nrow, 1), 0)
    sel = jnp.zeros((nrow, tm), F32)
    for k in range(TOP_K):
        sel = sel + jnp.where(r == pos_t[TOP_K + k:TOP_K + k + 1, :].astype(I32), 1.0, 0.0)
    xt = jnp.dot(sel.astype(BF16), h, preferred_element_type=F32)

    for slot in range(2):
        @pl.when(i % 2 == slot)
        def _(slot=slot):
            buf = bufs[slot]

            @pl.when(i >= 2)
            def _():
                _wait_rows(buf, xs_hbm, sem.at[slot], nrow)

            for c in range(ROW_CHUNKS):
                buf[pl.ds(c, nrow, stride=ROW_CHUNKS), :] = xt[:, c * LANES:(c + 1) * LANES]
            _for_each_run(tab_ref, lambda srow, trow, n: pltpu.make_async_copy(
                _span_rows(buf, trow, n), _span_rows(xs_hbm, srow, n), sem.at[slot]).start())

            @pl.when(i == n_steps - 1)
            def _():
                _wait_rows(buf, xs_hbm, sem.at[slot], nrow)

                @pl.when(i >= 1)
                def _():
                    _wait_rows(bufs[1 - slot], xs_hbm, sem.at[1 - slot], nrow)


def _dispatch(x2, mod4, meta, tab, fill_tab, n_rows, seq):
    n, d = x2.shape
    tm = TOK_TILE
    per_b = seq // tm
    return pl.pallas_call(
        _dispatch_kernel,
        grid=(n // tm,),
        in_specs=[pl.BlockSpec((SUBLANES, LANES), lambda i: (i, 0), memory_space=pltpu.SMEM),
                  pl.BlockSpec((SUBLANES, LANES), lambda i: (0, 0), memory_space=pltpu.SMEM),
                  pl.BlockSpec((tm, d), lambda i: (i, 0)),
                  pl.BlockSpec((1, 1, 1, d), lambda i: (i // per_b, 4, 0, 0)),
                  pl.BlockSpec((1, 1, 1, d), lambda i: (i // per_b, 3, 0, 0)),
                  pl.BlockSpec((tm, LANES), lambda i: (i, 0))],
        out_specs=pl.BlockSpec(memory_space=pl.ANY),
        out_shape=jax.ShapeDtypeStruct((n_rows * ROW_CHUNKS, LANES), F32),
        scratch_shapes=[pltpu.VMEM((tm * TOP_K * ROW_CHUNKS, LANES), F32),
                        pltpu.VMEM((tm * TOP_K * ROW_CHUNKS, LANES), F32),
                        pltpu.VMEM((EXPERT_ROWS * ROW_CHUNKS, LANES), F32),
                        pltpu.SemaphoreType.DMA((2,)),
                        pltpu.SemaphoreType.DMA],
        compiler_params=_cparams(("arbitrary",)),
        name="moe_dispatch",
    )(tab, fill_tab, x2, mod4, mod4, meta)


def _expert_kernel(be_ref, first_ref, used_ref, xs_ref, wgu_ref, bgu_ref, wd_ref, bd_ref, ys_ref,
                   x_sc, wgu_sc, wd_sc):
    del be_ref
    g = pl.program_id(0)

    @pl.when(g < used_ref[0])
    def _():
        _expert_block(g, first_ref, xs_ref, wgu_ref, bgu_ref, wd_ref, bd_ref, ys_ref, x_sc, wgu_sc, wd_sc)

    @pl.when(g >= used_ref[0])
    def _():
        ys_ref[...] = jnp.zeros(ys_ref.shape, ys_ref.dtype)


def _expert_block(g, first_ref, xs_ref, wgu_ref, bgu_ref, wd_ref, bd_ref, ys_ref, x_sc, wgu_sc, wd_sc):
    rows = x_sc.shape[0]
    grp = 2 * LANES

    @pl.when(first_ref[g] == 1)
    def _():
        r = lax.broadcasted_iota(I32, (grp, grp), 0)
        c = lax.broadcasted_iota(I32, (grp, grp), 1)
        src = jnp.where(c < LANES, 2 * c, 2 * (c - LANES) + 1)
        sel = jnp.where(r == src, 1.0, 0.0).astype(BF16)
        for j in range(wgu_sc.shape[1] // grp):
            blk = wgu_ref[0, :, j * grp:(j + 1) * grp].astype(BF16)
            wgu_sc[:, j * grp:(j + 1) * grp] = jnp.dot(blk, sel, preferred_element_type=F32).astype(BF16)
        wd_sc[...] = wd_ref[0].astype(BF16)

    for c in range(ROW_CHUNKS):
        x_sc[:, c * LANES:(c + 1) * LANES] = xs_ref[pl.ds(c, rows, stride=ROW_CHUNKS), :].astype(BF16)
    gu = jnp.dot(x_sc[...], wgu_sc[...], preferred_element_type=F32) + bgu_ref[0]
    acts = []
    for j in range(gu.shape[1] // grp):
        glu = jnp.minimum(gu[:, j * grp:j * grp + LANES], SWIGLU_LIMIT)
        lin = jnp.clip(gu[:, j * grp + LANES:(j + 1) * grp], -SWIGLU_LIMIT, SWIGLU_LIMIT)
        acts.append((glu * _sigmoid(SWIGLU_ALPHA * glu) * (lin + 1.0)).astype(BF16))
    act = jnp.concatenate(acts, axis=1)
    y = jnp.dot(act, wd_sc[...], preferred_element_type=F32) + bd_ref[0]
    for c in range(ROW_CHUNKS):
        ys_ref[pl.ds(c, rows, stride=ROW_CHUNKS), :] = y[:, c * LANES:(c + 1) * LANES]


def _experts(xs, block_exp, first, n_used, layer, w_gu, b_gu_grouped, w_dn, b_dn):
    d = D_MODEL
    n_blocks = block_exp.shape[0]
    rb = EXPERT_ROWS
    last = lambda g, nu: jnp.minimum(g, nu[0] - 1)
    wmap = lambda g, be, fi, nu: (layer, be[last(g, nu)], 0, 0)
    bmap = lambda g, be, fi, nu: (be[last(g, nu)], 0, 0)
    return pl.pallas_call(
        _expert_kernel,
        grid_spec=pltpu.PrefetchScalarGridSpec(
            num_scalar_prefetch=3,
            grid=(n_blocks,),
            in_specs=[pl.BlockSpec((rb * ROW_CHUNKS, LANES), lambda g, be, fi, nu: (last(g, nu), 0)),
                      pl.BlockSpec((None, 1, d, 2 * d), wmap),
                      pl.BlockSpec((1, 1, 2 * d), bmap),
                      pl.BlockSpec((None, 1, d, d), wmap),
                      pl.BlockSpec((1, 1, d), bmap)],
            out_specs=pl.BlockSpec((rb * ROW_CHUNKS, LANES), lambda g, be, fi, nu: (g, 0)),
            scratch_shapes=[pltpu.VMEM((rb, d), BF16),
                            pltpu.VMEM((d, 2 * d), BF16),
                            pltpu.VMEM((d, d), BF16)]),
        out_shape=jax.ShapeDtypeStruct(xs.shape, F32),
        compiler_params=_cparams(("arbitrary",)),
        name="moe_experts",
    )(block_exp, first, n_used, xs, w_gu, b_gu_grouped, w_dn, b_dn)


def _combine_kernel(tab_ref, tab_next_ref, meta_ref, gate_ref, x_ref, g2_ref, lng_ref, lnb_ref, ys_hbm, o_ref,
                    yb0, yb1, sem):
    i = pl.program_id(0)
    n_steps = pl.num_programs(0)
    tm = x_ref.shape[0]
    nrow = tm * TOP_K
    bufs = (yb0, yb1)

    def fetch(t_ref, slot):
        _for_each_run(t_ref, lambda srow, trow, n: pltpu.make_async_copy(
            _span_rows(ys_hbm, srow, n), _span_rows(bufs[slot], trow, n), sem.at[slot]).start())

    @pl.when(i == 0)
    def _():
        fetch(tab_ref, 0)

    for slot in range(2):
        @pl.when(i % 2 == slot)
        def _(slot=slot):
            buf = bufs[slot]

            @pl.when(i + 1 < n_steps)
            def _():
                fetch(tab_next_ref, 1 - slot)

            _wait_rows(ys_hbm, buf, sem.at[slot], nrow)
            yt = jnp.concatenate([buf[pl.ds(c, nrow, stride=ROW_CHUNKS), :] for c in range(ROW_CHUNKS)],
                                 axis=1).astype(BF16)
            meta = meta_ref[...]
            gates = gate_ref[...]
            col = lax.broadcasted_iota(I32, (1, nrow), 1)
            w = jnp.zeros((tm, nrow), F32)
            for k in range(TOP_K):
                w = w + jnp.where(col == meta[:, TOP_K + k:TOP_K + k + 1], gates[:, k:k + 1], 0.0)
            w_hi = w.astype(BF16)
            w_lo = (w - w_hi.astype(F32)).astype(BF16)
            y = (jnp.dot(w_hi, yt, preferred_element_type=F32) + jnp.dot(w_lo, yt, preferred_element_type=F32))
            z = DEEPNORM_ALPHA * x_ref[...] + g2_ref[0, 0] * y
            o_ref[...] = _ln(z) * lng_ref[...] + lnb_ref[...]


def _combine(ys, tab, meta, gates, x2, mod4, lng, lnb, seq):
    n, d = x2.shape
    tm = TOK_TILE
    per_b = seq // tm
    last = n // tm - 1
    return pl.pallas_call(
        _combine_kernel,
        grid=(n // tm,),
        in_specs=[pl.BlockSpec((SUBLANES, LANES), lambda i: (i, 0), memory_space=pltpu.SMEM),
                  pl.BlockSpec((SUBLANES, LANES), lambda i: (jnp.minimum(i + 1, last), 0),
                               memory_space=pltpu.SMEM),
                  pl.BlockSpec((tm, LANES), lambda i: (i, 0)),
                  pl.BlockSpec((tm, LANES), lambda i: (i, 0)),
                  pl.BlockSpec((tm, d), lambda i: (i, 0)),
                  pl.BlockSpec((1, 1, 1, d), lambda i: (i // per_b, 5, 0, 0)),
                  pl.BlockSpec((1, d), lambda i: (0, 0)),
                  pl.BlockSpec((1, d), lambda i: (0, 0)),
                  pl.BlockSpec(memory_space=pl.ANY)],
        out_specs=pl.BlockSpec((tm, d), lambda i: (i, 0)),
        out_shape=jax.ShapeDtypeStruct((n, d), F32),
        scratch_shapes=[pltpu.VMEM((tm * TOP_K * ROW_CHUNKS, LANES), F32),
                        pltpu.VMEM((tm * TOP_K * ROW_CHUNKS, LANES), F32),
                        pltpu.SemaphoreType.DMA((2,))],
        compiler_params=_cparams(("arbitrary",)),
        name="moe_combine",
    )(tab, tab, meta, gates, x2, mod4, lng, lnb, ys)


def _moe(x2, mod4, layer, rw, rb, w_gu, b_gu, w_dn, b_dn, lng, lnb, seq):
    n, d = x2.shape
    meta, gates, tile_meta, cnt = _router(x2, mod4, rw, rb, seq)
    counts = cnt[0, :N_EXPERTS]
    padded = -(-counts // EXPERT_ROWS) * EXPERT_ROWS
    pend = jnp.cumsum(padded)
    pstart = pend - padded
    n_blocks = -(-(n * TOP_K + N_EXPERTS * (EXPERT_ROWS - 1)) // EXPERT_ROWS)
    n_rows = n_blocks * EXPERT_ROWS
    block_row0 = jnp.arange(n_blocks, dtype=I32) * EXPERT_ROWS
    block_exp = jnp.minimum(jnp.sum((pend[None, :] <= block_row0[:, None]).astype(I32), axis=1), N_EXPERTS - 1)
    first = jnp.concatenate([jnp.ones((1,), I32), (block_exp[1:] != block_exp[:-1]).astype(I32)])
    n_used = (pend[-1:] // EXPERT_ROWS).astype(I32)
    n_tiles = n // TOK_TILE
    start_row = jnp.zeros((1, 1, LANES), I32).at[0, 0, :N_EXPERTS].set(pstart)
    sel_row0 = (jnp.arange(SUBLANES) == 0).astype(I32)[None, :, None]
    tab = (tile_meta.reshape(n_tiles, SUBLANES, LANES) + sel_row0 * start_row).reshape(n_tiles * SUBLANES, LANES)
    tail0 = pend[-1] + jnp.arange(N_FILL - N_EXPERTS, dtype=I32) * EXPERT_ROWS
    fill_start = jnp.concatenate([pstart + counts, tail0])
    fill_len = jnp.concatenate([padded - counts, jnp.clip(n_rows - tail0, 0, EXPERT_ROWS)])
    fill_tab = jnp.zeros((SUBLANES, LANES), I32).at[0, :N_FILL].set(fill_start).at[1, :N_FILL].set(fill_len)
    xs = _dispatch(x2, mod4, meta, tab, fill_tab, n_rows, seq)
    e = b_gu.shape[0]
    b_grouped = b_gu.reshape(e, d // LANES, LANES, 2).transpose(0, 1, 3, 2).reshape(e, 1, 2 * d)
    ys = _experts(xs, block_exp, first, n_used, layer, w_gu, b_grouped, w_dn, b_dn[:, None, :])
    return _combine(ys, tab, meta, gates, x2, mod4, lng.reshape(1, d), lnb.reshape(1, d), seq)


def _permute_w_in(w):
    d = w.shape[0]
    off = {}
    o = 0
    for name, wd in (("a_q", 256), ("a_k", 256), ("a_v", 256), ("b_q", 384), ("b_kc", 128), ("b_vc", 128),
                     ("b_ks", 128), ("b_vs", 128), ("b_kw", 128), ("b_vw", 128), ("b_gate", 18),
                     ("c_q", 384), ("c_k", 384), ("c_v", 384), ("merge_gate", 3 * D_MODEL)):
        off[name] = (o, o + wd)
        o += wd
    col = lambda name: w[:, off[name][0]:off[name][1]]
    scale = HEAD_DIM ** -0.5 * float(np.log2(np.e))
    b_q = col("b_q").reshape(d, B_KV_GROUPS, B_REP, HEAD_DIM).transpose(0, 2, 1, 3).reshape(d, B_HEADS * HEAD_DIM)
    gate_pad = jnp.zeros((d, LANES - B_HEADS * 3), w.dtype)
    parts = [col("merge_gate"),
             col("a_q") * scale, col("a_k"), b_q * scale, col("b_kc"), col("b_ks"), col("b_kw"),
             col("c_q") * scale, col("c_k"),
             col("a_v"), col("b_vc"), col("b_vs"), col("b_vw"), col("c_v"),
             col("b_gate"), gate_pad]
    return jnp.concatenate(parts, axis=1).astype(BF16)


def _dilate(arr, tile, batch, seq, dil):
    a = arr[:, tile * LANES:(tile + 1) * LANES].reshape(batch, seq // dil, dil, LANES)
    return a.transpose(0, 2, 1, 3).reshape(batch * seq, LANES)


def _undilate(arr, batch, seq, dil):
    a = arr.reshape(batch, dil, seq // dil, LANES)
    return a.transpose(0, 2, 1, 3).reshape(batch * seq, LANES)


def _mixer_layer(x2, mod4, cos_t, sin_t, batch, seq, w_in, cmp_w1_k, cmp_w2_k, cmp_pe_k,
                 cmp_w1_v, cmp_w2_v, cmp_pe_v, w_a, w_b, w_c, w_out, lng, lnb):
    d = D_MODEL
    proj = _inproj(x2, mod4, _permute_w_in(w_in), cos_t, sin_t, seq)

    bias_a = _moba_select(proj, _kmean(proj, batch, seq), batch, seq)
    o_a = _bb_attn(proj, bias_a, batch=batch, seq=seq, n_pairs=A_HEADS // 2,
                   q_tile=lambda p: T_AQ + p, k_tile=lambda p: T_AK + p, v_tile=lambda p: T_AV + p,
                   bias_tile=lambda p: p, blk=MOBA_BLOCK)

    kc = _compress(proj[:, T_BKC * LANES:(T_BKC + 1) * LANES], cmp_w1_k, cmp_w2_k, cmp_pe_k, batch, seq)
    vc = _compress(proj[:, T_BVC * LANES:(T_BVC + 1) * LANES], cmp_w1_v, cmp_w2_v, cmp_pe_v, batch, seq)
    o_cmp, bias_b = _nsa_cmp(proj, kc, vc, batch, seq)
    o_slc = _bb_attn(proj, bias_b, batch=batch, seq=seq, n_pairs=B_REP,
                     q_tile=lambda p: T_BQ + p, k_tile=lambda p: T_BKS, v_tile=lambda p: T_BVS,
                     bias_tile=lambda p: 0, blk=SLC_BLOCK)
    o_win, _ = _band_attn(proj, proj, proj, batch=batch, seq=seq, n_pairs=B_REP,
                          q_tile=lambda p: T_BQ + p, k_tile=lambda p: T_BKW, v_tile=lambda p: T_BVW,
                          window=NSA_WINDOW, t=256)

    oc, lse = [], []
    for gi, (window, dil) in enumerate(DILATED_PAIRS):
        wlen = window // dil + 1
        if dil == 1:
            o, l = _band_attn(proj, proj, proj, batch=batch, seq=seq, n_pairs=1,
                              q_tile=lambda p: T_CQ + gi, k_tile=lambda p: T_CK + gi,
                              v_tile=lambda p: T_CV + gi, window=wlen, t=256)
        else:
            qd = _dilate(proj, T_CQ + gi, batch, seq, dil)
            kd = _dilate(proj, T_CK + gi, batch, seq, dil)
            vd = _dilate(proj, T_CV + gi, batch, seq, dil)
            o, l = _band_attn(qd, kd, vd, batch=batch * dil, seq=seq // dil, n_pairs=1,
                              q_tile=lambda p: 0, k_tile=lambda p: 0, v_tile=lambda p: 0,
                              window=wlen, t=256)
            o, l = _undilate(o, batch, seq, dil), _undilate(l, batch, seq, dil)
        oc.append(o)
        lse.append(l)

    w_b_perm = w_b.reshape(B_KV_GROUPS, B_REP, HEAD_DIM, d).transpose(1, 0, 2, 3).reshape(B_HEADS * HEAD_DIM, d)
    return _merge(o_a, o_cmp, o_slc, o_win, proj, oc, lse, x2, mod4,
                  w_a.astype(BF16), w_b_perm.astype(BF16), w_c.astype(BF16), w_out.astype(BF16),
                  lng.reshape(1, d), lnb.reshape(1, d), seq)


def _rope_tables(positions):
    inv = ROPE_THETA ** (-jnp.arange(0, HEAD_DIM, 2, dtype=F32) / HEAD_DIM)
    ang = positions.astype(F32).reshape(-1, 1) * inv[None, :]
    cos, sin = jnp.cos(ang), jnp.sin(ang)
    return jnp.tile(cos, (1, 4)), jnp.tile(jnp.concatenate([-sin, sin], axis=1), (1, 2))


def kernel(x, c, positions, w_ada, b_ada, w_in, cmp_w1_k, cmp_w2_k, cmp_pe_k, cmp_w1_v, cmp_w2_v, cmp_pe_v, w_branch_a, w_branch_b, w_branch_c, w_out, ln1_g, ln1_b, router_w, router_b, w_gate_up, b_gate_up, w_down, b_down, ln2_g, ln2_b):
    batch, seq, d = x.shape
    depth = w_in.shape[0]
    cos_t, sin_t = _rope_tables(positions)
    mod = _ada(c, w_ada, b_ada)
    x2 = x.reshape(batch * seq, d)
    for l in range(depth):
        mod4 = mod[l].reshape(batch, N_ADA, 1, d)
        x2 = _mixer_layer(x2, mod4, cos_t, sin_t, batch, seq, w_in[l], cmp_w1_k[l], cmp_w2_k[l], cmp_pe_k[l],
                          cmp_w1_v[l], cmp_w2_v[l], cmp_pe_v[l], w_branch_a[l], w_branch_b[l],
                          w_branch_c[l], w_out[l], ln1_g[l], ln1_b[l])
        x2 = _moe(x2, mod4, l, router_w[l], router_b[l], w_gate_up, b_gate_up[l], w_down, b_down[l],
                  ln2_g[l], ln2_b[l], seq)
    return x2.reshape(batch, seq, d)
```

```python
import functools

import numpy as np
import jax
import jax.numpy as jnp
from jax import lax
from jax.experimental import pallas as pl
from jax.experimental.pallas import tpu as pltpu

F32 = jnp.float32
BF16 = jnp.bfloat16
I32 = jnp.int32
HIGHEST = lax.Precision.HIGHEST

D_MODEL = 1024
DEPTH = 2
HEAD_DIM = 64
ROPE_THETA = 10000.0
LN_EPS = 1e-5
DEEPNORM_ALPHA = (2 * DEPTH) ** 0.25
N_ADA = 6
A_HEADS = 4
MOBA_BLOCK = 256
MOBA_TOPK = 3
B_HEADS = 6
B_KV_GROUPS = 2
B_REP = 3
CMP_BLOCK = 32
CMP_STRIDE = 16
CMP_HIDDEN = 128
SLC_BLOCK = 64
SLC_TOPK = 16
NSA_WINDOW = 512
FORCE_SCORE = 1e6
DILATED_PAIRS = ((128, 1), (512, 4), (2048, 16))
N_EXPERTS = 32
TOP_K = 4
SWIGLU_ALPHA = 1.702
SWIGLU_LIMIT = 7.0

LANES = 128
SUBLANES = 8
HALF = LANES // 2
NEG = -1e30
ROW_CHUNKS = D_MODEL // LANES
VMEM_LIMIT = 56 * 1024 * 1024

T_MG = 0
T_ROPE0, T_ROPE1 = 24, 40
T_AQ, T_AK, T_BQ, T_BKC, T_BKS, T_BKW, T_CQ, T_CK = 24, 26, 28, 31, 32, 33, 34, 37
T_AV, T_BVC, T_BVS, T_BVW, T_CV, T_BG = 40, 42, 43, 44, 45, 48
N_PROJ_TILES = 49
PROJ_W = N_PROJ_TILES * LANES

EXPERT_ROWS = 512
TOK_TILE = 256
N_FILL = 2 * N_EXPERTS + 2


def _cparams(sem, **kw):
    return pltpu.CompilerParams(dimension_semantics=sem, vmem_limit_bytes=VMEM_LIMIT, **kw)


def _lane_iota(shape=(1, LANES)):
    return lax.broadcasted_iota(I32, shape, len(shape) - 1)


def _sigmoid(x):
    return 1.0 / (1.0 + jnp.exp(-x))


def _ln(x):
    mu = jnp.mean(x, axis=-1, keepdims=True)
    xc = x - mu
    var = jnp.mean(xc * xc, axis=-1, keepdims=True)
    return xc * lax.rsqrt(var + LN_EPS)


def _dot_nt(a, b, precision=None):
    return lax.dot_general(a, b, (((1,), (1,)), ((), ())), precision=precision,
                           preferred_element_type=F32)


def _ada_kernel(c_ref, w_ref, b_ref, o_ref):
    c = c_ref[...]
    cond = c * _sigmoid(c)
    o_ref[0] = jnp.dot(cond, w_ref[0], precision=HIGHEST, preferred_element_type=F32) + b_ref[0]


def _ada(c, w_ada, b_ada):
    depth, d, n = w_ada.shape
    b = c.shape[0]
    tn = 1536
    return pl.pallas_call(
        _ada_kernel,
        grid=(depth, n // tn),
        in_specs=[pl.BlockSpec((b, d), lambda l, j: (0, 0)),
                  pl.BlockSpec((1, d, tn), lambda l, j: (l, 0, j)),
                  pl.BlockSpec((1, 1, tn), lambda l, j: (l, 0, j))],
        out_specs=pl.BlockSpec((1, b, tn), lambda l, j: (l, 0, j)),
        out_shape=jax.ShapeDtypeStruct((depth, b, n), F32),
        compiler_params=_cparams(("parallel", "parallel")),
        name="ada",
    )(c, w_ada, b_ada.reshape(depth, 1, n))


def _inproj_kernel(x_ref, sc_ref, sh_ref, w_ref, cos_ref, sin_ref, o_ref, *, chunks):
    h = _ln(x_ref[...]) * (1.0 + sc_ref[0, 0]) + sh_ref[0, 0]
    hb = h.astype(BF16)
    first_half = (_lane_iota() & (HEAD_DIM - 1)) < (HEAD_DIM // 2)
    for c0, cw, rope in chunks:
        acc = jnp.dot(hb, w_ref[:, c0:c0 + cw], preferred_element_type=F32)
        if rope:
            cos = cos_ref[...]
            sin = sin_ref[...]
            for t in range(cw // LANES):
                a = acc[:, t * LANES:(t + 1) * LANES]
                rot = jnp.where(first_half, pltpu.roll(a, LANES - HEAD_DIM // 2, 1),
                                pltpu.roll(a, HEAD_DIM // 2, 1))
                o_ref[:, c0 + t * LANES:c0 + (t + 1) * LANES] = (a * cos + rot * sin).astype(o_ref.dtype)
        else:
            o_ref[:, c0:c0 + cw] = acc.astype(o_ref.dtype)


def _inproj_chunks():
    chunks = []
    for lo, hi, rope in ((0, T_ROPE0, False), (T_ROPE0, T_ROPE1, True), (T_ROPE1, N_PROJ_TILES, False)):
        c = lo * LANES
        while c < hi * LANES:
            cw = min(512, hi * LANES - c)
            chunks.append((c, cw, rope))
            c += cw
    return tuple(chunks)


def _inproj(x2, mod4, w_perm, cos_t, sin_t, seq):
    n, d = x2.shape
    tm = 256
    per_b = seq // tm
    return pl.pallas_call(
        functools.partial(_inproj_kernel, chunks=_inproj_chunks()),
        grid=(n // tm,),
        in_specs=[pl.BlockSpec((tm, d), lambda i: (i, 0)),
                  pl.BlockSpec((1, 1, 1, d), lambda i: (i // per_b, 1, 0, 0)),
                  pl.BlockSpec((1, 1, 1, d), lambda i: (i // per_b, 0, 0, 0)),
                  pl.BlockSpec((d, PROJ_W), lambda i: (0, 0)),
                  pl.BlockSpec((tm, LANES), lambda i: (i, 0)),
                  pl.BlockSpec((tm, LANES), lambda i: (i, 0))],
        out_specs=pl.BlockSpec((tm, PROJ_W), lambda i: (i, 0)),
        out_shape=jax.ShapeDtypeStruct((n, PROJ_W), BF16),
        compiler_params=_cparams(("parallel",)),
        name="inproj",
    )(x2, mod4, mod4, w_perm, cos_t, sin_t)


def _bb_attn_kernel(q_ref, k_ref, v_ref, b_ref, o_ref, qa_sc, s_sc, mp_sc, m_sc, acc_sc,
                    *, blk_shift, tq, tu, big):
    i = pl.program_id(2)
    n_diag = tq // tu
    lane = _lane_iota()
    lo = lane < HALF
    q = q_ref[...]
    b = b_ref[...]
    qa_sc[0] = jnp.where(lo, q, b)
    qa_sc[1] = jnp.where(lo, b, q)
    mp_sc[...] = jnp.full(mp_sc.shape, -jnp.inf, F32)
    acc_sc[...] = jnp.zeros(acc_sc.shape, F32)

    def scores(u0, n_units, diagonal_last):
        tk = n_units * tu
        start = pl.multiple_of(u0 * tu, tu)
        k = k_ref[pl.ds(start, tk), :]
        kblk = (start + lax.broadcasted_iota(I32, (tk, 1), 0)) >> blk_shift
        onehot = jnp.where(kblk == (lane & (HALF - 1)), 1.0, 0.0).astype(BF16)
        ka = (jnp.where(lo, k, onehot), jnp.where(lo, onehot, k))
        row = lax.broadcasted_iota(I32, (tq, 1), 0)
        col = lax.broadcasted_iota(I32, (1, tu), 1)
        for h in range(2):
            s = _dot_nt(qa_sc[h], ka[h])
            mp = mp_sc[h]
            for u in range(n_units):
                su = s[:, u * tu:(u + 1) * tu]
                d = u - (n_units - n_diag)
                if diagonal_last and d >= 0:
                    su = jnp.where(col + d * tu <= row, su, NEG)
                s_sc[h, u0 + u] = su
                for c in range(tu // LANES):
                    mp = jnp.maximum(mp, su[:, c * LANES:(c + 1) * LANES])
            mp_sc[h] = mp

    def weighted(u0, n_units):
        tk = n_units * tu
        start = pl.multiple_of(u0 * tu, tu)
        v = v_ref[pl.ds(start, tk), :]
        one = jnp.ones_like(v)
        vs = (jnp.where(lo, v, one), jnp.where(lo, one, v))
        for h in range(2):
            m_row = m_sc[h]
            ps = []
            for u in range(n_units):
                su = s_sc[h, u0 + u]
                for c in range(tu // LANES):
                    ps.append(jnp.exp2(su[:, c * LANES:(c + 1) * LANES] - m_row).astype(BF16))
            acc_sc[h] = acc_sc[h] + jnp.dot(jnp.concatenate(ps, axis=1), vs[h], preferred_element_type=F32)

    n_full = i * n_diag
    n_big = n_full // big
    n_tail = n_full - n_big * big + n_diag

    def run(fn):
        def big_body(t, c):
            fn(t * big, big, False) if fn is scores else fn(t * big, big)
            return c

        lax.fori_loop(0, n_big, big_body, 0)
        for n in range(n_diag, big + n_diag, n_diag):
            @pl.when(n_tail == n)
            def _(n=n):
                fn(n_big * big, n, True) if fn is scores else fn(n_big * big, n)

    run(scores)
    for h in range(2):
        m_sc[h] = jnp.broadcast_to(jnp.max(mp_sc[h], axis=1, keepdims=True), (tq, LANES))
    run(weighted)
    outs = [acc_sc[h] / pltpu.roll(acc_sc[h], HALF, 1) for h in range(2)]
    o_ref[...] = jnp.where(lo, outs[0], outs[1]).astype(o_ref.dtype)


def _bb_attn(proj, bias, *, batch, seq, n_pairs, q_tile, k_tile, v_tile, bias_tile, blk):
    t, tu = 512, 256
    nq = seq // t
    n = batch * seq
    kern = functools.partial(_bb_attn_kernel, blk_shift=int(np.log2(blk)), tq=t, tu=tu, big=4)
    return pl.pallas_call(
        kern,
        grid=(batch, n_pairs, nq),
        in_specs=[pl.BlockSpec((t, LANES), lambda b, p, i: (b * nq + i, q_tile(p))),
                  pl.BlockSpec((seq, LANES), lambda b, p, i: (b, k_tile(p))),
                  pl.BlockSpec((seq, LANES), lambda b, p, i: (b, v_tile(p))),
                  pl.BlockSpec((t, LANES), lambda b, p, i: (b * nq + i, bias_tile(p)))],
        out_specs=pl.BlockSpec((t, LANES), lambda b, p, i: (b * nq + i, p)),
        out_shape=jax.ShapeDtypeStruct((n, n_pairs * LANES), BF16),
        scratch_shapes=[pltpu.VMEM((2, t, LANES), BF16),
                        pltpu.VMEM((2, seq // tu, t, tu), F32),
                        pltpu.VMEM((2, t, LANES), F32),
                        pltpu.VMEM((2, t, LANES), F32),
                        pltpu.VMEM((2, t, LANES), F32)],
        compiler_params=_cparams(("parallel", "parallel", "parallel")),
        name="bb_attn_%d" % blk,
    )(proj, proj, proj, bias)


def _band_kernel(*refs, tq, kw, window, seq, n_q, group, tiles_per_seq):
    q_refs = refs[:n_q]
    k_ref, v_ref, o_ref, lse_ref = refs[n_q:n_q + 4]
    i = pl.program_id(1)
    lane = _lane_iota()
    lo = lane < HALF
    for g in range(group):
        if tiles_per_seq == 1:
            tile, base = 0, g * seq
        else:
            tile, base = i * group + g, 0
        start = jnp.clip(tile * tq - (kw - tq), 0, seq - kw)
        k = k_ref[pl.ds(pl.multiple_of(base + start, LANES), kw), :]
        v = v_ref[pl.ds(pl.multiple_of(base + start, LANES), kw), :]
        rel = (tile * tq + lax.broadcasted_iota(I32, (tq, 1), 0)) - (start + lax.broadcasted_iota(I32, (1, kw), 1))
        ok = (rel >= 0) & (rel < window)
        for qi in range(n_q):
            q = q_refs[qi][g * tq:(g + 1) * tq, :]
            zero = jnp.zeros_like(q)
            qs = (jnp.where(lo, q, zero), jnp.where(lo, zero, q))
            outs, lses = [], []
            for h in range(2):
                s = jnp.where(ok, _dot_nt(qs[h], k), NEG)
                m = jnp.max(s, axis=1, keepdims=True)
                p = jnp.exp2(s - m)
                l = jnp.sum(p, axis=1, keepdims=True)
                outs.append(jnp.dot(p.astype(BF16), v, preferred_element_type=F32) / l)
                lses.append(m + jnp.log2(l))
            rows, cols = slice(g * tq, (g + 1) * tq), slice(qi * LANES, (qi + 1) * LANES)
            o_ref[rows, cols] = jnp.where(lo, outs[0], outs[1]).astype(o_ref.dtype)
            lse_ref[rows, cols] = jnp.where(lo, lses[0], lses[1])


def _band_attn(q_arr, k_arr, v_arr, *, batch, seq, q_tiles, k_tile, v_tile, window, t, group):
    t = min(t, seq)
    tiles_per_seq = seq // t
    n = batch * seq
    n_q = len(q_tiles)
    if tiles_per_seq == 1:
        group = min(group, batch)
        grid = (batch // group, 1)
        kv_rows = group * seq
        row_blk = lambda b, i: b
    else:
        group = min(group, tiles_per_seq)
        grid = (batch, tiles_per_seq // group)
        kv_rows = seq
        row_blk = lambda b, i: b * (tiles_per_seq // group) + i
    kw = min(seq, t + -(-(window - 1) // LANES) * LANES)
    kern = functools.partial(_band_kernel, tq=t, kw=kw, window=window, seq=seq, n_q=n_q, group=group,
                             tiles_per_seq=tiles_per_seq)
    qspecs = [pl.BlockSpec((group * t, LANES), (lambda b, i, c=c: (row_blk(b, i), c))) for c in q_tiles]
    return pl.pallas_call(
        kern,
        grid=grid,
        in_specs=qspecs + [pl.BlockSpec((kv_rows, LANES), lambda b, i: (b, k_tile)),
                           pl.BlockSpec((kv_rows, LANES), lambda b, i: (b, v_tile))],
        out_specs=[pl.BlockSpec((group * t, n_q * LANES), lambda b, i: (row_blk(b, i), 0)),
                   pl.BlockSpec((group * t, n_q * LANES), lambda b, i: (row_blk(b, i), 0))],
        out_shape=[jax.ShapeDtypeStruct((n, n_q * LANES), BF16),
                   jax.ShapeDtypeStruct((n, n_q * LANES), F32)],
        compiler_params=_cparams(("parallel", "parallel")),
        name="band_attn_%d" % window,
    )(*([q_arr] * n_q), k_arr, v_arr)


def _kmean_kernel(k_ref, o_ref, *, nb):
    s = k_ref.shape[0]
    blk = lax.broadcasted_iota(I32, (nb, s), 1) >> int(np.log2(MOBA_BLOCK))
    avg = jnp.where(blk == lax.broadcasted_iota(I32, (nb, s), 0), 1.0 / MOBA_BLOCK, 0.0).astype(BF16)
    o_ref[0] = jnp.dot(avg, k_ref[...], preferred_element_type=F32)


def _kmean(proj, batch, seq):
    nb = seq // MOBA_BLOCK
    w = A_HEADS * HEAD_DIM
    return pl.pallas_call(
        functools.partial(_kmean_kernel, nb=nb),
        grid=(batch,),
        in_specs=[pl.BlockSpec((seq, w), lambda b: (b, T_AK * LANES // w))],
        out_specs=pl.BlockSpec((1, nb, w), lambda b: (b, 0, 0)),
        out_shape=jax.ShapeDtypeStruct((batch, nb, w), F32),
        compiler_params=_cparams(("parallel",)),
        name="moba_kmean",
    )(proj)


def _rank_desc(g, n_idx, n):
    del n_idx
    gs = min(SUBLANES, n)
    groups = [g[r0:r0 + gs, :] for r0 in range(0, n, gs)]
    ranks = [jnp.zeros(x.shape, I32) for x in groups]
    sub = lax.broadcasted_iota(I32, (gs, 1), 0)
    for m in range(n):
        c = g[m:m + 1, :]
        for gi, x in enumerate(groups):
            if gi * gs > m:
                beats = c >= x
            elif gi * gs + gs - 1 <= m:
                beats = c > x
            else:
                beats = (c > x) | ((c == x) & (sub + gi * gs > m))
            ranks[gi] = ranks[gi] + jnp.where(beats, 1, 0)
    return jnp.concatenate(ranks, axis=0)


def _moba_sel_kernel(q_ref, km_ref, b_ref, *, nb, n_sel):
    i = pl.program_id(2)
    t = q_ref.shape[0]
    q = q_ref[...].astype(F32)
    km = km_ref[0]
    lo = _lane_iota() < HALF
    zero = jnp.zeros_like(km)
    pad = jnp.zeros((HALF - nb, LANES), F32)
    kmt = jnp.concatenate([jnp.where(lo, zero, km), pad, jnp.where(lo, km, zero), pad], axis=0)
    gt = _dot_nt(kmt, q, precision=HIGHEST)
    n_idx = lax.broadcasted_iota(I32, (nb, 1), 0)
    valid = n_idx < i
    rows = []
    for r0 in (0, HALF):
        g = jnp.where(valid, gt[r0:r0 + nb, :], -jnp.inf)
        rank = _rank_desc(g, n_idx, nb)
        allowed = (valid & (rank < n_sel)) | (n_idx == i)
        rows.append(jnp.where(allowed, 0.0, NEG))
        rows.append(jnp.zeros((HALF - nb, t), F32))
    b_ref[...] = jnp.concatenate(rows, axis=0).T.astype(b_ref.dtype)


def _moba_select(proj, kmean, batch, seq):
    t = MOBA_BLOCK
    nq = seq // t
    nb = seq // MOBA_BLOCK
    n_sel = min(MOBA_TOPK, nb - 1)
    n_pairs = A_HEADS // 2
    return pl.pallas_call(
        functools.partial(_moba_sel_kernel, nb=nb, n_sel=n_sel),
        grid=(batch, n_pairs, nq),
        in_specs=[pl.BlockSpec((t, LANES), lambda b, p, i: (b * nq + i, T_AQ + p)),
                  pl.BlockSpec((1, nb, LANES), lambda b, p, i: (b, 0, p))],
        out_specs=pl.BlockSpec((t, LANES), lambda b, p, i: (b * nq + i, p)),
        out_shape=jax.ShapeDtypeStruct((batch * seq, n_pairs * LANES), BF16),
        compiler_params=_cparams(("parallel", "parallel", "parallel")),
        name="moba_select",
    )(proj, kmean)


def _compress_kernel(x_ref, w1_ref, pe_ref, w1f_ref, w2_ref, o_ref):
    x = x_ref[0]
    nblk = x.shape[0]
    outs = []
    for g in range(B_KV_GROUPS):
        u = jnp.dot(x, w1_ref[g, 0], preferred_element_type=F32)
        v = jnp.dot(x, w1_ref[g, 1], preferred_element_type=F32)
        pe_h = jnp.dot(pe_ref[...], w1f_ref[...], precision=HIGHEST, preferred_element_type=F32)
        hid = u + pltpu.roll(v, nblk - 1, 0) + pe_h[:1]
        hid = hid * _sigmoid(hid)
        outs.append(jnp.dot(hid, w2_ref[g], precision=HIGHEST, preferred_element_type=F32))
    o = outs[0] + outs[1]
    rows = lax.broadcasted_iota(I32, (nblk, 1), 0)
    o_ref[0] = jnp.where(rows < nblk - 1, o, 0.0).astype(o_ref.dtype)


def _compress(xt, w1, w2, pe, batch, seq):
    nblk = seq // CMP_STRIDE
    xg = xt.reshape(batch, nblk, CMP_STRIDE * LANES)
    w1r = w1.reshape(2, CMP_STRIDE, HEAD_DIM, CMP_HIDDEN)
    w1e = jnp.zeros((B_KV_GROUPS, 2, CMP_STRIDE, B_KV_GROUPS, HEAD_DIM, CMP_HIDDEN), F32)
    for g in range(B_KV_GROUPS):
        w1e = w1e.at[g, :, :, g].set(w1r)
    w1e = w1e.reshape(B_KV_GROUPS, 2, CMP_STRIDE * LANES, CMP_HIDDEN).astype(BF16)
    pe_flat = jnp.broadcast_to(pe.reshape(1, CMP_BLOCK * HEAD_DIM), (SUBLANES, CMP_BLOCK * HEAD_DIM))
    w2e = jnp.zeros((B_KV_GROUPS, CMP_HIDDEN, LANES), F32)
    for g in range(B_KV_GROUPS):
        w2e = w2e.at[g, :, g * HEAD_DIM:(g + 1) * HEAD_DIM].set(w2)
    return pl.pallas_call(
        _compress_kernel,
        grid=(batch,),
        in_specs=[pl.BlockSpec((1, nblk, CMP_STRIDE * LANES), lambda b: (b, 0, 0)),
                  pl.BlockSpec(w1e.shape, lambda b: (0, 0, 0, 0)),
                  pl.BlockSpec(pe_flat.shape, lambda b: (0, 0)),
                  pl.BlockSpec(w1.shape, lambda b: (0, 0)),
                  pl.BlockSpec(w2e.shape, lambda b: (0, 0, 0))],
        out_specs=pl.BlockSpec((1, nblk, LANES), lambda b: (b, 0, 0)),
        out_shape=jax.ShapeDtypeStruct((batch, nblk, LANES), BF16),
        compiler_params=_cparams(("parallel",)),
        name="nsa_compress",
    )(xg, w1e, pe_flat, w1, w2e)


def _nsa_cmp_kernel(q0_ref, q1_ref, q2_ref, kc_ref, vc_ref, ov_ref, o_ref, b_ref, *, n_sel, ns):
    i = pl.program_id(1)
    t = q0_ref.shape[0]
    ncp = kc_ref.shape[1]
    lane = _lane_iota()
    lo = lane < HALF
    kc = kc_ref[0]
    vc = vc_ref[0]
    tq = i * t + lax.broadcasted_iota(I32, (t, 1), 0)
    cmp_end = lax.broadcasted_iota(I32, (1, ncp), 1) * CMP_STRIDE + (CMP_BLOCK - 1)
    vis = cmp_end <= tq
    psum = [jnp.zeros((t, ncp), F32) for _ in range(B_KV_GROUPS)]
    for r, q_ref in enumerate((q0_ref, q1_ref, q2_ref)):
        q = q_ref[...]
        zero = jnp.zeros_like(q)
        outs = []
        for g in range(B_KV_GROUPS):
            qg = jnp.where(lo, q, zero) if g == 0 else jnp.where(lo, zero, q)
            s = jnp.where(vis, _dot_nt(qg, kc), NEG)
            m = jnp.max(s, axis=1, keepdims=True)
            m = jnp.where(m > 0.5 * NEG, m, 0.0)
            e = jnp.where(vis, jnp.exp2(s - m), 0.0)
            p = e / jnp.maximum(jnp.sum(e, axis=1, keepdims=True), 1e-30)
            psum[g] = psum[g] + p
            outs.append(jnp.dot(p.astype(BF16), vc, preferred_element_type=F32))
        o_ref[:, r * LANES:(r + 1) * LANES] = jnp.where(lo, outs[0], outs[1]).astype(o_ref.dtype)
    n_idx = lax.broadcasted_iota(I32, (ns, 1), 0)
    tcol = i * t + lax.broadcasted_iota(I32, (1, t), 1)
    qblk = tcol >> int(np.log2(SLC_BLOCK))
    forced = (n_idx == 0) | (n_idx == qblk) | (n_idx == qblk - 1)
    valid = n_idx <= qblk
    rows = []
    for g in (1, 0):
        imp = _dot_nt(ov_ref[...], psum[g], precision=HIGHEST)
        imp = jnp.where(forced, FORCE_SCORE, imp)
        imp = jnp.where(valid, imp, -jnp.inf)
        rank = _rank_desc(imp, n_idx, ns)
        allowed = valid & (rank < n_sel)
        rows.append(jnp.where(allowed, 0.0, NEG))
        if ns < HALF:
            rows.append(jnp.zeros((HALF - ns, t), F32))
    b_ref[...] = jnp.concatenate(rows, axis=0).T.astype(b_ref.dtype)


def _nsa_cmp(proj, kc, vc, batch, seq):
    t = 256
    nq = seq // t
    ns = seq // SLC_BLOCK
    n_sel = min(SLC_TOPK, ns)
    ncp = seq // CMP_STRIDE
    cs = np.arange(ncp)[None, :] * CMP_STRIDE
    ss = np.arange(ns)[:, None] * SLC_BLOCK
    ov = ((cs < ss + SLC_BLOCK) & (cs + CMP_BLOCK > ss)).astype(np.float32)
    ov[:, ncp - 1] = 0.0
    n = batch * seq
    qspec = [pl.BlockSpec((t, LANES), (lambda b, i, r=r: (b * nq + i, T_BQ + r))) for r in range(B_REP)]
    return pl.pallas_call(
        functools.partial(_nsa_cmp_kernel, n_sel=n_sel, ns=ns),
        grid=(batch, nq),
        in_specs=qspec + [pl.BlockSpec((1, ncp, LANES), lambda b, i: (b, 0, 0)),
                          pl.BlockSpec((1, ncp, LANES), lambda b, i: (b, 0, 0)),
                          pl.BlockSpec((ns, ncp), lambda b, i: (0, 0))],
        out_specs=[pl.BlockSpec((t, B_REP * LANES), lambda b, i: (b * nq + i, 0)),
                   pl.BlockSpec((t, LANES), lambda b, i: (b * nq + i, 0))],
        out_shape=[jax.ShapeDtypeStruct((n, B_REP * LANES), BF16),
                   jax.ShapeDtypeStruct((n, LANES), BF16)],
        compiler_params=_cparams(("parallel", "parallel")),
        name="nsa_cmp_select",
    )(proj, proj, proj, kc, vc, jnp.asarray(ov))


def _merge_kernel(oa_ref, ocmp_ref, oslc_ref, owin_ref, bg_ref, oc0_ref, oc1_ref, oc2_ref,
                  l0_ref, l1_ref, l2_ref, mg_ref, x_ref, g1_ref, wa_ref, wb_ref, wc_ref, wo_ref,
                  eg_ref, lng_ref, lnb_ref, o_ref):
    d = D_MODEL
    sg = _sigmoid(bg_ref[...].astype(F32))
    sg_hi = sg.astype(BF16)
    sg_lo = (sg - sg_hi.astype(F32)).astype(BF16)
    ob = jnp.zeros(ocmp_ref.shape, F32)
    for br, ref in enumerate((ocmp_ref, oslc_ref, owin_ref)):
        gexp = (jnp.dot(sg_hi, eg_ref[br], preferred_element_type=F32)
                + jnp.dot(sg_lo, eg_ref[br], preferred_element_type=F32))
        ob = ob + gexp * ref[...].astype(F32)
    l0, l1, l2 = l0_ref[...], l1_ref[...], l2_ref[...]
    mx = jnp.maximum(jnp.maximum(l0, l1), l2)
    e0, e1, e2 = jnp.exp2(l0 - mx), jnp.exp2(l1 - mx), jnp.exp2(l2 - mx)
    den = e0 + e1 + e2
    oc = ((e0 / den) * oc0_ref[...].astype(F32) + (e1 / den) * oc1_ref[...].astype(F32)
          + (e2 / den) * oc2_ref[...].astype(F32))
    pa = jnp.dot(oa_ref[...], wa_ref[...], preferred_element_type=F32)
    pb = jnp.dot(ob.astype(BF16), wb_ref[...], preferred_element_type=F32)
    pc = jnp.dot(oc.astype(BF16), wc_ref[...], preferred_element_type=F32)
    merged = (_sigmoid(mg_ref[:, 0:d].astype(F32)) * pa
              + _sigmoid(mg_ref[:, d:2 * d].astype(F32)) * pb
              + _sigmoid(mg_ref[:, 2 * d:3 * d].astype(F32)) * pc)
    y = jnp.dot(merged.astype(BF16), wo_ref[...], preferred_element_type=F32)
    z = DEEPNORM_ALPHA * x_ref[...] + g1_ref[0, 0] * y
    o_ref[...] = _ln(z) * lng_ref[...] + lnb_ref[...]


def _gate_expand():
    eg = np.zeros((3, LANES, B_HEADS * HEAD_DIM), np.float32)
    for g in range(B_KV_GROUPS):
        for r in range(B_REP):
            for br in range(3):
                c0 = (r * B_KV_GROUPS + g) * HEAD_DIM
                eg[br, (g * B_REP + r) * 3 + br, c0:c0 + HEAD_DIM] = 1.0
    return jnp.asarray(eg)


def _merge(o_a, o_cmp, o_slc, o_win, proj, oc, lse, x2, mod4, wa, wb, wc, wo, lng, lnb, seq):
    n, d = x2.shape
    tm = 256
    per_b = seq // tm
    row = lambda w: pl.BlockSpec((tm, w), lambda i: (i, 0))
    full = lambda a: pl.BlockSpec(a.shape, lambda i: (0,) * a.ndim)
    eg = _gate_expand().astype(BF16)
    return pl.pallas_call(
        _merge_kernel,
        grid=(n // tm,),
        in_specs=[row(o_a.shape[1]), row(o_cmp.shape[1]), row(o_slc.shape[1]), row(o_win.shape[1]),
                  pl.BlockSpec((tm, LANES), lambda i: (i, T_BG)),
                  row(LANES), row(LANES), row(LANES), row(LANES), row(LANES), row(LANES),
                  pl.BlockSpec((tm, 3 * d), lambda i: (i, T_MG)),
                  row(d),
                  pl.BlockSpec((1, 1, 1, d), lambda i: (i // per_b, 2, 0, 0)),
                  full(wa), full(wb), full(wc), full(wo), full(eg), full(lng), full(lnb)],
        out_specs=row(d),
        out_shape=jax.ShapeDtypeStruct((n, d), F32),
        compiler_params=_cparams(("parallel",)),
        name="merge_out",
    )(o_a, o_cmp, o_slc, o_win, proj, oc[0], oc[1], oc[2], lse[0], lse[1], lse[2],
      proj, x2, mod4, wa, wb, wc, wo, eg, lng, lnb)


def _router_kernel(x_ref, sc_ref, sh_ref, rw_ref, rb_ref, meta_ref, gate_ref, tile_ref, cnt_ref, carry_sc):
    i = pl.program_id(0)
    tm = TOK_TILE

    @pl.when(i == 0)
    def _():
        carry_sc[...] = jnp.zeros(carry_sc.shape, F32)

    for j in range(x_ref.shape[0] // tm):
        rows = pl.ds(j * tm, tm)
        _router_tile(x_ref[rows, :], sc_ref, sh_ref, rw_ref, rb_ref, meta_ref.at[rows], gate_ref.at[rows],
                     tile_ref.at[pl.ds(j * SUBLANES, SUBLANES)], carry_sc)
    cnt_ref[...] = jnp.broadcast_to(carry_sc[...], cnt_ref.shape).astype(I32)


def _router_tile(x, sc_ref, sh_ref, rw_ref, rb_ref, meta_ref, gate_ref, tile_ref, carry_sc):
    tm = x.shape[0]
    h = _ln(x) * (1.0 + sc_ref[0, 0]) + sh_ref[0, 0]
    lg = jnp.dot(h, rw_ref[...], precision=HIGHEST, preferred_element_type=F32) + rb_ref[...]
    lane = _lane_iota()
    lane_f = lane.astype(F32)
    onehots, vals, idxs = [], [], []
    for _ in range(TOP_K):
        m = jnp.max(lg, axis=1, keepdims=True)
        idx = jnp.min(jnp.where(lg == m, lane_f, float(LANES)), axis=1, keepdims=True).astype(I32)
        oh = lane == idx
        onehots.append(oh)
        vals.append(m)
        idxs.append(idx)
        lg = jnp.where(oh, -jnp.inf, lg)
    es = [jnp.exp(v - vals[0]) for v in vals]
    den = es[0] + es[1] + es[2] + es[3]
    cnt = jnp.zeros((tm, LANES), F32)
    for oh in onehots:
        cnt = cnt + jnp.where(oh, 1.0, 0.0)
    tri = jnp.where(lax.broadcasted_iota(I32, (tm, tm), 0) > lax.broadcasted_iota(I32, (tm, tm), 1), 1.0, 0.0)
    earlier = jnp.dot(tri.astype(BF16), cnt.astype(BF16), preferred_element_type=F32)
    tile_cnt = jnp.sum(cnt, axis=0, keepdims=True)
    upper = jnp.where(lax.broadcasted_iota(I32, (LANES, LANES), 0) < lax.broadcasted_iota(I32, (LANES, LANES), 1),
                      1.0, 0.0)
    run_off = jnp.dot(jnp.broadcast_to(tile_cnt, (SUBLANES, LANES)).astype(BF16), upper.astype(BF16),
                      preferred_element_type=F32)[:1]
    pos = earlier + run_off
    meta = jnp.zeros((tm, LANES), I32)
    gates = jnp.zeros((tm, LANES), F32)
    for k in range(TOP_K):
        local = jnp.sum(jnp.where(onehots[k], pos, 0.0), axis=1, keepdims=True).astype(I32)
        meta = jnp.where(lane == k, idxs[k], meta)
        meta = jnp.where(lane == TOP_K + k, local, meta)
        gates = jnp.where(lane == k, es[k] / den, gates)
    meta_ref[...] = meta
    gate_ref[...] = gates
    row = lax.broadcasted_iota(I32, (SUBLANES, LANES), 0)
    tile_ref[...] = jnp.where(row == 0, carry_sc[...], jnp.where(row == 1, tile_cnt, run_off)).astype(I32)
    carry_sc[...] = carry_sc[...] + tile_cnt


def _router(x2, mod4, rw, rb, seq):
    n, d = x2.shape
    sub = 1
    tm = sub * TOK_TILE
    per_b = seq // tm
    rw_p = jnp.zeros((d, LANES), F32).at[:, :N_EXPERTS].set(rw)
    rb_p = jnp.full((1, LANES), NEG, F32).at[0, :N_EXPERTS].set(rb)
    return pl.pallas_call(
        _router_kernel,
        grid=(n // tm,),
        in_specs=[pl.BlockSpec((tm, d), lambda i: (i, 0)),
                  pl.BlockSpec((1, 1, 1, d), lambda i: (i // per_b, 4, 0, 0)),
                  pl.BlockSpec((1, 1, 1, d), lambda i: (i // per_b, 3, 0, 0)),
                  pl.BlockSpec((d, LANES), lambda i: (0, 0)),
                  pl.BlockSpec((1, LANES), lambda i: (0, 0))],
        out_specs=[pl.BlockSpec((tm, LANES), lambda i: (i, 0)),
                   pl.BlockSpec((tm, LANES), lambda i: (i, 0)),
                   pl.BlockSpec((sub * SUBLANES, LANES), lambda i: (i, 0)),
                   pl.BlockSpec((SUBLANES, LANES), lambda i: (0, 0))],
        out_shape=[jax.ShapeDtypeStruct((n, LANES), I32),
                   jax.ShapeDtypeStruct((n, LANES), F32),
                   jax.ShapeDtypeStruct((n // TOK_TILE * SUBLANES, LANES), I32),
                   jax.ShapeDtypeStruct((SUBLANES, LANES), I32)],
        scratch_shapes=[pltpu.VMEM((1, LANES), F32)],
        compiler_params=_cparams(("arbitrary",)),
        name="router",
    )(x2, mod4, mod4, rw_p, rb_p)


def _row_slice(ref, r):
    return ref.at[pl.ds(pl.multiple_of(r * ROW_CHUNKS, ROW_CHUNKS), ROW_CHUNKS)]


def _wait_rows(src_like, dst_like, sem, n_rows):
    span = pl.ds(0, n_rows * ROW_CHUNKS)
    pltpu.make_async_copy(src_like.at[span], dst_like.at[span], sem).wait()


def _span_rows(ref, row, n):
    return ref.at[pl.ds(pl.multiple_of(row * ROW_CHUNKS, ROW_CHUNKS), n * ROW_CHUNKS)]


def _for_each_run(tab_ref, fn):
    for e in range(N_EXPERTS):
        n = tab_ref[1, e]

        @pl.when(n > 0)
        def _(e=e, n=n):
            fn(tab_ref[0, e], tab_ref[2, e], n)


def _dispatch_kernel(tab_ref, fill_ref, x_ref, sc_ref, sh_ref, meta_ref, xs_hbm, xb0, xb1, zb, sem, fsem):
    i = pl.program_id(0)
    n_steps = pl.num_programs(0)
    tm = x_ref.shape[0]
    nrow = tm * TOP_K
    bufs = (xb0, xb1)

    @pl.when(i == 0)
    def _():
        zb[...] = jnp.zeros(zb.shape, F32)
        for j in range(N_FILL):
            n = fill_ref[1, j]

            @pl.when(n > 0)
            def _(j=j, n=n):
                fill = pltpu.make_async_copy(_span_rows(zb, 0, n), _span_rows(xs_hbm, fill_ref[0, j], n), fsem)
                fill.start()
                fill.wait()

    h = (_ln(x_ref[...]) * (1.0 + sc_ref[0, 0]) + sh_ref[0, 0]).astype(BF16)
    pos_t = meta_ref[...].astype(F32).T
    r = lax.broadcasted_iota(I32, (nrow, 1), 0)
    sel = jnp.zeros((nrow, tm), F32)
    for k in range(TOP_K):
        sel = sel + jnp.where(r == pos_t[TOP_K + k:TOP_K + k + 1, :].astype(I32), 1.0, 0.0)
    xt = jnp.dot(sel.astype(BF16), h, preferred_element_type=F32)

    for slot in range(2):
        @pl.when(i % 2 == slot)
        def _(slot=slot):
            buf = bufs[slot]

            @pl.when(i >= 2)
            def _():
                _wait_rows(buf, xs_hbm, sem.at[slot], nrow)

            for c in range(ROW_CHUNKS):
                buf[pl.ds(c, nrow, stride=ROW_CHUNKS), :] = xt[:, c * LANES:(c + 1) * LANES]
            _for_each_run(tab_ref, lambda srow, trow, n: pltpu.make_async_copy(
                _span_rows(buf, trow, n), _span_rows(xs_hbm, srow, n), sem.at[slot]).start())

            @pl.when(i == n_steps - 1)
            def _():
                _wait_rows(buf, xs_hbm, sem.at[slot], nrow)

                @pl.when(i >= 1)
                def _():
                    _wait_rows(bufs[1 - slot], xs_hbm, sem.at[1 - slot], nrow)


def _dispatch(x2, mod4, meta, tab, fill_tab, n_rows, seq):
    n, d = x2.shape
    tm = TOK_TILE
    per_b = seq // tm
    return pl.pallas_call(
        _dispatch_kernel,
        grid=(n // tm,),
        in_specs=[pl.BlockSpec((SUBLANES, LANES), lambda i: (i, 0), memory_space=pltpu.SMEM),
                  pl.BlockSpec((SUBLANES, LANES), lambda i: (0, 0), memory_space=pltpu.SMEM),
                  pl.BlockSpec((tm, d), lambda i: (i, 0)),
                  pl.BlockSpec((1, 1, 1, d), lambda i: (i // per_b, 4, 0, 0)),
                  pl.BlockSpec((1, 1, 1, d), lambda i: (i // per_b, 3, 0, 0)),
                  pl.BlockSpec((tm, LANES), lambda i: (i, 0))],
        out_specs=pl.BlockSpec(memory_space=pl.ANY),
        out_shape=jax.ShapeDtypeStruct((n_rows * ROW_CHUNKS, LANES), F32),
        scratch_shapes=[pltpu.VMEM((tm * TOP_K * ROW_CHUNKS, LANES), F32),
                        pltpu.VMEM((tm * TOP_K * ROW_CHUNKS, LANES), F32),
                        pltpu.VMEM((EXPERT_ROWS * ROW_CHUNKS, LANES), F32),
                        pltpu.SemaphoreType.DMA((2,)),
                        pltpu.SemaphoreType.DMA],
        compiler_params=_cparams(("arbitrary",)),
        name="moe_dispatch",
    )(tab, fill_tab, x2, mod4, mod4, meta)


def _expert_kernel(be_ref, first_ref, used_ref, xs_ref, wgu_ref, bgu_ref, wd_ref, bd_ref, ys_ref,
                   x_sc, wgu_sc, wd_sc):
    del be_ref
    g = pl.program_id(0)

    @pl.when(g < used_ref[0])
    def _():
        _expert_block(g, first_ref, xs_ref, wgu_ref, bgu_ref, wd_ref, bd_ref, ys_ref, x_sc, wgu_sc, wd_sc)

    @pl.when(g >= used_ref[0])
    def _():
        ys_ref[...] = jnp.zeros(ys_ref.shape, ys_ref.dtype)


def _expert_block(g, first_ref, xs_ref, wgu_ref, bgu_ref, wd_ref, bd_ref, ys_ref, x_sc, wgu_sc, wd_sc):
    rows = x_sc.shape[0]
    grp = 2 * LANES

    @pl.when(first_ref[g] == 1)
    def _():
        r = lax.broadcasted_iota(I32, (grp, grp), 0)
        c = lax.broadcasted_iota(I32, (grp, grp), 1)
        src = jnp.where(c < LANES, 2 * c, 2 * (c - LANES) + 1)
        sel = jnp.where(r == src, 1.0, 0.0).astype(BF16)
        for j in range(wgu_sc.shape[1] // grp):
            blk = wgu_ref[0, :, j * grp:(j + 1) * grp].astype(BF16)
            wgu_sc[:, j * grp:(j + 1) * grp] = jnp.dot(blk, sel, preferred_element_type=F32).astype(BF16)
        wd_sc[...] = wd_ref[0].astype(BF16)

    for c in range(ROW_CHUNKS):
        x_sc[:, c * LANES:(c + 1) * LANES] = xs_ref[pl.ds(c, rows, stride=ROW_CHUNKS), :].astype(BF16)
    gu = jnp.dot(x_sc[...], wgu_sc[...], preferred_element_type=F32) + bgu_ref[0]
    acts = []
    for j in range(gu.shape[1] // grp):
        glu = jnp.minimum(gu[:, j * grp:j * grp + LANES], SWIGLU_LIMIT)
        lin = jnp.clip(gu[:, j * grp + LANES:(j + 1) * grp], -SWIGLU_LIMIT, SWIGLU_LIMIT)
        acts.append((glu * _sigmoid(SWIGLU_ALPHA * glu) * (lin + 1.0)).astype(BF16))
    act = jnp.concatenate(acts, axis=1)
    y = jnp.dot(act, wd_sc[...], preferred_element_type=F32) + bd_ref[0]
    for c in range(ROW_CHUNKS):
        ys_ref[pl.ds(c, rows, stride=ROW_CHUNKS), :] = y[:, c * LANES:(c + 1) * LANES]


def _experts(xs, block_exp, first, n_used, layer, w_gu, b_gu_grouped, w_dn, b_dn):
    d = D_MODEL
    n_blocks = block_exp.shape[0]
    rb = EXPERT_ROWS
    last = lambda g, nu: jnp.minimum(g, nu[0] - 1)
    wmap = lambda g, be, fi, nu: (layer, be[last(g, nu)], 0, 0)
    bmap = lambda g, be, fi, nu: (be[last(g, nu)], 0, 0)
    return pl.pallas_call(
        _expert_kernel,
        grid_spec=pltpu.PrefetchScalarGridSpec(
            num_scalar_prefetch=3,
            grid=(n_blocks,),
            in_specs=[pl.BlockSpec((rb * ROW_CHUNKS, LANES), lambda g, be, fi, nu: (last(g, nu), 0)),
                      pl.BlockSpec((None, 1, d, 2 * d), wmap),
                      pl.BlockSpec((1, 1, 2 * d), bmap),
                      pl.BlockSpec((None, 1, d, d), wmap),
                      pl.BlockSpec((1, 1, d), bmap)],
            out_specs=pl.BlockSpec((rb * ROW_CHUNKS, LANES), lambda g, be, fi, nu: (g, 0)),
            scratch_shapes=[pltpu.VMEM((rb, d), BF16),
                            pltpu.VMEM((d, 2 * d), BF16),
                            pltpu.VMEM((d, d), BF16)]),
        out_shape=jax.ShapeDtypeStruct(xs.shape, F32),
        compiler_params=_cparams(("arbitrary",)),
        name="moe_experts",
    )(block_exp, first, n_used, xs, w_gu, b_gu_grouped, w_dn, b_dn)


def _combine_kernel(tab_ref, tab_next_ref, meta_ref, gate_ref, x_ref, g2_ref, lng_ref, lnb_ref, ys_hbm, o_ref,
                    yb0, yb1, sem):
    i = pl.program_id(0)
    n_steps = pl.num_programs(0)
    tm = x_ref.shape[0]
    nrow = tm * TOP_K
    bufs = (yb0, yb1)

    def fetch(t_ref, slot):
        _for_each_run(t_ref, lambda srow, trow, n: pltpu.make_async_copy(
            _span_rows(ys_hbm, srow, n), _span_rows(bufs[slot], trow, n), sem.at[slot]).start())

    @pl.when(i == 0)
    def _():
        fetch(tab_ref, 0)

    for slot in range(2):
        @pl.when(i % 2 == slot)
        def _(slot=slot):
            buf = bufs[slot]

            @pl.when(i + 1 < n_steps)
            def _():
                fetch(tab_next_ref, 1 - slot)

            _wait_rows(ys_hbm, buf, sem.at[slot], nrow)
            yt = jnp.concatenate([buf[pl.ds(c, nrow, stride=ROW_CHUNKS), :] for c in range(ROW_CHUNKS)],
                                 axis=1).astype(BF16)
            meta = meta_ref[...]
            gates = gate_ref[...]
            col = lax.broadcasted_iota(I32, (1, nrow), 1)
            w = jnp.zeros((tm, nrow), F32)
            for k in range(TOP_K):
                w = w + jnp.where(col == meta[:, TOP_K + k:TOP_K + k + 1], gates[:, k:k + 1], 0.0)
            w_hi = w.astype(BF16)
            w_lo = (w - w_hi.astype(F32)).astype(BF16)
            y = (jnp.dot(w_hi, yt, preferred_element_type=F32) + jnp.dot(w_lo, yt, preferred_element_type=F32))
            z = DEEPNORM_ALPHA * x_ref[...] + g2_ref[0, 0] * y
            o_ref[...] = _ln(z) * lng_ref[...] + lnb_ref[...]


def _combine(ys, tab, meta, gates, x2, mod4, lng, lnb, seq):
    n, d = x2.shape
    tm = TOK_TILE
    per_b = seq // tm
    last = n // tm - 1
    return pl.pallas_call(
        _combine_kernel,
        grid=(n // tm,),
        in_specs=[pl.BlockSpec((SUBLANES, LANES), lambda i: (i, 0), memory_space=pltpu.SMEM),
                  pl.BlockSpec((SUBLANES, LANES), lambda i: (jnp.minimum(i + 1, last), 0),
                               memory_space=pltpu.SMEM),
                  pl.BlockSpec((tm, LANES), lambda i: (i, 0)),
                  pl.BlockSpec((tm, LANES), lambda i: (i, 0)),
                  pl.BlockSpec((tm, d), lambda i: (i, 0)),
                  pl.BlockSpec((1, 1, 1, d), lambda i: (i // per_b, 5, 0, 0)),
                  pl.BlockSpec((1, d), lambda i: (0, 0)),
                  pl.BlockSpec((1, d), lambda i: (0, 0)),
                  pl.BlockSpec(memory_space=pl.ANY)],
        out_specs=pl.BlockSpec((tm, d), lambda i: (i, 0)),
        out_shape=jax.ShapeDtypeStruct((n, d), F32),
        scratch_shapes=[pltpu.VMEM((tm * TOP_K * ROW_CHUNKS, LANES), F32),
                        pltpu.VMEM((tm * TOP_K * ROW_CHUNKS, LANES), F32),
                        pltpu.SemaphoreType.DMA((2,))],
        compiler_params=_cparams(("arbitrary",)),
        name="moe_combine",
    )(tab, tab, meta, gates, x2, mod4, lng, lnb, ys)


def _moe(x2, mod4, layer, rw, rb, w_gu, b_gu, w_dn, b_dn, lng, lnb, seq):
    n, d = x2.shape
    meta, gates, tile_meta, cnt = _router(x2, mod4, rw, rb, seq)
    counts = cnt[0, :N_EXPERTS]
    padded = -(-counts // EXPERT_ROWS) * EXPERT_ROWS
    pend = jnp.cumsum(padded)
    pstart = pend - padded
    n_blocks = -(-(n * TOP_K + N_EXPERTS * (EXPERT_ROWS - 1)) // EXPERT_ROWS)
    n_rows = n_blocks * EXPERT_ROWS
    block_row0 = jnp.arange(n_blocks, dtype=I32) * EXPERT_ROWS
    block_exp = jnp.minimum(jnp.sum((pend[None, :] <= block_row0[:, None]).astype(I32), axis=1), N_EXPERTS - 1)
    first = jnp.concatenate([jnp.ones((1,), I32), (block_exp[1:] != block_exp[:-1]).astype(I32)])
    n_used = (pend[-1:] // EXPERT_ROWS).astype(I32)
    n_tiles = n // TOK_TILE
    start_row = jnp.zeros((1, 1, LANES), I32).at[0, 0, :N_EXPERTS].set(pstart)
    sel_row0 = (jnp.arange(SUBLANES) == 0).astype(I32)[None, :, None]
    tab = (tile_meta.reshape(n_tiles, SUBLANES, LANES) + sel_row0 * start_row).reshape(n_tiles * SUBLANES, LANES)
    tail0 = pend[-1] + jnp.arange(N_FILL - N_EXPERTS, dtype=I32) * EXPERT_ROWS
    fill_start = jnp.concatenate([pstart + counts, tail0])
    fill_len = jnp.concatenate([padded - counts, jnp.clip(n_rows - tail0, 0, EXPERT_ROWS)])
    fill_tab = jnp.zeros((SUBLANES, LANES), I32).at[0, :N_FILL].set(fill_start).at[1, :N_FILL].set(fill_len)
    xs = _dispatch(x2, mod4, meta, tab, fill_tab, n_rows, seq)
    e = b_gu.shape[0]
    b_grouped = b_gu.reshape(e, d // LANES, LANES, 2).transpose(0, 1, 3, 2).reshape(e, 1, 2 * d)
    ys = _experts(xs, block_exp, first, n_used, layer, w_gu, b_grouped, w_dn, b_dn[:, None, :])
    return _combine(ys, tab, meta, gates, x2, mod4, lng.reshape(1, d), lnb.reshape(1, d), seq)


def _permute_w_in(w):
    d = w.shape[0]
    off = {}
    o = 0
    for name, wd in (("a_q", 256), ("a_k", 256), ("a_v", 256), ("b_q", 384), ("b_kc", 128), ("b_vc", 128),
                     ("b_ks", 128), ("b_vs", 128), ("b_kw", 128), ("b_vw", 128), ("b_gate", 18),
                     ("c_q", 384), ("c_k", 384), ("c_v", 384), ("merge_gate", 3 * D_MODEL)):
        off[name] = (o, o + wd)
        o += wd
    col = lambda name: w[:, off[name][0]:off[name][1]]
    scale = HEAD_DIM ** -0.5 * float(np.log2(np.e))
    b_q = col("b_q").reshape(d, B_KV_GROUPS, B_REP, HEAD_DIM).transpose(0, 2, 1, 3).reshape(d, B_HEADS * HEAD_DIM)
    gate_pad = jnp.zeros((d, LANES - B_HEADS * 3), w.dtype)
    parts = [col("merge_gate"),
             col("a_q") * scale, col("a_k"), b_q * scale, col("b_kc"), col("b_ks"), col("b_kw"),
             col("c_q") * scale, col("c_k"),
             col("a_v"), col("b_vc"), col("b_vs"), col("b_vw"), col("c_v"),
             col("b_gate"), gate_pad]
    return jnp.concatenate(parts, axis=1).astype(BF16)


def _dilate(arr, tile, batch, seq, dil):
    a = arr[:, tile * LANES:(tile + 1) * LANES].reshape(batch, seq // dil, dil, LANES)
    return a.transpose(0, 2, 1, 3).reshape(batch * seq, LANES)


def _undilate(arr, batch, seq, dil):
    a = arr.reshape(batch, dil, seq // dil, LANES)
    return a.transpose(0, 2, 1, 3).reshape(batch * seq, LANES)


def _mixer_layer(x2, mod4, cos_t, sin_t, batch, seq, w_in, cmp_w1_k, cmp_w2_k, cmp_pe_k,
                 cmp_w1_v, cmp_w2_v, cmp_pe_v, w_a, w_b, w_c, w_out, lng, lnb):
    d = D_MODEL
    proj = _inproj(x2, mod4, _permute_w_in(w_in), cos_t, sin_t, seq)

    bias_a = _moba_select(proj, _kmean(proj, batch, seq), batch, seq)
    o_a = _bb_attn(proj, bias_a, batch=batch, seq=seq, n_pairs=A_HEADS // 2,
                   q_tile=lambda p: T_AQ + p, k_tile=lambda p: T_AK + p, v_tile=lambda p: T_AV + p,
                   bias_tile=lambda p: p, blk=MOBA_BLOCK)

    kc = _compress(proj[:, T_BKC * LANES:(T_BKC + 1) * LANES], cmp_w1_k, cmp_w2_k, cmp_pe_k, batch, seq)
    vc = _compress(proj[:, T_BVC * LANES:(T_BVC + 1) * LANES], cmp_w1_v, cmp_w2_v, cmp_pe_v, batch, seq)
    o_cmp, bias_b = _nsa_cmp(proj, kc, vc, batch, seq)
    o_slc = _bb_attn(proj, bias_b, batch=batch, seq=seq, n_pairs=B_REP,
                     q_tile=lambda p: T_BQ + p, k_tile=lambda p: T_BKS, v_tile=lambda p: T_BVS,
                     bias_tile=lambda p: 0, blk=SLC_BLOCK)
    o_win, _ = _band_attn(proj, proj, proj, batch=batch, seq=seq, q_tiles=[T_BQ + r for r in range(B_REP)],
                          k_tile=T_BKW, v_tile=T_BVW, window=NSA_WINDOW, t=256, group=1)

    oc, lse = [], []
    for gi, (window, dil) in enumerate(DILATED_PAIRS):
        wlen = window // dil + 1
        if dil == 1:
            o, l = _band_attn(proj, proj, proj, batch=batch, seq=seq, q_tiles=[T_CQ + gi], k_tile=T_CK + gi,
                              v_tile=T_CV + gi, window=wlen, t=256, group=4)
        else:
            qd = _dilate(proj, T_CQ + gi, batch, seq, dil)
            kd = _dilate(proj, T_CK + gi, batch, seq, dil)
            vd = _dilate(proj, T_CV + gi, batch, seq, dil)
            o, l = _band_attn(qd, kd, vd, batch=batch * dil, seq=seq // dil, q_tiles=[0], k_tile=0, v_tile=0,
                              window=wlen, t=256, group=4)
            o, l = _undilate(o, batch, seq, dil), _undilate(l, batch, seq, dil)
        oc.append(o)
        lse.append(l)

    w_b_perm = w_b.reshape(B_KV_GROUPS, B_REP, HEAD_DIM, d).transpose(1, 0, 2, 3).reshape(B_HEADS * HEAD_DIM, d)
    return _merge(o_a, o_cmp, o_slc, o_win, proj, oc, lse, x2, mod4,
                  w_a.astype(BF16), w_b_perm.astype(BF16), w_c.astype(BF16), w_out.astype(BF16),
                  lng.reshape(1, d), lnb.reshape(1, d), seq)


def _rope_tables(positions):
    inv = ROPE_THETA ** (-jnp.arange(0, HEAD_DIM, 2, dtype=F32) / HEAD_DIM)
    ang = positions.astype(F32).reshape(-1, 1) * inv[None, :]
    cos, sin = jnp.cos(ang), jnp.sin(ang)
    return jnp.tile(cos, (1, 4)), jnp.tile(jnp.concatenate([-sin, sin], axis=1), (1, 2))


def kernel(x, c, positions, w_ada, b_ada, w_in, cmp_w1_k, cmp_w2_k, cmp_pe_k, cmp_w1_v, cmp_w2_v, cmp_pe_v, w_branch_a, w_branch_b, w_branch_c, w_out, ln1_g, ln1_b, router_w, router_b, w_gate_up, b_gate_up, w_down, b_down, ln2_g, ln2_b):
    batch, seq, d = x.shape
    depth = w_in.shape[0]
    cos_t, sin_t = _rope_tables(positions)
    mod = _ada(c, w_ada, b_ada)
    x2 = x.reshape(batch * seq, d)
    for l in range(depth):
        mod4 = mod[l].reshape(batch, N_ADA, 1, d)
        x2 = _mixer_layer(x2, mod4, cos_t, sin_t, batch, seq, w_in[l], cmp_w1_k[l], cmp_w2_k[l], cmp_pe_k[l],
                          cmp_w1_v[l], cmp_w2_v[l], cmp_pe_v[l], w_branch_a[l], w_branch_b[l],
                          w_branch_c[l], w_out[l], ln1_g[l], ln1_b[l])
        x2 = _moe(x2, mod4, l, router_w[l], router_b[l], w_gate_up, b_gate_up[l], w_down, b_down[l],
                  ln2_g[l], ln2_b[l], seq)
    return x2.reshape(batch, seq, d)
```

```python
import functools

import numpy as np
import jax
import jax.numpy as jnp
from jax import lax
from jax.experimental import pallas as pl
from jax.experimental.pallas import tpu as pltpu

F32 = jnp.float32
BF16 = jnp.bfloat16
I32 = jnp.int32
HIGHEST = lax.Precision.HIGHEST

D_MODEL = 1024
DEPTH = 2
HEAD_DIM = 64
ROPE_THETA = 10000.0
LN_EPS = 1e-5
DEEPNORM_ALPHA = (2 * DEPTH) ** 0.25
N_ADA = 6
A_HEADS = 4
MOBA_BLOCK = 256
MOBA_TOPK = 3
B_HEADS = 6
B_KV_GROUPS = 2
B_REP = 3
CMP_BLOCK = 32
CMP_STRIDE = 16
CMP_HIDDEN = 128
SLC_BLOCK = 64
SLC_TOPK = 16
NSA_WINDOW = 512
FORCE_SCORE = 1e6
DILATED_PAIRS = ((128, 1), (512, 4), (2048, 16))
N_EXPERTS = 32
TOP_K = 4
SWIGLU_ALPHA = 1.702
SWIGLU_LIMIT = 7.0

LANES = 128
SUBLANES = 8
HALF = LANES // 2
NEG = -1e30
ROW_CHUNKS = D_MODEL // LANES
VMEM_LIMIT = 56 * 1024 * 1024

T_MG = 0
T_ROPE0, T_ROPE1 = 24, 40
T_AQ, T_AK, T_BQ, T_BKC, T_BKS, T_BKW, T_CQ, T_CK = 24, 26, 28, 31, 32, 33, 34, 37
T_AV, T_BVC, T_BVS, T_BVW, T_CV, T_BG = 40, 42, 43, 44, 45, 48
N_PROJ_TILES = 49
PROJ_W = N_PROJ_TILES * LANES

EXPERT_ROWS = 512
TOK_TILE = 256
N_FILL = 2 * N_EXPERTS + 2


def _cparams(sem, **kw):
    return pltpu.CompilerParams(dimension_semantics=sem, vmem_limit_bytes=VMEM_LIMIT, **kw)


def _lane_iota(shape=(1, LANES)):
    return lax.broadcasted_iota(I32, shape, len(shape) - 1)


def _sigmoid(x):
    return 1.0 / (1.0 + jnp.exp(-x))


def _ln(x):
    mu = jnp.mean(x, axis=-1, keepdims=True)
    xc = x - mu
    var = jnp.mean(xc * xc, axis=-1, keepdims=True)
    return xc * lax.rsqrt(var + LN_EPS)


def _dot_nt(a, b, precision=None):
    return lax.dot_general(a, b, (((1,), (1,)), ((), ())), precision=precision,
                           preferred_element_type=F32)


def _ada_kernel(c_ref, w_ref, b_ref, o_ref):
    c = c_ref[...]
    cond = c * _sigmoid(c)
    o_ref[0] = jnp.dot(cond, w_ref[0], precision=HIGHEST, preferred_element_type=F32) + b_ref[0]


def _ada(c, w_ada, b_ada):
    depth, d, n = w_ada.shape
    b = c.shape[0]
    tn = 1536
    return pl.pallas_call(
        _ada_kernel,
        grid=(depth, n // tn),
        in_specs=[pl.BlockSpec((b, d), lambda l, j: (0, 0)),
                  pl.BlockSpec((1, d, tn), lambda l, j: (l, 0, j)),
                  pl.BlockSpec((1, 1, tn), lambda l, j: (l, 0, j))],
        out_specs=pl.BlockSpec((1, b, tn), lambda l, j: (l, 0, j)),
        out_shape=jax.ShapeDtypeStruct((depth, b, n), F32),
        compiler_params=_cparams(("parallel", "parallel")),
        name="ada",
    )(c, w_ada, b_ada.reshape(depth, 1, n))


def _inproj_kernel(x_ref, sc_ref, sh_ref, w_ref, cos_ref, sin_ref, o_ref, *, chunks):
    h = _ln(x_ref[...]) * (1.0 + sc_ref[0, 0]) + sh_ref[0, 0]
    hb = h.astype(BF16)
    first_half = (_lane_iota() & (HEAD_DIM - 1)) < (HEAD_DIM // 2)
    for c0, cw, rope in chunks:
        acc = jnp.dot(hb, w_ref[:, c0:c0 + cw], preferred_element_type=F32)
        if rope:
            cos = cos_ref[...]
            sin = sin_ref[...]
            for t in range(cw // LANES):
                a = acc[:, t * LANES:(t + 1) * LANES]
                rot = jnp.where(first_half, pltpu.roll(a, LANES - HEAD_DIM // 2, 1),
                                pltpu.roll(a, HEAD_DIM // 2, 1))
                o_ref[:, c0 + t * LANES:c0 + (t + 1) * LANES] = (a * cos + rot * sin).astype(o_ref.dtype)
        else:
            o_ref[:, c0:c0 + cw] = acc.astype(o_ref.dtype)


def _inproj_chunks():
    chunks = []
    for lo, hi, rope in ((0, T_ROPE0, False), (T_ROPE0, T_ROPE1, True), (T_ROPE1, N_PROJ_TILES, False)):
        c = lo * LANES
        while c < hi * LANES:
            cw = min(512, hi * LANES - c)
            chunks.append((c, cw, rope))
            c += cw
    return tuple(chunks)


def _inproj(x2, mod4, w_perm, cos_t, sin_t, seq):
    n, d = x2.shape
    tm = 256
    per_b = seq // tm
    return pl.pallas_call(
        functools.partial(_inproj_kernel, chunks=_inproj_chunks()),
        grid=(n // tm,),
        in_specs=[pl.BlockSpec((tm, d), lambda i: (i, 0)),
                  pl.BlockSpec((1, 1, 1, d), lambda i: (i // per_b, 1, 0, 0)),
                  pl.BlockSpec((1, 1, 1, d), lambda i: (i // per_b, 0, 0, 0)),
                  pl.BlockSpec((d, PROJ_W), lambda i: (0, 0)),
                  pl.BlockSpec((tm, LANES), lambda i: (i, 0)),
                  pl.BlockSpec((tm, LANES), lambda i: (i, 0))],
        out_specs=pl.BlockSpec((tm, PROJ_W), lambda i: (i, 0)),
        out_shape=jax.ShapeDtypeStruct((n, PROJ_W), BF16),
        compiler_params=_cparams(("parallel",)),
        name="inproj",
    )(x2, mod4, mod4, w_perm, cos_t, sin_t)


def _bb_attn_kernel(q_ref, k_ref, v_ref, b_ref, o_ref, qa_sc, s_sc, mp_sc, m_sc, acc_sc,
                    *, blk_shift, tq, tu, big):
    i = pl.program_id(2)
    n_diag = tq // tu
    lane = _lane_iota()
    lo = lane < HALF
    q = q_ref[...]
    b = b_ref[...]
    qa_sc[0] = jnp.where(lo, q, b)
    qa_sc[1] = jnp.where(lo, b, q)
    mp_sc[...] = jnp.full(mp_sc.shape, -jnp.inf, F32)
    acc_sc[...] = jnp.zeros(acc_sc.shape, F32)

    def scores(u0, n_units, diagonal_last):
        tk = n_units * tu
        start = pl.multiple_of(u0 * tu, tu)
        k = k_ref[pl.ds(start, tk), :]
        kblk = (start + lax.broadcasted_iota(I32, (tk, 1), 0)) >> blk_shift
        onehot = jnp.where(kblk == (lane & (HALF - 1)), 1.0, 0.0).astype(BF16)
        ka = (jnp.where(lo, k, onehot), jnp.where(lo, onehot, k))
        row = lax.broadcasted_iota(I32, (tq, 1), 0)
        col = lax.broadcasted_iota(I32, (1, tu), 1)
        for h in range(2):
            s = _dot_nt(qa_sc[h], ka[h])
            mp = mp_sc[h]
            for u in range(n_units):
                su = s[:, u * tu:(u + 1) * tu]
                d = u - (n_units - n_diag)
                if diagonal_last and d >= 0:
                    su = jnp.where(col + d * tu <= row, su, NEG)
                s_sc[h, u0 + u] = su
                for c in range(tu // LANES):
                    mp = jnp.maximum(mp, su[:, c * LANES:(c + 1) * LANES])
            mp_sc[h] = mp

    def weighted(u0, n_units):
        tk = n_units * tu
        start = pl.multiple_of(u0 * tu, tu)
        v = v_ref[pl.ds(start, tk), :]
        one = jnp.ones_like(v)
        vs = (jnp.where(lo, v, one), jnp.where(lo, one, v))
        for h in range(2):
            m_row = m_sc[h]
            ps = []
            for u in range(n_units):
                su = s_sc[h, u0 + u]
                for c in range(tu // LANES):
                    ps.append(jnp.exp2(su[:, c * LANES:(c + 1) * LANES] - m_row).astype(BF16))
            acc_sc[h] = acc_sc[h] + jnp.dot(jnp.concatenate(ps, axis=1), vs[h], preferred_element_type=F32)

    n_full = i * n_diag
    n_big = n_full // big
    n_tail = n_full - n_big * big + n_diag

    def run(fn):
        def big_body(t, c):
            fn(t * big, big, False) if fn is scores else fn(t * big, big)
            return c

        lax.fori_loop(0, n_big, big_body, 0)
        for n in range(n_diag, big + n_diag, n_diag):
            @pl.when(n_tail == n)
            def _(n=n):
                fn(n_big * big, n, True) if fn is scores else fn(n_big * big, n)

    run(scores)
    for h in range(2):
        m_sc[h] = jnp.broadcast_to(jnp.max(mp_sc[h], axis=1, keepdims=True), (tq, LANES))
    run(weighted)
    outs = [acc_sc[h] / pltpu.roll(acc_sc[h], HALF, 1) for h in range(2)]
    o_ref[...] = jnp.where(lo, outs[0], outs[1]).astype(o_ref.dtype)


def _bb_attn(proj, bias, *, batch, seq, n_pairs, q_tile, k_tile, v_tile, bias_tile, blk):
    t, tu = 512, 256
    nq = seq // t
    n = batch * seq
    kern = functools.partial(_bb_attn_kernel, blk_shift=int(np.log2(blk)), tq=t, tu=tu, big=4)
    return pl.pallas_call(
        kern,
        grid=(batch, n_pairs, nq),
        in_specs=[pl.BlockSpec((t, LANES), lambda b, p, i: (b * nq + i, q_tile(p))),
                  pl.BlockSpec((seq, LANES), lambda b, p, i: (b, k_tile(p))),
                  pl.BlockSpec((seq, LANES), lambda b, p, i: (b, v_tile(p))),
                  pl.BlockSpec((t, LANES), lambda b, p, i: (b * nq + i, bias_tile(p)))],
        out_specs=pl.BlockSpec((t, LANES), lambda b, p, i: (b * nq + i, p)),
        out_shape=jax.ShapeDtypeStruct((n, n_pairs * LANES), BF16),
        scratch_shapes=[pltpu.VMEM((2, t, LANES), BF16),
                        pltpu.VMEM((2, seq // tu, t, tu), F32),
                        pltpu.VMEM((2, t, LANES), F32),
                        pltpu.VMEM((2, t, LANES), F32),
                        pltpu.VMEM((2, t, LANES), F32)],
        compiler_params=_cparams(("parallel", "parallel", "parallel")),
        name="bb_attn_%d" % blk,
    )(proj, proj, proj, bias)


def _band_kernel(*refs, tq, kw, window, seq, n_q, group, tiles_per_seq):
    q_refs = refs[:n_q]
    k_ref, v_ref, o_ref, lse_ref = refs[n_q:n_q + 4]
    i = pl.program_id(1)
    lane = _lane_iota()
    lo = lane < HALF
    for g in range(group):
        if tiles_per_seq == 1:
            tile, base = 0, g * seq
        else:
            tile, base = i * group + g, 0
        start = jnp.clip(tile * tq - (kw - tq), 0, seq - kw)
        k = k_ref[pl.ds(pl.multiple_of(base + start, LANES), kw), :]
        v = v_ref[pl.ds(pl.multiple_of(base + start, LANES), kw), :]
        rel = (tile * tq + lax.broadcasted_iota(I32, (tq, 1), 0)) - (start + lax.broadcasted_iota(I32, (1, kw), 1))
        ok = (rel >= 0) & (rel < window)
        for qi in range(n_q):
            q = q_refs[qi][g * tq:(g + 1) * tq, :]
            zero = jnp.zeros_like(q)
            qs = (jnp.where(lo, q, zero), jnp.where(lo, zero, q))
            outs, lses = [], []
            for h in range(2):
                s = jnp.where(ok, _dot_nt(qs[h], k), NEG)
                m = jnp.max(s, axis=1, keepdims=True)
                p = jnp.exp2(s - m)
                l = jnp.sum(p, axis=1, keepdims=True)
                outs.append(jnp.dot(p.astype(BF16), v, preferred_element_type=F32) / l)
                lses.append(m + jnp.log2(l))
            rows, cols = slice(g * tq, (g + 1) * tq), slice(qi * LANES, (qi + 1) * LANES)
            o_ref[rows, cols] = jnp.where(lo, outs[0], outs[1]).astype(o_ref.dtype)
            lse_ref[rows, cols] = jnp.where(lo, lses[0], lses[1])


def _band_attn(q_arr, k_arr, v_arr, *, batch, seq, q_tiles, k_tile, v_tile, window, t, group):
    t = min(t, seq)
    tiles_per_seq = seq // t
    n = batch * seq
    n_q = len(q_tiles)
    if tiles_per_seq == 1:
        group = min(group, batch)
        grid = (batch // group, 1)
        kv_rows = group * seq
        row_blk = lambda b, i: b
    else:
        group = min(group, tiles_per_seq)
        grid = (batch, tiles_per_seq // group)
        kv_rows = seq
        row_blk = lambda b, i: b * (tiles_per_seq // group) + i
    kw = min(seq, t + -(-(window - 1) // LANES) * LANES)
    kern = functools.partial(_band_kernel, tq=t, kw=kw, window=window, seq=seq, n_q=n_q, group=group,
                             tiles_per_seq=tiles_per_seq)
    qspecs = [pl.BlockSpec((group * t, LANES), (lambda b, i, c=c: (row_blk(b, i), c))) for c in q_tiles]
    return pl.pallas_call(
        kern,
        grid=grid,
        in_specs=qspecs + [pl.BlockSpec((kv_rows, LANES), lambda b, i: (b, k_tile)),
                           pl.BlockSpec((kv_rows, LANES), lambda b, i: (b, v_tile))],
        out_specs=[pl.BlockSpec((group * t, n_q * LANES), lambda b, i: (row_blk(b, i), 0)),
                   pl.BlockSpec((group * t, n_q * LANES), lambda b, i: (row_blk(b, i), 0))],
        out_shape=[jax.ShapeDtypeStruct((n, n_q * LANES), BF16),
                   jax.ShapeDtypeStruct((n, n_q * LANES), F32)],
        compiler_params=_cparams(("parallel", "parallel")),
        name="band_attn_%d" % window,
    )(*([q_arr] * n_q), k_arr, v_arr)


def _kmean_kernel(k_ref, o_ref, *, nb):
    s = k_ref.shape[0]
    blk = lax.broadcasted_iota(I32, (nb, s), 1) >> int(np.log2(MOBA_BLOCK))
    avg = jnp.where(blk == lax.broadcasted_iota(I32, (nb, s), 0), 1.0 / MOBA_BLOCK, 0.0).astype(BF16)
    o_ref[0] = jnp.dot(avg, k_ref[...], preferred_element_type=F32)


def _kmean(proj, batch, seq):
    nb = seq // MOBA_BLOCK
    w = A_HEADS * HEAD_DIM
    return pl.pallas_call(
        functools.partial(_kmean_kernel, nb=nb),
        grid=(batch,),
        in_specs=[pl.BlockSpec((seq, w), lambda b: (b, T_AK * LANES // w))],
        out_specs=pl.BlockSpec((1, nb, w), lambda b: (b, 0, 0)),
        out_shape=jax.ShapeDtypeStruct((batch, nb, w), F32),
        compiler_params=_cparams(("parallel",)),
        name="moba_kmean",
    )(proj)


def _rank_desc(g, n_idx, n):
    del n_idx
    gs = min(SUBLANES, n)
    groups = [g[r0:r0 + gs, :] for r0 in range(0, n, gs)]
    ranks = [jnp.zeros(x.shape, I32) for x in groups]
    sub = lax.broadcasted_iota(I32, (gs, 1), 0)
    for m in range(n):
        c = g[m:m + 1, :]
        for gi, x in enumerate(groups):
            if gi * gs > m:
                beats = c >= x
            elif gi * gs + gs - 1 <= m:
                beats = c > x
            else:
                beats = (c > x) | ((c == x) & (sub + gi * gs > m))
            ranks[gi] = ranks[gi] + jnp.where(beats, 1, 0)
    return jnp.concatenate(ranks, axis=0)


def _moba_sel_kernel(*refs, nb, n_sel, n_pairs):
    q_refs = refs[:n_pairs]
    km_ref, b_ref = refs[n_pairs:]
    i = pl.program_id(1)
    t = b_ref.shape[0]
    lo = _lane_iota() < HALF
    pad = jnp.zeros((HALF - nb, LANES), F32)
    n_idx = lax.broadcasted_iota(I32, (nb, 1), 0)
    valid = n_idx < i
    for p in range(n_pairs):
        q = q_refs[p][...].astype(F32)
        km = km_ref[0, :, p * LANES:(p + 1) * LANES]
        zero = jnp.zeros_like(km)
        kmt = jnp.concatenate([jnp.where(lo, zero, km), pad, jnp.where(lo, km, zero), pad], axis=0)
        gt = _dot_nt(kmt, q, precision=HIGHEST)
        rows = []
        for r0 in (0, HALF):
            g = jnp.where(valid, gt[r0:r0 + nb, :], -jnp.inf)
            rank = _rank_desc(g, n_idx, nb)
            allowed = (valid & (rank < n_sel)) | (n_idx == i)
            rows.append(jnp.where(allowed, 0.0, NEG))
            rows.append(jnp.zeros((HALF - nb, t), F32))
        b_ref[:, p * LANES:(p + 1) * LANES] = jnp.concatenate(rows, axis=0).T.astype(b_ref.dtype)


def _moba_select(proj, kmean, batch, seq):
    t = MOBA_BLOCK
    nq = seq // t
    nb = seq // MOBA_BLOCK
    n_sel = min(MOBA_TOPK, nb - 1)
    n_pairs = A_HEADS // 2
    qspecs = [pl.BlockSpec((t, LANES), (lambda b, i, p=p: (b * nq + i, T_AQ + p))) for p in range(n_pairs)]
    return pl.pallas_call(
        functools.partial(_moba_sel_kernel, nb=nb, n_sel=n_sel, n_pairs=n_pairs),
        grid=(batch, nq),
        in_specs=qspecs + [pl.BlockSpec((1, nb, n_pairs * LANES), lambda b, i: (b, 0, 0))],
        out_specs=pl.BlockSpec((t, n_pairs * LANES), lambda b, i: (b * nq + i, 0)),
        out_shape=jax.ShapeDtypeStruct((batch * seq, n_pairs * LANES), BF16),
        compiler_params=_cparams(("parallel", "parallel")),
        name="moba_select",
    )(*([proj] * n_pairs), kmean)


def _compress_kernel(x_ref, w1_ref, pe_ref, w1f_ref, w2_ref, o_ref):
    x = x_ref[0]
    nblk = x.shape[0]
    outs = []
    for g in range(B_KV_GROUPS):
        u = jnp.dot(x, w1_ref[g, 0], preferred_element_type=F32)
        v = jnp.dot(x, w1_ref[g, 1], preferred_element_type=F32)
        pe_h = jnp.dot(pe_ref[...], w1f_ref[...], precision=HIGHEST, preferred_element_type=F32)
        hid = u + pltpu.roll(v, nblk - 1, 0) + pe_h[:1]
        hid = hid * _sigmoid(hid)
        outs.append(jnp.dot(hid, w2_ref[g], precision=HIGHEST, preferred_element_type=F32))
    o = outs[0] + outs[1]
    rows = lax.broadcasted_iota(I32, (nblk, 1), 0)
    o_ref[0] = jnp.where(rows < nblk - 1, o, 0.0).astype(o_ref.dtype)


def _compress(xt, w1, w2, pe, batch, seq):
    nblk = seq // CMP_STRIDE
    xg = xt.reshape(batch, nblk, CMP_STRIDE * LANES)
    w1r = w1.reshape(2, CMP_STRIDE, HEAD_DIM, CMP_HIDDEN)
    w1e = jnp.zeros((B_KV_GROUPS, 2, CMP_STRIDE, B_KV_GROUPS, HEAD_DIM, CMP_HIDDEN), F32)
    for g in range(B_KV_GROUPS):
        w1e = w1e.at[g, :, :, g].set(w1r)
    w1e = w1e.reshape(B_KV_GROUPS, 2, CMP_STRIDE * LANES, CMP_HIDDEN).astype(BF16)
    pe_flat = jnp.broadcast_to(pe.reshape(1, CMP_BLOCK * HEAD_DIM), (SUBLANES, CMP_BLOCK * HEAD_DIM))
    w2e = jnp.zeros((B_KV_GROUPS, CMP_HIDDEN, LANES), F32)
    for g in range(B_KV_GROUPS):
        w2e = w2e.at[g, :, g * HEAD_DIM:(g + 1) * HEAD_DIM].set(w2)
    return pl.pallas_call(
        _compress_kernel,
        grid=(batch,),
        in_specs=[pl.BlockSpec((1, nblk, CMP_STRIDE * LANES), lambda b: (b, 0, 0)),
                  pl.BlockSpec(w1e.shape, lambda b: (0, 0, 0, 0)),
                  pl.BlockSpec(pe_flat.shape, lambda b: (0, 0)),
                  pl.BlockSpec(w1.shape, lambda b: (0, 0)),
                  pl.BlockSpec(w2e.shape, lambda b: (0, 0, 0))],
        out_specs=pl.BlockSpec((1, nblk, LANES), lambda b: (b, 0, 0)),
        out_shape=jax.ShapeDtypeStruct((batch, nblk, LANES), BF16),
        compiler_params=_cparams(("parallel",)),
        name="nsa_compress",
    )(xg, w1e, pe_flat, w1, w2e)


def _nsa_cmp_kernel(q0_ref, q1_ref, q2_ref, kc_ref, vc_ref, ov_ref, o_ref, b_ref, *, n_sel, ns):
    i = pl.program_id(1)
    t = q0_ref.shape[0]
    ncp = kc_ref.shape[1]
    lane = _lane_iota()
    lo = lane < HALF
    kc = kc_ref[0]
    vc = vc_ref[0]
    tq = i * t + lax.broadcasted_iota(I32, (t, 1), 0)
    cmp_end = lax.broadcasted_iota(I32, (1, ncp), 1) * CMP_STRIDE + (CMP_BLOCK - 1)
    vis = cmp_end <= tq
    psum = [jnp.zeros((t, ncp), F32) for _ in range(B_KV_GROUPS)]
    for r, q_ref in enumerate((q0_ref, q1_ref, q2_ref)):
        q = q_ref[...]
        zero = jnp.zeros_like(q)
        outs = []
        for g in range(B_KV_GROUPS):
            qg = jnp.where(lo, q, zero) if g == 0 else jnp.where(lo, zero, q)
            s = jnp.where(vis, _dot_nt(qg, kc), NEG)
            m = jnp.max(s, axis=1, keepdims=True)
            m = jnp.where(m > 0.5 * NEG, m, 0.0)
            e = jnp.exp2(s - m)
            p = e / jnp.maximum(jnp.sum(e, axis=1, keepdims=True), 1e-30)
            psum[g] = psum[g] + p
            outs.append(jnp.dot(p.astype(BF16), vc, preferred_element_type=F32))
        o_ref[:, r * LANES:(r + 1) * LANES] = jnp.where(lo, outs[0], outs[1]).astype(o_ref.dtype)
    n_idx = lax.broadcasted_iota(I32, (ns, 1), 0)
    tcol = i * t + lax.broadcasted_iota(I32, (1, t), 1)
    qblk = tcol >> int(np.log2(SLC_BLOCK))
    forced = (n_idx == 0) | (n_idx == qblk) | (n_idx == qblk - 1)
    valid = n_idx <= qblk
    rows = []
    for g in (1, 0):
        imp = _dot_nt(ov_ref[...], psum[g], precision=HIGHEST)
        imp = jnp.where(forced, FORCE_SCORE, imp)
        imp = jnp.where(valid, imp, -jnp.inf)
        rank = _rank_desc(imp, n_idx, ns)
        allowed = valid & (rank < n_sel)
        rows.append(jnp.where(allowed, 0.0, NEG))
        if ns < HALF:
            rows.append(jnp.zeros((HALF - ns, t), F32))
    b_ref[...] = jnp.concatenate(rows, axis=0).T.astype(b_ref.dtype)


def _nsa_cmp(proj, kc, vc, batch, seq):
    t = 256
    nq = seq // t
    ns = seq // SLC_BLOCK
    n_sel = min(SLC_TOPK, ns)
    ncp = seq // CMP_STRIDE
    cs = np.arange(ncp)[None, :] * CMP_STRIDE
    ss = np.arange(ns)[:, None] * SLC_BLOCK
    ov = ((cs < ss + SLC_BLOCK) & (cs + CMP_BLOCK > ss)).astype(np.float32)
    ov[:, ncp - 1] = 0.0
    n = batch * seq
    qspec = [pl.BlockSpec((t, LANES), (lambda b, i, r=r: (b * nq + i, T_BQ + r))) for r in range(B_REP)]
    return pl.pallas_call(
        functools.partial(_nsa_cmp_kernel, n_sel=n_sel, ns=ns),
        grid=(batch, nq),
        in_specs=qspec + [pl.BlockSpec((1, ncp, LANES), lambda b, i: (b, 0, 0)),
                          pl.BlockSpec((1, ncp, LANES), lambda b, i: (b, 0, 0)),
                          pl.BlockSpec((ns, ncp), lambda b, i: (0, 0))],
        out_specs=[pl.BlockSpec((t, B_REP * LANES), lambda b, i: (b * nq + i, 0)),
                   pl.BlockSpec((t, LANES), lambda b, i: (b * nq + i, 0))],
        out_shape=[jax.ShapeDtypeStruct((n, B_REP * LANES), BF16),
                   jax.ShapeDtypeStruct((n, LANES), BF16)],
        compiler_params=_cparams(("parallel", "parallel")),
        name="nsa_cmp_select",
    )(proj, proj, proj, kc, vc, jnp.asarray(ov))


def _merge_kernel(oa_ref, ocmp_ref, oslc_ref, owin_ref, bg_ref, oc0_ref, oc1_ref, oc2_ref,
                  l0_ref, l1_ref, l2_ref, mg_ref, x_ref, g1_ref, wa_ref, wb_ref, wc_ref, wo_ref,
                  eg_ref, lng_ref, lnb_ref, o_ref):
    d = D_MODEL
    sg = _sigmoid(bg_ref[...].astype(F32))
    sg_hi = sg.astype(BF16)
    sg_lo = (sg - sg_hi.astype(F32)).astype(BF16)
    ob = jnp.zeros(ocmp_ref.shape, F32)
    for br, ref in enumerate((ocmp_ref, oslc_ref, owin_ref)):
        gexp = (jnp.dot(sg_hi, eg_ref[br], preferred_element_type=F32)
                + jnp.dot(sg_lo, eg_ref[br], preferred_element_type=F32))
        ob = ob + gexp * ref[...].astype(F32)
    l0, l1, l2 = l0_ref[...], l1_ref[...], l2_ref[...]
    mx = jnp.maximum(jnp.maximum(l0, l1), l2)
    e0, e1, e2 = jnp.exp2(l0 - mx), jnp.exp2(l1 - mx), jnp.exp2(l2 - mx)
    den = e0 + e1 + e2
    oc = ((e0 / den) * oc0_ref[...].astype(F32) + (e1 / den) * oc1_ref[...].astype(F32)
          + (e2 / den) * oc2_ref[...].astype(F32))
    pa = jnp.dot(oa_ref[...], wa_ref[...], preferred_element_type=F32)
    pb = jnp.dot(ob.astype(BF16), wb_ref[...], preferred_element_type=F32)
    pc = jnp.dot(oc.astype(BF16), wc_ref[...], preferred_element_type=F32)
    merged = (_sigmoid(mg_ref[:, 0:d].astype(F32)) * pa
              + _sigmoid(mg_ref[:, d:2 * d].astype(F32)) * pb
              + _sigmoid(mg_ref[:, 2 * d:3 * d].astype(F32)) * pc)
    y = jnp.dot(merged.astype(BF16), wo_ref[...], preferred_element_type=F32)
    z = DEEPNORM_ALPHA * x_ref[...] + g1_ref[0, 0] * y
    o_ref[...] = _ln(z) * lng_ref[...] + lnb_ref[...]


def _gate_expand():
    eg = np.zeros((3, LANES, B_HEADS * HEAD_DIM), np.float32)
    for g in range(B_KV_GROUPS):
        for r in range(B_REP):
            for br in range(3):
                c0 = (r * B_KV_GROUPS + g) * HEAD_DIM
                eg[br, (g * B_REP + r) * 3 + br, c0:c0 + HEAD_DIM] = 1.0
    return jnp.asarray(eg)


def _merge(o_a, o_cmp, o_slc, o_win, proj, oc, lse, x2, mod4, wa, wb, wc, wo, lng, lnb, seq):
    n, d = x2.shape
    tm = 256
    per_b = seq // tm
    row = lambda w: pl.BlockSpec((tm, w), lambda i: (i, 0))
    full = lambda a: pl.BlockSpec(a.shape, lambda i: (0,) * a.ndim)
    eg = _gate_expand().astype(BF16)
    return pl.pallas_call(
        _merge_kernel,
        grid=(n // tm,),
        in_specs=[row(o_a.shape[1]), row(o_cmp.shape[1]), row(o_slc.shape[1]), row(o_win.shape[1]),
                  pl.BlockSpec((tm, LANES), lambda i: (i, T_BG)),
                  row(LANES), row(LANES), row(LANES), row(LANES), row(LANES), row(LANES),
                  pl.BlockSpec((tm, 3 * d), lambda i: (i, T_MG)),
                  row(d),
                  pl.BlockSpec((1, 1, 1, d), lambda i: (i // per_b, 2, 0, 0)),
                  full(wa), full(wb), full(wc), full(wo), full(eg), full(lng), full(lnb)],
        out_specs=row(d),
        out_shape=jax.ShapeDtypeStruct((n, d), F32),
        compiler_params=_cparams(("parallel",)),
        name="merge_out",
    )(o_a, o_cmp, o_slc, o_win, proj, oc[0], oc[1], oc[2], lse[0], lse[1], lse[2],
      proj, x2, mod4, wa, wb, wc, wo, eg, lng, lnb)


def _router_kernel(x_ref, sc_ref, sh_ref, rw_ref, rb_ref, meta_ref, gate_ref, tile_ref, cnt_ref, carry_sc):
    @pl.when(pl.program_id(0) == 0)
    def _():
        carry_sc[...] = jnp.zeros(carry_sc.shape, F32)

    _router_tile(x_ref[...], sc_ref, sh_ref, rw_ref, rb_ref, meta_ref, gate_ref, tile_ref, carry_sc)
    cnt_ref[...] = jnp.broadcast_to(carry_sc[...], cnt_ref.shape).astype(I32)


def _router_tile(x, sc_ref, sh_ref, rw_ref, rb_ref, meta_ref, gate_ref, tile_ref, carry_sc):
    tm = x.shape[0]
    h = _ln(x) * (1.0 + sc_ref[0, 0]) + sh_ref[0, 0]
    lg = jnp.dot(h, rw_ref[...], precision=HIGHEST, preferred_element_type=F32) + rb_ref[...]
    lane = _lane_iota()
    lane_f = lane.astype(F32)
    onehots, vals, idxs = [], [], []
    for _ in range(TOP_K):
        m = jnp.max(lg, axis=1, keepdims=True)
        idx = jnp.min(jnp.where(lg == m, lane_f, float(LANES)), axis=1, keepdims=True).astype(I32)
        oh = lane == idx
        onehots.append(oh)
        vals.append(m)
        idxs.append(idx)
        lg = jnp.where(oh, -jnp.inf, lg)
    es = [jnp.exp(v - vals[0]) for v in vals]
    den = es[0] + es[1] + es[2] + es[3]
    cnt = jnp.zeros((tm, LANES), F32)
    for oh in onehots:
        cnt = cnt + jnp.where(oh, 1.0, 0.0)
    tri = jnp.where(lax.broadcasted_iota(I32, (tm, tm), 0) > lax.broadcasted_iota(I32, (tm, tm), 1), 1.0, 0.0)
    earlier = jnp.dot(tri.astype(BF16), cnt.astype(BF16), preferred_element_type=F32)
    tile_cnt = jnp.sum(cnt, axis=0, keepdims=True)
    upper = jnp.where(lax.broadcasted_iota(I32, (LANES, LANES), 0) < lax.broadcasted_iota(I32, (LANES, LANES), 1),
                      1.0, 0.0)
    run_off = jnp.dot(jnp.broadcast_to(tile_cnt, (SUBLANES, LANES)).astype(BF16), upper.astype(BF16),
                      preferred_element_type=F32)[:1]
    pos = earlier + run_off
    meta = jnp.zeros((tm, LANES), I32)
    gates = jnp.zeros((tm, LANES), F32)
    for k in range(TOP_K):
        local = jnp.sum(jnp.where(onehots[k], pos, 0.0), axis=1, keepdims=True).astype(I32)
        meta = jnp.where(lane == k, idxs[k], meta)
        meta = jnp.where(lane == TOP_K + k, local, meta)
        gates = jnp.where(lane == k, es[k] / den, gates)
    meta_ref[...] = meta
    gate_ref[...] = gates
    row = lax.broadcasted_iota(I32, (SUBLANES, LANES), 0)
    tile_ref[...] = jnp.where(row == 0, carry_sc[...], jnp.where(row == 1, tile_cnt, run_off)).astype(I32)
    carry_sc[...] = carry_sc[...] + tile_cnt


def _router(x2, mod4, rw, rb, seq):
    n, d = x2.shape
    tm = TOK_TILE
    per_b = seq // tm
    rw_p = jnp.zeros((d, LANES), F32).at[:, :N_EXPERTS].set(rw)
    rb_p = jnp.full((1, LANES), NEG, F32).at[0, :N_EXPERTS].set(rb)
    return pl.pallas_call(
        _router_kernel,
        grid=(n // tm,),
        in_specs=[pl.BlockSpec((tm, d), lambda i: (i, 0)),
                  pl.BlockSpec((1, 1, 1, d), lambda i: (i // per_b, 4, 0, 0)),
                  pl.BlockSpec((1, 1, 1, d), lambda i: (i // per_b, 3, 0, 0)),
                  pl.BlockSpec((d, LANES), lambda i: (0, 0)),
                  pl.BlockSpec((1, LANES), lambda i: (0, 0))],
        out_specs=[pl.BlockSpec((tm, LANES), lambda i: (i, 0)),
                   pl.BlockSpec((tm, LANES), lambda i: (i, 0)),
                   pl.BlockSpec((SUBLANES, LANES), lambda i: (i, 0)),
                   pl.BlockSpec((SUBLANES, LANES), lambda i: (0, 0))],
        out_shape=[jax.ShapeDtypeStruct((n, LANES), I32),
                   jax.ShapeDtypeStruct((n, LANES), F32),
                   jax.ShapeDtypeStruct((n // tm * SUBLANES, LANES), I32),
                   jax.ShapeDtypeStruct((SUBLANES, LANES), I32)],
        scratch_shapes=[pltpu.VMEM((1, LANES), F32)],
        compiler_params=_cparams(("arbitrary",)),
        name="router",
    )(x2, mod4, mod4, rw_p, rb_p)


def _wait_rows(src_like, dst_like, sem, n_rows):
    span = pl.ds(0, n_rows * ROW_CHUNKS)
    pltpu.make_async_copy(src_like.at[span], dst_like.at[span], sem).wait()


def _span_rows(ref, row, n):
    return ref.at[pl.ds(pl.multiple_of(row * ROW_CHUNKS, ROW_CHUNKS), n * ROW_CHUNKS)]


def _for_each_run(tab_ref, fn):
    for e in range(N_EXPERTS):
        n = tab_ref[1, e]

        @pl.when(n > 0)
        def _(e=e, n=n):
            fn(tab_ref[0, e], tab_ref[2, e], n, e % 2)


def _dispatch_kernel(tab_ref, fill_ref, x_ref, sc_ref, sh_ref, meta_ref, xs_hbm, xb0, xb1, zb, sem, fsem):
    i = pl.program_id(0)
    n_steps = pl.num_programs(0)
    tm = x_ref.shape[0]
    nrow = tm * TOP_K
    bufs = (xb0, xb1)

    @pl.when(i == 0)
    def _():
        zb[...] = jnp.zeros(zb.shape, F32)
        for j in range(N_FILL):
            n = fill_ref[1, j]

            @pl.when(n > 0)
            def _(j=j, n=n):
                fill = pltpu.make_async_copy(_span_rows(zb, 0, n), _span_rows(xs_hbm, fill_ref[0, j], n), fsem)
                fill.start()
                fill.wait()

    h = (_ln(x_ref[...]) * (1.0 + sc_ref[0, 0]) + sh_ref[0, 0]).astype(BF16)
    pos_t = meta_ref[...].astype(F32).T
    r = lax.broadcasted_iota(I32, (nrow, 1), 0)
    sel = jnp.zeros((nrow, tm), F32)
    for k in range(TOP_K):
        sel = sel + jnp.where(r == pos_t[TOP_K + k:TOP_K + k + 1, :].astype(I32), 1.0, 0.0)
    xt = jnp.dot(sel.astype(BF16), h, preferred_element_type=F32)

    for slot in range(2):
        @pl.when(i % 2 == slot)
        def _(slot=slot):
            buf = bufs[slot]

            @pl.when(i >= 2)
            def _():
                _wait_rows(buf, xs_hbm, sem.at[slot], nrow)

            for c in range(ROW_CHUNKS):
                buf[pl.ds(c, nrow, stride=ROW_CHUNKS), :] = xt[:, c * LANES:(c + 1) * LANES]
            _for_each_run(tab_ref, lambda srow, trow, n, prio: pltpu.make_async_copy(
                _span_rows(buf, trow, n), _span_rows(xs_hbm, srow, n), sem.at[slot]).start(priority=prio))

            @pl.when(i == n_steps - 1)
            def _():
                _wait_rows(buf, xs_hbm, sem.at[slot], nrow)

                @pl.when(i >= 1)
                def _():
                    _wait_rows(bufs[1 - slot], xs_hbm, sem.at[1 - slot], nrow)


def _dispatch(x2, mod4, meta, tab, fill_tab, n_rows, seq):
    n, d = x2.shape
    tm = TOK_TILE
    per_b = seq // tm
    return pl.pallas_call(
        _dispatch_kernel,
        grid=(n // tm,),
        in_specs=[pl.BlockSpec((SUBLANES, LANES), lambda i: (i, 0), memory_space=pltpu.SMEM),
                  pl.BlockSpec((SUBLANES, LANES), lambda i: (0, 0), memory_space=pltpu.SMEM),
                  pl.BlockSpec((tm, d), lambda i: (i, 0)),
                  pl.BlockSpec((1, 1, 1, d), lambda i: (i // per_b, 4, 0, 0)),
                  pl.BlockSpec((1, 1, 1, d), lambda i: (i // per_b, 3, 0, 0)),
                  pl.BlockSpec((tm, LANES), lambda i: (i, 0))],
        out_specs=pl.BlockSpec(memory_space=pl.ANY),
        out_shape=jax.ShapeDtypeStruct((n_rows * ROW_CHUNKS, LANES), F32),
        scratch_shapes=[pltpu.VMEM((tm * TOP_K * ROW_CHUNKS, LANES), F32),
                        pltpu.VMEM((tm * TOP_K * ROW_CHUNKS, LANES), F32),
                        pltpu.VMEM((EXPERT_ROWS * ROW_CHUNKS, LANES), F32),
                        pltpu.SemaphoreType.DMA((2,)),
                        pltpu.SemaphoreType.DMA],
        compiler_params=_cparams(("arbitrary",)),
        name="moe_dispatch",
    )(tab, fill_tab, x2, mod4, mod4, meta)


def _expert_kernel(be_ref, first_ref, used_ref, xs_ref, wgu_ref, bgu_ref, wd_ref, bd_ref, ys_ref,
                   x_sc, wgu_sc, wd_sc):
    del be_ref
    g = pl.program_id(0)

    @pl.when(g < used_ref[0])
    def _():
        _expert_block(g, first_ref, xs_ref, wgu_ref, bgu_ref, wd_ref, bd_ref, ys_ref, x_sc, wgu_sc, wd_sc)

    @pl.when(g >= used_ref[0])
    def _():
        ys_ref[...] = jnp.zeros(ys_ref.shape, ys_ref.dtype)


def _expert_block(g, first_ref, xs_ref, wgu_ref, bgu_ref, wd_ref, bd_ref, ys_ref, x_sc, wgu_sc, wd_sc):
    rows = x_sc.shape[0]
    grp = 2 * LANES

    @pl.when(first_ref[g] == 1)
    def _():
        r = lax.broadcasted_iota(I32, (grp, grp), 0)
        c = lax.broadcasted_iota(I32, (grp, grp), 1)
        src = jnp.where(c < LANES, 2 * c, 2 * (c - LANES) + 1)
        sel = jnp.where(r == src, 1.0, 0.0).astype(BF16)
        for j in range(wgu_sc.shape[1] // grp):
            blk = wgu_ref[0, :, j * grp:(j + 1) * grp].astype(BF16)
            wgu_sc[:, j * grp:(j + 1) * grp] = jnp.dot(blk, sel, preferred_element_type=F32).astype(BF16)
        wd_sc[...] = wd_ref[0].astype(BF16)

    for c in range(ROW_CHUNKS):
        x_sc[:, c * LANES:(c + 1) * LANES] = xs_ref[pl.ds(c, rows, stride=ROW_CHUNKS), :].astype(BF16)
    gu = jnp.dot(x_sc[...], wgu_sc[...], preferred_element_type=F32) + bgu_ref[0]
    acts = []
    for j in range(gu.shape[1] // grp):
        glu = jnp.minimum(gu[:, j * grp:j * grp + LANES], SWIGLU_LIMIT)
        lin = jnp.clip(gu[:, j * grp + LANES:(j + 1) * grp], -SWIGLU_LIMIT, SWIGLU_LIMIT)
        acts.append((glu * _sigmoid(SWIGLU_ALPHA * glu) * (lin + 1.0)).astype(BF16))
    act = jnp.concatenate(acts, axis=1)
    y = jnp.dot(act, wd_sc[...], preferred_element_type=F32) + bd_ref[0]
    for c in range(ROW_CHUNKS):
        ys_ref[pl.ds(c, rows, stride=ROW_CHUNKS), :] = y[:, c * LANES:(c + 1) * LANES]


def _experts(xs, block_exp, first, n_used, layer, w_gu, b_gu_grouped, w_dn, b_dn):
    d = D_MODEL
    n_blocks = block_exp.shape[0]
    rb = EXPERT_ROWS
    last = lambda g, nu: jnp.minimum(g, nu[0] - 1)
    wmap = lambda g, be, fi, nu: (layer, be[last(g, nu)], 0, 0)
    bmap = lambda g, be, fi, nu: (be[last(g, nu)], 0, 0)
    return pl.pallas_call(
        _expert_kernel,
        grid_spec=pltpu.PrefetchScalarGridSpec(
            num_scalar_prefetch=3,
            grid=(n_blocks,),
            in_specs=[pl.BlockSpec((rb * ROW_CHUNKS, LANES), lambda g, be, fi, nu: (last(g, nu), 0)),
                      pl.BlockSpec((None, 1, d, 2 * d), wmap),
                      pl.BlockSpec((1, 1, 2 * d), bmap),
                      pl.BlockSpec((None, 1, d, d), wmap),
                      pl.BlockSpec((1, 1, d), bmap)],
            out_specs=pl.BlockSpec((rb * ROW_CHUNKS, LANES), lambda g, be, fi, nu: (g, 0)),
            scratch_shapes=[pltpu.VMEM((rb, d), BF16),
                            pltpu.VMEM((d, 2 * d), BF16),
                            pltpu.VMEM((d, d), BF16)]),
        out_shape=jax.ShapeDtypeStruct(xs.shape, F32),
        compiler_params=_cparams(("arbitrary",)),
        name="moe_experts",
    )(block_exp, first, n_used, xs, w_gu, b_gu_grouped, w_dn, b_dn)


def _combine_kernel(tab_ref, tab_next_ref, meta_ref, gate_ref, x_ref, g2_ref, lng_ref, lnb_ref, ys_hbm, o_ref,
                    yb0, yb1, sem):
    i = pl.program_id(0)
    n_steps = pl.num_programs(0)
    tm = x_ref.shape[0]
    nrow = tm * TOP_K
    bufs = (yb0, yb1)

    def fetch(t_ref, slot):
        _for_each_run(t_ref, lambda srow, trow, n, prio: pltpu.make_async_copy(
            _span_rows(ys_hbm, srow, n), _span_rows(bufs[slot], trow, n), sem.at[slot]).start(priority=prio))

    @pl.when(i == 0)
    def _():
        fetch(tab_ref, 0)

    for slot in range(2):
        @pl.when(i % 2 == slot)
        def _(slot=slot):
            buf = bufs[slot]

            @pl.when(i + 1 < n_steps)
            def _():
                fetch(tab_next_ref, 1 - slot)

            _wait_rows(ys_hbm, buf, sem.at[slot], nrow)
            yt = jnp.concatenate([buf[pl.ds(c, nrow, stride=ROW_CHUNKS), :] for c in range(ROW_CHUNKS)],
                                 axis=1).astype(BF16)
            meta = meta_ref[...]
            gates = gate_ref[...]
            col = lax.broadcasted_iota(I32, (1, nrow), 1)
            w = jnp.zeros((tm, nrow), F32)
            for k in range(TOP_K):
                w = w + jnp.where(col == meta[:, TOP_K + k:TOP_K + k + 1], gates[:, k:k + 1], 0.0)
            w_hi = w.astype(BF16)
            w_lo = (w - w_hi.astype(F32)).astype(BF16)
            y = (jnp.dot(w_hi, yt, preferred_element_type=F32) + jnp.dot(w_lo, yt, preferred_element_type=F32))
            z = DEEPNORM_ALPHA * x_ref[...] + g2_ref[0, 0] * y
            o_ref[...] = _ln(z) * lng_ref[...] + lnb_ref[...]


def _combine(ys, tab, meta, gates, x2, mod4, lng, lnb, seq):
    n, d = x2.shape
    tm = TOK_TILE
    per_b = seq // tm
    last = n // tm - 1
    return pl.pallas_call(
        _combine_kernel,
        grid=(n // tm,),
        in_specs=[pl.BlockSpec((SUBLANES, LANES), lambda i: (i, 0), memory_space=pltpu.SMEM),
                  pl.BlockSpec((SUBLANES, LANES), lambda i: (jnp.minimum(i + 1, last), 0),
                               memory_space=pltpu.SMEM),
                  pl.BlockSpec((tm, LANES), lambda i: (i, 0)),
                  pl.BlockSpec((tm, LANES), lambda i: (i, 0)),
                  pl.BlockSpec((tm, d), lambda i: (i, 0)),
                  pl.BlockSpec((1, 1, 1, d), lambda i: (i // per_b, 5, 0, 0)),
                  pl.BlockSpec((1, d), lambda i: (0, 0)),
                  pl.BlockSpec((1, d), lambda i: (0, 0)),
                  pl.BlockSpec(memory_space=pl.ANY)],
        out_specs=pl.BlockSpec((tm, d), lambda i: (i, 0)),
        out_shape=jax.ShapeDtypeStruct((n, d), F32),
        scratch_shapes=[pltpu.VMEM((tm * TOP_K * ROW_CHUNKS, LANES), F32),
                        pltpu.VMEM((tm * TOP_K * ROW_CHUNKS, LANES), F32),
                        pltpu.SemaphoreType.DMA((2,))],
        compiler_params=_cparams(("arbitrary",)),
        name="moe_combine",
    )(tab, tab, meta, gates, x2, mod4, lng, lnb, ys)


def _moe(x2, mod4, layer, rw, rb, w_gu, b_gu, w_dn, b_dn, lng, lnb, seq):
    n, d = x2.shape
    meta, gates, tile_meta, cnt = _router(x2, mod4, rw, rb, seq)
    counts = cnt[0, :N_EXPERTS]
    padded = -(-counts // EXPERT_ROWS) * EXPERT_ROWS
    pend = jnp.cumsum(padded)
    pstart = pend - padded
    n_blocks = -(-(n * TOP_K + N_EXPERTS * (EXPERT_ROWS - 1)) // EXPERT_ROWS)
    n_rows = n_blocks * EXPERT_ROWS
    block_row0 = jnp.arange(n_blocks, dtype=I32) * EXPERT_ROWS
    block_exp = jnp.minimum(jnp.sum((pend[None, :] <= block_row0[:, None]).astype(I32), axis=1), N_EXPERTS - 1)
    first = jnp.concatenate([jnp.ones((1,), I32), (block_exp[1:] != block_exp[:-1]).astype(I32)])
    n_used = (pend[-1:] // EXPERT_ROWS).astype(I32)
    n_tiles = n // TOK_TILE
    start_row = jnp.zeros((1, 1, LANES), I32).at[0, 0, :N_EXPERTS].set(pstart)
    sel_row0 = (jnp.arange(SUBLANES) == 0).astype(I32)[None, :, None]
    tab = (tile_meta.reshape(n_tiles, SUBLANES, LANES) + sel_row0 * start_row).reshape(n_tiles * SUBLANES, LANES)
    tail0 = pend[-1] + jnp.arange(N_FILL - N_EXPERTS, dtype=I32) * EXPERT_ROWS
    fill_start = jnp.concatenate([pstart + counts, tail0])
    fill_len = jnp.concatenate([padded - counts, jnp.clip(n_rows - tail0, 0, EXPERT_ROWS)])
    fill_tab = jnp.zeros((SUBLANES, LANES), I32).at[0, :N_FILL].set(fill_start).at[1, :N_FILL].set(fill_len)
    xs = _dispatch(x2, mod4, meta, tab, fill_tab, n_rows, seq)
    e = b_gu.shape[0]
    b_grouped = b_gu.reshape(e, d // LANES, LANES, 2).transpose(0, 1, 3, 2).reshape(e, 1, 2 * d)
    ys = _experts(xs, block_exp, first, n_used, layer, w_gu, b_grouped, w_dn, b_dn[:, None, :])
    return _combine(ys, tab, meta, gates, x2, mod4, lng.reshape(1, d), lnb.reshape(1, d), seq)


def _permute_w_in(w):
    d = w.shape[0]
    off = {}
    o = 0
    for name, wd in (("a_q", 256), ("a_k", 256), ("a_v", 256), ("b_q", 384), ("b_kc", 128), ("b_vc", 128),
                     ("b_ks", 128), ("b_vs", 128), ("b_kw", 128), ("b_vw", 128), ("b_gate", 18),
                     ("c_q", 384), ("c_k", 384), ("c_v", 384), ("merge_gate", 3 * D_MODEL)):
        off[name] = (o, o + wd)
        o += wd
    col = lambda name: w[:, off[name][0]:off[name][1]]
    scale = HEAD_DIM ** -0.5 * float(np.log2(np.e))
    b_q = col("b_q").reshape(d, B_KV_GROUPS, B_REP, HEAD_DIM).transpose(0, 2, 1, 3).reshape(d, B_HEADS * HEAD_DIM)
    gate_pad = jnp.zeros((d, LANES - B_HEADS * 3), w.dtype)
    parts = [col("merge_gate"),
             col("a_q") * scale, col("a_k"), b_q * scale, col("b_kc"), col("b_ks"), col("b_kw"),
             col("c_q") * scale, col("c_k"),
             col("a_v"), col("b_vc"), col("b_vs"), col("b_vw"), col("c_v"),
             col("b_gate"), gate_pad]
    return jnp.concatenate(parts, axis=1).astype(BF16)


def _dilate(arr, tile, batch, seq, dil):
    a = arr[:, tile * LANES:(tile + 1) * LANES].reshape(batch, seq // dil, dil, LANES)
    return a.transpose(0, 2, 1, 3).reshape(batch * seq, LANES)


def _undilate(arr, batch, seq, dil):
    a = arr.reshape(batch, dil, seq // dil, LANES)
    return a.transpose(0, 2, 1, 3).reshape(batch * seq, LANES)


def _mixer_layer(x2, mod4, cos_t, sin_t, batch, seq, w_in, cmp_w1_k, cmp_w2_k, cmp_pe_k,
                 cmp_w1_v, cmp_w2_v, cmp_pe_v, w_a, w_b, w_c, w_out, lng, lnb):
    d = D_MODEL
    proj = _inproj(x2, mod4, _permute_w_in(w_in), cos_t, sin_t, seq)

    bias_a = _moba_select(proj, _kmean(proj, batch, seq), batch, seq)
    o_a = _bb_attn(proj, bias_a, batch=batch, seq=seq, n_pairs=A_HEADS // 2,
                   q_tile=lambda p: T_AQ + p, k_tile=lambda p: T_AK + p, v_tile=lambda p: T_AV + p,
                   bias_tile=lambda p: p, blk=MOBA_BLOCK)

    kc = _compress(proj[:, T_BKC * LANES:(T_BKC + 1) * LANES], cmp_w1_k, cmp_w2_k, cmp_pe_k, batch, seq)
    vc = _compress(proj[:, T_BVC * LANES:(T_BVC + 1) * LANES], cmp_w1_v, cmp_w2_v, cmp_pe_v, batch, seq)
    o_cmp, bias_b = _nsa_cmp(proj, kc, vc, batch, seq)
    o_slc = _bb_attn(proj, bias_b, batch=batch, seq=seq, n_pairs=B_REP,
                     q_tile=lambda p: T_BQ + p, k_tile=lambda p: T_BKS, v_tile=lambda p: T_BVS,
                     bias_tile=lambda p: 0, blk=SLC_BLOCK)
    o_win, _ = _band_attn(proj, proj, proj, batch=batch, seq=seq, q_tiles=[T_BQ + r for r in range(B_REP)],
                          k_tile=T_BKW, v_tile=T_BVW, window=NSA_WINDOW, t=256, group=1)

    oc, lse = [], []
    for gi, (window, dil) in enumerate(DILATED_PAIRS):
        wlen = window // dil + 1
        if dil == 1:
            o, l = _band_attn(proj, proj, proj, batch=batch, seq=seq, q_tiles=[T_CQ + gi], k_tile=T_CK + gi,
                              v_tile=T_CV + gi, window=wlen, t=256, group=4)
        else:
            qd = _dilate(proj, T_CQ + gi, batch, seq, dil)
            kd = _dilate(proj, T_CK + gi, batch, seq, dil)
            vd = _dilate(proj, T_CV + gi, batch, seq, dil)
            o, l = _band_attn(qd, kd, vd, batch=batch * dil, seq=seq // dil, q_tiles=[0], k_tile=0, v_tile=0,
                              window=wlen, t=256, group=4)
            o, l = _undilate(o, batch, seq, dil), _undilate(l, batch, seq, dil)
        oc.append(o)
        lse.append(l)

    w_b_perm = w_b.reshape(B_KV_GROUPS, B_REP, HEAD_DIM, d).transpose(1, 0, 2, 3).reshape(B_HEADS * HEAD_DIM, d)
    return _merge(o_a, o_cmp, o_slc, o_win, proj, oc, lse, x2, mod4,
                  w_a.astype(BF16), w_b_perm.astype(BF16), w_c.astype(BF16), w_out.astype(BF16),
                  lng.reshape(1, d), lnb.reshape(1, d), seq)


def _rope_tables(positions):
    inv = ROPE_THETA ** (-jnp.arange(0, HEAD_DIM, 2, dtype=F32) / HEAD_DIM)
    ang = positions.astype(F32).reshape(-1, 1) * inv[None, :]
    cos, sin = jnp.cos(ang), jnp.sin(ang)
    return jnp.tile(cos, (1, 4)), jnp.tile(jnp.concatenate([-sin, sin], axis=1), (1, 2))


def kernel(x, c, positions, w_ada, b_ada, w_in, cmp_w1_k, cmp_w2_k, cmp_pe_k, cmp_w1_v, cmp_w2_v, cmp_pe_v, w_branch_a, w_branch_b, w_branch_c, w_out, ln1_g, ln1_b, router_w, router_b, w_gate_up, b_gate_up, w_down, b_down, ln2_g, ln2_b):
    batch, seq, d = x.shape
    depth = w_in.shape[0]
    cos_t, sin_t = _rope_tables(positions)
    mod = _ada(c, w_ada, b_ada)
    x2 = x.reshape(batch * seq, d)
    for l in range(depth):
        mod4 = mod[l].reshape(batch, N_ADA, 1, d)
        x2 = _mixer_layer(x2, mod4, cos_t, sin_t, batch, seq, w_in[l], cmp_w1_k[l], cmp_w2_k[l], cmp_pe_k[l],
                          cmp_w1_v[l], cmp_w2_v[l], cmp_pe_v[l], w_branch_a[l], w_branch_b[l],
                          w_branch_c[l], w_out[l], ln1_g[l], ln1_b[l])
        x2 = _moe(x2, mod4, l, router_w[l], router_b[l], w_gate_up, b_gate_up[l], w_down, b_down[l],
                  ln2_g[l], ln2_b[l], seq)
    return x2.reshape(batch, seq, d)
```

```python
import functools

import numpy as np
import jax
import jax.numpy as jnp
from jax import lax
from jax.experimental import pallas as pl
from jax.experimental.pallas import tpu as pltpu

F32 = jnp.float32
BF16 = jnp.bfloat16
I32 = jnp.int32
HIGHEST = lax.Precision.HIGHEST

D_MODEL = 1024
DEPTH = 2
HEAD_DIM = 64
ROPE_THETA = 10000.0
LN_EPS = 1e-5
DEEPNORM_ALPHA = (2 * DEPTH) ** 0.25
N_ADA = 6
A_HEADS = 4
MOBA_BLOCK = 256
MOBA_TOPK = 3
B_HEADS = 6
B_KV_GROUPS = 2
B_REP = 3
CMP_BLOCK = 32
CMP_STRIDE = 16
CMP_HIDDEN = 128
SLC_BLOCK = 64
SLC_TOPK = 16
NSA_WINDOW = 512
FORCE_SCORE = 1e6
DILATED_PAIRS = ((128, 1), (512, 4), (2048, 16))
N_EXPERTS = 32
TOP_K = 4
SWIGLU_ALPHA = 1.702
SWIGLU_LIMIT = 7.0

LANES = 128
SUBLANES = 8
HALF = LANES // 2
NEG = -1e30
ROW_CHUNKS = D_MODEL // LANES
VMEM_LIMIT = 56 * 1024 * 1024

T_MG = 0
T_ROPE0, T_ROPE1 = 24, 40
T_AQ, T_AK, T_BQ, T_BKC, T_BKS, T_BKW, T_CQ, T_CK = 24, 26, 28, 31, 32, 33, 34, 37
T_AV, T_BVC, T_BVS, T_BVW, T_CV, T_BG = 40, 42, 43, 44, 45, 48
N_PROJ_TILES = 49
PROJ_W = N_PROJ_TILES * LANES

EXPERT_ROWS = 512
TOK_TILE = 256
N_FILL = 2 * N_EXPERTS + 2


def _cparams(sem, **kw):
    return pltpu.CompilerParams(dimension_semantics=sem, vmem_limit_bytes=VMEM_LIMIT, **kw)


def _lane_iota(shape=(1, LANES)):
    return lax.broadcasted_iota(I32, shape, len(shape) - 1)


def _sigmoid(x):
    return 1.0 / (1.0 + jnp.exp(-x))


def _ln(x):
    mu = jnp.mean(x, axis=-1, keepdims=True)
    xc = x - mu
    var = jnp.mean(xc * xc, axis=-1, keepdims=True)
    return xc * lax.rsqrt(var + LN_EPS)


def _dot_nt(a, b, precision=None):
    return lax.dot_general(a, b, (((1,), (1,)), ((), ())), precision=precision,
                           preferred_element_type=F32)


def _ada_kernel(c_ref, w_ref, b_ref, o_ref):
    c = c_ref[...]
    cond = c * _sigmoid(c)
    o_ref[0] = jnp.dot(cond, w_ref[0], precision=HIGHEST, preferred_element_type=F32) + b_ref[0]


def _ada(c, w_ada, b_ada):
    depth, d, n = w_ada.shape
    b = c.shape[0]
    tn = 1536
    return pl.pallas_call(
        _ada_kernel,
        grid=(depth, n // tn),
        in_specs=[pl.BlockSpec((b, d), lambda l, j: (0, 0)),
                  pl.BlockSpec((1, d, tn), lambda l, j: (l, 0, j)),
                  pl.BlockSpec((1, 1, tn), lambda l, j: (l, 0, j))],
        out_specs=pl.BlockSpec((1, b, tn), lambda l, j: (l, 0, j)),
        out_shape=jax.ShapeDtypeStruct((depth, b, n), F32),
        compiler_params=_cparams(("parallel", "parallel")),
        name="ada",
    )(c, w_ada, b_ada.reshape(depth, 1, n))


def _inproj_kernel(x_ref, sc_ref, sh_ref, w_ref, cos_ref, sin_ref, o_ref, *, chunks):
    h = _ln(x_ref[...]) * (1.0 + sc_ref[0, 0]) + sh_ref[0, 0]
    hb = h.astype(BF16)
    first_half = (_lane_iota() & (HEAD_DIM - 1)) < (HEAD_DIM // 2)
    for c0, cw, rope in chunks:
        acc = jnp.dot(hb, w_ref[:, c0:c0 + cw], preferred_element_type=F32)
        if rope:
            cos = cos_ref[...]
            sin = sin_ref[...]
            for t in range(cw // LANES):
                a = acc[:, t * LANES:(t + 1) * LANES]
                rot = jnp.where(first_half, pltpu.roll(a, LANES - HEAD_DIM // 2, 1),
                                pltpu.roll(a, HEAD_DIM // 2, 1))
                o_ref[:, c0 + t * LANES:c0 + (t + 1) * LANES] = (a * cos + rot * sin).astype(o_ref.dtype)
        else:
            o_ref[:, c0:c0 + cw] = acc.astype(o_ref.dtype)


def _inproj_chunks():
    chunks = []
    for lo, hi, rope in ((0, T_ROPE0, False), (T_ROPE0, T_ROPE1, True), (T_ROPE1, N_PROJ_TILES, False)):
        c = lo * LANES
        while c < hi * LANES:
            cw = min(512, hi * LANES - c)
            chunks.append((c, cw, rope))
            c += cw
    return tuple(chunks)


def _inproj(x2, mod4, w_perm, cos_t, sin_t, seq):
    n, d = x2.shape
    tm = 256
    per_b = seq // tm
    return pl.pallas_call(
        functools.partial(_inproj_kernel, chunks=_inproj_chunks()),
        grid=(n // tm,),
        in_specs=[pl.BlockSpec((tm, d), lambda i: (i, 0)),
                  pl.BlockSpec((1, 1, 1, d), lambda i: (i // per_b, 1, 0, 0)),
                  pl.BlockSpec((1, 1, 1, d), lambda i: (i // per_b, 0, 0, 0)),
                  pl.BlockSpec((d, PROJ_W), lambda i: (0, 0)),
                  pl.BlockSpec((tm, LANES), lambda i: (i, 0)),
                  pl.BlockSpec((tm, LANES), lambda i: (i, 0))],
        out_specs=pl.BlockSpec((tm, PROJ_W), lambda i: (i, 0)),
        out_shape=jax.ShapeDtypeStruct((n, PROJ_W), BF16),
        compiler_params=_cparams(("parallel",)),
        name="inproj",
    )(x2, mod4, mod4, w_perm, cos_t, sin_t)


def _bb_attn_kernel(q_ref, k_ref, v_ref, b_ref, o_ref, qa_sc, s_sc, mp_sc, m_sc, acc_sc,
                    *, blk_shift, tq, tu, big):
    i = pl.program_id(2)
    n_diag = tq // tu
    lane = _lane_iota()
    lo = lane < HALF
    q = q_ref[...]
    b = b_ref[...]
    qa_sc[0] = jnp.where(lo, q, b)
    qa_sc[1] = jnp.where(lo, b, q)
    mp_sc[...] = jnp.full(mp_sc.shape, -jnp.inf, F32)
    acc_sc[...] = jnp.zeros(acc_sc.shape, F32)

    def scores(u0, n_units, diagonal_last):
        tk = n_units * tu
        start = pl.multiple_of(u0 * tu, tu)
        k = k_ref[pl.ds(start, tk), :]
        kblk = (start + lax.broadcasted_iota(I32, (tk, 1), 0)) >> blk_shift
        onehot = jnp.where(kblk == (lane & (HALF - 1)), 1.0, 0.0).astype(BF16)
        ka = (jnp.where(lo, k, onehot), jnp.where(lo, onehot, k))
        row = lax.broadcasted_iota(I32, (tq, 1), 0)
        col = lax.broadcasted_iota(I32, (1, tu), 1)
        for h in range(2):
            s = _dot_nt(qa_sc[h], ka[h])
            mp = mp_sc[h]
            for u in range(n_units):
                su = s[:, u * tu:(u + 1) * tu]
                d = u - (n_units - n_diag)
                if diagonal_last and d >= 0:
                    su = jnp.where(col + d * tu <= row, su, NEG)
                s_sc[h, u0 + u] = su
                for c in range(tu // LANES):
                    mp = jnp.maximum(mp, su[:, c * LANES:(c + 1) * LANES])
            mp_sc[h] = mp

    def weighted(u0, n_units):
        tk = n_units * tu
        start = pl.multiple_of(u0 * tu, tu)
        v = v_ref[pl.ds(start, tk), :]
        one = jnp.ones_like(v)
        vs = (jnp.where(lo, v, one), jnp.where(lo, one, v))
        for h in range(2):
            m_row = m_sc[h]
            ps = []
            for u in range(n_units):
                su = s_sc[h, u0 + u]
                for c in range(tu // LANES):
                    ps.append(jnp.exp2(su[:, c * LANES:(c + 1) * LANES] - m_row).astype(BF16))
            acc_sc[h] = acc_sc[h] + jnp.dot(jnp.concatenate(ps, axis=1), vs[h], preferred_element_type=F32)

    n_full = i * n_diag
    n_big = n_full // big
    n_tail = n_full - n_big * big + n_diag

    def run(fn):
        def big_body(t, c):
            fn(t * big, big, False) if fn is scores else fn(t * big, big)
            return c

        lax.fori_loop(0, n_big, big_body, 0)
        for n in range(n_diag, big + n_diag, n_diag):
            @pl.when(n_tail == n)
            def _(n=n):
                fn(n_big * big, n, True) if fn is scores else fn(n_big * big, n)

    run(scores)
    for h in range(2):
        m_sc[h] = jnp.broadcast_to(jnp.max(mp_sc[h], axis=1, keepdims=True), (tq, LANES))
    run(weighted)
    outs = [acc_sc[h] / pltpu.roll(acc_sc[h], HALF, 1) for h in range(2)]
    o_ref[...] = jnp.where(lo, outs[0], outs[1]).astype(o_ref.dtype)


def _bb_attn(proj, bias, *, batch, seq, n_pairs, q_tile, k_tile, v_tile, bias_tile, blk):
    t, tu = 512, 256
    nq = seq // t
    n = batch * seq
    kern = functools.partial(_bb_attn_kernel, blk_shift=int(np.log2(blk)), tq=t, tu=tu, big=4)
    return pl.pallas_call(
        kern,
        grid=(batch, n_pairs, nq),
        in_specs=[pl.BlockSpec((t, LANES), lambda b, p, i: (b * nq + i, q_tile(p))),
                  pl.BlockSpec((seq, LANES), lambda b, p, i: (b, k_tile(p))),
                  pl.BlockSpec((seq, LANES), lambda b, p, i: (b, v_tile(p))),
                  pl.BlockSpec((t, LANES), lambda b, p, i: (b * nq + i, bias_tile(p)))],
        out_specs=pl.BlockSpec((t, LANES), lambda b, p, i: (b * nq + i, p)),
        out_shape=jax.ShapeDtypeStruct((n, n_pairs * LANES), BF16),
        scratch_shapes=[pltpu.VMEM((2, t, LANES), BF16),
                        pltpu.VMEM((2, seq // tu, t, tu), F32),
                        pltpu.VMEM((2, t, LANES), F32),
                        pltpu.VMEM((2, t, LANES), F32),
                        pltpu.VMEM((2, t, LANES), F32)],
        compiler_params=_cparams(("parallel", "parallel", "parallel")),
        name="bb_attn_%d" % blk,
    )(proj, proj, proj, bias)


def _band_kernel(*refs, tq, kw, window, seq, n_q, group, tiles_per_seq):
    q_refs = refs[:n_q]
    k_ref, v_ref, o_ref, lse_ref = refs[n_q:n_q + 4]
    i = pl.program_id(1)
    lane = _lane_iota()
    lo = lane < HALF
    for g in range(group):
        if tiles_per_seq == 1:
            tile, base = 0, g * seq
        else:
            tile, base = i * group + g, 0
        start = jnp.clip(tile * tq - (kw - tq), 0, seq - kw)
        k = k_ref[pl.ds(pl.multiple_of(base + start, LANES), kw), :]
        v = v_ref[pl.ds(pl.multiple_of(base + start, LANES), kw), :]
        rel = (tile * tq + lax.broadcasted_iota(I32, (tq, 1), 0)) - (start + lax.broadcasted_iota(I32, (1, kw), 1))
        ok = (rel >= 0) & (rel < window)
        for qi in range(n_q):
            q = q_refs[qi][g * tq:(g + 1) * tq, :]
            zero = jnp.zeros_like(q)
            qs = (jnp.where(lo, q, zero), jnp.where(lo, zero, q))
            outs, lses = [], []
            for h in range(2):
                s = jnp.where(ok, _dot_nt(qs[h], k), NEG)
                m = jnp.max(s, axis=1, keepdims=True)
                p = jnp.exp2(s - m)
                l = jnp.sum(p, axis=1, keepdims=True)
                outs.append(jnp.dot(p.astype(BF16), v, preferred_element_type=F32) / l)
                lses.append(m + jnp.log2(l))
            rows, cols = slice(g * tq, (g + 1) * tq), slice(qi * LANES, (qi + 1) * LANES)
            o_ref[rows, cols] = jnp.where(lo, outs[0], outs[1]).astype(o_ref.dtype)
            lse_ref[rows, cols] = jnp.where(lo, lses[0], lses[1])


def _band_attn(q_arr, k_arr, v_arr, *, batch, seq, q_tiles, k_tile, v_tile, window, t, group):
    t = min(t, seq)
    tiles_per_seq = seq // t
    n = batch * seq
    n_q = len(q_tiles)
    if tiles_per_seq == 1:
        group = min(group, batch)
        grid = (batch // group, 1)
        kv_rows = group * seq
        row_blk = lambda b, i: b
    else:
        group = min(group, tiles_per_seq)
        grid = (batch, tiles_per_seq // group)
        kv_rows = seq
        row_blk = lambda b, i: b * (tiles_per_seq // group) + i
    kw = min(seq, t + -(-(window - 1) // LANES) * LANES)
    kern = functools.partial(_band_kernel, tq=t, kw=kw, window=window, seq=seq, n_q=n_q, group=group,
                             tiles_per_seq=tiles_per_seq)
    qspecs = [pl.BlockSpec((group * t, LANES), (lambda b, i, c=c: (row_blk(b, i), c))) for c in q_tiles]
    return pl.pallas_call(
        kern,
        grid=grid,
        in_specs=qspecs + [pl.BlockSpec((kv_rows, LANES), lambda b, i: (b, k_tile)),
                           pl.BlockSpec((kv_rows, LANES), lambda b, i: (b, v_tile))],
        out_specs=[pl.BlockSpec((group * t, n_q * LANES), lambda b, i: (row_blk(b, i), 0)),
                   pl.BlockSpec((group * t, n_q * LANES), lambda b, i: (row_blk(b, i), 0))],
        out_shape=[jax.ShapeDtypeStruct((n, n_q * LANES), BF16),
                   jax.ShapeDtypeStruct((n, n_q * LANES), F32)],
        compiler_params=_cparams(("parallel", "parallel")),
        name="band_attn_%d" % window,
    )(*([q_arr] * n_q), k_arr, v_arr)


def _kmean_kernel(k_ref, o_ref, *, nb):
    s = k_ref.shape[0]
    blk = lax.broadcasted_iota(I32, (nb, s), 1) >> int(np.log2(MOBA_BLOCK))
    avg = jnp.where(blk == lax.broadcasted_iota(I32, (nb, s), 0), 1.0 / MOBA_BLOCK, 0.0).astype(BF16)
    o_ref[0] = jnp.dot(avg, k_ref[...], preferred_element_type=F32)


def _kmean(proj, batch, seq):
    nb = seq // MOBA_BLOCK
    w = A_HEADS * HEAD_DIM
    return pl.pallas_call(
        functools.partial(_kmean_kernel, nb=nb),
        grid=(batch,),
        in_specs=[pl.BlockSpec((seq, w), lambda b: (b, T_AK * LANES // w))],
        out_specs=pl.BlockSpec((1, nb, w), lambda b: (b, 0, 0)),
        out_shape=jax.ShapeDtypeStruct((batch, nb, w), F32),
        compiler_params=_cparams(("parallel",)),
        name="moba_kmean",
    )(proj)


def _rank_desc(g, n_idx, n):
    del n_idx
    gs = min(SUBLANES, n)
    groups = [g[r0:r0 + gs, :] for r0 in range(0, n, gs)]
    ranks = [jnp.zeros(x.shape, I32) for x in groups]
    sub = lax.broadcasted_iota(I32, (gs, 1), 0)
    for m in range(n):
        c = g[m:m + 1, :]
        for gi, x in enumerate(groups):
            if gi * gs > m:
                beats = c >= x
            elif gi * gs + gs - 1 <= m:
                beats = c > x
            else:
                beats = (c > x) | ((c == x) & (sub + gi * gs > m))
            ranks[gi] = ranks[gi] + jnp.where(beats, 1, 0)
    return jnp.concatenate(ranks, axis=0)


def _moba_sel_kernel(*refs, nb, n_sel, n_pairs):
    q_refs = refs[:n_pairs]
    km_ref, b_ref = refs[n_pairs:]
    i = pl.program_id(1)
    t = b_ref.shape[0]
    lo = _lane_iota() < HALF
    pad = jnp.zeros((HALF - nb, LANES), F32)
    n_idx = lax.broadcasted_iota(I32, (nb, 1), 0)
    valid = n_idx < i
    for p in range(n_pairs):
        q = q_refs[p][...].astype(F32)
        km = km_ref[0, :, p * LANES:(p + 1) * LANES]
        zero = jnp.zeros_like(km)
        kmt = jnp.concatenate([jnp.where(lo, zero, km), pad, jnp.where(lo, km, zero), pad], axis=0)
        gt = _dot_nt(kmt, q, precision=HIGHEST)
        rows = []
        for r0 in (0, HALF):
            g = jnp.where(valid, gt[r0:r0 + nb, :], -jnp.inf)
            rank = _rank_desc(g, n_idx, nb)
            allowed = (valid & (rank < n_sel)) | (n_idx == i)
            rows.append(jnp.where(allowed, 0.0, NEG))
            rows.append(jnp.zeros((HALF - nb, t), F32))
        b_ref[:, p * LANES:(p + 1) * LANES] = jnp.concatenate(rows, axis=0).T.astype(b_ref.dtype)


def _moba_select(proj, kmean, batch, seq):
    t = MOBA_BLOCK
    nq = seq // t
    nb = seq // MOBA_BLOCK
    n_sel = min(MOBA_TOPK, nb - 1)
    n_pairs = A_HEADS // 2
    qspecs = [pl.BlockSpec((t, LANES), (lambda b, i, p=p: (b * nq + i, T_AQ + p))) for p in range(n_pairs)]
    return pl.pallas_call(
        functools.partial(_moba_sel_kernel, nb=nb, n_sel=n_sel, n_pairs=n_pairs),
        grid=(batch, nq),
        in_specs=qspecs + [pl.BlockSpec((1, nb, n_pairs * LANES), lambda b, i: (b, 0, 0))],
        out_specs=pl.BlockSpec((t, n_pairs * LANES), lambda b, i: (b * nq + i, 0)),
        out_shape=jax.ShapeDtypeStruct((batch * seq, n_pairs * LANES), BF16),
        compiler_params=_cparams(("parallel", "parallel")),
        name="moba_select",
    )(*([proj] * n_pairs), kmean)


def _compress_kernel(x_ref, w1_ref, pe_ref, w1f_ref, w2_ref, o_ref):
    x = x_ref[0]
    nblk = x.shape[0]
    outs = []
    for g in range(B_KV_GROUPS):
        u = jnp.dot(x, w1_ref[g, 0], preferred_element_type=F32)
        v = jnp.dot(x, w1_ref[g, 1], preferred_element_type=F32)
        pe_h = jnp.dot(pe_ref[...], w1f_ref[...], precision=HIGHEST, preferred_element_type=F32)
        hid = u + pltpu.roll(v, nblk - 1, 0) + pe_h[:1]
        hid = hid * _sigmoid(hid)
        outs.append(jnp.dot(hid, w2_ref[g], precision=HIGHEST, preferred_element_type=F32))
    o = outs[0] + outs[1]
    rows = lax.broadcasted_iota(I32, (nblk, 1), 0)
    o_ref[0] = jnp.where(rows < nblk - 1, o, 0.0).astype(o_ref.dtype)


def _compress(xt, w1, w2, pe, batch, seq):
    nblk = seq // CMP_STRIDE
    xg = xt.reshape(batch, nblk, CMP_STRIDE * LANES)
    w1r = w1.reshape(2, CMP_STRIDE, HEAD_DIM, CMP_HIDDEN)
    w1e = jnp.zeros((B_KV_GROUPS, 2, CMP_STRIDE, B_KV_GROUPS, HEAD_DIM, CMP_HIDDEN), F32)
    for g in range(B_KV_GROUPS):
        w1e = w1e.at[g, :, :, g].set(w1r)
    w1e = w1e.reshape(B_KV_GROUPS, 2, CMP_STRIDE * LANES, CMP_HIDDEN).astype(BF16)
    pe_flat = jnp.broadcast_to(pe.reshape(1, CMP_BLOCK * HEAD_DIM), (SUBLANES, CMP_BLOCK * HEAD_DIM))
    w2e = jnp.zeros((B_KV_GROUPS, CMP_HIDDEN, LANES), F32)
    for g in range(B_KV_GROUPS):
        w2e = w2e.at[g, :, g * HEAD_DIM:(g + 1) * HEAD_DIM].set(w2)
    return pl.pallas_call(
        _compress_kernel,
        grid=(batch,),
        in_specs=[pl.BlockSpec((1, nblk, CMP_STRIDE * LANES), lambda b: (b, 0, 0)),
                  pl.BlockSpec(w1e.shape, lambda b: (0, 0, 0, 0)),
                  pl.BlockSpec(pe_flat.shape, lambda b: (0, 0)),
                  pl.BlockSpec(w1.shape, lambda b: (0, 0)),
                  pl.BlockSpec(w2e.shape, lambda b: (0, 0, 0))],
        out_specs=pl.BlockSpec((1, nblk, LANES), lambda b: (b, 0, 0)),
        out_shape=jax.ShapeDtypeStruct((batch, nblk, LANES), BF16),
        compiler_params=_cparams(("parallel",)),
        name="nsa_compress",
    )(xg, w1e, pe_flat, w1, w2e)


def _nsa_cmp_kernel(q0_ref, q1_ref, q2_ref, kc_ref, vc_ref, ov_ref, o_ref, b_ref, *, n_sel, ns):
    i = pl.program_id(1)
    t = q0_ref.shape[0]
    ncp = kc_ref.shape[1]
    lane = _lane_iota()
    lo = lane < HALF
    kc = kc_ref[0]
    vc = vc_ref[0]
    tq = i * t + lax.broadcasted_iota(I32, (t, 1), 0)
    cmp_end = lax.broadcasted_iota(I32, (1, ncp), 1) * CMP_STRIDE + (CMP_BLOCK - 1)
    vis = cmp_end <= tq
    psum = [jnp.zeros((t, ncp), F32) for _ in range(B_KV_GROUPS)]
    for r, q_ref in enumerate((q0_ref, q1_ref, q2_ref)):
        q = q_ref[...]
        zero = jnp.zeros_like(q)
        outs = []
        for g in range(B_KV_GROUPS):
            qg = jnp.where(lo, q, zero) if g == 0 else jnp.where(lo, zero, q)
            s = jnp.where(vis, _dot_nt(qg, kc), NEG)
            m = jnp.max(s, axis=1, keepdims=True)
            m = jnp.where(m > 0.5 * NEG, m, 0.0)
            e = jnp.exp2(s - m)
            p = e / jnp.maximum(jnp.sum(e, axis=1, keepdims=True), 1e-30)
            psum[g] = psum[g] + p
            outs.append(jnp.dot(p.astype(BF16), vc, preferred_element_type=F32))
        o_ref[:, r * LANES:(r + 1) * LANES] = jnp.where(lo, outs[0], outs[1]).astype(o_ref.dtype)
    n_idx = lax.broadcasted_iota(I32, (ns, 1), 0)
    tcol = i * t + lax.broadcasted_iota(I32, (1, t), 1)
    qblk = tcol >> int(np.log2(SLC_BLOCK))
    forced = (n_idx == 0) | (n_idx == qblk) | (n_idx == qblk - 1)
    valid = n_idx <= qblk
    rows = []
    for g in (1, 0):
        imp = _dot_nt(ov_ref[...], psum[g], precision=HIGHEST)
        imp = jnp.where(forced, FORCE_SCORE, imp)
        imp = jnp.where(valid, imp, -jnp.inf)
        rank = _rank_desc(imp, n_idx, ns)
        allowed = valid & (rank < n_sel)
        rows.append(jnp.where(allowed, 0.0, NEG))
        if ns < HALF:
            rows.append(jnp.zeros((HALF - ns, t), F32))
    b_ref[...] = jnp.concatenate(rows, axis=0).T.astype(b_ref.dtype)


def _nsa_cmp(proj, kc, vc, batch, seq):
    t = 256
    nq = seq // t
    ns = seq // SLC_BLOCK
    n_sel = min(SLC_TOPK, ns)
    ncp = seq // CMP_STRIDE
    cs = np.arange(ncp)[None, :] * CMP_STRIDE
    ss = np.arange(ns)[:, None] * SLC_BLOCK
    ov = ((cs < ss + SLC_BLOCK) & (cs + CMP_BLOCK > ss)).astype(np.float32)
    ov[:, ncp - 1] = 0.0
    n = batch * seq
    qspec = [pl.BlockSpec((t, LANES), (lambda b, i, r=r: (b * nq + i, T_BQ + r))) for r in range(B_REP)]
    return pl.pallas_call(
        functools.partial(_nsa_cmp_kernel, n_sel=n_sel, ns=ns),
        grid=(batch, nq),
        in_specs=qspec + [pl.BlockSpec((1, ncp, LANES), lambda b, i: (b, 0, 0)),
                          pl.BlockSpec((1, ncp, LANES), lambda b, i: (b, 0, 0)),
                          pl.BlockSpec((ns, ncp), lambda b, i: (0, 0))],
        out_specs=[pl.BlockSpec((t, B_REP * LANES), lambda b, i: (b * nq + i, 0)),
                   pl.BlockSpec((t, LANES), lambda b, i: (b * nq + i, 0))],
        out_shape=[jax.ShapeDtypeStruct((n, B_REP * LANES), BF16),
                   jax.ShapeDtypeStruct((n, LANES), BF16)],
        compiler_params=_cparams(("parallel", "parallel")),
        name="nsa_cmp_select",
    )(proj, proj, proj, kc, vc, jnp.asarray(ov))


def _merge_kernel(oa_ref, ocmp_ref, oslc_ref, owin_ref, bg_ref, oc0_ref, oc1_ref, oc2_ref,
                  l0_ref, l1_ref, l2_ref, mg_ref, x_ref, g1_ref, wa_ref, wb_ref, wc_ref, wo_ref,
                  eg_ref, lng_ref, lnb_ref, o_ref):
    d = D_MODEL
    sg = _sigmoid(bg_ref[...].astype(F32))
    sg_hi = sg.astype(BF16)
    sg_lo = (sg - sg_hi.astype(F32)).astype(BF16)
    ob = jnp.zeros(ocmp_ref.shape, F32)
    for br, ref in enumerate((ocmp_ref, oslc_ref, owin_ref)):
        gexp = (jnp.dot(sg_hi, eg_ref[br], preferred_element_type=F32)
                + jnp.dot(sg_lo, eg_ref[br], preferred_element_type=F32))
        ob = ob + gexp * ref[...].astype(F32)
    l0, l1, l2 = l0_ref[...], l1_ref[...], l2_ref[...]
    mx = jnp.maximum(jnp.maximum(l0, l1), l2)
    e0, e1, e2 = jnp.exp2(l0 - mx), jnp.exp2(l1 - mx), jnp.exp2(l2 - mx)
    den = e0 + e1 + e2
    oc = ((e0 / den) * oc0_ref[...].astype(F32) + (e1 / den) * oc1_ref[...].astype(F32)
          + (e2 / den) * oc2_ref[...].astype(F32))
    pa = jnp.dot(oa_ref[...], wa_ref[...], preferred_element_type=F32)
    pb = jnp.dot(ob.astype(BF16), wb_ref[...], preferred_element_type=F32)
    pc = jnp.dot(oc.astype(BF16), wc_ref[...], preferred_element_type=F32)
    merged = (_sigmoid(mg_ref[:, 0:d].astype(F32)) * pa
              + _sigmoid(mg_ref[:, d:2 * d].astype(F32)) * pb
              + _sigmoid(mg_ref[:, 2 * d:3 * d].astype(F32)) * pc)
    y = jnp.dot(merged.astype(BF16), wo_ref[...], preferred_element_type=F32)
    z = DEEPNORM_ALPHA * x_ref[...] + g1_ref[0, 0] * y
    o_ref[...] = _ln(z) * lng_ref[...] + lnb_ref[...]


def _gate_expand():
    eg = np.zeros((3, LANES, B_HEADS * HEAD_DIM), np.float32)
    for g in range(B_KV_GROUPS):
        for r in range(B_REP):
            for br in range(3):
                c0 = (r * B_KV_GROUPS + g) * HEAD_DIM
                eg[br, (g * B_REP + r) * 3 + br, c0:c0 + HEAD_DIM] = 1.0
    return jnp.asarray(eg)


def _merge(o_a, o_cmp, o_slc, o_win, proj, oc, lse, x2, mod4, wa, wb, wc, wo, lng, lnb, seq):
    n, d = x2.shape
    tm = 256
    per_b = seq // tm
    row = lambda w: pl.BlockSpec((tm, w), lambda i: (i, 0))
    full = lambda a: pl.BlockSpec(a.shape, lambda i: (0,) * a.ndim)
    eg = _gate_expand().astype(BF16)
    return pl.pallas_call(
        _merge_kernel,
        grid=(n // tm,),
        in_specs=[row(o_a.shape[1]), row(o_cmp.shape[1]), row(o_slc.shape[1]), row(o_win.shape[1]),
                  pl.BlockSpec((tm, LANES), lambda i: (i, T_BG)),
                  row(LANES), row(LANES), row(LANES), row(LANES), row(LANES), row(LANES),
                  pl.BlockSpec((tm, 3 * d), lambda i: (i, T_MG)),
                  row(d),
                  pl.BlockSpec((1, 1, 1, d), lambda i: (i // per_b, 2, 0, 0)),
                  full(wa), full(wb), full(wc), full(wo), full(eg), full(lng), full(lnb)],
        out_specs=row(d),
        out_shape=jax.ShapeDtypeStruct((n, d), F32),
        compiler_params=_cparams(("parallel",)),
        name="merge_out",
    )(o_a, o_cmp, o_slc, o_win, proj, oc[0], oc[1], oc[2], lse[0], lse[1], lse[2],
      proj, x2, mod4, wa, wb, wc, wo, eg, lng, lnb)


def _router_kernel(x_ref, sc_ref, sh_ref, rw_ref, rb_ref, meta_ref, gate_ref, tile_ref, cnt_ref, carry_sc):
    tm = x_ref.shape[0]
    ne = N_EXPERTS

    @pl.when(pl.program_id(0) == 0)
    def _():
        carry_sc[...] = jnp.zeros(carry_sc.shape, F32)

    h = _ln(x_ref[...]) * (1.0 + sc_ref[0, 0]) + sh_ref[0, 0]
    lg = _dot_nt(rw_ref[...], h, precision=HIGHEST)[:ne] + rb_ref[...][:ne, :1]
    e_idx = lax.broadcasted_iota(I32, (ne, 1), 0).astype(F32)
    onehots, vals, idxs = [], [], []
    for _ in range(TOP_K):
        m = jnp.max(lg, axis=0, keepdims=True)
        idx = jnp.min(jnp.where(lg == m, e_idx, float(ne)), axis=0, keepdims=True)
        oh = e_idx == idx
        onehots.append(oh)
        vals.append(m)
        idxs.append(idx)
        lg = jnp.where(oh, -jnp.inf, lg)
    es = [jnp.exp(v - vals[0]) for v in vals]
    den = es[0] + es[1] + es[2] + es[3]
    cnt = jnp.zeros((ne, tm), F32)
    for oh in onehots:
        cnt = cnt + jnp.where(oh, 1.0, 0.0)
    before = jnp.where(lax.broadcasted_iota(I32, (tm, tm), 0) < lax.broadcasted_iota(I32, (tm, tm), 1), 1.0, 0.0)
    earlier = jnp.dot(cnt.astype(BF16), before.astype(BF16), preferred_element_type=F32)
    tile_cnt = jnp.broadcast_to(jnp.sum(cnt, axis=1, keepdims=True), (ne, LANES))
    lower = jnp.where(lax.broadcasted_iota(I32, (ne, ne), 0) > lax.broadcasted_iota(I32, (ne, ne), 1), 1.0, 0.0)
    run_off = jnp.dot(lower.astype(BF16), tile_cnt.astype(BF16), preferred_element_type=F32)
    pos = earlier + run_off[:, :1]
    locals_ = [jnp.sum(jnp.where(oh, pos, 0.0), axis=0, keepdims=True) for oh in onehots]

    def token_major(rows):
        stack = jnp.concatenate(rows + [jnp.zeros((LANES - len(rows), tm), F32)], axis=0)
        return stack.T

    meta_ref[...] = token_major(idxs + locals_).astype(I32)
    gate_ref[...] = token_major([e / den for e in es])
    lane = _lane_iota()
    cols = jnp.where(lane == 0, carry_sc[...], jnp.where(lane == 1, tile_cnt, jnp.where(lane == 2, run_off, 0.0)))
    table = jnp.concatenate([cols, jnp.zeros((LANES - ne, LANES), F32)], axis=0).T
    tile_ref[...] = table[:SUBLANES].astype(I32)
    carry_sc[...] = carry_sc[...] + tile_cnt
    total = jnp.concatenate([carry_sc[...], jnp.zeros((LANES - ne, LANES), F32)], axis=0).T
    cnt_ref[...] = total[:SUBLANES].astype(I32)


def _router(x2, mod4, rw, rb, seq):
    n, d = x2.shape
    tm = TOK_TILE
    per_b = seq // tm
    rw_p = jnp.zeros((LANES, d), F32).at[:N_EXPERTS].set(rw.T)
    rb_p = jnp.zeros((LANES, LANES), F32).at[:N_EXPERTS].set(rb[:, None])
    return pl.pallas_call(
        _router_kernel,
        grid=(n // tm,),
        in_specs=[pl.BlockSpec((tm, d), lambda i: (i, 0)),
                  pl.BlockSpec((1, 1, 1, d), lambda i: (i // per_b, 4, 0, 0)),
                  pl.BlockSpec((1, 1, 1, d), lambda i: (i // per_b, 3, 0, 0)),
                  pl.BlockSpec((LANES, d), lambda i: (0, 0)),
                  pl.BlockSpec((LANES, LANES), lambda i: (0, 0))],
        out_specs=[pl.BlockSpec((tm, LANES), lambda i: (i, 0)),
                   pl.BlockSpec((tm, LANES), lambda i: (i, 0)),
                   pl.BlockSpec((SUBLANES, LANES), lambda i: (i, 0)),
                   pl.BlockSpec((SUBLANES, LANES), lambda i: (0, 0))],
        out_shape=[jax.ShapeDtypeStruct((n, LANES), I32),
                   jax.ShapeDtypeStruct((n, LANES), F32),
                   jax.ShapeDtypeStruct((n // tm * SUBLANES, LANES), I32),
                   jax.ShapeDtypeStruct((SUBLANES, LANES), I32)],
        scratch_shapes=[pltpu.VMEM((N_EXPERTS, LANES), F32)],
        compiler_params=_cparams(("arbitrary",)),
        name="router",
    )(x2, mod4, mod4, rw_p, rb_p)


def _wait_rows(src_like, dst_like, sem, n_rows):
    span = pl.ds(0, n_rows * ROW_CHUNKS)
    pltpu.make_async_copy(src_like.at[span], dst_like.at[span], sem).wait()


def _span_rows(ref, row, n):
    return ref.at[pl.ds(pl.multiple_of(row * ROW_CHUNKS, ROW_CHUNKS), n * ROW_CHUNKS)]


def _for_each_run(tab_ref, fn):
    for e in range(N_EXPERTS):
        n = tab_ref[1, e]

        @pl.when(n > 0)
        def _(e=e, n=n):
            fn(tab_ref[0, e], tab_ref[2, e], n, e % 2)


def _dispatch_kernel(tab_ref, fill_ref, x_ref, sc_ref, sh_ref, meta_ref, xs_hbm, xb0, xb1, zb, sem, fsem):
    i = pl.program_id(0)
    n_steps = pl.num_programs(0)
    tm = x_ref.shape[0]
    nrow = tm * TOP_K
    bufs = (xb0, xb1)

    @pl.when(i == 0)
    def _():
        zb[...] = jnp.zeros(zb.shape, F32)
        for j in range(N_FILL):
            n = fill_ref[1, j]

            @pl.when(n > 0)
            def _(j=j, n=n):
                fill = pltpu.make_async_copy(_span_rows(zb, 0, n), _span_rows(xs_hbm, fill_ref[0, j], n), fsem)
                fill.start()
                fill.wait()

    h = (_ln(x_ref[...]) * (1.0 + sc_ref[0, 0]) + sh_ref[0, 0]).astype(BF16)
    pos_t = meta_ref[...].astype(F32).T
    r = lax.broadcasted_iota(I32, (nrow, 1), 0)
    sel = jnp.zeros((nrow, tm), F32)
    for k in range(TOP_K):
        sel = sel + jnp.where(r == pos_t[TOP_K + k:TOP_K + k + 1, :].astype(I32), 1.0, 0.0)
    xt = jnp.dot(sel.astype(BF16), h, preferred_element_type=F32)

    for slot in range(2):
        @pl.when(i % 2 == slot)
        def _(slot=slot):
            buf = bufs[slot]

            @pl.when(i >= 2)
            def _():
                _wait_rows(buf, xs_hbm, sem.at[slot], nrow)

            for c in range(ROW_CHUNKS):
                buf[pl.ds(c, nrow, stride=ROW_CHUNKS), :] = xt[:, c * LANES:(c + 1) * LANES]
            _for_each_run(tab_ref, lambda srow, trow, n, prio: pltpu.make_async_copy(
                _span_rows(buf, trow, n), _span_rows(xs_hbm, srow, n), sem.at[slot]).start(priority=prio))

            @pl.when(i == n_steps - 1)
            def _():
                _wait_rows(buf, xs_hbm, sem.at[slot], nrow)

                @pl.when(i >= 1)
                def _():
                    _wait_rows(bufs[1 - slot], xs_hbm, sem.at[1 - slot], nrow)


def _dispatch(x2, mod4, meta, tab, fill_tab, n_rows, seq):
    n, d = x2.shape
    tm = TOK_TILE
    per_b = seq // tm
    return pl.pallas_call(
        _dispatch_kernel,
        grid=(n // tm,),
        in_specs=[pl.BlockSpec((SUBLANES, LANES), lambda i: (i, 0), memory_space=pltpu.SMEM),
                  pl.BlockSpec((SUBLANES, LANES), lambda i: (0, 0), memory_space=pltpu.SMEM),
                  pl.BlockSpec((tm, d), lambda i: (i, 0)),
                  pl.BlockSpec((1, 1, 1, d), lambda i: (i // per_b, 4, 0, 0)),
                  pl.BlockSpec((1, 1, 1, d), lambda i: (i // per_b, 3, 0, 0)),
                  pl.BlockSpec((tm, LANES), lambda i: (i, 0))],
        out_specs=pl.BlockSpec(memory_space=pl.ANY),
        out_shape=jax.ShapeDtypeStruct((n_rows * ROW_CHUNKS, LANES), F32),
        scratch_shapes=[pltpu.VMEM((tm * TOP_K * ROW_CHUNKS, LANES), F32),
                        pltpu.VMEM((tm * TOP_K * ROW_CHUNKS, LANES), F32),
                        pltpu.VMEM((EXPERT_ROWS * ROW_CHUNKS, LANES), F32),
                        pltpu.SemaphoreType.DMA((2,)),
                        pltpu.SemaphoreType.DMA],
        compiler_params=_cparams(("arbitrary",)),
        name="moe_dispatch",
    )(tab, fill_tab, x2, mod4, mod4, meta)


def _expert_kernel(be_ref, first_ref, used_ref, xs_ref, wgu_ref, bgu_ref, wd_ref, bd_ref, ys_ref,
                   x_sc, wgu_sc, wd_sc):
    del be_ref
    g = pl.program_id(0)

    @pl.when(g < used_ref[0])
    def _():
        _expert_block(g, first_ref, xs_ref, wgu_ref, bgu_ref, wd_ref, bd_ref, ys_ref, x_sc, wgu_sc, wd_sc)

    @pl.when(g >= used_ref[0])
    def _():
        ys_ref[...] = jnp.zeros(ys_ref.shape, ys_ref.dtype)


def _expert_block(g, first_ref, xs_ref, wgu_ref, bgu_ref, wd_ref, bd_ref, ys_ref, x_sc, wgu_sc, wd_sc):
    rows = x_sc.shape[0]
    grp = 2 * LANES

    @pl.when(first_ref[g] == 1)
    def _():
        r = lax.broadcasted_iota(I32, (grp, grp), 0)
        c = lax.broadcasted_iota(I32, (grp, grp), 1)
        src = jnp.where(c < LANES, 2 * c, 2 * (c - LANES) + 1)
        sel = jnp.where(r == src, 1.0, 0.0).astype(BF16)
        for j in range(wgu_sc.shape[1] // grp):
            blk = wgu_ref[0, :, j * grp:(j + 1) * grp].astype(BF16)
            wgu_sc[:, j * grp:(j + 1) * grp] = jnp.dot(blk, sel, preferred_element_type=F32).astype(BF16)
        wd_sc[...] = wd_ref[0].astype(BF16)

    for c in range(ROW_CHUNKS):
        x_sc[:, c * LANES:(c + 1) * LANES] = xs_ref[pl.ds(c, rows, stride=ROW_CHUNKS), :].astype(BF16)
    gu = jnp.dot(x_sc[...], wgu_sc[...], preferred_element_type=F32) + bgu_ref[0]
    acts = []
    for j in range(gu.shape[1] // grp):
        glu = jnp.minimum(gu[:, j * grp:j * grp + LANES], SWIGLU_LIMIT)
        lin = jnp.clip(gu[:, j * grp + LANES:(j + 1) * grp], -SWIGLU_LIMIT, SWIGLU_LIMIT)
        acts.append((glu * _sigmoid(SWIGLU_ALPHA * glu) * (lin + 1.0)).astype(BF16))
    act = jnp.concatenate(acts, axis=1)
    y = jnp.dot(act, wd_sc[...], preferred_element_type=F32) + bd_ref[0]
    for c in range(ROW_CHUNKS):
        ys_ref[pl.ds(c, rows, stride=ROW_CHUNKS), :] = y[:, c * LANES:(c + 1) * LANES]


def _experts(xs, block_exp, first, n_used, layer, w_gu, b_gu_grouped, w_dn, b_dn):
    d = D_MODEL
    n_blocks = block_exp.shape[0]
    rb = EXPERT_ROWS
    last = lambda g, nu: jnp.minimum(g, nu[0] - 1)
    wmap = lambda g, be, fi, nu: (layer, be[last(g, nu)], 0, 0)
    bmap = lambda g, be, fi, nu: (be[last(g, nu)], 0, 0)
    return pl.pallas_call(
        _expert_kernel,
        grid_spec=pltpu.PrefetchScalarGridSpec(
            num_scalar_prefetch=3,
            grid=(n_blocks,),
            in_specs=[pl.BlockSpec((rb * ROW_CHUNKS, LANES), lambda g, be, fi, nu: (last(g, nu), 0)),
                      pl.BlockSpec((None, 1, d, 2 * d), wmap),
                      pl.BlockSpec((1, 1, 2 * d), bmap),
                      pl.BlockSpec((None, 1, d, d), wmap),
                      pl.BlockSpec((1, 1, d), bmap)],
            out_specs=pl.BlockSpec((rb * ROW_CHUNKS, LANES), lambda g, be, fi, nu: (g, 0)),
            scratch_shapes=[pltpu.VMEM((rb, d), BF16),
                            pltpu.VMEM((d, 2 * d), BF16),
                            pltpu.VMEM((d, d), BF16)]),
        out_shape=jax.ShapeDtypeStruct(xs.shape, F32),
        compiler_params=_cparams(("arbitrary",)),
        name="moe_experts",
    )(block_exp, first, n_used, xs, w_gu, b_gu_grouped, w_dn, b_dn)


def _combine_kernel(tab_ref, tab_next_ref, meta_ref, gate_ref, x_ref, g2_ref, lng_ref, lnb_ref, ys_hbm, o_ref,
                    yb0, yb1, sem):
    i = pl.program_id(0)
    n_steps = pl.num_programs(0)
    tm = x_ref.shape[0]
    nrow = tm * TOP_K
    bufs = (yb0, yb1)

    def fetch(t_ref, slot):
        _for_each_run(t_ref, lambda srow, trow, n, prio: pltpu.make_async_copy(
            _span_rows(ys_hbm, srow, n), _span_rows(bufs[slot], trow, n), sem.at[slot]).start(priority=prio))

    @pl.when(i == 0)
    def _():
        fetch(tab_ref, 0)

    for slot in range(2):
        @pl.when(i % 2 == slot)
        def _(slot=slot):
            buf = bufs[slot]

            @pl.when(i + 1 < n_steps)
            def _():
                fetch(tab_next_ref, 1 - slot)

            _wait_rows(ys_hbm, buf, sem.at[slot], nrow)
            yt = jnp.concatenate([buf[pl.ds(c, nrow, stride=ROW_CHUNKS), :] for c in range(ROW_CHUNKS)],
                                 axis=1).astype(BF16)
            meta = meta_ref[...]
            gates = gate_ref[...]
            col = lax.broadcasted_iota(I32, (1, nrow), 1)
            w = jnp.zeros((tm, nrow), F32)
            for k in range(TOP_K):
                w = w + jnp.where(col == meta[:, TOP_K + k:TOP_K + k + 1], gates[:, k:k + 1], 0.0)
            w_hi = w.astype(BF16)
            w_lo = (w - w_hi.astype(F32)).astype(BF16)
            y = (jnp.dot(w_hi, yt, preferred_element_type=F32) + jnp.dot(w_lo, yt, preferred_element_type=F32))
            z = DEEPNORM_ALPHA * x_ref[...] + g2_ref[0, 0] * y
            o_ref[...] = _ln(z) * lng_ref[...] + lnb_ref[...]


def _combine(ys, tab, meta, gates, x2, mod4, lng, lnb, seq):
    n, d = x2.shape
    tm = TOK_TILE
    per_b = seq // tm
    last = n // tm - 1
    return pl.pallas_call(
        _combine_kernel,
        grid=(n // tm,),
        in_specs=[pl.BlockSpec((SUBLANES, LANES), lambda i: (i, 0), memory_space=pltpu.SMEM),
                  pl.BlockSpec((SUBLANES, LANES), lambda i: (jnp.minimum(i + 1, last), 0),
                               memory_space=pltpu.SMEM),
                  pl.BlockSpec((tm, LANES), lambda i: (i, 0)),
                  pl.BlockSpec((tm, LANES), lambda i: (i, 0)),
                  pl.BlockSpec((tm, d), lambda i: (i, 0)),
                  pl.BlockSpec((1, 1, 1, d), lambda i: (i // per_b, 5, 0, 0)),
                  pl.BlockSpec((1, d), lambda i: (0, 0)),
                  pl.BlockSpec((1, d), lambda i: (0, 0)),
                  pl.BlockSpec(memory_space=pl.ANY)],
        out_specs=pl.BlockSpec((tm, d), lambda i: (i, 0)),
        out_shape=jax.ShapeDtypeStruct((n, d), F32),
        scratch_shapes=[pltpu.VMEM((tm * TOP_K * ROW_CHUNKS, LANES), F32),
                        pltpu.VMEM((tm * TOP_K * ROW_CHUNKS, LANES), F32),
                        pltpu.SemaphoreType.DMA((2,))],
        compiler_params=_cparams(("arbitrary",)),
        name="moe_combine",
    )(tab, tab, meta, gates, x2, mod4, lng, lnb, ys)


def _moe(x2, mod4, layer, rw, rb, w_gu, b_gu, w_dn, b_dn, lng, lnb, seq):
    n, d = x2.shape
    meta, gates, tile_meta, cnt = _router(x2, mod4, rw, rb, seq)
    counts = cnt[0, :N_EXPERTS]
    padded = -(-counts // EXPERT_ROWS) * EXPERT_ROWS
    pend = jnp.cumsum(padded)
    pstart = pend - padded
    n_blocks = -(-(n * TOP_K + N_EXPERTS * (EXPERT_ROWS - 1)) // EXPERT_ROWS)
    n_rows = n_blocks * EXPERT_ROWS
    block_row0 = jnp.arange(n_blocks, dtype=I32) * EXPERT_ROWS
    block_exp = jnp.minimum(jnp.sum((pend[None, :] <= block_row0[:, None]).astype(I32), axis=1), N_EXPERTS - 1)
    first = jnp.concatenate([jnp.ones((1,), I32), (block_exp[1:] != block_exp[:-1]).astype(I32)])
    n_used = (pend[-1:] // EXPERT_ROWS).astype(I32)
    n_tiles = n // TOK_TILE
    start_row = jnp.zeros((1, 1, LANES), I32).at[0, 0, :N_EXPERTS].set(pstart)
    sel_row0 = (jnp.arange(SUBLANES) == 0).astype(I32)[None, :, None]
    tab = (tile_meta.reshape(n_tiles, SUBLANES, LANES) + sel_row0 * start_row).reshape(n_tiles * SUBLANES, LANES)
    tail0 = pend[-1] + jnp.arange(N_FILL - N_EXPERTS, dtype=I32) * EXPERT_ROWS
    fill_start = jnp.concatenate([pstart + counts, tail0])
    fill_len = jnp.concatenate([padded - counts, jnp.clip(n_rows - tail0, 0, EXPERT_ROWS)])
    fill_tab = jnp.zeros((SUBLANES, LANES), I32).at[0, :N_FILL].set(fill_start).at[1, :N_FILL].set(fill_len)
    xs = _dispatch(x2, mod4, meta, tab, fill_tab, n_rows, seq)
    e = b_gu.shape[0]
    b_grouped = b_gu.reshape(e, d // LANES, LANES, 2).transpose(0, 1, 3, 2).reshape(e, 1, 2 * d)
    ys = _experts(xs, block_exp, first, n_used, layer, w_gu, b_grouped, w_dn, b_dn[:, None, :])
    return _combine(ys, tab, meta, gates, x2, mod4, lng.reshape(1, d), lnb.reshape(1, d), seq)


def _permute_w_in(w):
    d = w.shape[0]
    off = {}
    o = 0
    for name, wd in (("a_q", 256), ("a_k", 256), ("a_v", 256), ("b_q", 384), ("b_kc", 128), ("b_vc", 128),
                     ("b_ks", 128), ("b_vs", 128), ("b_kw", 128), ("b_vw", 128), ("b_gate", 18),
                     ("c_q", 384), ("c_k", 384), ("c_v", 384), ("merge_gate", 3 * D_MODEL)):
        off[name] = (o, o + wd)
        o += wd
    col = lambda name: w[:, off[name][0]:off[name][1]]
    scale = HEAD_DIM ** -0.5 * float(np.log2(np.e))
    b_q = col("b_q").reshape(d, B_KV_GROUPS, B_REP, HEAD_DIM).transpose(0, 2, 1, 3).reshape(d, B_HEADS * HEAD_DIM)
    gate_pad = jnp.zeros((d, LANES - B_HEADS * 3), w.dtype)
    parts = [col("merge_gate"),
             col("a_q") * scale, col("a_k"), b_q * scale, col("b_kc"), col("b_ks"), col("b_kw"),
             col("c_q") * scale, col("c_k"),
             col("a_v"), col("b_vc"), col("b_vs"), col("b_vw"), col("c_v"),
             col("b_gate"), gate_pad]
    return jnp.concatenate(parts, axis=1).astype(BF16)


def _dilate(arr, tile, batch, seq, dil):
    a = arr[:, tile * LANES:(tile + 1) * LANES].reshape(batch, seq // dil, dil, LANES)
    return a.transpose(0, 2, 1, 3).reshape(batch * seq, LANES)


def _undilate(arr, batch, seq, dil):
    a = arr.reshape(batch, dil, seq // dil, LANES)
    return a.transpose(0, 2, 1, 3).reshape(batch * seq, LANES)


def _mixer_layer(x2, mod4, cos_t, sin_t, batch, seq, w_in, cmp_w1_k, cmp_w2_k, cmp_pe_k,
                 cmp_w1_v, cmp_w2_v, cmp_pe_v, w_a, w_b, w_c, w_out, lng, lnb):
    d = D_MODEL
    proj = _inproj(x2, mod4, _permute_w_in(w_in), cos_t, sin_t, seq)

    bias_a = _moba_select(proj, _kmean(proj, batch, seq), batch, seq)
    o_a = _bb_attn(proj, bias_a, batch=batch, seq=seq, n_pairs=A_HEADS // 2,
                   q_tile=lambda p: T_AQ + p, k_tile=lambda p: T_AK + p, v_tile=lambda p: T_AV + p,
                   bias_tile=lambda p: p, blk=MOBA_BLOCK)

    kc = _compress(proj[:, T_BKC * LANES:(T_BKC + 1) * LANES], cmp_w1_k, cmp_w2_k, cmp_pe_k, batch, seq)
    vc = _compress(proj[:, T_BVC * LANES:(T_BVC + 1) * LANES], cmp_w1_v, cmp_w2_v, cmp_pe_v, batch, seq)
    o_cmp, bias_b = _nsa_cmp(proj, kc, vc, batch, seq)
    o_slc = _bb_attn(proj, bias_b, batch=batch, seq=seq, n_pairs=B_REP,
                     q_tile=lambda p: T_BQ + p, k_tile=lambda p: T_BKS, v_tile=lambda p: T_BVS,
                     bias_tile=lambda p: 0, blk=SLC_BLOCK)
    o_win, _ = _band_attn(proj, proj, proj, batch=batch, seq=seq, q_tiles=[T_BQ + r for r in range(B_REP)],
                          k_tile=T_BKW, v_tile=T_BVW, window=NSA_WINDOW, t=256, group=1)

    oc, lse = [], []
    for gi, (window, dil) in enumerate(DILATED_PAIRS):
        wlen = window // dil + 1
        if dil == 1:
            o, l = _band_attn(proj, proj, proj, batch=batch, seq=seq, q_tiles=[T_CQ + gi], k_tile=T_CK + gi,
                              v_tile=T_CV + gi, window=wlen, t=256, group=4)
        else:
            qd = _dilate(proj, T_CQ + gi, batch, seq, dil)
            kd = _dilate(proj, T_CK + gi, batch, seq, dil)
            vd = _dilate(proj, T_CV + gi, batch, seq, dil)
            o, l = _band_attn(qd, kd, vd, batch=batch * dil, seq=seq // dil, q_tiles=[0], k_tile=0, v_tile=0,
                              window=wlen, t=256, group=4)
            o, l = _undilate(o, batch, seq, dil), _undilate(l, batch, seq, dil)
        oc.append(o)
        lse.append(l)

    w_b_perm = w_b.reshape(B_KV_GROUPS, B_REP, HEAD_DIM, d).transpose(1, 0, 2, 3).reshape(B_HEADS * HEAD_DIM, d)
    return _merge(o_a, o_cmp, o_slc, o_win, proj, oc, lse, x2, mod4,
                  w_a.astype(BF16), w_b_perm.astype(BF16), w_c.astype(BF16), w_out.astype(BF16),
                  lng.reshape(1, d), lnb.reshape(1, d), seq)


def _rope_tables(positions):
    inv = ROPE_THETA ** (-jnp.arange(0, HEAD_DIM, 2, dtype=F32) / HEAD_DIM)
    ang = positions.astype(F32).reshape(-1, 1) * inv[None, :]
    cos, sin = jnp.cos(ang), jnp.sin(ang)
    return jnp.tile(cos, (1, 4)), jnp.tile(jnp.concatenate([-sin, sin], axis=1), (1, 2))


def kernel(x, c, positions, w_ada, b_ada, w_in, cmp_w1_k, cmp_w2_k, cmp_pe_k, cmp_w1_v, cmp_w2_v, cmp_pe_v, w_branch_a, w_branch_b, w_branch_c, w_out, ln1_g, ln1_b, router_w, router_b, w_gate_up, b_gate_up, w_down, b_down, ln2_g, ln2_b):
    batch, seq, d = x.shape
    depth = w_in.shape[0]
    cos_t, sin_t = _rope_tables(positions)
    mod = _ada(c, w_ada, b_ada)
    x2 = x.reshape(batch * seq, d)
    for l in range(depth):
        mod4 = mod[l].reshape(batch, N_ADA, 1, d)
        x2 = _mixer_layer(x2, mod4, cos_t, sin_t, batch, seq, w_in[l], cmp_w1_k[l], cmp_w2_k[l], cmp_pe_k[l],
                          cmp_w1_v[l], cmp_w2_v[l], cmp_pe_v[l], w_branch_a[l], w_branch_b[l],
                          w_branch_c[l], w_out[l], ln1_g[l], ln1_b[l])
        x2 = _moe(x2, mod4, l, router_w[l], router_b[l], w_gate_up, b_gate_up[l], w_down, b_down[l],
                  ln2_g[l], ln2_b[l], seq)
    return x2.reshape(batch, seq, d)
```

```python
import functools

import numpy as np
import jax
import jax.numpy as jnp
from jax import lax
from jax.experimental import pallas as pl
from jax.experimental.pallas import tpu as pltpu

F32 = jnp.float32
BF16 = jnp.bfloat16
I32 = jnp.int32
HIGHEST = lax.Precision.HIGHEST

D_MODEL = 1024
DEPTH = 2
HEAD_DIM = 64
ROPE_THETA = 10000.0
LN_EPS = 1e-5
DEEPNORM_ALPHA = (2 * DEPTH) ** 0.25
N_ADA = 6
A_HEADS = 4
MOBA_BLOCK = 256
MOBA_TOPK = 3
B_HEADS = 6
B_KV_GROUPS = 2
B_REP = 3
CMP_BLOCK = 32
CMP_STRIDE = 16
CMP_HIDDEN = 128
SLC_BLOCK = 64
SLC_TOPK = 16
NSA_WINDOW = 512
FORCE_SCORE = 1e6
DILATED_PAIRS = ((128, 1), (512, 4), (2048, 16))
N_EXPERTS = 32
TOP_K = 4
SWIGLU_ALPHA = 1.702
SWIGLU_LIMIT = 7.0

LANES = 128
SUBLANES = 8
HALF = LANES // 2
NEG = -1e30
ROW_CHUNKS = D_MODEL // LANES
VMEM_LIMIT = 56 * 1024 * 1024

T_MG = 0
T_ROPE0, T_ROPE1 = 24, 40
T_AQ, T_AK, T_BQ, T_BKC, T_BKS, T_BKW, T_CQ, T_CK = 24, 26, 28, 31, 32, 33, 34, 37
T_AV, T_BVC, T_BVS, T_BVW, T_CV, T_BG = 40, 42, 43, 44, 45, 48
N_PROJ_TILES = 49
PROJ_W = N_PROJ_TILES * LANES

EXPERT_ROWS = 512
TOK_TILE = 256
N_FILL = 2 * N_EXPERTS + 2


def _cparams(sem, **kw):
    return pltpu.CompilerParams(dimension_semantics=sem, vmem_limit_bytes=VMEM_LIMIT, **kw)


def _lane_iota(shape=(1, LANES)):
    return lax.broadcasted_iota(I32, shape, len(shape) - 1)


def _sigmoid(x):
    return 1.0 / (1.0 + jnp.exp(-x))


def _sigmoid_tanh(x):
    return 0.5 * jnp.tanh(0.5 * x) + 0.5


def _ln(x):
    mu = jnp.mean(x, axis=-1, keepdims=True)
    xc = x - mu
    var = jnp.mean(xc * xc, axis=-1, keepdims=True)
    return xc * lax.rsqrt(var + LN_EPS)


def _dot_nt(a, b, precision=None):
    return lax.dot_general(a, b, (((1,), (1,)), ((), ())), precision=precision,
                           preferred_element_type=F32)


def _ada_kernel(c_ref, w_ref, b_ref, o_ref):
    c = c_ref[...]
    cond = c * _sigmoid(c)
    o_ref[0] = jnp.dot(cond, w_ref[0], precision=HIGHEST, preferred_element_type=F32) + b_ref[0]


def _ada(c, w_ada, b_ada):
    depth, d, n = w_ada.shape
    b = c.shape[0]
    tn = 1536
    return pl.pallas_call(
        _ada_kernel,
        grid=(depth, n // tn),
        in_specs=[pl.BlockSpec((b, d), lambda l, j: (0, 0)),
                  pl.BlockSpec((1, d, tn), lambda l, j: (l, 0, j)),
                  pl.BlockSpec((1, 1, tn), lambda l, j: (l, 0, j))],
        out_specs=pl.BlockSpec((1, b, tn), lambda l, j: (l, 0, j)),
        out_shape=jax.ShapeDtypeStruct((depth, b, n), F32),
        compiler_params=_cparams(("parallel", "parallel")),
        name="ada",
    )(c, w_ada, b_ada.reshape(depth, 1, n))


def _inproj_kernel(x_ref, sc_ref, sh_ref, w_ref, pos_ref, inv_ref, o_ref, *, chunks):
    h = _ln(x_ref[...]) * (1.0 + sc_ref[0, 0]) + sh_ref[0, 0]
    hb = h.astype(BF16)
    first_half = (_lane_iota() & (HEAD_DIM - 1)) < (HEAD_DIM // 2)
    ang = pos_ref[...].astype(F32) * inv_ref[...]
    cos = jnp.cos(ang)
    sin = jnp.where(first_half, -jnp.sin(ang), jnp.sin(ang))
    for c0, cw, rope in chunks:
        acc = jnp.dot(hb, w_ref[:, c0:c0 + cw], preferred_element_type=F32)
        if rope:
            for t in range(cw // LANES):
                a = acc[:, t * LANES:(t + 1) * LANES]
                rot = jnp.where(first_half, pltpu.roll(a, LANES - HEAD_DIM // 2, 1),
                                pltpu.roll(a, HEAD_DIM // 2, 1))
                o_ref[:, c0 + t * LANES:c0 + (t + 1) * LANES] = (a * cos + rot * sin).astype(o_ref.dtype)
        else:
            o_ref[:, c0:c0 + cw] = acc.astype(o_ref.dtype)


def _inproj_chunks():
    chunks = []
    for lo, hi, rope in ((0, T_ROPE0, False), (T_ROPE0, T_ROPE1, True), (T_ROPE1, N_PROJ_TILES, False)):
        c = lo * LANES
        while c < hi * LANES:
            cw = min(512, hi * LANES - c)
            chunks.append((c, cw, rope))
            c += cw
    return tuple(chunks)


def _inproj(x2, mod4, w_perm, pos2, inv_t, seq):
    n, d = x2.shape
    tm = 256
    per_b = seq // tm
    return pl.pallas_call(
        functools.partial(_inproj_kernel, chunks=_inproj_chunks()),
        grid=(n // tm,),
        in_specs=[pl.BlockSpec((tm, d), lambda i: (i, 0)),
                  pl.BlockSpec((1, 1, 1, d), lambda i: (i // per_b, 1, 0, 0)),
                  pl.BlockSpec((1, 1, 1, d), lambda i: (i // per_b, 0, 0, 0)),
                  pl.BlockSpec((d, PROJ_W), lambda i: (0, 0)),
                  pl.BlockSpec((tm, 1), lambda i: (i, 0)),
                  pl.BlockSpec((1, LANES), lambda i: (0, 0))],
        out_specs=pl.BlockSpec((tm, PROJ_W), lambda i: (i, 0)),
        out_shape=jax.ShapeDtypeStruct((n, PROJ_W), BF16),
        compiler_params=_cparams(("parallel",)),
        name="inproj",
    )(x2, mod4, mod4, w_perm, pos2, inv_t)


def _bb_attn_kernel(q_ref, k_ref, v_ref, b_ref, o_ref, qa_sc, s_sc, mp_sc, m_sc, acc_sc,
                    *, blk_shift, tq, tu, big):
    i = pl.program_id(2)
    n_diag = tq // tu
    lane = _lane_iota()
    lo = lane < HALF
    q = q_ref[...]
    b = b_ref[...]
    qa_sc[0] = jnp.where(lo, q, b)
    qa_sc[1] = jnp.where(lo, b, q)
    mp_sc[...] = jnp.full(mp_sc.shape, -jnp.inf, F32)
    acc_sc[...] = jnp.zeros(acc_sc.shape, F32)

    def scores(u0, n_units, diagonal_last):
        tk = n_units * tu
        start = pl.multiple_of(u0 * tu, tu)
        k = k_ref[pl.ds(start, tk), :]
        kblk = (start + lax.broadcasted_iota(I32, (tk, 1), 0)) >> blk_shift
        onehot = jnp.where(kblk == (lane & (HALF - 1)), 1.0, 0.0).astype(BF16)
        ka = (jnp.where(lo, k, onehot), jnp.where(lo, onehot, k))
        row = lax.broadcasted_iota(I32, (tq, 1), 0)
        col = lax.broadcasted_iota(I32, (1, tu), 1)
        for h in range(2):
            s = _dot_nt(qa_sc[h], ka[h])
            mp = mp_sc[h]
            for u in range(n_units):
                su = s[:, u * tu:(u + 1) * tu]
                d = u - (n_units - n_diag)
                if diagonal_last and d >= 0:
                    su = jnp.where(col + d * tu <= row, su, NEG)
                s_sc[h, u0 + u] = su
                for c in range(tu // LANES):
                    mp = jnp.maximum(mp, su[:, c * LANES:(c + 1) * LANES])
            mp_sc[h] = mp

    def weighted(u0, n_units):
        tk = n_units * tu
        start = pl.multiple_of(u0 * tu, tu)
        v = v_ref[pl.ds(start, tk), :]
        one = jnp.ones_like(v)
        vs = (jnp.where(lo, v, one), jnp.where(lo, one, v))
        for h in range(2):
            m_row = m_sc[h]
            ps = []
            for u in range(n_units):
                su = s_sc[h, u0 + u]
                for c in range(tu // LANES):
                    ps.append(jnp.exp2(su[:, c * LANES:(c + 1) * LANES] - m_row).astype(BF16))
            acc_sc[h] = acc_sc[h] + jnp.dot(jnp.concatenate(ps, axis=1), vs[h], preferred_element_type=F32)

    n_full = i * n_diag
    n_big = n_full // big
    n_tail = n_full - n_big * big + n_diag

    def run(fn):
        def big_body(t, c):
            fn(t * big, big, False) if fn is scores else fn(t * big, big)
            return c

        lax.fori_loop(0, n_big, big_body, 0)
        for n in range(n_diag, big + n_diag, n_diag):
            @pl.when(n_tail == n)
            def _(n=n):
                fn(n_big * big, n, True) if fn is scores else fn(n_big * big, n)

    run(scores)
    for h in range(2):
        m_sc[h] = jnp.broadcast_to(jnp.max(mp_sc[h], axis=1, keepdims=True), (tq, LANES))
    run(weighted)
    outs = [acc_sc[h] / pltpu.roll(acc_sc[h], HALF, 1) for h in range(2)]
    o_ref[...] = jnp.where(lo, outs[0], outs[1]).astype(o_ref.dtype)


def _bb_attn(proj, bias, *, batch, seq, n_pairs, q_tile, k_tile, v_tile, bias_tile, blk):
    t, tu = 512, 256
    nq = seq // t
    n = batch * seq
    kern = functools.partial(_bb_attn_kernel, blk_shift=int(np.log2(blk)), tq=t, tu=tu, big=4)
    return pl.pallas_call(
        kern,
        grid=(batch, n_pairs, nq),
        in_specs=[pl.BlockSpec((t, LANES), lambda b, p, i: (b * nq + i, q_tile(p))),
                  pl.BlockSpec((seq, LANES), lambda b, p, i: (b, k_tile(p))),
                  pl.BlockSpec((seq, LANES), lambda b, p, i: (b, v_tile(p))),
                  pl.BlockSpec((t, LANES), lambda b, p, i: (b * nq + i, bias_tile(p)))],
        out_specs=pl.BlockSpec((t, LANES), lambda b, p, i: (b * nq + i, p)),
        out_shape=jax.ShapeDtypeStruct((n, n_pairs * LANES), BF16),
        scratch_shapes=[pltpu.VMEM((2, t, LANES), BF16),
                        pltpu.VMEM((2, seq // tu, t, tu), F32),
                        pltpu.VMEM((2, t, LANES), F32),
                        pltpu.VMEM((2, t, LANES), F32),
                        pltpu.VMEM((2, t, LANES), F32)],
        compiler_params=_cparams(("parallel", "parallel", "parallel")),
        name="bb_attn_%d" % blk,
    )(proj, proj, proj, bias)


def _band_kernel(*refs, tq, kw, window, seq, n_q, group, tiles_per_seq):
    q_refs = refs[:n_q]
    k_ref, v_ref, o_ref, lse_ref = refs[n_q:n_q + 4]
    i = pl.program_id(1)
    lane = _lane_iota()
    lo = lane < HALF
    for g in range(group):
        if tiles_per_seq == 1:
            tile, base = 0, g * seq
        else:
            tile, base = i * group + g, 0
        start = jnp.clip(tile * tq - (kw - tq), 0, seq - kw)
        k = k_ref[pl.ds(pl.multiple_of(base + start, LANES), kw), :]
        v = v_ref[pl.ds(pl.multiple_of(base + start, LANES), kw), :]
        rel = (tile * tq + lax.broadcasted_iota(I32, (tq, 1), 0)) - (start + lax.broadcasted_iota(I32, (1, kw), 1))
        ok = (rel >= 0) & (rel < window)
        for qi in range(n_q):
            q = q_refs[qi][g * tq:(g + 1) * tq, :]
            zero = jnp.zeros_like(q)
            qs = (jnp.where(lo, q, zero), jnp.where(lo, zero, q))
            outs, lses = [], []
            for h in range(2):
                s = jnp.where(ok, _dot_nt(qs[h], k), NEG)
                m = jnp.max(s, axis=1, keepdims=True)
                p = jnp.exp2(s - m)
                l = jnp.sum(p, axis=1, keepdims=True)
                outs.append(jnp.dot(p.astype(BF16), v, preferred_element_type=F32) / l)
                lses.append(m + jnp.log2(l))
            rows, cols = slice(g * tq, (g + 1) * tq), slice(qi * LANES, (qi + 1) * LANES)
            o_ref[rows, cols] = jnp.where(lo, outs[0], outs[1]).astype(o_ref.dtype)
            lse_ref[rows, cols] = jnp.where(lo, lses[0], lses[1])


def _band_attn(q_arr, k_arr, v_arr, *, batch, seq, q_tiles, k_tile, v_tile, window, t, group):
    t = min(t, seq)
    tiles_per_seq = seq // t
    n = batch * seq
    n_q = len(q_tiles)
    if tiles_per_seq == 1:
        group = min(group, batch)
        grid = (batch // group, 1)
        kv_rows = group * seq
        row_blk = lambda b, i: b
    else:
        group = min(group, tiles_per_seq)
        grid = (batch, tiles_per_seq // group)
        kv_rows = seq
        row_blk = lambda b, i: b * (tiles_per_seq // group) + i
    kw = min(seq, t + -(-(window - 1) // LANES) * LANES)
    kern = functools.partial(_band_kernel, tq=t, kw=kw, window=window, seq=seq, n_q=n_q, group=group,
                             tiles_per_seq=tiles_per_seq)
    qspecs = [pl.BlockSpec((group * t, LANES), (lambda b, i, c=c: (row_blk(b, i), c))) for c in q_tiles]
    return pl.pallas_call(
        kern,
        grid=grid,
        in_specs=qspecs + [pl.BlockSpec((kv_rows, LANES), lambda b, i: (b, k_tile)),
                           pl.BlockSpec((kv_rows, LANES), lambda b, i: (b, v_tile))],
        out_specs=[pl.BlockSpec((group * t, n_q * LANES), lambda b, i: (row_blk(b, i), 0)),
                   pl.BlockSpec((group * t, n_q * LANES), lambda b, i: (row_blk(b, i), 0))],
        out_shape=[jax.ShapeDtypeStruct((n, n_q * LANES), BF16),
                   jax.ShapeDtypeStruct((n, n_q * LANES), F32)],
        compiler_params=_cparams(("parallel", "parallel")),
        name="band_attn_%d" % window,
    )(*([q_arr] * n_q), k_arr, v_arr)


def _kmean_kernel(k_ref, o_ref, *, nb):
    s = k_ref.shape[0]
    blk = lax.broadcasted_iota(I32, (nb, s), 1) >> int(np.log2(MOBA_BLOCK))
    avg = jnp.where(blk == lax.broadcasted_iota(I32, (nb, s), 0), 1.0 / MOBA_BLOCK, 0.0).astype(BF16)
    o_ref[0] = jnp.dot(avg, k_ref[...], preferred_element_type=F32)


def _kmean(proj, batch, seq):
    nb = seq // MOBA_BLOCK
    w = A_HEADS * HEAD_DIM
    return pl.pallas_call(
        functools.partial(_kmean_kernel, nb=nb),
        grid=(batch,),
        in_specs=[pl.BlockSpec((seq, w), lambda b: (b, T_AK * LANES // w))],
        out_specs=pl.BlockSpec((1, nb, w), lambda b: (b, 0, 0)),
        out_shape=jax.ShapeDtypeStruct((batch, nb, w), F32),
        compiler_params=_cparams(("parallel",)),
        name="moba_kmean",
    )(proj)


def _rank_desc(g, n):
    gs = min(SUBLANES, n)
    groups = [g[r0:r0 + gs, :] for r0 in range(0, n, gs)]
    ranks = [jnp.zeros(x.shape, I32) for x in groups]
    sub = lax.broadcasted_iota(I32, (gs, 1), 0)
    for m in range(n):
        c = g[m:m + 1, :]
        for gi, x in enumerate(groups):
            if gi * gs > m:
                beats = c >= x
            elif gi * gs + gs - 1 <= m:
                beats = c > x
            else:
                beats = (c > x) | ((c == x) & (sub + gi * gs > m))
            ranks[gi] = ranks[gi] + jnp.where(beats, 1, 0)
    return jnp.concatenate(ranks, axis=0)


def _moba_sel_kernel(*refs, nb, n_sel, n_pairs):
    q_refs = refs[:n_pairs]
    km_ref, b_ref = refs[n_pairs:]
    i = pl.program_id(1)
    t = b_ref.shape[0]
    lo = _lane_iota() < HALF
    pad = jnp.zeros((HALF - nb, LANES), F32)
    n_idx = lax.broadcasted_iota(I32, (nb, 1), 0)
    valid = n_idx < i
    for p in range(n_pairs):
        q = q_refs[p][...].astype(F32)
        km = km_ref[0, :, p * LANES:(p + 1) * LANES]
        zero = jnp.zeros_like(km)
        kmt = jnp.concatenate([jnp.where(lo, zero, km), pad, jnp.where(lo, km, zero), pad], axis=0)
        gt = _dot_nt(kmt, q, precision=HIGHEST)
        rows = []
        for r0 in (0, HALF):
            g = jnp.where(valid, gt[r0:r0 + nb, :], -jnp.inf)
            rank = _rank_desc(g, nb)
            allowed = (valid & (rank < n_sel)) | (n_idx == i)
            rows.append(jnp.where(allowed, 0.0, NEG))
            rows.append(jnp.zeros((HALF - nb, t), F32))
        b_ref[:, p * LANES:(p + 1) * LANES] = jnp.concatenate(rows, axis=0).T.astype(b_ref.dtype)


def _moba_select(proj, kmean, batch, seq):
    t = MOBA_BLOCK
    nq = seq // t
    nb = seq // MOBA_BLOCK
    n_sel = min(MOBA_TOPK, nb - 1)
    n_pairs = A_HEADS // 2
    qspecs = [pl.BlockSpec((t, LANES), (lambda b, i, p=p: (b * nq + i, T_AQ + p))) for p in range(n_pairs)]
    return pl.pallas_call(
        functools.partial(_moba_sel_kernel, nb=nb, n_sel=n_sel, n_pairs=n_pairs),
        grid=(batch, nq),
        in_specs=qspecs + [pl.BlockSpec((1, nb, n_pairs * LANES), lambda b, i: (b, 0, 0))],
        out_specs=pl.BlockSpec((t, n_pairs * LANES), lambda b, i: (b * nq + i, 0)),
        out_shape=jax.ShapeDtypeStruct((batch * seq, n_pairs * LANES), BF16),
        compiler_params=_cparams(("parallel", "parallel")),
        name="moba_select",
    )(*([proj] * n_pairs), kmean)


def _compress_kernel(x_ref, w1_ref, pe_ref, w1f_ref, w2_ref, o_ref):
    x = x_ref[0]
    nblk = x.shape[0]
    outs = []
    for g in range(B_KV_GROUPS):
        u = jnp.dot(x, w1_ref[g, 0], preferred_element_type=F32)
        v = jnp.dot(x, w1_ref[g, 1], preferred_element_type=F32)
        pe_h = jnp.dot(pe_ref[...], w1f_ref[...], precision=HIGHEST, preferred_element_type=F32)
        hid = u + pltpu.roll(v, nblk - 1, 0) + pe_h[:1]
        hid = hid * _sigmoid(hid)
        outs.append(jnp.dot(hid, w2_ref[g], precision=HIGHEST, preferred_element_type=F32))
    o = outs[0] + outs[1]
    rows = lax.broadcasted_iota(I32, (nblk, 1), 0)
    o_ref[0] = jnp.where(rows < nblk - 1, o, 0.0).astype(o_ref.dtype)


def _compress(xt, w1, w2, pe, batch, seq):
    nblk = seq // CMP_STRIDE
    xg = xt.reshape(batch, nblk, CMP_STRIDE * LANES)
    w1r = w1.reshape(2, CMP_STRIDE, HEAD_DIM, CMP_HIDDEN)
    w1e = jnp.zeros((B_KV_GROUPS, 2, CMP_STRIDE, B_KV_GROUPS, HEAD_DIM, CMP_HIDDEN), F32)
    for g in range(B_KV_GROUPS):
        w1e = w1e.at[g, :, :, g].set(w1r)
    w1e = w1e.reshape(B_KV_GROUPS, 2, CMP_STRIDE * LANES, CMP_HIDDEN).astype(BF16)
    pe_flat = jnp.broadcast_to(pe.reshape(1, CMP_BLOCK * HEAD_DIM), (SUBLANES, CMP_BLOCK * HEAD_DIM))
    w2e = jnp.zeros((B_KV_GROUPS, CMP_HIDDEN, LANES), F32)
    for g in range(B_KV_GROUPS):
        w2e = w2e.at[g, :, g * HEAD_DIM:(g + 1) * HEAD_DIM].set(w2)
    return pl.pallas_call(
        _compress_kernel,
        grid=(batch,),
        in_specs=[pl.BlockSpec((1, nblk, CMP_STRIDE * LANES), lambda b: (b, 0, 0)),
                  pl.BlockSpec(w1e.shape, lambda b: (0, 0, 0, 0)),
                  pl.BlockSpec(pe_flat.shape, lambda b: (0, 0)),
                  pl.BlockSpec(w1.shape, lambda b: (0, 0)),
                  pl.BlockSpec(w2e.shape, lambda b: (0, 0, 0))],
        out_specs=pl.BlockSpec((1, nblk, LANES), lambda b: (b, 0, 0)),
        out_shape=jax.ShapeDtypeStruct((batch, nblk, LANES), BF16),
        compiler_params=_cparams(("parallel",)),
        name="nsa_compress",
    )(xg, w1e, pe_flat, w1, w2e)


def _nsa_cmp_kernel(q0_ref, q1_ref, q2_ref, kc_ref, vc_ref, ov_ref, o_ref, b_ref, *, n_sel, ns):
    i = pl.program_id(1)
    t = q0_ref.shape[0]
    ncp = kc_ref.shape[1]
    lane = _lane_iota()
    lo = lane < HALF
    kc = kc_ref[0]
    vc = vc_ref[0]
    tq = i * t + lax.broadcasted_iota(I32, (t, 1), 0)
    cmp_end = lax.broadcasted_iota(I32, (1, ncp), 1) * CMP_STRIDE + (CMP_BLOCK - 1)
    vis = cmp_end <= tq
    psum = [jnp.zeros((t, ncp), F32) for _ in range(B_KV_GROUPS)]
    for r, q_ref in enumerate((q0_ref, q1_ref, q2_ref)):
        q = q_ref[...]
        zero = jnp.zeros_like(q)
        outs = []
        for g in range(B_KV_GROUPS):
            qg = jnp.where(lo, q, zero) if g == 0 else jnp.where(lo, zero, q)
            s = jnp.where(vis, _dot_nt(qg, kc), NEG)
            m = jnp.max(s, axis=1, keepdims=True)
            m = jnp.where(m > 0.5 * NEG, m, 0.0)
            e = jnp.exp2(s - m)
            p = e / jnp.maximum(jnp.sum(e, axis=1, keepdims=True), 1e-30)
            psum[g] = psum[g] + p
            outs.append(jnp.dot(p.astype(BF16), vc, preferred_element_type=F32))
        o_ref[:, r * LANES:(r + 1) * LANES] = jnp.where(lo, outs[0], outs[1]).astype(o_ref.dtype)
    n_idx = lax.broadcasted_iota(I32, (ns, 1), 0)
    tcol = i * t + lax.broadcasted_iota(I32, (1, t), 1)
    qblk = tcol >> int(np.log2(SLC_BLOCK))
    forced = (n_idx == 0) | (n_idx == qblk) | (n_idx == qblk - 1)
    valid = n_idx <= qblk
    rows = []
    for g in (1, 0):
        imp = _dot_nt(ov_ref[...], psum[g], precision=HIGHEST)
        imp = jnp.where(forced, FORCE_SCORE, imp)
        imp = jnp.where(valid, imp, -jnp.inf)
        rank = _rank_desc(imp, ns)
        allowed = valid & (rank < n_sel)
        rows.append(jnp.where(allowed, 0.0, NEG))
        if ns < HALF:
            rows.append(jnp.zeros((HALF - ns, t), F32))
    b_ref[...] = jnp.concatenate(rows, axis=0).T.astype(b_ref.dtype)


def _nsa_cmp(proj, kc, vc, batch, seq):
    t = 256
    nq = seq // t
    ns = seq // SLC_BLOCK
    n_sel = min(SLC_TOPK, ns)
    ncp = seq // CMP_STRIDE
    cs = np.arange(ncp)[None, :] * CMP_STRIDE
    ss = np.arange(ns)[:, None] * SLC_BLOCK
    ov = ((cs < ss + SLC_BLOCK) & (cs + CMP_BLOCK > ss)).astype(np.float32)
    ov[:, ncp - 1] = 0.0
    n = batch * seq
    qspec = [pl.BlockSpec((t, LANES), (lambda b, i, r=r: (b * nq + i, T_BQ + r))) for r in range(B_REP)]
    return pl.pallas_call(
        functools.partial(_nsa_cmp_kernel, n_sel=n_sel, ns=ns),
        grid=(batch, nq),
        in_specs=qspec + [pl.BlockSpec((1, ncp, LANES), lambda b, i: (b, 0, 0)),
                          pl.BlockSpec((1, ncp, LANES), lambda b, i: (b, 0, 0)),
                          pl.BlockSpec((ns, ncp), lambda b, i: (0, 0))],
        out_specs=[pl.BlockSpec((t, B_REP * LANES), lambda b, i: (b * nq + i, 0)),
                   pl.BlockSpec((t, LANES), lambda b, i: (b * nq + i, 0))],
        out_shape=[jax.ShapeDtypeStruct((n, B_REP * LANES), BF16),
                   jax.ShapeDtypeStruct((n, LANES), BF16)],
        compiler_params=_cparams(("parallel", "parallel")),
        name="nsa_cmp_select",
    )(proj, proj, proj, kc, vc, jnp.asarray(ov))


def _merge_kernel(oa_ref, ocmp_ref, oslc_ref, owin_ref, bg_ref, oc0_ref, oc1_ref, oc2_ref,
                  l0_ref, l1_ref, l2_ref, mg_ref, x_ref, g1_ref, wa_ref, wb_ref, wc_ref, wo_ref,
                  eg_ref, lng_ref, lnb_ref, o_ref):
    d = D_MODEL
    sg = _sigmoid(bg_ref[...].astype(F32))
    sg_hi = sg.astype(BF16)
    sg_lo = (sg - sg_hi.astype(F32)).astype(BF16)
    ob = jnp.zeros(ocmp_ref.shape, F32)
    for br, ref in enumerate((ocmp_ref, oslc_ref, owin_ref)):
        gexp = (jnp.dot(sg_hi, eg_ref[br], preferred_element_type=F32)
                + jnp.dot(sg_lo, eg_ref[br], preferred_element_type=F32))
        ob = ob + gexp * ref[...].astype(F32)
    l0, l1, l2 = l0_ref[...], l1_ref[...], l2_ref[...]
    mx = jnp.maximum(jnp.maximum(l0, l1), l2)
    e0, e1, e2 = jnp.exp2(l0 - mx), jnp.exp2(l1 - mx), jnp.exp2(l2 - mx)
    den = e0 + e1 + e2
    oc = ((e0 / den) * oc0_ref[...].astype(F32) + (e1 / den) * oc1_ref[...].astype(F32)
          + (e2 / den) * oc2_ref[...].astype(F32))
    pa = jnp.dot(oa_ref[...], wa_ref[...], preferred_element_type=F32)
    pb = jnp.dot(ob.astype(BF16), wb_ref[...], preferred_element_type=F32)
    pc = jnp.dot(oc.astype(BF16), wc_ref[...], preferred_element_type=F32)
    merged = (_sigmoid_tanh(mg_ref[:, 0:d].astype(F32)) * pa
              + _sigmoid_tanh(mg_ref[:, d:2 * d].astype(F32)) * pb
              + _sigmoid_tanh(mg_ref[:, 2 * d:3 * d].astype(F32)) * pc)
    y = jnp.dot(merged.astype(BF16), wo_ref[...], preferred_element_type=F32)
    z = DEEPNORM_ALPHA * x_ref[...] + g1_ref[0, 0] * y
    o_ref[...] = _ln(z) * lng_ref[...] + lnb_ref[...]


def _gate_expand():
    eg = np.zeros((3, LANES, B_HEADS * HEAD_DIM), np.float32)
    for g in range(B_KV_GROUPS):
        for r in range(B_REP):
            for br in range(3):
                c0 = (r * B_KV_GROUPS + g) * HEAD_DIM
                eg[br, (g * B_REP + r) * 3 + br, c0:c0 + HEAD_DIM] = 1.0
    return jnp.asarray(eg)


def _merge(o_a, o_cmp, o_slc, o_win, proj, oc, lse, x2, mod4, wa, wb, wc, wo, lng, lnb, seq):
    n, d = x2.shape
    tm = 256
    per_b = seq // tm
    row = lambda w: pl.BlockSpec((tm, w), lambda i: (i, 0))
    full = lambda a: pl.BlockSpec(a.shape, lambda i: (0,) * a.ndim)
    eg = _gate_expand().astype(BF16)
    return pl.pallas_call(
        _merge_kernel,
        grid=(n // tm,),
        in_specs=[row(o_a.shape[1]), row(o_cmp.shape[1]), row(o_slc.shape[1]), row(o_win.shape[1]),
                  pl.BlockSpec((tm, LANES), lambda i: (i, T_BG)),
                  row(LANES), row(LANES), row(LANES), row(LANES), row(LANES), row(LANES),
                  pl.BlockSpec((tm, 3 * d), lambda i: (i, T_MG)),
                  row(d),
                  pl.BlockSpec((1, 1, 1, d), lambda i: (i // per_b, 2, 0, 0)),
                  full(wa), full(wb), full(wc), full(wo), full(eg), full(lng), full(lnb)],
        out_specs=row(d),
        out_shape=jax.ShapeDtypeStruct((n, d), F32),
        compiler_params=_cparams(("parallel",)),
        name="merge_out",
    )(o_a, o_cmp, o_slc, o_win, proj, oc[0], oc[1], oc[2], lse[0], lse[1], lse[2],
      proj, x2, mod4, wa, wb, wc, wo, eg, lng, lnb)


def _router_kernel(x_ref, sc_ref, sh_ref, rw_ref, rb_ref, meta_ref, gate_ref, tile_ref, cnt_ref, carry_sc):
    tm = x_ref.shape[0]
    ne = N_EXPERTS

    @pl.when(pl.program_id(0) == 0)
    def _():
        carry_sc[...] = jnp.zeros(carry_sc.shape, F32)

    h = _ln(x_ref[...]) * (1.0 + sc_ref[0, 0]) + sh_ref[0, 0]
    lg = _dot_nt(rw_ref[...], h, precision=HIGHEST)[:ne] + rb_ref[...][:ne, :1]
    e_idx = lax.broadcasted_iota(I32, (ne, 1), 0).astype(F32)
    onehots, vals, idxs = [], [], []
    for _ in range(TOP_K):
        m = jnp.max(lg, axis=0, keepdims=True)
        idx = jnp.min(jnp.where(lg == m, e_idx, float(ne)), axis=0, keepdims=True)
        oh = e_idx == idx
        onehots.append(oh)
        vals.append(m)
        idxs.append(idx)
        lg = jnp.where(oh, -jnp.inf, lg)
    es = [jnp.exp(v - vals[0]) for v in vals]
    den = es[0] + es[1] + es[2] + es[3]
    cnt = jnp.zeros((ne, tm), F32)
    for oh in onehots:
        cnt = cnt + jnp.where(oh, 1.0, 0.0)
    before = jnp.where(lax.broadcasted_iota(I32, (tm, tm), 0) < lax.broadcasted_iota(I32, (tm, tm), 1), 1.0, 0.0)
    earlier = jnp.dot(cnt.astype(BF16), before.astype(BF16), preferred_element_type=F32)
    tile_cnt = jnp.broadcast_to(jnp.sum(cnt, axis=1, keepdims=True), (ne, LANES))
    lower = jnp.where(lax.broadcasted_iota(I32, (ne, ne), 0) > lax.broadcasted_iota(I32, (ne, ne), 1), 1.0, 0.0)
    run_off = jnp.dot(lower.astype(BF16), tile_cnt.astype(BF16), preferred_element_type=F32)
    pos = earlier + run_off[:, :1]
    locals_ = [jnp.sum(jnp.where(oh, pos, 0.0), axis=0, keepdims=True) for oh in onehots]

    def token_major(rows):
        stack = jnp.concatenate(rows + [jnp.zeros((LANES - len(rows), tm), F32)], axis=0)
        return stack.T

    meta_ref[...] = token_major(idxs + locals_).astype(I32)
    gate_ref[...] = token_major([e / den for e in es])
    lane = _lane_iota()
    cols = jnp.where(lane == 0, carry_sc[...], jnp.where(lane == 1, tile_cnt, jnp.where(lane == 2, run_off, 0.0)))
    table = jnp.concatenate([cols, jnp.zeros((LANES - ne, LANES), F32)], axis=0).T
    tile_ref[...] = table[:SUBLANES].astype(I32)
    carry_sc[...] = carry_sc[...] + tile_cnt
    total = jnp.concatenate([carry_sc[...], jnp.zeros((LANES - ne, LANES), F32)], axis=0).T
    cnt_ref[...] = total[:SUBLANES].astype(I32)


def _router(x2, mod4, rw, rb, seq):
    n, d = x2.shape
    tm = TOK_TILE
    per_b = seq // tm
    rw_p = jnp.zeros((LANES, d), F32).at[:N_EXPERTS].set(rw.T)
    rb_p = jnp.zeros((LANES, LANES), F32).at[:N_EXPERTS].set(rb[:, None])
    return pl.pallas_call(
        _router_kernel,
        grid=(n // tm,),
        in_specs=[pl.BlockSpec((tm, d), lambda i: (i, 0)),
                  pl.BlockSpec((1, 1, 1, d), lambda i: (i // per_b, 4, 0, 0)),
                  pl.BlockSpec((1, 1, 1, d), lambda i: (i // per_b, 3, 0, 0)),
                  pl.BlockSpec((LANES, d), lambda i: (0, 0)),
                  pl.BlockSpec((LANES, LANES), lambda i: (0, 0))],
        out_specs=[pl.BlockSpec((tm, LANES), lambda i: (i, 0)),
                   pl.BlockSpec((tm, LANES), lambda i: (i, 0)),
                   pl.BlockSpec((SUBLANES, LANES), lambda i: (i, 0)),
                   pl.BlockSpec((SUBLANES, LANES), lambda i: (0, 0))],
        out_shape=[jax.ShapeDtypeStruct((n, LANES), I32),
                   jax.ShapeDtypeStruct((n, LANES), F32),
                   jax.ShapeDtypeStruct((n // tm * SUBLANES, LANES), I32),
                   jax.ShapeDtypeStruct((SUBLANES, LANES), I32)],
        scratch_shapes=[pltpu.VMEM((N_EXPERTS, LANES), F32)],
        compiler_params=_cparams(("arbitrary",)),
        name="router",
    )(x2, mod4, mod4, rw_p, rb_p)


def _wait_rows(src_like, dst_like, sem, n_rows):
    span = pl.ds(0, n_rows * ROW_CHUNKS)
    pltpu.make_async_copy(src_like.at[span], dst_like.at[span], sem).wait()


def _span_rows(ref, row, n):
    return ref.at[pl.ds(pl.multiple_of(row * ROW_CHUNKS, ROW_CHUNKS), n * ROW_CHUNKS)]


def _for_each_run(tab_ref, fn):
    for e in range(N_EXPERTS):
        n = tab_ref[1, e]

        @pl.when(n > 0)
        def _(e=e, n=n):
            fn(tab_ref[0, e], tab_ref[2, e], n, e % 2)


def _dispatch_kernel(tab_ref, fill_ref, x_ref, sc_ref, sh_ref, meta_ref, xs_hbm, xb0, xb1, zb, sem, fsem):
    i = pl.program_id(0)
    n_steps = pl.num_programs(0)
    tm = x_ref.shape[0]
    nrow = tm * TOP_K
    bufs = (xb0, xb1)

    @pl.when(i == 0)
    def _():
        zb[...] = jnp.zeros(zb.shape, F32)
        for j in range(N_FILL):
            n = fill_ref[1, j]

            @pl.when(n > 0)
            def _(j=j, n=n):
                fill = pltpu.make_async_copy(_span_rows(zb, 0, n), _span_rows(xs_hbm, fill_ref[0, j], n), fsem)
                fill.start()
                fill.wait()

    h = (_ln(x_ref[...]) * (1.0 + sc_ref[0, 0]) + sh_ref[0, 0]).astype(BF16)
    pos_t = meta_ref[...].astype(F32).T
    r = lax.broadcasted_iota(I32, (nrow, 1), 0)
    sel = jnp.zeros((nrow, tm), F32)
    for k in range(TOP_K):
        sel = sel + jnp.where(r == pos_t[TOP_K + k:TOP_K + k + 1, :].astype(I32), 1.0, 0.0)
    xt = jnp.dot(sel.astype(BF16), h, preferred_element_type=F32)

    for slot in range(2):
        @pl.when(i % 2 == slot)
        def _(slot=slot):
            buf = bufs[slot]

            @pl.when(i >= 2)
            def _():
                _wait_rows(buf, xs_hbm, sem.at[slot], nrow)

            for c in range(ROW_CHUNKS):
                buf[pl.ds(c, nrow, stride=ROW_CHUNKS), :] = xt[:, c * LANES:(c + 1) * LANES]
            _for_each_run(tab_ref, lambda srow, trow, n, prio: pltpu.make_async_copy(
                _span_rows(buf, trow, n), _span_rows(xs_hbm, srow, n), sem.at[slot]).start(priority=prio))

            @pl.when(i == n_steps - 1)
            def _():
                _wait_rows(buf, xs_hbm, sem.at[slot], nrow)

                @pl.when(i >= 1)
                def _():
                    _wait_rows(bufs[1 - slot], xs_hbm, sem.at[1 - slot], nrow)


def _dispatch(x2, mod4, meta, tab, fill_tab, n_rows, seq):
    n, d = x2.shape
    tm = TOK_TILE
    per_b = seq // tm
    return pl.pallas_call(
        _dispatch_kernel,
        grid=(n // tm,),
        in_specs=[pl.BlockSpec((SUBLANES, LANES), lambda i: (i, 0), memory_space=pltpu.SMEM),
                  pl.BlockSpec((SUBLANES, LANES), lambda i: (0, 0), memory_space=pltpu.SMEM),
                  pl.BlockSpec((tm, d), lambda i: (i, 0)),
                  pl.BlockSpec((1, 1, 1, d), lambda i: (i // per_b, 4, 0, 0)),
                  pl.BlockSpec((1, 1, 1, d), lambda i: (i // per_b, 3, 0, 0)),
                  pl.BlockSpec((tm, LANES), lambda i: (i, 0))],
        out_specs=pl.BlockSpec(memory_space=pl.ANY),
        out_shape=jax.ShapeDtypeStruct((n_rows * ROW_CHUNKS, LANES), F32),
        scratch_shapes=[pltpu.VMEM((tm * TOP_K * ROW_CHUNKS, LANES), F32),
                        pltpu.VMEM((tm * TOP_K * ROW_CHUNKS, LANES), F32),
                        pltpu.VMEM((EXPERT_ROWS * ROW_CHUNKS, LANES), F32),
                        pltpu.SemaphoreType.DMA((2,)),
                        pltpu.SemaphoreType.DMA],
        compiler_params=_cparams(("arbitrary",)),
        name="moe_dispatch",
    )(tab, fill_tab, x2, mod4, mod4, meta)


def _expert_kernel(be_ref, first_ref, used_ref, xs_ref, wgu_ref, bgu_ref, wd_ref, bd_ref, ys_ref,
                   x_sc, wgu_sc, wd_sc):
    del be_ref
    g = pl.program_id(0)

    @pl.when(g < used_ref[0])
    def _():
        _expert_block(g, first_ref, xs_ref, wgu_ref, bgu_ref, wd_ref, bd_ref, ys_ref, x_sc, wgu_sc, wd_sc)

    @pl.when(g >= used_ref[0])
    def _():
        ys_ref[...] = jnp.zeros(ys_ref.shape, ys_ref.dtype)


def _expert_block(g, first_ref, xs_ref, wgu_ref, bgu_ref, wd_ref, bd_ref, ys_ref, x_sc, wgu_sc, wd_sc):
    rows = x_sc.shape[0]
    grp = 2 * LANES

    @pl.when(first_ref[g] == 1)
    def _():
        r = lax.broadcasted_iota(I32, (grp, grp), 0)
        c = lax.broadcasted_iota(I32, (grp, grp), 1)
        src = jnp.where(c < LANES, 2 * c, 2 * (c - LANES) + 1)
        sel = jnp.where(r == src, 1.0, 0.0).astype(BF16)
        for j in range(wgu_sc.shape[1] // grp):
            blk = wgu_ref[0, :, j * grp:(j + 1) * grp].astype(BF16)
            wgu_sc[:, j * grp:(j + 1) * grp] = jnp.dot(blk, sel, preferred_element_type=F32).astype(BF16)
        wd_sc[...] = wd_ref[0].astype(BF16)

    for c in range(ROW_CHUNKS):
        x_sc[:, c * LANES:(c + 1) * LANES] = xs_ref[pl.ds(c, rows, stride=ROW_CHUNKS), :].astype(BF16)
    gu = jnp.dot(x_sc[...], wgu_sc[...], preferred_element_type=F32) + bgu_ref[0]
    acts = []
    for j in range(gu.shape[1] // grp):
        glu = jnp.minimum(gu[:, j * grp:j * grp + LANES], SWIGLU_LIMIT)
        lin = jnp.clip(gu[:, j * grp + LANES:(j + 1) * grp], -SWIGLU_LIMIT, SWIGLU_LIMIT)
        acts.append((glu * _sigmoid(SWIGLU_ALPHA * glu) * (lin + 1.0)).astype(BF16))
    act = jnp.concatenate(acts, axis=1)
    y = jnp.dot(act, wd_sc[...], preferred_element_type=F32) + bd_ref[0]
    for c in range(ROW_CHUNKS):
        ys_ref[pl.ds(c, rows, stride=ROW_CHUNKS), :] = y[:, c * LANES:(c + 1) * LANES]


def _experts(xs, block_exp, first, n_used, layer, w_gu, b_gu_grouped, w_dn, b_dn):
    d = D_MODEL
    n_blocks = block_exp.shape[0]
    rb = EXPERT_ROWS
    last = lambda g, nu: jnp.minimum(g, nu[0] - 1)
    wmap = lambda g, be, fi, nu: (layer, be[last(g, nu)], 0, 0)
    bmap = lambda g, be, fi, nu: (be[last(g, nu)], 0, 0)
    return pl.pallas_call(
        _expert_kernel,
        grid_spec=pltpu.PrefetchScalarGridSpec(
            num_scalar_prefetch=3,
            grid=(n_blocks,),
            in_specs=[pl.BlockSpec((rb * ROW_CHUNKS, LANES), lambda g, be, fi, nu: (last(g, nu), 0)),
                      pl.BlockSpec((None, 1, d, 2 * d), wmap),
                      pl.BlockSpec((1, 1, 2 * d), bmap),
                      pl.BlockSpec((None, 1, d, d), wmap),
                      pl.BlockSpec((1, 1, d), bmap)],
            out_specs=pl.BlockSpec((rb * ROW_CHUNKS, LANES), lambda g, be, fi, nu: (g, 0)),
            scratch_shapes=[pltpu.VMEM((rb, d), BF16),
                            pltpu.VMEM((d, 2 * d), BF16),
                            pltpu.VMEM((d, d), BF16)]),
        out_shape=jax.ShapeDtypeStruct(xs.shape, F32),
        compiler_params=_cparams(("arbitrary",)),
        name="moe_experts",
    )(block_exp, first, n_used, xs, w_gu, b_gu_grouped, w_dn, b_dn)


def _combine_kernel(tab_ref, tab_next_ref, meta_ref, gate_ref, x_ref, g2_ref, lng_ref, lnb_ref, ys_hbm, o_ref,
                    yb0, yb1, sem):
    i = pl.program_id(0)
    n_steps = pl.num_programs(0)
    tm = x_ref.shape[0]
    nrow = tm * TOP_K
    bufs = (yb0, yb1)

    def fetch(t_ref, slot):
        _for_each_run(t_ref, lambda srow, trow, n, prio: pltpu.make_async_copy(
            _span_rows(ys_hbm, srow, n), _span_rows(bufs[slot], trow, n), sem.at[slot]).start(priority=prio))

    @pl.when(i == 0)
    def _():
        fetch(tab_ref, 0)

    for slot in range(2):
        @pl.when(i % 2 == slot)
        def _(slot=slot):
            buf = bufs[slot]

            @pl.when(i + 1 < n_steps)
            def _():
                fetch(tab_next_ref, 1 - slot)

            _wait_rows(ys_hbm, buf, sem.at[slot], nrow)
            yt = jnp.concatenate([buf[pl.ds(c, nrow, stride=ROW_CHUNKS), :] for c in range(ROW_CHUNKS)],
                                 axis=1).astype(BF16)
            meta = meta_ref[...]
            gates = gate_ref[...]
            col = lax.broadcasted_iota(I32, (1, nrow), 1)
            w = jnp.zeros((tm, nrow), F32)
            for k in range(TOP_K):
                w = w + jnp.where(col == meta[:, TOP_K + k:TOP_K + k + 1], gates[:, k:k + 1], 0.0)
            w_hi = w.astype(BF16)
            w_lo = (w - w_hi.astype(F32)).astype(BF16)
            y = (jnp.dot(w_hi, yt, preferred_element_type=F32) + jnp.dot(w_lo, yt, preferred_element_type=F32))
            z = DEEPNORM_ALPHA * x_ref[...] + g2_ref[0, 0] * y
            o_ref[...] = _ln(z) * lng_ref[...] + lnb_ref[...]


def _combine(ys, tab, meta, gates, x2, mod4, lng, lnb, seq):
    n, d = x2.shape
    tm = TOK_TILE
    per_b = seq // tm
    last = n // tm - 1
    return pl.pallas_call(
        _combine_kernel,
        grid=(n // tm,),
        in_specs=[pl.BlockSpec((SUBLANES, LANES), lambda i: (i, 0), memory_space=pltpu.SMEM),
                  pl.BlockSpec((SUBLANES, LANES), lambda i: (jnp.minimum(i + 1, last), 0),
                               memory_space=pltpu.SMEM),
                  pl.BlockSpec((tm, LANES), lambda i: (i, 0)),
                  pl.BlockSpec((tm, LANES), lambda i: (i, 0)),
                  pl.BlockSpec((tm, d), lambda i: (i, 0)),
                  pl.BlockSpec((1, 1, 1, d), lambda i: (i // per_b, 5, 0, 0)),
                  pl.BlockSpec((1, d), lambda i: (0, 0)),
                  pl.BlockSpec((1, d), lambda i: (0, 0)),
                  pl.BlockSpec(memory_space=pl.ANY)],
        out_specs=pl.BlockSpec((tm, d), lambda i: (i, 0)),
        out_shape=jax.ShapeDtypeStruct((n, d), F32),
        scratch_shapes=[pltpu.VMEM((tm * TOP_K * ROW_CHUNKS, LANES), F32),
                        pltpu.VMEM((tm * TOP_K * ROW_CHUNKS, LANES), F32),
                        pltpu.SemaphoreType.DMA((2,))],
        compiler_params=_cparams(("arbitrary",)),
        name="moe_combine",
    )(tab, tab, meta, gates, x2, mod4, lng, lnb, ys)


def _moe(x2, mod4, layer, rw, rb, w_gu, b_gu, w_dn, b_dn, lng, lnb, seq):
    n, d = x2.shape
    meta, gates, tile_meta, cnt = _router(x2, mod4, rw, rb, seq)
    counts = cnt[0, :N_EXPERTS]
    padded = -(-counts // EXPERT_ROWS) * EXPERT_ROWS
    pend = jnp.cumsum(padded)
    pstart = pend - padded
    n_blocks = -(-(n * TOP_K + N_EXPERTS * (EXPERT_ROWS - 1)) // EXPERT_ROWS)
    n_rows = n_blocks * EXPERT_ROWS
    block_row0 = jnp.arange(n_blocks, dtype=I32) * EXPERT_ROWS
    block_exp = jnp.minimum(jnp.sum((pend[None, :] <= block_row0[:, None]).astype(I32), axis=1), N_EXPERTS - 1)
    first = jnp.concatenate([jnp.ones((1,), I32), (block_exp[1:] != block_exp[:-1]).astype(I32)])
    n_used = (pend[-1:] // EXPERT_ROWS).astype(I32)
    n_tiles = n // TOK_TILE
    start_row = jnp.zeros((1, 1, LANES), I32).at[0, 0, :N_EXPERTS].set(pstart)
    sel_row0 = (jnp.arange(SUBLANES) == 0).astype(I32)[None, :, None]
    tab = (tile_meta.reshape(n_tiles, SUBLANES, LANES) + sel_row0 * start_row).reshape(n_tiles * SUBLANES, LANES)
    tail0 = pend[-1] + jnp.arange(N_FILL - N_EXPERTS, dtype=I32) * EXPERT_ROWS
    fill_start = jnp.concatenate([pstart + counts, tail0])
    fill_len = jnp.concatenate([padded - counts, jnp.clip(n_rows - tail0, 0, EXPERT_ROWS)])
    fill_tab = jnp.zeros((SUBLANES, LANES), I32).at[0, :N_FILL].set(fill_start).at[1, :N_FILL].set(fill_len)
    xs = _dispatch(x2, mod4, meta, tab, fill_tab, n_rows, seq)
    e = b_gu.shape[0]
    b_grouped = b_gu.reshape(e, d // LANES, LANES, 2).transpose(0, 1, 3, 2).reshape(e, 1, 2 * d)
    ys = _experts(xs, block_exp, first, n_used, layer, w_gu, b_grouped, w_dn, b_dn[:, None, :])
    return _combine(ys, tab, meta, gates, x2, mod4, lng.reshape(1, d), lnb.reshape(1, d), seq)


def _permute_w_in(w):
    d = w.shape[0]
    off = {}
    o = 0
    for name, wd in (("a_q", 256), ("a_k", 256), ("a_v", 256), ("b_q", 384), ("b_kc", 128), ("b_vc", 128),
                     ("b_ks", 128), ("b_vs", 128), ("b_kw", 128), ("b_vw", 128), ("b_gate", 18),
                     ("c_q", 384), ("c_k", 384), ("c_v", 384), ("merge_gate", 3 * D_MODEL)):
        off[name] = (o, o + wd)
        o += wd
    col = lambda name: w[:, off[name][0]:off[name][1]]
    scale = HEAD_DIM ** -0.5 * float(np.log2(np.e))
    b_q = col("b_q").reshape(d, B_KV_GROUPS, B_REP, HEAD_DIM).transpose(0, 2, 1, 3).reshape(d, B_HEADS * HEAD_DIM)
    gate_pad = jnp.zeros((d, LANES - B_HEADS * 3), w.dtype)
    parts = [col("merge_gate"),
             col("a_q") * scale, col("a_k"), b_q * scale, col("b_kc"), col("b_ks"), col("b_kw"),
             col("c_q") * scale, col("c_k"),
             col("a_v"), col("b_vc"), col("b_vs"), col("b_vw"), col("c_v"),
             col("b_gate"), gate_pad]
    return jnp.concatenate(parts, axis=1).astype(BF16)


def _dilate(arr, tile, batch, seq, dil):
    a = arr[:, tile * LANES:(tile + 1) * LANES].reshape(batch, seq // dil, dil, LANES)
    return a.transpose(0, 2, 1, 3).reshape(batch * seq, LANES)


def _undilate(arr, batch, seq, dil):
    a = arr.reshape(batch, dil, seq // dil, LANES)
    return a.transpose(0, 2, 1, 3).reshape(batch * seq, LANES)


def _mixer_layer(x2, mod4, pos2, inv_t, batch, seq, w_in, cmp_w1_k, cmp_w2_k, cmp_pe_k,
                 cmp_w1_v, cmp_w2_v, cmp_pe_v, w_a, w_b, w_c, w_out, lng, lnb):
    d = D_MODEL
    proj = _inproj(x2, mod4, _permute_w_in(w_in), pos2, inv_t, seq)

    bias_a = _moba_select(proj, _kmean(proj, batch, seq), batch, seq)
    o_a = _bb_attn(proj, bias_a, batch=batch, seq=seq, n_pairs=A_HEADS // 2,
                   q_tile=lambda p: T_AQ + p, k_tile=lambda p: T_AK + p, v_tile=lambda p: T_AV + p,
                   bias_tile=lambda p: p, blk=MOBA_BLOCK)

    kc = _compress(proj[:, T_BKC * LANES:(T_BKC + 1) * LANES], cmp_w1_k, cmp_w2_k, cmp_pe_k, batch, seq)
    vc = _compress(proj[:, T_BVC * LANES:(T_BVC + 1) * LANES], cmp_w1_v, cmp_w2_v, cmp_pe_v, batch, seq)
    o_cmp, bias_b = _nsa_cmp(proj, kc, vc, batch, seq)
    o_slc = _bb_attn(proj, bias_b, batch=batch, seq=seq, n_pairs=B_REP,
                     q_tile=lambda p: T_BQ + p, k_tile=lambda p: T_BKS, v_tile=lambda p: T_BVS,
                     bias_tile=lambda p: 0, blk=SLC_BLOCK)
    o_win, _ = _band_attn(proj, proj, proj, batch=batch, seq=seq, q_tiles=[T_BQ + r for r in range(B_REP)],
                          k_tile=T_BKW, v_tile=T_BVW, window=NSA_WINDOW, t=256, group=2)

    oc, lse = [], []
    for gi, (window, dil) in enumerate(DILATED_PAIRS):
        wlen = window // dil + 1
        if dil == 1:
            o, l = _band_attn(proj, proj, proj, batch=batch, seq=seq, q_tiles=[T_CQ + gi], k_tile=T_CK + gi,
                              v_tile=T_CV + gi, window=wlen, t=256, group=4)
        else:
            qd = _dilate(proj, T_CQ + gi, batch, seq, dil)
            kd = _dilate(proj, T_CK + gi, batch, seq, dil)
            vd = _dilate(proj, T_CV + gi, batch, seq, dil)
            o, l = _band_attn(qd, kd, vd, batch=batch * dil, seq=seq // dil, q_tiles=[0], k_tile=0, v_tile=0,
                              window=wlen, t=256, group=4)
            o, l = _undilate(o, batch, seq, dil), _undilate(l, batch, seq, dil)
        oc.append(o)
        lse.append(l)

    w_b_perm = w_b.reshape(B_KV_GROUPS, B_REP, HEAD_DIM, d).transpose(1, 0, 2, 3).reshape(B_HEADS * HEAD_DIM, d)
    return _merge(o_a, o_cmp, o_slc, o_win, proj, oc, lse, x2, mod4,
                  w_a.astype(BF16), w_b_perm.astype(BF16), w_c.astype(BF16), w_out.astype(BF16),
                  lng.reshape(1, d), lnb.reshape(1, d), seq)


def _rope_inputs(positions):
    inv = ROPE_THETA ** (-jnp.arange(0, HEAD_DIM, 2, dtype=F32) / HEAD_DIM)
    return positions.reshape(-1, 1), jnp.tile(inv, 2 * LANES // HEAD_DIM)[None, :]


def kernel(x, c, positions, w_ada, b_ada, w_in, cmp_w1_k, cmp_w2_k, cmp_pe_k, cmp_w1_v, cmp_w2_v, cmp_pe_v, w_branch_a, w_branch_b, w_branch_c, w_out, ln1_g, ln1_b, router_w, router_b, w_gate_up, b_gate_up, w_down, b_down, ln2_g, ln2_b):
    batch, seq, d = x.shape
    depth = w_in.shape[0]
    pos2, inv_t = _rope_inputs(positions)
    mod = _ada(c, w_ada, b_ada)
    x2 = x.reshape(batch * seq, d)
    for l in range(depth):
        mod4 = mod[l].reshape(batch, N_ADA, 1, d)
        x2 = _mixer_layer(x2, mod4, pos2, inv_t, batch, seq, w_in[l], cmp_w1_k[l], cmp_w2_k[l], cmp_pe_k[l],
                          cmp_w1_v[l], cmp_w2_v[l], cmp_pe_v[l], w_branch_a[l], w_branch_b[l],
                          w_branch_c[l], w_out[l], ln1_g[l], ln1_b[l])
        x2 = _moe(x2, mod4, l, router_w[l], router_b[l], w_gate_up, b_gate_up[l], w_down, b_down[l],
                  ln2_g[l], ln2_b[l], seq)
    return x2.reshape(batch, seq, d)
```

```python
import functools

import numpy as np
import jax
import jax.numpy as jnp
from jax import lax
from jax.experimental import pallas as pl
from jax.experimental.pallas import tpu as pltpu

F32 = jnp.float32
BF16 = jnp.bfloat16
I32 = jnp.int32
HIGHEST = lax.Precision.HIGHEST

D_MODEL = 1024
DEPTH = 2
HEAD_DIM = 64
ROPE_THETA = 10000.0
LN_EPS = 1e-5
DEEPNORM_ALPHA = (2 * DEPTH) ** 0.25
N_ADA = 6
A_HEADS = 4
MOBA_BLOCK = 256
MOBA_TOPK = 3
B_HEADS = 6
B_KV_GROUPS = 2
B_REP = 3
CMP_BLOCK = 32
CMP_STRIDE = 16
CMP_HIDDEN = 128
SLC_BLOCK = 64
SLC_TOPK = 16
NSA_WINDOW = 512
FORCE_SCORE = 1e6
DILATED_PAIRS = ((128, 1), (512, 4), (2048, 16))
N_EXPERTS = 32
TOP_K = 4
SWIGLU_ALPHA = 1.702
SWIGLU_LIMIT = 7.0

LANES = 128
SUBLANES = 8
HALF = LANES // 2
NEG = -1e30
ROW_CHUNKS = D_MODEL // LANES
VMEM_LIMIT = 56 * 1024 * 1024

T_MG = 0
T_ROPE0, T_ROPE1 = 24, 40
T_AQ, T_AK, T_BQ, T_BKC, T_BKS, T_BKW, T_CQ, T_CK = 24, 26, 28, 31, 32, 33, 34, 37
T_AV, T_BVC, T_BVS, T_BVW, T_CV, T_BG = 40, 42, 43, 44, 45, 48
N_PROJ_TILES = 49
PROJ_W = N_PROJ_TILES * LANES

EXPERT_ROWS = 512
TOK_TILE = 256
N_FILL = 2 * N_EXPERTS + 2


def _cparams(sem, **kw):
    return pltpu.CompilerParams(dimension_semantics=sem, vmem_limit_bytes=VMEM_LIMIT, **kw)


def _lane_iota(shape=(1, LANES)):
    return lax.broadcasted_iota(I32, shape, len(shape) - 1)


def _sigmoid(x):
    return 1.0 / (1.0 + jnp.exp(-x))


def _sigmoid_tanh(x):
    return 0.5 * jnp.tanh(0.5 * x) + 0.5


def _ln(x):
    mu = jnp.mean(x, axis=-1, keepdims=True)
    xc = x - mu
    var = jnp.mean(xc * xc, axis=-1, keepdims=True)
    return xc * lax.rsqrt(var + LN_EPS)


def _dot_nt(a, b, precision=None):
    return lax.dot_general(a, b, (((1,), (1,)), ((), ())), precision=precision,
                           preferred_element_type=F32)


def _ada_kernel(c_ref, w_ref, b_ref, o_ref):
    c = c_ref[...]
    cond = c * _sigmoid(c)
    o_ref[0] = jnp.dot(cond, w_ref[0], precision=HIGHEST, preferred_element_type=F32) + b_ref[0]


def _ada(c, w_ada, b_ada):
    depth, d, n = w_ada.shape
    b = c.shape[0]
    tn = 1536
    return pl.pallas_call(
        _ada_kernel,
        grid=(depth, n // tn),
        in_specs=[pl.BlockSpec((b, d), lambda l, j: (0, 0)),
                  pl.BlockSpec((1, d, tn), lambda l, j: (l, 0, j)),
                  pl.BlockSpec((1, 1, tn), lambda l, j: (l, 0, j))],
        out_specs=pl.BlockSpec((1, b, tn), lambda l, j: (l, 0, j)),
        out_shape=jax.ShapeDtypeStruct((depth, b, n), F32),
        compiler_params=_cparams(("parallel", "parallel")),
        name="ada",
    )(c, w_ada, b_ada.reshape(depth, 1, n))


def _inproj_kernel(x_ref, sc_ref, sh_ref, w_ref, cos_ref, sin_ref, o_ref, *, chunks):
    h = _ln(x_ref[...]) * (1.0 + sc_ref[0, 0]) + sh_ref[0, 0]
    hb = h.astype(BF16)
    first_half = (_lane_iota() & (HEAD_DIM - 1)) < (HEAD_DIM // 2)
    for c0, cw, rope in chunks:
        acc = jnp.dot(hb, w_ref[:, c0:c0 + cw], preferred_element_type=F32)
        if rope:
            cos = cos_ref[...]
            sin = sin_ref[...]
            for t in range(cw // LANES):
                a = acc[:, t * LANES:(t + 1) * LANES]
                rot = jnp.where(first_half, pltpu.roll(a, LANES - HEAD_DIM // 2, 1),
                                pltpu.roll(a, HEAD_DIM // 2, 1))
                o_ref[:, c0 + t * LANES:c0 + (t + 1) * LANES] = (a * cos + rot * sin).astype(o_ref.dtype)
        else:
            o_ref[:, c0:c0 + cw] = acc.astype(o_ref.dtype)


def _inproj_chunks():
    chunks = []
    for lo, hi, rope in ((0, T_ROPE0, False), (T_ROPE0, T_ROPE1, True), (T_ROPE1, N_PROJ_TILES, False)):
        c = lo * LANES
        while c < hi * LANES:
            cw = min(512, hi * LANES - c)
            chunks.append((c, cw, rope))
            c += cw
    return tuple(chunks)


def _inproj(x2, mod4, w_perm, cos_t, sin_t, seq):
    n, d = x2.shape
    tm = 256
    per_b = seq // tm
    return pl.pallas_call(
        functools.partial(_inproj_kernel, chunks=_inproj_chunks()),
        grid=(n // tm,),
        in_specs=[pl.BlockSpec((tm, d), lambda i: (i, 0)),
                  pl.BlockSpec((1, 1, 1, d), lambda i: (i // per_b, 1, 0, 0)),
                  pl.BlockSpec((1, 1, 1, d), lambda i: (i // per_b, 0, 0, 0)),
                  pl.BlockSpec((d, PROJ_W), lambda i: (0, 0)),
                  pl.BlockSpec((tm, LANES), lambda i: (i, 0)),
                  pl.BlockSpec((tm, LANES), lambda i: (i, 0))],
        out_specs=pl.BlockSpec((tm, PROJ_W), lambda i: (i, 0)),
        out_shape=jax.ShapeDtypeStruct((n, PROJ_W), BF16),
        compiler_params=_cparams(("parallel",)),
        name="inproj",
    )(x2, mod4, mod4, w_perm, cos_t, sin_t)


def _bb_attn_kernel(q_ref, k_ref, v_ref, b_ref, o_ref, qa_sc, s_sc, mp_sc, m_sc, acc_sc,
                    *, blk_shift, tq, tu, big):
    i = pl.program_id(2)
    n_diag = tq // tu
    lane = _lane_iota()
    lo = lane < HALF
    q = q_ref[...]
    b = b_ref[...]
    qa_sc[0] = jnp.where(lo, q, b)
    qa_sc[1] = jnp.where(lo, b, q)
    mp_sc[...] = jnp.full(mp_sc.shape, -jnp.inf, F32)
    acc_sc[...] = jnp.zeros(acc_sc.shape, F32)

    def scores(u0, n_units, diagonal_last):
        tk = n_units * tu
        start = pl.multiple_of(u0 * tu, tu)
        k = k_ref[pl.ds(start, tk), :]
        kblk = (start + lax.broadcasted_iota(I32, (tk, 1), 0)) >> blk_shift
        onehot = jnp.where(kblk == (lane & (HALF - 1)), 1.0, 0.0).astype(BF16)
        ka = (jnp.where(lo, k, onehot), jnp.where(lo, onehot, k))
        row = lax.broadcasted_iota(I32, (tq, 1), 0)
        col = lax.broadcasted_iota(I32, (1, tu), 1)
        for h in range(2):
            s = _dot_nt(qa_sc[h], ka[h])
            mp = mp_sc[h]
            for u in range(n_units):
                su = s[:, u * tu:(u + 1) * tu]
                d = u - (n_units - n_diag)
                if diagonal_last and d >= 0:
                    su = jnp.where(col + d * tu <= row, su, NEG)
                s_sc[h, u0 + u] = su
                for c in range(tu // LANES):
                    mp = jnp.maximum(mp, su[:, c * LANES:(c + 1) * LANES])
            mp_sc[h] = mp

    def weighted(u0, n_units):
        tk = n_units * tu
        start = pl.multiple_of(u0 * tu, tu)
        v = v_ref[pl.ds(start, tk), :]
        one = jnp.ones_like(v)
        vs = (jnp.where(lo, v, one), jnp.where(lo, one, v))
        for h in range(2):
            m_row = m_sc[h]
            ps = []
            for u in range(n_units):
                su = s_sc[h, u0 + u]
                for c in range(tu // LANES):
                    ps.append(jnp.exp2(su[:, c * LANES:(c + 1) * LANES] - m_row).astype(BF16))
            acc_sc[h] = acc_sc[h] + jnp.dot(jnp.concatenate(ps, axis=1), vs[h], preferred_element_type=F32)

    n_full = i * n_diag
    n_big = n_full // big
    n_tail = n_full - n_big * big + n_diag

    def run(fn):
        def big_body(t, c):
            fn(t * big, big, False) if fn is scores else fn(t * big, big)
            return c

        lax.fori_loop(0, n_big, big_body, 0)
        for n in range(n_diag, big + n_diag, n_diag):
            @pl.when(n_tail == n)
            def _(n=n):
                fn(n_big * big, n, True) if fn is scores else fn(n_big * big, n)

    run(scores)
    for h in range(2):
        m_sc[h] = jnp.broadcast_to(jnp.max(mp_sc[h], axis=1, keepdims=True), (tq, LANES))
    run(weighted)
    outs = [acc_sc[h] / pltpu.roll(acc_sc[h], HALF, 1) for h in range(2)]
    o_ref[...] = jnp.where(lo, outs[0], outs[1]).astype(o_ref.dtype)


def _bb_attn(proj, bias, *, batch, seq, n_pairs, q_tile, k_tile, v_tile, bias_tile, blk):
    t, tu = 512, 256
    nq = seq // t
    n = batch * seq
    kern = functools.partial(_bb_attn_kernel, blk_shift=int(np.log2(blk)), tq=t, tu=tu, big=4)
    return pl.pallas_call(
        kern,
        grid=(batch, n_pairs, nq),
        in_specs=[pl.BlockSpec((t, LANES), lambda b, p, i: (b * nq + i, q_tile(p))),
                  pl.BlockSpec((seq, LANES), lambda b, p, i: (b, k_tile(p))),
                  pl.BlockSpec((seq, LANES), lambda b, p, i: (b, v_tile(p))),
                  pl.BlockSpec((t, LANES), lambda b, p, i: (b * nq + i, bias_tile(p)))],
        out_specs=pl.BlockSpec((t, LANES), lambda b, p, i: (b * nq + i, p)),
        out_shape=jax.ShapeDtypeStruct((n, n_pairs * LANES), BF16),
        scratch_shapes=[pltpu.VMEM((2, t, LANES), BF16),
                        pltpu.VMEM((2, seq // tu, t, tu), F32),
                        pltpu.VMEM((2, t, LANES), F32),
                        pltpu.VMEM((2, t, LANES), F32),
                        pltpu.VMEM((2, t, LANES), F32)],
        compiler_params=_cparams(("parallel", "parallel", "parallel")),
        name="bb_attn_%d" % blk,
    )(proj, proj, proj, bias)


def _band_kernel(*refs, tq, kw, window, seq, n_q, group, tiles_per_seq):
    q_refs = refs[:n_q]
    k_ref, v_ref, o_ref, lse_ref = refs[n_q:n_q + 4]
    i = pl.program_id(1)
    lane = _lane_iota()
    lo = lane < HALF
    for g in range(group):
        if tiles_per_seq == 1:
            tile, base = 0, g * seq
        else:
            tile, base = i * group + g, 0
        start = jnp.clip(tile * tq - (kw - tq), 0, seq - kw)
        k = k_ref[pl.ds(pl.multiple_of(base + start, LANES), kw), :]
        v = v_ref[pl.ds(pl.multiple_of(base + start, LANES), kw), :]
        rel = (tile * tq + lax.broadcasted_iota(I32, (tq, 1), 0)) - (start + lax.broadcasted_iota(I32, (1, kw), 1))
        ok = (rel >= 0) & (rel < window)
        for qi in range(n_q):
            q = q_refs[qi][g * tq:(g + 1) * tq, :]
            zero = jnp.zeros_like(q)
            qs = (jnp.where(lo, q, zero), jnp.where(lo, zero, q))
            outs, lses = [], []
            for h in range(2):
                s = jnp.where(ok, _dot_nt(qs[h], k), NEG)
                m = jnp.max(s, axis=1, keepdims=True)
                p = jnp.exp2(s - m)
                l = jnp.sum(p, axis=1, keepdims=True)
                outs.append(jnp.dot(p.astype(BF16), v, preferred_element_type=F32) / l)
                lses.append(m + jnp.log2(l))
            rows, cols = slice(g * tq, (g + 1) * tq), slice(qi * LANES, (qi + 1) * LANES)
            o_ref[rows, cols] = jnp.where(lo, outs[0], outs[1]).astype(o_ref.dtype)
            lse_ref[rows, cols] = jnp.where(lo, lses[0], lses[1])


def _band_attn(q_arr, k_arr, v_arr, *, batch, seq, q_tiles, k_tile, v_tile, window, t, group):
    t = min(t, seq)
    tiles_per_seq = seq // t
    n = batch * seq
    n_q = len(q_tiles)
    if tiles_per_seq == 1:
        group = min(group, batch)
        grid = (batch // group, 1)
        kv_rows = group * seq
        row_blk = lambda b, i: b
    else:
        group = min(group, tiles_per_seq)
        grid = (batch, tiles_per_seq // group)
        kv_rows = seq
        row_blk = lambda b, i: b * (tiles_per_seq // group) + i
    kw = min(seq, t + -(-(window - 1) // LANES) * LANES)
    kern = functools.partial(_band_kernel, tq=t, kw=kw, window=window, seq=seq, n_q=n_q, group=group,
                             tiles_per_seq=tiles_per_seq)
    qspecs = [pl.BlockSpec((group * t, LANES), (lambda b, i, c=c: (row_blk(b, i), c))) for c in q_tiles]
    return pl.pallas_call(
        kern,
        grid=grid,
        in_specs=qspecs + [pl.BlockSpec((kv_rows, LANES), lambda b, i: (b, k_tile)),
                           pl.BlockSpec((kv_rows, LANES), lambda b, i: (b, v_tile))],
        out_specs=[pl.BlockSpec((group * t, n_q * LANES), lambda b, i: (row_blk(b, i), 0)),
                   pl.BlockSpec((group * t, n_q * LANES), lambda b, i: (row_blk(b, i), 0))],
        out_shape=[jax.ShapeDtypeStruct((n, n_q * LANES), BF16),
                   jax.ShapeDtypeStruct((n, n_q * LANES), F32)],
        compiler_params=_cparams(("parallel", "parallel")),
        name="band_attn_%d" % window,
    )(*([q_arr] * n_q), k_arr, v_arr)


def _kmean_kernel(k_ref, o_ref, *, nb):
    s = k_ref.shape[0]
    blk = lax.broadcasted_iota(I32, (nb, s), 1) >> int(np.log2(MOBA_BLOCK))
    avg = jnp.where(blk == lax.broadcasted_iota(I32, (nb, s), 0), 1.0 / MOBA_BLOCK, 0.0).astype(BF16)
    o_ref[0] = jnp.dot(avg, k_ref[...], preferred_element_type=F32)


def _kmean(proj, batch, seq):
    nb = seq // MOBA_BLOCK
    w = A_HEADS * HEAD_DIM
    return pl.pallas_call(
        functools.partial(_kmean_kernel, nb=nb),
        grid=(batch,),
        in_specs=[pl.BlockSpec((seq, w), lambda b: (b, T_AK * LANES // w))],
        out_specs=pl.BlockSpec((1, nb, w), lambda b: (b, 0, 0)),
        out_shape=jax.ShapeDtypeStruct((batch, nb, w), F32),
        compiler_params=_cparams(("parallel",)),
        name="moba_kmean",
    )(proj)


def _rank_desc(g, n):
    gs = min(SUBLANES, n)
    groups = [g[r0:r0 + gs, :] for r0 in range(0, n, gs)]
    ranks = [jnp.zeros(x.shape, I32) for x in groups]
    sub = lax.broadcasted_iota(I32, (gs, 1), 0)
    for m in range(n):
        c = g[m:m + 1, :]
        for gi, x in enumerate(groups):
            if gi * gs > m:
                beats = c >= x
            elif gi * gs + gs - 1 <= m:
                beats = c > x
            else:
                beats = (c > x) | ((c == x) & (sub + gi * gs > m))
            ranks[gi] = ranks[gi] + jnp.where(beats, 1, 0)
    return jnp.concatenate(ranks, axis=0)


def _moba_sel_kernel(*refs, nb, n_sel, n_pairs):
    q_refs = refs[:n_pairs]
    km_ref, b_ref = refs[n_pairs:]
    i = pl.program_id(1)
    t = b_ref.shape[0]
    lo = _lane_iota() < HALF
    pad = jnp.zeros((HALF - nb, LANES), F32)
    n_idx = lax.broadcasted_iota(I32, (nb, 1), 0)
    valid = n_idx < i
    for p in range(n_pairs):
        q = q_refs[p][...].astype(F32)
        km = km_ref[0, :, p * LANES:(p + 1) * LANES]
        zero = jnp.zeros_like(km)
        kmt = jnp.concatenate([jnp.where(lo, zero, km), pad, jnp.where(lo, km, zero), pad], axis=0)
        gt = _dot_nt(kmt, q, precision=HIGHEST)
        rows = []
        for r0 in (0, HALF):
            g = jnp.where(valid, gt[r0:r0 + nb, :], -jnp.inf)
            rank = _rank_desc(g, nb)
            allowed = (valid & (rank < n_sel)) | (n_idx == i)
            rows.append(jnp.where(allowed, 0.0, NEG))
            rows.append(jnp.zeros((HALF - nb, t), F32))
        b_ref[:, p * LANES:(p + 1) * LANES] = jnp.concatenate(rows, axis=0).T.astype(b_ref.dtype)


def _moba_select(proj, kmean, batch, seq):
    t = MOBA_BLOCK
    nq = seq // t
    nb = seq // MOBA_BLOCK
    n_sel = min(MOBA_TOPK, nb - 1)
    n_pairs = A_HEADS // 2
    qspecs = [pl.BlockSpec((t, LANES), (lambda b, i, p=p: (b * nq + i, T_AQ + p))) for p in range(n_pairs)]
    return pl.pallas_call(
        functools.partial(_moba_sel_kernel, nb=nb, n_sel=n_sel, n_pairs=n_pairs),
        grid=(batch, nq),
        in_specs=qspecs + [pl.BlockSpec((1, nb, n_pairs * LANES), lambda b, i: (b, 0, 0))],
        out_specs=pl.BlockSpec((t, n_pairs * LANES), lambda b, i: (b * nq + i, 0)),
        out_shape=jax.ShapeDtypeStruct((batch * seq, n_pairs * LANES), BF16),
        compiler_params=_cparams(("parallel", "parallel")),
        name="moba_select",
    )(*([proj] * n_pairs), kmean)


def _compress_kernel(x_ref, w1_ref, pe_ref, w1f_ref, w2_ref, o_ref):
    x = x_ref[0]
    nblk = x.shape[0]
    outs = []
    for g in range(B_KV_GROUPS):
        u = jnp.dot(x, w1_ref[g, 0], preferred_element_type=F32)
        v = jnp.dot(x, w1_ref[g, 1], preferred_element_type=F32)
        pe_h = jnp.dot(pe_ref[...], w1f_ref[...], precision=HIGHEST, preferred_element_type=F32)
        hid = u + pltpu.roll(v, nblk - 1, 0) + pe_h[:1]
        hid = hid * _sigmoid(hid)
        outs.append(jnp.dot(hid, w2_ref[g], precision=HIGHEST, preferred_element_type=F32))
    o = outs[0] + outs[1]
    rows = lax.broadcasted_iota(I32, (nblk, 1), 0)
    o_ref[0] = jnp.where(rows < nblk - 1, o, 0.0).astype(o_ref.dtype)


def _compress(xt, w1, w2, pe, batch, seq):
    nblk = seq // CMP_STRIDE
    xg = xt.reshape(batch, nblk, CMP_STRIDE * LANES)
    w1r = w1.reshape(2, CMP_STRIDE, HEAD_DIM, CMP_HIDDEN)
    w1e = jnp.zeros((B_KV_GROUPS, 2, CMP_STRIDE, B_KV_GROUPS, HEAD_DIM, CMP_HIDDEN), F32)
    for g in range(B_KV_GROUPS):
        w1e = w1e.at[g, :, :, g].set(w1r)
    w1e = w1e.reshape(B_KV_GROUPS, 2, CMP_STRIDE * LANES, CMP_HIDDEN).astype(BF16)
    pe_flat = jnp.broadcast_to(pe.reshape(1, CMP_BLOCK * HEAD_DIM), (SUBLANES, CMP_BLOCK * HEAD_DIM))
    w2e = jnp.zeros((B_KV_GROUPS, CMP_HIDDEN, LANES), F32)
    for g in range(B_KV_GROUPS):
        w2e = w2e.at[g, :, g * HEAD_DIM:(g + 1) * HEAD_DIM].set(w2)
    return pl.pallas_call(
        _compress_kernel,
        grid=(batch,),
        in_specs=[pl.BlockSpec((1, nblk, CMP_STRIDE * LANES), lambda b: (b, 0, 0)),
                  pl.BlockSpec(w1e.shape, lambda b: (0, 0, 0, 0)),
                  pl.BlockSpec(pe_flat.shape, lambda b: (0, 0)),
                  pl.BlockSpec(w1.shape, lambda b: (0, 0)),
                  pl.BlockSpec(w2e.shape, lambda b: (0, 0, 0))],
        out_specs=pl.BlockSpec((1, nblk, LANES), lambda b: (b, 0, 0)),
        out_shape=jax.ShapeDtypeStruct((batch, nblk, LANES), BF16),
        compiler_params=_cparams(("parallel",)),
        name="nsa_compress",
    )(xg, w1e, pe_flat, w1, w2e)


def _nsa_cmp_kernel(q0_ref, q1_ref, q2_ref, kc_ref, vc_ref, ov_ref, o_ref, b_ref, *, n_sel, ns):
    i = pl.program_id(1)
    t = q0_ref.shape[0]
    ncp = kc_ref.shape[1]
    lane = _lane_iota()
    lo = lane < HALF
    kc = kc_ref[0]
    vc = vc_ref[0]
    tq = i * t + lax.broadcasted_iota(I32, (t, 1), 0)
    cmp_end = lax.broadcasted_iota(I32, (1, ncp), 1) * CMP_STRIDE + (CMP_BLOCK - 1)
    vis = cmp_end <= tq
    psum = [jnp.zeros((t, ncp), F32) for _ in range(B_KV_GROUPS)]
    for r, q_ref in enumerate((q0_ref, q1_ref, q2_ref)):
        q = q_ref[...]
        zero = jnp.zeros_like(q)
        outs = []
        for g in range(B_KV_GROUPS):
            qg = jnp.where(lo, q, zero) if g == 0 else jnp.where(lo, zero, q)
            s = jnp.where(vis, _dot_nt(qg, kc), NEG)
            m = jnp.max(s, axis=1, keepdims=True)
            m = jnp.where(m > 0.5 * NEG, m, 0.0)
            e = jnp.exp2(s - m)
            p = e / jnp.maximum(jnp.sum(e, axis=1, keepdims=True), 1e-30)
            psum[g] = psum[g] + p
            outs.append(jnp.dot(p.astype(BF16), vc, preferred_element_type=F32))
        o_ref[:, r * LANES:(r + 1) * LANES] = jnp.where(lo, outs[0], outs[1]).astype(o_ref.dtype)
    n_idx = lax.broadcasted_iota(I32, (ns, 1), 0)
    tcol = i * t + lax.broadcasted_iota(I32, (1, t), 1)
    qblk = tcol >> int(np.log2(SLC_BLOCK))
    forced = (n_idx == 0) | (n_idx == qblk) | (n_idx == qblk - 1)
    valid = n_idx <= qblk
    rows = []
    for g in (1, 0):
        imp = _dot_nt(ov_ref[...], psum[g], precision=HIGHEST)
        imp = jnp.where(forced, FORCE_SCORE, imp)
        imp = jnp.where(valid, imp, -jnp.inf)
        rank = _rank_desc(imp, ns)
        allowed = valid & (rank < n_sel)
        rows.append(jnp.where(allowed, 0.0, NEG))
        if ns < HALF:
            rows.append(jnp.zeros((HALF - ns, t), F32))
    b_ref[...] = jnp.concatenate(rows, axis=0).T.astype(b_ref.dtype)


def _nsa_cmp(proj, kc, vc, batch, seq):
    t = 256
    nq = seq // t
    ns = seq // SLC_BLOCK
    n_sel = min(SLC_TOPK, ns)
    ncp = seq // CMP_STRIDE
    cs = np.arange(ncp)[None, :] * CMP_STRIDE
    ss = np.arange(ns)[:, None] * SLC_BLOCK
    ov = ((cs < ss + SLC_BLOCK) & (cs + CMP_BLOCK > ss)).astype(np.float32)
    ov[:, ncp - 1] = 0.0
    n = batch * seq
    qspec = [pl.BlockSpec((t, LANES), (lambda b, i, r=r: (b * nq + i, T_BQ + r))) for r in range(B_REP)]
    return pl.pallas_call(
        functools.partial(_nsa_cmp_kernel, n_sel=n_sel, ns=ns),
        grid=(batch, nq),
        in_specs=qspec + [pl.BlockSpec((1, ncp, LANES), lambda b, i: (b, 0, 0)),
                          pl.BlockSpec((1, ncp, LANES), lambda b, i: (b, 0, 0)),
                          pl.BlockSpec((ns, ncp), lambda b, i: (0, 0))],
        out_specs=[pl.BlockSpec((t, B_REP * LANES), lambda b, i: (b * nq + i, 0)),
                   pl.BlockSpec((t, LANES), lambda b, i: (b * nq + i, 0))],
        out_shape=[jax.ShapeDtypeStruct((n, B_REP * LANES), BF16),
                   jax.ShapeDtypeStruct((n, LANES), BF16)],
        compiler_params=_cparams(("parallel", "parallel")),
        name="nsa_cmp_select",
    )(proj, proj, proj, kc, vc, jnp.asarray(ov))


def _merge_kernel(oa_ref, ocmp_ref, oslc_ref, owin_ref, bg_ref, oc0_ref, oc1_ref, oc2_ref,
                  l0_ref, l1_ref, l2_ref, mg_ref, x_ref, g1_ref, wa_ref, wb_ref, wc_ref, wo_ref,
                  eg_ref, lng_ref, lnb_ref, o_ref):
    d = D_MODEL
    sg = _sigmoid(bg_ref[...].astype(F32))
    sg_hi = sg.astype(BF16)
    sg_lo = (sg - sg_hi.astype(F32)).astype(BF16)
    ob = jnp.zeros(ocmp_ref.shape, F32)
    for br, ref in enumerate((ocmp_ref, oslc_ref, owin_ref)):
        gexp = (jnp.dot(sg_hi, eg_ref[br], preferred_element_type=F32)
                + jnp.dot(sg_lo, eg_ref[br], preferred_element_type=F32))
        ob = ob + gexp * ref[...].astype(F32)
    l0, l1, l2 = l0_ref[...], l1_ref[...], l2_ref[...]
    mx = jnp.maximum(jnp.maximum(l0, l1), l2)
    e0, e1, e2 = jnp.exp2(l0 - mx), jnp.exp2(l1 - mx), jnp.exp2(l2 - mx)
    den = e0 + e1 + e2
    oc = ((e0 / den) * oc0_ref[...].astype(F32) + (e1 / den) * oc1_ref[...].astype(F32)
          + (e2 / den) * oc2_ref[...].astype(F32))
    pa = jnp.dot(oa_ref[...], wa_ref[...], preferred_element_type=F32)
    pb = jnp.dot(ob.astype(BF16), wb_ref[...], preferred_element_type=F32)
    pc = jnp.dot(oc.astype(BF16), wc_ref[...], preferred_element_type=F32)
    merged = (_sigmoid_tanh(mg_ref[:, 0:d].astype(F32)) * pa
              + _sigmoid_tanh(mg_ref[:, d:2 * d].astype(F32)) * pb
              + _sigmoid_tanh(mg_ref[:, 2 * d:3 * d].astype(F32)) * pc)
    y = jnp.dot(merged.astype(BF16), wo_ref[...], preferred_element_type=F32)
    z = DEEPNORM_ALPHA * x_ref[...] + g1_ref[0, 0] * y
    o_ref[...] = _ln(z) * lng_ref[...] + lnb_ref[...]


def _gate_expand():
    eg = np.zeros((3, LANES, B_HEADS * HEAD_DIM), np.float32)
    for g in range(B_KV_GROUPS):
        for r in range(B_REP):
            for br in range(3):
                c0 = (r * B_KV_GROUPS + g) * HEAD_DIM
                eg[br, (g * B_REP + r) * 3 + br, c0:c0 + HEAD_DIM] = 1.0
    return jnp.asarray(eg)


def _merge(o_a, o_cmp, o_slc, o_win, proj, oc, lse, x2, mod4, wa, wb, wc, wo, lng, lnb, seq):
    n, d = x2.shape
    tm = 256
    per_b = seq // tm
    row = lambda w: pl.BlockSpec((tm, w), lambda i: (i, 0))
    full = lambda a: pl.BlockSpec(a.shape, lambda i: (0,) * a.ndim)
    eg = _gate_expand().astype(BF16)
    return pl.pallas_call(
        _merge_kernel,
        grid=(n // tm,),
        in_specs=[row(o_a.shape[1]), row(o_cmp.shape[1]), row(o_slc.shape[1]), row(o_win.shape[1]),
                  pl.BlockSpec((tm, LANES), lambda i: (i, T_BG)),
                  row(LANES), row(LANES), row(LANES), row(LANES), row(LANES), row(LANES),
                  pl.BlockSpec((tm, 3 * d), lambda i: (i, T_MG)),
                  row(d),
                  pl.BlockSpec((1, 1, 1, d), lambda i: (i // per_b, 2, 0, 0)),
                  full(wa), full(wb), full(wc), full(wo), full(eg), full(lng), full(lnb)],
        out_specs=row(d),
        out_shape=jax.ShapeDtypeStruct((n, d), F32),
        compiler_params=_cparams(("parallel",)),
        name="merge_out",
    )(o_a, o_cmp, o_slc, o_win, proj, oc[0], oc[1], oc[2], lse[0], lse[1], lse[2],
      proj, x2, mod4, wa, wb, wc, wo, eg, lng, lnb)


def _router_kernel(x_ref, sc_ref, sh_ref, rw_ref, rb_ref, meta_ref, gate_ref, tile_ref, cnt_ref, carry_sc):
    tm = x_ref.shape[0]
    ne = N_EXPERTS

    @pl.when(pl.program_id(0) == 0)
    def _():
        carry_sc[...] = jnp.zeros(carry_sc.shape, F32)

    h = _ln(x_ref[...]) * (1.0 + sc_ref[0, 0]) + sh_ref[0, 0]
    lg = _dot_nt(rw_ref[...], h, precision=HIGHEST)[:ne] + rb_ref[...][:ne, :1]
    e_idx = lax.broadcasted_iota(I32, (ne, 1), 0).astype(F32)
    onehots, vals, idxs = [], [], []
    for _ in range(TOP_K):
        m = jnp.max(lg, axis=0, keepdims=True)
        idx = jnp.min(jnp.where(lg == m, e_idx, float(ne)), axis=0, keepdims=True)
        oh = e_idx == idx
        onehots.append(oh)
        vals.append(m)
        idxs.append(idx)
        lg = jnp.where(oh, -jnp.inf, lg)
    es = [jnp.exp(v - vals[0]) for v in vals]
    den = es[0] + es[1] + es[2] + es[3]
    cnt = jnp.zeros((ne, tm), F32)
    for oh in onehots:
        cnt = cnt + jnp.where(oh, 1.0, 0.0)
    before = jnp.where(lax.broadcasted_iota(I32, (tm, tm), 0) < lax.broadcasted_iota(I32, (tm, tm), 1), 1.0, 0.0)
    earlier = jnp.dot(cnt.astype(BF16), before.astype(BF16), preferred_element_type=F32)
    tile_cnt = jnp.broadcast_to(jnp.sum(cnt, axis=1, keepdims=True), (ne, LANES))
    lower = jnp.where(lax.broadcasted_iota(I32, (ne, ne), 0) > lax.broadcasted_iota(I32, (ne, ne), 1), 1.0, 0.0)
    run_off = jnp.dot(lower.astype(BF16), tile_cnt.astype(BF16), preferred_element_type=F32)
    pos = earlier + run_off[:, :1]
    locals_ = [jnp.sum(jnp.where(oh, pos, 0.0), axis=0, keepdims=True) for oh in onehots]

    def token_major(rows):
        stack = jnp.concatenate(rows + [jnp.zeros((LANES - len(rows), tm), F32)], axis=0)
        return stack.T

    meta_ref[...] = token_major(idxs + locals_).astype(I32)
    gate_ref[...] = token_major([e / den for e in es])
    lane = _lane_iota()
    cols = jnp.where(lane == 0, carry_sc[...], jnp.where(lane == 1, tile_cnt, jnp.where(lane == 2, run_off, 0.0)))
    table = jnp.concatenate([cols, jnp.zeros((LANES - ne, LANES), F32)], axis=0).T
    tile_ref[...] = table[:SUBLANES].astype(I32)
    carry_sc[...] = carry_sc[...] + tile_cnt
    total = jnp.concatenate([carry_sc[...], jnp.zeros((LANES - ne, LANES), F32)], axis=0).T
    cnt_ref[...] = total[:SUBLANES].astype(I32)


def _router(x2, mod4, rw, rb, seq):
    n, d = x2.shape
    tm = TOK_TILE
    per_b = seq // tm
    rw_p = jnp.zeros((LANES, d), F32).at[:N_EXPERTS].set(rw.T)
    rb_p = jnp.zeros((LANES, LANES), F32).at[:N_EXPERTS].set(rb[:, None])
    return pl.pallas_call(
        _router_kernel,
        grid=(n // tm,),
        in_specs=[pl.BlockSpec((tm, d), lambda i: (i, 0)),
                  pl.BlockSpec((1, 1, 1, d), lambda i: (i // per_b, 4, 0, 0)),
                  pl.BlockSpec((1, 1, 1, d), lambda i: (i // per_b, 3, 0, 0)),
                  pl.BlockSpec((LANES, d), lambda i: (0, 0)),
                  pl.BlockSpec((LANES, LANES), lambda i: (0, 0))],
        out_specs=[pl.BlockSpec((tm, LANES), lambda i: (i, 0)),
                   pl.BlockSpec((tm, LANES), lambda i: (i, 0)),
                   pl.BlockSpec((SUBLANES, LANES), lambda i: (i, 0)),
                   pl.BlockSpec((SUBLANES, LANES), lambda i: (0, 0))],
        out_shape=[jax.ShapeDtypeStruct((n, LANES), I32),
                   jax.ShapeDtypeStruct((n, LANES), F32),
                   jax.ShapeDtypeStruct((n // tm * SUBLANES, LANES), I32),
                   jax.ShapeDtypeStruct((SUBLANES, LANES), I32)],
        scratch_shapes=[pltpu.VMEM((N_EXPERTS, LANES), F32)],
        compiler_params=_cparams(("arbitrary",)),
        name="router",
    )(x2, mod4, mod4, rw_p, rb_p)


def _wait_rows(src_like, dst_like, sem, n_rows):
    span = pl.ds(0, n_rows * ROW_CHUNKS)
    pltpu.make_async_copy(src_like.at[span], dst_like.at[span], sem).wait()


def _span_rows(ref, row, n):
    return ref.at[pl.ds(pl.multiple_of(row * ROW_CHUNKS, ROW_CHUNKS), n * ROW_CHUNKS)]


def _for_each_run(tab_ref, fn):
    for e in range(N_EXPERTS):
        n = tab_ref[1, e]

        @pl.when(n > 0)
        def _(e=e, n=n):
            fn(tab_ref[0, e], tab_ref[2, e], n, e % 2)


def _dispatch_kernel(tab_ref, fill_ref, x_ref, sc_ref, sh_ref, meta_ref, xs_hbm, xb0, xb1, zb, sem, fsem):
    i = pl.program_id(0)
    n_steps = pl.num_programs(0)
    tm = x_ref.shape[0]
    nrow = tm * TOP_K
    bufs = (xb0, xb1)

    @pl.when(i == 0)
    def _():
        zb[...] = jnp.zeros(zb.shape, F32)
        for j in range(N_FILL):
            n = fill_ref[1, j]

            @pl.when(n > 0)
            def _(j=j, n=n):
                fill = pltpu.make_async_copy(_span_rows(zb, 0, n), _span_rows(xs_hbm, fill_ref[0, j], n), fsem)
                fill.start()
                fill.wait()

    h = (_ln(x_ref[...]) * (1.0 + sc_ref[0, 0]) + sh_ref[0, 0]).astype(BF16)
    pos_t = meta_ref[...].astype(F32).T
    r = lax.broadcasted_iota(I32, (nrow, 1), 0)
    sel = jnp.zeros((nrow, tm), F32)
    for k in range(TOP_K):
        sel = sel + jnp.where(r == pos_t[TOP_K + k:TOP_K + k + 1, :].astype(I32), 1.0, 0.0)
    xt = jnp.dot(sel.astype(BF16), h, preferred_element_type=F32)

    for slot in range(2):
        @pl.when(i % 2 == slot)
        def _(slot=slot):
            buf = bufs[slot]

            @pl.when(i >= 2)
            def _():
                _wait_rows(buf, xs_hbm, sem.at[slot], nrow)

            for c in range(ROW_CHUNKS):
                buf[pl.ds(c, nrow, stride=ROW_CHUNKS), :] = xt[:, c * LANES:(c + 1) * LANES]
            _for_each_run(tab_ref, lambda srow, trow, n, prio: pltpu.make_async_copy(
                _span_rows(buf, trow, n), _span_rows(xs_hbm, srow, n), sem.at[slot]).start(priority=prio))

            @pl.when(i == n_steps - 1)
            def _():
                _wait_rows(buf, xs_hbm, sem.at[slot], nrow)

                @pl.when(i >= 1)
                def _():
                    _wait_rows(bufs[1 - slot], xs_hbm, sem.at[1 - slot], nrow)


def _dispatch(x2, mod4, meta, tab, fill_tab, n_rows, seq):
    n, d = x2.shape
    tm = TOK_TILE
    per_b = seq // tm
    return pl.pallas_call(
        _dispatch_kernel,
        grid=(n // tm,),
        in_specs=[pl.BlockSpec((SUBLANES, LANES), lambda i: (i, 0), memory_space=pltpu.SMEM),
                  pl.BlockSpec((SUBLANES, LANES), lambda i: (0, 0), memory_space=pltpu.SMEM),
                  pl.BlockSpec((tm, d), lambda i: (i, 0)),
                  pl.BlockSpec((1, 1, 1, d), lambda i: (i // per_b, 4, 0, 0)),
                  pl.BlockSpec((1, 1, 1, d), lambda i: (i // per_b, 3, 0, 0)),
                  pl.BlockSpec((tm, LANES), lambda i: (i, 0))],
        out_specs=pl.BlockSpec(memory_space=pl.ANY),
        out_shape=jax.ShapeDtypeStruct((n_rows * ROW_CHUNKS, LANES), F32),
        scratch_shapes=[pltpu.VMEM((tm * TOP_K * ROW_CHUNKS, LANES), F32),
                        pltpu.VMEM((tm * TOP_K * ROW_CHUNKS, LANES), F32),
                        pltpu.VMEM((EXPERT_ROWS * ROW_CHUNKS, LANES), F32),
                        pltpu.SemaphoreType.DMA((2,)),
                        pltpu.SemaphoreType.DMA],
        compiler_params=_cparams(("arbitrary",)),
        name="moe_dispatch",
    )(tab, fill_tab, x2, mod4, mod4, meta)


def _expert_kernel(be_ref, first_ref, used_ref, xs_ref, wgu_ref, bgu_ref, wd_ref, bd_ref, ys_ref,
                   x_sc, wgu_sc, wd_sc):
    del be_ref
    g = pl.program_id(0)

    @pl.when(g < used_ref[0])
    def _():
        _expert_block(g, first_ref, xs_ref, wgu_ref, bgu_ref, wd_ref, bd_ref, ys_ref, x_sc, wgu_sc, wd_sc)

    @pl.when(g >= used_ref[0])
    def _():
        ys_ref[...] = jnp.zeros(ys_ref.shape, ys_ref.dtype)


def _expert_block(g, first_ref, xs_ref, wgu_ref, bgu_ref, wd_ref, bd_ref, ys_ref, x_sc, wgu_sc, wd_sc):
    rows = x_sc.shape[0]
    grp = 2 * LANES

    @pl.when(first_ref[g] == 1)
    def _():
        r = lax.broadcasted_iota(I32, (grp, grp), 0)
        c = lax.broadcasted_iota(I32, (grp, grp), 1)
        src = jnp.where(c < LANES, 2 * c, 2 * (c - LANES) + 1)
        sel = jnp.where(r == src, 1.0, 0.0).astype(BF16)
        for j in range(wgu_sc.shape[1] // grp):
            blk = wgu_ref[0, :, j * grp:(j + 1) * grp].astype(BF16)
            wgu_sc[:, j * grp:(j + 1) * grp] = jnp.dot(blk, sel, preferred_element_type=F32).astype(BF16)
        wd_sc[...] = wd_ref[0].astype(BF16)

    for c in range(ROW_CHUNKS):
        x_sc[:, c * LANES:(c + 1) * LANES] = xs_ref[pl.ds(c, rows, stride=ROW_CHUNKS), :].astype(BF16)
    gu = jnp.dot(x_sc[...], wgu_sc[...], preferred_element_type=F32) + bgu_ref[0]
    acts = []
    for j in range(gu.shape[1] // grp):
        glu = jnp.minimum(gu[:, j * grp:j * grp + LANES], SWIGLU_LIMIT)
        lin = jnp.clip(gu[:, j * grp + LANES:(j + 1) * grp], -SWIGLU_LIMIT, SWIGLU_LIMIT)
        acts.append((glu * _sigmoid(SWIGLU_ALPHA * glu) * (lin + 1.0)).astype(BF16))
    act = jnp.concatenate(acts, axis=1)
    y = jnp.dot(act, wd_sc[...], preferred_element_type=F32) + bd_ref[0]
    for c in range(ROW_CHUNKS):
        ys_ref[pl.ds(c, rows, stride=ROW_CHUNKS), :] = y[:, c * LANES:(c + 1) * LANES]


def _experts(xs, block_exp, first, n_used, layer, w_gu, b_gu_grouped, w_dn, b_dn):
    d = D_MODEL
    n_blocks = block_exp.shape[0]
    rb = EXPERT_ROWS
    last = lambda g, nu: jnp.minimum(g, nu[0] - 1)
    wmap = lambda g, be, fi, nu: (layer, be[last(g, nu)], 0, 0)
    bmap = lambda g, be, fi, nu: (be[last(g, nu)], 0, 0)
    return pl.pallas_call(
        _expert_kernel,
        grid_spec=pltpu.PrefetchScalarGridSpec(
            num_scalar_prefetch=3,
            grid=(n_blocks,),
            in_specs=[pl.BlockSpec((rb * ROW_CHUNKS, LANES), lambda g, be, fi, nu: (last(g, nu), 0)),
                      pl.BlockSpec((None, 1, d, 2 * d), wmap),
                      pl.BlockSpec((1, 1, 2 * d), bmap),
                      pl.BlockSpec((None, 1, d, d), wmap),
                      pl.BlockSpec((1, 1, d), bmap)],
            out_specs=pl.BlockSpec((rb * ROW_CHUNKS, LANES), lambda g, be, fi, nu: (g, 0)),
            scratch_shapes=[pltpu.VMEM((rb, d), BF16),
                            pltpu.VMEM((d, 2 * d), BF16),
                            pltpu.VMEM((d, d), BF16)]),
        out_shape=jax.ShapeDtypeStruct(xs.shape, F32),
        compiler_params=_cparams(("arbitrary",)),
        name="moe_experts",
    )(block_exp, first, n_used, xs, w_gu, b_gu_grouped, w_dn, b_dn)


def _combine_kernel(tab_ref, tab_next_ref, meta_ref, gate_ref, x_ref, g2_ref, lng_ref, lnb_ref, ys_hbm, o_ref,
                    yb0, yb1, sem):
    i = pl.program_id(0)
    n_steps = pl.num_programs(0)
    tm = x_ref.shape[0]
    nrow = tm * TOP_K
    bufs = (yb0, yb1)

    def fetch(t_ref, slot):
        _for_each_run(t_ref, lambda srow, trow, n, prio: pltpu.make_async_copy(
            _span_rows(ys_hbm, srow, n), _span_rows(bufs[slot], trow, n), sem.at[slot]).start(priority=prio))

    @pl.when(i == 0)
    def _():
        fetch(tab_ref, 0)

    for slot in range(2):
        @pl.when(i % 2 == slot)
        def _(slot=slot):
            buf = bufs[slot]

            @pl.when(i + 1 < n_steps)
            def _():
                fetch(tab_next_ref, 1 - slot)

            _wait_rows(ys_hbm, buf, sem.at[slot], nrow)
            yt = jnp.concatenate([buf[pl.ds(c, nrow, stride=ROW_CHUNKS), :] for c in range(ROW_CHUNKS)],
                                 axis=1).astype(BF16)
            meta = meta_ref[...]
            gates = gate_ref[...]
            col = lax.broadcasted_iota(I32, (1, nrow), 1)
            w = jnp.zeros((tm, nrow), F32)
            for k in range(TOP_K):
                w = w + jnp.where(col == meta[:, TOP_K + k:TOP_K + k + 1], gates[:, k:k + 1], 0.0)
            w_hi = w.astype(BF16)
            w_lo = (w - w_hi.astype(F32)).astype(BF16)
            y = (jnp.dot(w_hi, yt, preferred_element_type=F32) + jnp.dot(w_lo, yt, preferred_element_type=F32))
            z = DEEPNORM_ALPHA * x_ref[...] + g2_ref[0, 0] * y
            o_ref[...] = _ln(z) * lng_ref[...] + lnb_ref[...]


def _combine(ys, tab, meta, gates, x2, mod4, lng, lnb, seq):
    n, d = x2.shape
    tm = TOK_TILE
    per_b = seq // tm
    last = n // tm - 1
    return pl.pallas_call(
        _combine_kernel,
        grid=(n // tm,),
        in_specs=[pl.BlockSpec((SUBLANES, LANES), lambda i: (i, 0), memory_space=pltpu.SMEM),
                  pl.BlockSpec((SUBLANES, LANES), lambda i: (jnp.minimum(i + 1, last), 0),
                               memory_space=pltpu.SMEM),
                  pl.BlockSpec((tm, LANES), lambda i: (i, 0)),
                  pl.BlockSpec((tm, LANES), lambda i: (i, 0)),
                  pl.BlockSpec((tm, d), lambda i: (i, 0)),
                  pl.BlockSpec((1, 1, 1, d), lambda i: (i // per_b, 5, 0, 0)),
                  pl.BlockSpec((1, d), lambda i: (0, 0)),
                  pl.BlockSpec((1, d), lambda i: (0, 0)),
                  pl.BlockSpec(memory_space=pl.ANY)],
        out_specs=pl.BlockSpec((tm, d), lambda i: (i, 0)),
        out_shape=jax.ShapeDtypeStruct((n, d), F32),
        scratch_shapes=[pltpu.VMEM((tm * TOP_K * ROW_CHUNKS, LANES), F32),
                        pltpu.VMEM((tm * TOP_K * ROW_CHUNKS, LANES), F32),
                        pltpu.SemaphoreType.DMA((2,))],
        compiler_params=_cparams(("arbitrary",)),
        name="moe_combine",
    )(tab, tab, meta, gates, x2, mod4, lng, lnb, ys)


def _moe(x2, mod4, layer, rw, rb, w_gu, b_gu, w_dn, b_dn, lng, lnb, seq):
    n, d = x2.shape
    meta, gates, tile_meta, cnt = _router(x2, mod4, rw, rb, seq)
    counts = cnt[0, :N_EXPERTS]
    padded = -(-counts // EXPERT_ROWS) * EXPERT_ROWS
    pend = jnp.cumsum(padded)
    pstart = pend - padded
    n_blocks = -(-(n * TOP_K + N_EXPERTS * (EXPERT_ROWS - 1)) // EXPERT_ROWS)
    n_rows = n_blocks * EXPERT_ROWS
    block_row0 = jnp.arange(n_blocks, dtype=I32) * EXPERT_ROWS
    block_exp = jnp.minimum(jnp.sum((pend[None, :] <= block_row0[:, None]).astype(I32), axis=1), N_EXPERTS - 1)
    first = jnp.concatenate([jnp.ones((1,), I32), (block_exp[1:] != block_exp[:-1]).astype(I32)])
    n_used = (pend[-1:] // EXPERT_ROWS).astype(I32)
    n_tiles = n // TOK_TILE
    start_row = jnp.zeros((1, 1, LANES), I32).at[0, 0, :N_EXPERTS].set(pstart)
    sel_row0 = (jnp.arange(SUBLANES) == 0).astype(I32)[None, :, None]
    tab = (tile_meta.reshape(n_tiles, SUBLANES, LANES) + sel_row0 * start_row).reshape(n_tiles * SUBLANES, LANES)
    tail0 = pend[-1] + jnp.arange(N_FILL - N_EXPERTS, dtype=I32) * EXPERT_ROWS
    fill_start = jnp.concatenate([pstart + counts, tail0])
    fill_len = jnp.concatenate([padded - counts, jnp.clip(n_rows - tail0, 0, EXPERT_ROWS)])
    fill_tab = jnp.zeros((SUBLANES, LANES), I32).at[0, :N_FILL].set(fill_start).at[1, :N_FILL].set(fill_len)
    xs = _dispatch(x2, mod4, meta, tab, fill_tab, n_rows, seq)
    e = b_gu.shape[0]
    b_grouped = b_gu.reshape(e, d // LANES, LANES, 2).transpose(0, 1, 3, 2).reshape(e, 1, 2 * d)
    ys = _experts(xs, block_exp, first, n_used, layer, w_gu, b_grouped, w_dn, b_dn[:, None, :])
    return _combine(ys, tab, meta, gates, x2, mod4, lng.reshape(1, d), lnb.reshape(1, d), seq)


def _permute_w_in(w):
    d = w.shape[0]
    off = {}
    o = 0
    for name, wd in (("a_q", 256), ("a_k", 256), ("a_v", 256), ("b_q", 384), ("b_kc", 128), ("b_vc", 128),
                     ("b_ks", 128), ("b_vs", 128), ("b_kw", 128), ("b_vw", 128), ("b_gate", 18),
                     ("c_q", 384), ("c_k", 384), ("c_v", 384), ("merge_gate", 3 * D_MODEL)):
        off[name] = (o, o + wd)
        o += wd
    col = lambda name: w[:, off[name][0]:off[name][1]]
    scale = HEAD_DIM ** -0.5 * float(np.log2(np.e))
    b_q = col("b_q").reshape(d, B_KV_GROUPS, B_REP, HEAD_DIM).transpose(0, 2, 1, 3).reshape(d, B_HEADS * HEAD_DIM)
    gate_pad = jnp.zeros((d, LANES - B_HEADS * 3), w.dtype)
    parts = [col("merge_gate"),
             col("a_q") * scale, col("a_k"), b_q * scale, col("b_kc"), col("b_ks"), col("b_kw"),
             col("c_q") * scale, col("c_k"),
             col("a_v"), col("b_vc"), col("b_vs"), col("b_vw"), col("c_v"),
             col("b_gate"), gate_pad]
    return jnp.concatenate(parts, axis=1).astype(BF16)


def _dilate(arr, tile, batch, seq, dil):
    a = arr[:, tile * LANES:(tile + 1) * LANES].reshape(batch, seq // dil, dil, LANES)
    return a.transpose(0, 2, 1, 3).reshape(batch * seq, LANES)


def _undilate(arr, batch, seq, dil):
    a = arr.reshape(batch, dil, seq // dil, LANES)
    return a.transpose(0, 2, 1, 3).reshape(batch * seq, LANES)


def _mixer_layer(x2, mod4, cos_t, sin_t, batch, seq, w_in, cmp_w1_k, cmp_w2_k, cmp_pe_k,
                 cmp_w1_v, cmp_w2_v, cmp_pe_v, w_a, w_b, w_c, w_out, lng, lnb):
    d = D_MODEL
    proj = _inproj(x2, mod4, _permute_w_in(w_in), cos_t, sin_t, seq)

    bias_a = _moba_select(proj, _kmean(proj, batch, seq), batch, seq)
    o_a = _bb_attn(proj, bias_a, batch=batch, seq=seq, n_pairs=A_HEADS // 2,
                   q_tile=lambda p: T_AQ + p, k_tile=lambda p: T_AK + p, v_tile=lambda p: T_AV + p,
                   bias_tile=lambda p: p, blk=MOBA_BLOCK)

    kc = _compress(proj[:, T_BKC * LANES:(T_BKC + 1) * LANES], cmp_w1_k, cmp_w2_k, cmp_pe_k, batch, seq)
    vc = _compress(proj[:, T_BVC * LANES:(T_BVC + 1) * LANES], cmp_w1_v, cmp_w2_v, cmp_pe_v, batch, seq)
    o_cmp, bias_b = _nsa_cmp(proj, kc, vc, batch, seq)
    o_slc = _bb_attn(proj, bias_b, batch=batch, seq=seq, n_pairs=B_REP,
                     q_tile=lambda p: T_BQ + p, k_tile=lambda p: T_BKS, v_tile=lambda p: T_BVS,
                     bias_tile=lambda p: 0, blk=SLC_BLOCK)
    o_win, _ = _band_attn(proj, proj, proj, batch=batch, seq=seq, q_tiles=[T_BQ + r for r in range(B_REP)],
                          k_tile=T_BKW, v_tile=T_BVW, window=NSA_WINDOW, t=256, group=2)

    oc, lse = [], []
    for gi, (window, dil) in enumerate(DILATED_PAIRS):
        wlen = window // dil + 1
        if dil == 1:
            o, l = _band_attn(proj, proj, proj, batch=batch, seq=seq, q_tiles=[T_CQ + gi], k_tile=T_CK + gi,
                              v_tile=T_CV + gi, window=wlen, t=256, group=4)
        else:
            qd = _dilate(proj, T_CQ + gi, batch, seq, dil)
            kd = _dilate(proj, T_CK + gi, batch, seq, dil)
            vd = _dilate(proj, T_CV + gi, batch, seq, dil)
            o, l = _band_attn(qd, kd, vd, batch=batch * dil, seq=seq // dil, q_tiles=[0], k_tile=0, v_tile=0,
                              window=wlen, t=256, group=4)
            o, l = _undilate(o, batch, seq, dil), _undilate(l, batch, seq, dil)
        oc.append(o)
        lse.append(l)

    w_b_perm = w_b.reshape(B_KV_GROUPS, B_REP, HEAD_DIM, d).transpose(1, 0, 2, 3).reshape(B_HEADS * HEAD_DIM, d)
    return _merge(o_a, o_cmp, o_slc, o_win, proj, oc, lse, x2, mod4,
                  w_a.astype(BF16), w_b_perm.astype(BF16), w_c.astype(BF16), w_out.astype(BF16),
                  lng.reshape(1, d), lnb.reshape(1, d), seq)


def _rope_tables(positions):
    inv = ROPE_THETA ** (-jnp.arange(0, HEAD_DIM, 2, dtype=F32) / HEAD_DIM)
    ang = positions.astype(F32).reshape(-1, 1) * inv[None, :]
    cos, sin = jnp.cos(ang), jnp.sin(ang)
    return jnp.tile(cos, (1, 4)), jnp.tile(jnp.concatenate([-sin, sin], axis=1), (1, 2))


def kernel(x, c, positions, w_ada, b_ada, w_in, cmp_w1_k, cmp_w2_k, cmp_pe_k, cmp_w1_v, cmp_w2_v, cmp_pe_v, w_branch_a, w_branch_b, w_branch_c, w_out, ln1_g, ln1_b, router_w, router_b, w_gate_up, b_gate_up, w_down, b_down, ln2_g, ln2_b):
    batch, seq, d = x.shape
    depth = w_in.shape[0]
    cos_t, sin_t = _rope_tables(positions)
    mod = _ada(c, w_ada, b_ada)
    x2 = x.reshape(batch * seq, d)
    for l in range(depth):
        mod4 = mod[l].reshape(batch, N_ADA, 1, d)
        x2 = _mixer_layer(x2, mod4, cos_t, sin_t, batch, seq, w_in[l], cmp_w1_k[l], cmp_w2_k[l], cmp_pe_k[l],
                          cmp_w1_v[l], cmp_w2_v[l], cmp_pe_v[l], w_branch_a[l], w_branch_b[l],
                          w_branch_c[l], w_out[l], ln1_g[l], ln1_b[l])
        x2 = _moe(x2, mod4, l, router_w[l], router_b[l], w_gate_up, b_gate_up[l], w_down, b_down[l],
                  ln2_g[l], ln2_b[l], seq)
    return x2.reshape(batch, seq, d)
```

```python
import functools

import numpy as np
import jax
import jax.numpy as jnp
from jax import lax
from jax.experimental import pallas as pl
from jax.experimental.pallas import tpu as pltpu

F32 = jnp.float32
BF16 = jnp.bfloat16
I32 = jnp.int32
HIGHEST = lax.Precision.HIGHEST

D_MODEL = 1024
DEPTH = 2
HEAD_DIM = 64
ROPE_THETA = 10000.0
LN_EPS = 1e-5
DEEPNORM_ALPHA = (2 * DEPTH) ** 0.25
N_ADA = 6
A_HEADS = 4
MOBA_BLOCK = 256
MOBA_TOPK = 3
B_HEADS = 6
B_KV_GROUPS = 2
B_REP = 3
CMP_BLOCK = 32
CMP_STRIDE = 16
CMP_HIDDEN = 128
SLC_BLOCK = 64
SLC_TOPK = 16
NSA_WINDOW = 512
FORCE_SCORE = 1e6
DILATED_PAIRS = ((128, 1), (512, 4), (2048, 16))
N_EXPERTS = 32
TOP_K = 4
SWIGLU_ALPHA = 1.702
SWIGLU_LIMIT = 7.0

LANES = 128
SUBLANES = 8
HALF = LANES // 2
NEG = -1e30
ROW_CHUNKS = D_MODEL // LANES
VMEM_LIMIT = 56 * 1024 * 1024

T_MG = 0
T_ROPE0, T_ROPE1 = 24, 40
T_AQ, T_AK, T_BQ, T_BKC, T_BKS, T_BKW, T_CQ, T_CK = 24, 26, 28, 31, 32, 33, 34, 37
T_AV, T_BVC, T_BVS, T_BVW, T_CV, T_BG = 40, 42, 43, 44, 45, 48
N_PROJ_TILES = 49
PROJ_W = N_PROJ_TILES * LANES

EXPERT_ROWS = 512
TOK_TILE = 256
N_FILL = 2 * N_EXPERTS + 2


def _cparams(sem, **kw):
    return pltpu.CompilerParams(dimension_semantics=sem, vmem_limit_bytes=VMEM_LIMIT, **kw)


def _lane_iota(shape=(1, LANES)):
    return lax.broadcasted_iota(I32, shape, len(shape) - 1)


def _sigmoid(x):
    return 1.0 / (1.0 + jnp.exp(-x))


def _sigmoid_tanh(x):
    return 0.5 * jnp.tanh(0.5 * x) + 0.5


def _ln(x):
    mu = jnp.mean(x, axis=-1, keepdims=True)
    xc = x - mu
    var = jnp.mean(xc * xc, axis=-1, keepdims=True)
    return xc * lax.rsqrt(var + LN_EPS)


def _dot_nt(a, b, precision=None):
    return lax.dot_general(a, b, (((1,), (1,)), ((), ())), precision=precision,
                           preferred_element_type=F32)


def _ada_kernel(c_ref, w_ref, b_ref, o_ref):
    c = c_ref[...]
    cond = c * _sigmoid(c)
    o_ref[0] = jnp.dot(cond, w_ref[0], precision=HIGHEST, preferred_element_type=F32) + b_ref[0]


def _ada(c, w_ada, b_ada):
    depth, d, n = w_ada.shape
    b = c.shape[0]
    tn = 1536
    return pl.pallas_call(
        _ada_kernel,
        grid=(depth, n // tn),
        in_specs=[pl.BlockSpec((b, d), lambda l, j: (0, 0)),
                  pl.BlockSpec((1, d, tn), lambda l, j: (l, 0, j)),
                  pl.BlockSpec((1, 1, tn), lambda l, j: (l, 0, j))],
        out_specs=pl.BlockSpec((1, b, tn), lambda l, j: (l, 0, j)),
        out_shape=jax.ShapeDtypeStruct((depth, b, n), F32),
        compiler_params=_cparams(("parallel", "parallel")),
        name="ada",
    )(c, w_ada, b_ada.reshape(depth, 1, n))


def _inproj_kernel(x_ref, sc_ref, sh_ref, w_ref, cos_ref, sin_ref, o_ref, *, chunks):
    h = _ln(x_ref[...]) * (1.0 + sc_ref[0, 0]) + sh_ref[0, 0]
    hb = h.astype(BF16)
    first_half = (_lane_iota() & (HEAD_DIM - 1)) < (HEAD_DIM // 2)
    for c0, cw, rope in chunks:
        acc = jnp.dot(hb, w_ref[:, c0:c0 + cw], preferred_element_type=F32)
        if rope:
            cos = cos_ref[...]
            sin = sin_ref[...]
            for t in range(cw // LANES):
                a = acc[:, t * LANES:(t + 1) * LANES]
                rot = jnp.where(first_half, pltpu.roll(a, LANES - HEAD_DIM // 2, 1),
                                pltpu.roll(a, HEAD_DIM // 2, 1))
                o_ref[:, c0 + t * LANES:c0 + (t + 1) * LANES] = (a * cos + rot * sin).astype(o_ref.dtype)
        else:
            o_ref[:, c0:c0 + cw] = acc.astype(o_ref.dtype)


def _inproj_chunks():
    chunks = []
    for lo, hi, rope in ((0, T_ROPE0, False), (T_ROPE0, T_ROPE1, True), (T_ROPE1, N_PROJ_TILES, False)):
        c = lo * LANES
        while c < hi * LANES:
            cw = min(512, hi * LANES - c)
            chunks.append((c, cw, rope))
            c += cw
    return tuple(chunks)


def _inproj(x2, mod4, w_perm, cos_t, sin_t, seq):
    n, d = x2.shape
    tm = 512
    per_b = seq // tm
    return pl.pallas_call(
        functools.partial(_inproj_kernel, chunks=_inproj_chunks()),
        grid=(n // tm,),
        in_specs=[pl.BlockSpec((tm, d), lambda i: (i, 0)),
                  pl.BlockSpec((1, 1, 1, d), lambda i: (i // per_b, 1, 0, 0)),
                  pl.BlockSpec((1, 1, 1, d), lambda i: (i // per_b, 0, 0, 0)),
                  pl.BlockSpec((d, PROJ_W), lambda i: (0, 0)),
                  pl.BlockSpec((tm, LANES), lambda i: (i, 0)),
                  pl.BlockSpec((tm, LANES), lambda i: (i, 0))],
        out_specs=pl.BlockSpec((tm, PROJ_W), lambda i: (i, 0)),
        out_shape=jax.ShapeDtypeStruct((n, PROJ_W), BF16),
        compiler_params=_cparams(("parallel",)),
        name="inproj",
    )(x2, mod4, mod4, w_perm, cos_t, sin_t)


def _bb_attn_kernel(q_ref, k_ref, v_ref, b_ref, o_ref, qa_sc, s_sc, mp_sc, m_sc, acc_sc,
                    *, blk_shift, tq, tu, big):
    i = pl.program_id(2)
    n_diag = tq // tu
    lane = _lane_iota()
    lo = lane < HALF
    q = q_ref[...]
    b = b_ref[...]
    qa_sc[0] = jnp.where(lo, q, b)
    qa_sc[1] = jnp.where(lo, b, q)
    mp_sc[...] = jnp.full(mp_sc.shape, -jnp.inf, F32)
    acc_sc[...] = jnp.zeros(acc_sc.shape, F32)

    def scores(u0, n_units, diagonal_last):
        tk = n_units * tu
        start = pl.multiple_of(u0 * tu, tu)
        k = k_ref[pl.ds(start, tk), :]
        kblk = (start + lax.broadcasted_iota(I32, (tk, 1), 0)) >> blk_shift
        onehot = jnp.where(kblk == (lane & (HALF - 1)), 1.0, 0.0).astype(BF16)
        ka = (jnp.where(lo, k, onehot), jnp.where(lo, onehot, k))
        row = lax.broadcasted_iota(I32, (tq, 1), 0)
        col = lax.broadcasted_iota(I32, (1, tu), 1)
        for h in range(2):
            s = _dot_nt(qa_sc[h], ka[h])
            mp = mp_sc[h]
            for u in range(n_units):
                su = s[:, u * tu:(u + 1) * tu]
                d = u - (n_units - n_diag)
                if diagonal_last and d >= 0:
                    su = jnp.where(col + d * tu <= row, su, NEG)
                s_sc[h, u0 + u] = su
                for c in range(tu // LANES):
                    mp = jnp.maximum(mp, su[:, c * LANES:(c + 1) * LANES])
            mp_sc[h] = mp

    def weighted(u0, n_units):
        tk = n_units * tu
        start = pl.multiple_of(u0 * tu, tu)
        v = v_ref[pl.ds(start, tk), :]
        one = jnp.ones_like(v)
        vs = (jnp.where(lo, v, one), jnp.where(lo, one, v))
        for h in range(2):
            m_row = m_sc[h]
            ps = []
            for u in range(n_units):
                su = s_sc[h, u0 + u]
                for c in range(tu // LANES):
                    ps.append(jnp.exp2(su[:, c * LANES:(c + 1) * LANES] - m_row).astype(BF16))
            acc_sc[h] = acc_sc[h] + jnp.dot(jnp.concatenate(ps, axis=1), vs[h], preferred_element_type=F32)

    n_full = i * n_diag
    n_big = n_full // big
    n_tail = n_full - n_big * big + n_diag

    def run(fn):
        def big_body(t, c):
            fn(t * big, big, False) if fn is scores else fn(t * big, big)
            return c

        lax.fori_loop(0, n_big, big_body, 0)
        for n in range(n_diag, big + n_diag, n_diag):
            @pl.when(n_tail == n)
            def _(n=n):
                fn(n_big * big, n, True) if fn is scores else fn(n_big * big, n)

    run(scores)
    for h in range(2):
        m_sc[h] = jnp.broadcast_to(jnp.max(mp_sc[h], axis=1, keepdims=True), (tq, LANES))
    run(weighted)
    outs = [acc_sc[h] / pltpu.roll(acc_sc[h], HALF, 1) for h in range(2)]
    o_ref[...] = jnp.where(lo, outs[0], outs[1]).astype(o_ref.dtype)


def _bb_attn(proj, bias, *, batch, seq, n_pairs, q_tile, k_tile, v_tile, bias_tile, blk):
    t, tu = 512, 256
    nq = seq // t
    n = batch * seq
    kern = functools.partial(_bb_attn_kernel, blk_shift=int(np.log2(blk)), tq=t, tu=tu, big=4)
    return pl.pallas_call(
        kern,
        grid=(batch, n_pairs, nq),
        in_specs=[pl.BlockSpec((t, LANES), lambda b, p, i: (b * nq + i, q_tile(p))),
                  pl.BlockSpec((seq, LANES), lambda b, p, i: (b, k_tile(p))),
                  pl.BlockSpec((seq, LANES), lambda b, p, i: (b, v_tile(p))),
                  pl.BlockSpec((t, LANES), lambda b, p, i: (b * nq + i, bias_tile(p)))],
        out_specs=pl.BlockSpec((t, LANES), lambda b, p, i: (b * nq + i, p)),
        out_shape=jax.ShapeDtypeStruct((n, n_pairs * LANES), BF16),
        scratch_shapes=[pltpu.VMEM((2, t, LANES), BF16),
                        pltpu.VMEM((2, seq // tu, t, tu), F32),
                        pltpu.VMEM((2, t, LANES), F32),
                        pltpu.VMEM((2, t, LANES), F32),
                        pltpu.VMEM((2, t, LANES), F32)],
        compiler_params=_cparams(("parallel", "parallel", "parallel")),
        name="bb_attn_%d" % blk,
    )(proj, proj, proj, bias)


def _band_kernel(*refs, tq, kw, window, seq, n_q, group, tiles_per_seq):
    q_refs = refs[:n_q]
    k_ref, v_ref, o_ref, lse_ref = refs[n_q:n_q + 4]
    i = pl.program_id(1)
    lane = _lane_iota()
    lo = lane < HALF
    for g in range(group):
        if tiles_per_seq == 1:
            tile, base = 0, g * seq
        else:
            tile, base = i * group + g, 0
        start = jnp.clip(tile * tq - (kw - tq), 0, seq - kw)
        k = k_ref[pl.ds(pl.multiple_of(base + start, LANES), kw), :]
        v = v_ref[pl.ds(pl.multiple_of(base + start, LANES), kw), :]
        rel = (tile * tq + lax.broadcasted_iota(I32, (tq, 1), 0)) - (start + lax.broadcasted_iota(I32, (1, kw), 1))
        ok = (rel >= 0) & (rel < window)
        for qi in range(n_q):
            q = q_refs[qi][g * tq:(g + 1) * tq, :]
            zero = jnp.zeros_like(q)
            qs = (jnp.where(lo, q, zero), jnp.where(lo, zero, q))
            outs, lses = [], []
            for h in range(2):
                s = jnp.where(ok, _dot_nt(qs[h], k), NEG)
                m = jnp.max(s, axis=1, keepdims=True)
                p = jnp.exp2(s - m)
                l = jnp.sum(p, axis=1, keepdims=True)
                outs.append(jnp.dot(p.astype(BF16), v, preferred_element_type=F32) / l)
                lses.append(m + jnp.log2(l))
            rows, cols = slice(g * tq, (g + 1) * tq), slice(qi * LANES, (qi + 1) * LANES)
            o_ref[rows, cols] = jnp.where(lo, outs[0], outs[1]).astype(o_ref.dtype)
            lse_ref[rows, cols] = jnp.where(lo, lses[0], lses[1])


def _band_attn(q_arr, k_arr, v_arr, *, batch, seq, q_tiles, k_tile, v_tile, window, t, group):
    t = min(t, seq)
    tiles_per_seq = seq // t
    n = batch * seq
    n_q = len(q_tiles)
    if tiles_per_seq == 1:
        group = min(group, batch)
        grid = (batch // group, 1)
        kv_rows = group * seq
        row_blk = lambda b, i: b
    else:
        group = min(group, tiles_per_seq)
        grid = (batch, tiles_per_seq // group)
        kv_rows = seq
        row_blk = lambda b, i: b * (tiles_per_seq // group) + i
    kw = min(seq, t + -(-(window - 1) // LANES) * LANES)
    kern = functools.partial(_band_kernel, tq=t, kw=kw, window=window, seq=seq, n_q=n_q, group=group,
                             tiles_per_seq=tiles_per_seq)
    qspecs = [pl.BlockSpec((group * t, LANES), (lambda b, i, c=c: (row_blk(b, i), c))) for c in q_tiles]
    return pl.pallas_call(
        kern,
        grid=grid,
        in_specs=qspecs + [pl.BlockSpec((kv_rows, LANES), lambda b, i: (b, k_tile)),
                           pl.BlockSpec((kv_rows, LANES), lambda b, i: (b, v_tile))],
        out_specs=[pl.BlockSpec((group * t, n_q * LANES), lambda b, i: (row_blk(b, i), 0)),
                   pl.BlockSpec((group * t, n_q * LANES), lambda b, i: (row_blk(b, i), 0))],
        out_shape=[jax.ShapeDtypeStruct((n, n_q * LANES), BF16),
                   jax.ShapeDtypeStruct((n, n_q * LANES), F32)],
        compiler_params=_cparams(("parallel", "parallel")),
        name="band_attn_%d" % window,
    )(*([q_arr] * n_q), k_arr, v_arr)


def _kmean_kernel(k_ref, o_ref, *, nb):
    s = k_ref.shape[0]
    blk = lax.broadcasted_iota(I32, (nb, s), 1) >> int(np.log2(MOBA_BLOCK))
    avg = jnp.where(blk == lax.broadcasted_iota(I32, (nb, s), 0), 1.0 / MOBA_BLOCK, 0.0).astype(BF16)
    o_ref[0] = jnp.dot(avg, k_ref[...], preferred_element_type=F32)


def _kmean(proj, batch, seq):
    nb = seq // MOBA_BLOCK
    w = A_HEADS * HEAD_DIM
    return pl.pallas_call(
        functools.partial(_kmean_kernel, nb=nb),
        grid=(batch,),
        in_specs=[pl.BlockSpec((seq, w), lambda b: (b, T_AK * LANES // w))],
        out_specs=pl.BlockSpec((1, nb, w), lambda b: (b, 0, 0)),
        out_shape=jax.ShapeDtypeStruct((batch, nb, w), F32),
        compiler_params=_cparams(("parallel",)),
        name="moba_kmean",
    )(proj)


def _rank_desc(g, n):
    gs = min(SUBLANES, n)
    groups = [g[r0:r0 + gs, :] for r0 in range(0, n, gs)]
    ranks = [jnp.zeros(x.shape, I32) for x in groups]
    sub = lax.broadcasted_iota(I32, (gs, 1), 0)
    for m in range(n):
        c = g[m:m + 1, :]
        for gi, x in enumerate(groups):
            if gi * gs > m:
                beats = c >= x
            elif gi * gs + gs - 1 <= m:
                beats = c > x
            else:
                beats = (c > x) | ((c == x) & (sub + gi * gs > m))
            ranks[gi] = ranks[gi] + jnp.where(beats, 1, 0)
    return jnp.concatenate(ranks, axis=0)


def _moba_sel_kernel(*refs, nb, n_sel, n_pairs):
    q_refs = refs[:n_pairs]
    km_ref, b_ref = refs[n_pairs:]
    i = pl.program_id(1)
    t = b_ref.shape[0]
    lo = _lane_iota() < HALF
    pad = jnp.zeros((HALF - nb, LANES), F32)
    n_idx = lax.broadcasted_iota(I32, (nb, 1), 0)
    valid = n_idx < i
    for p in range(n_pairs):
        q = q_refs[p][...].astype(F32)
        km = km_ref[0, :, p * LANES:(p + 1) * LANES]
        zero = jnp.zeros_like(km)
        kmt = jnp.concatenate([jnp.where(lo, zero, km), pad, jnp.where(lo, km, zero), pad], axis=0)
        gt = _dot_nt(kmt, q, precision=HIGHEST)
        rows = []
        for r0 in (0, HALF):
            g = jnp.where(valid, gt[r0:r0 + nb, :], -jnp.inf)
            rank = _rank_desc(g, nb)
            allowed = (valid & (rank < n_sel)) | (n_idx == i)
            rows.append(jnp.where(allowed, 0.0, NEG))
            rows.append(jnp.zeros((HALF - nb, t), F32))
        b_ref[:, p * LANES:(p + 1) * LANES] = jnp.concatenate(rows, axis=0).T.astype(b_ref.dtype)


def _moba_select(proj, kmean, batch, seq):
    t = MOBA_BLOCK
    nq = seq // t
    nb = seq // MOBA_BLOCK
    n_sel = min(MOBA_TOPK, nb - 1)
    n_pairs = A_HEADS // 2
    qspecs = [pl.BlockSpec((t, LANES), (lambda b, i, p=p: (b * nq + i, T_AQ + p))) for p in range(n_pairs)]
    return pl.pallas_call(
        functools.partial(_moba_sel_kernel, nb=nb, n_sel=n_sel, n_pairs=n_pairs),
        grid=(batch, nq),
        in_specs=qspecs + [pl.BlockSpec((1, nb, n_pairs * LANES), lambda b, i: (b, 0, 0))],
        out_specs=pl.BlockSpec((t, n_pairs * LANES), lambda b, i: (b * nq + i, 0)),
        out_shape=jax.ShapeDtypeStruct((batch * seq, n_pairs * LANES), BF16),
        compiler_params=_cparams(("parallel", "parallel")),
        name="moba_select",
    )(*([proj] * n_pairs), kmean)


def _compress_kernel(x_ref, w1_ref, pe_ref, w1f_ref, w2_ref, o_ref):
    x = x_ref[0]
    nblk = x.shape[0]
    outs = []
    for g in range(B_KV_GROUPS):
        u = jnp.dot(x, w1_ref[g, 0], preferred_element_type=F32)
        v = jnp.dot(x, w1_ref[g, 1], preferred_element_type=F32)
        pe_h = jnp.dot(pe_ref[...], w1f_ref[...], precision=HIGHEST, preferred_element_type=F32)
        hid = u + pltpu.roll(v, nblk - 1, 0) + pe_h[:1]
        hid = hid * _sigmoid(hid)
        outs.append(jnp.dot(hid, w2_ref[g], precision=HIGHEST, preferred_element_type=F32))
    o = outs[0] + outs[1]
    rows = lax.broadcasted_iota(I32, (nblk, 1), 0)
    o_ref[0] = jnp.where(rows < nblk - 1, o, 0.0).astype(o_ref.dtype)


def _compress(xt, w1, w2, pe, batch, seq):
    nblk = seq // CMP_STRIDE
    xg = xt.reshape(batch, nblk, CMP_STRIDE * LANES)
    w1r = w1.reshape(2, CMP_STRIDE, HEAD_DIM, CMP_HIDDEN)
    w1e = jnp.zeros((B_KV_GROUPS, 2, CMP_STRIDE, B_KV_GROUPS, HEAD_DIM, CMP_HIDDEN), F32)
    for g in range(B_KV_GROUPS):
        w1e = w1e.at[g, :, :, g].set(w1r)
    w1e = w1e.reshape(B_KV_GROUPS, 2, CMP_STRIDE * LANES, CMP_HIDDEN).astype(BF16)
    pe_flat = jnp.broadcast_to(pe.reshape(1, CMP_BLOCK * HEAD_DIM), (SUBLANES, CMP_BLOCK * HEAD_DIM))
    w2e = jnp.zeros((B_KV_GROUPS, CMP_HIDDEN, LANES), F32)
    for g in range(B_KV_GROUPS):
        w2e = w2e.at[g, :, g * HEAD_DIM:(g + 1) * HEAD_DIM].set(w2)
    return pl.pallas_call(
        _compress_kernel,
        grid=(batch,),
        in_specs=[pl.BlockSpec((1, nblk, CMP_STRIDE * LANES), lambda b: (b, 0, 0)),
                  pl.BlockSpec(w1e.shape, lambda b: (0, 0, 0, 0)),
                  pl.BlockSpec(pe_flat.shape, lambda b: (0, 0)),
                  pl.BlockSpec(w1.shape, lambda b: (0, 0)),
                  pl.BlockSpec(w2e.shape, lambda b: (0, 0, 0))],
        out_specs=pl.BlockSpec((1, nblk, LANES), lambda b: (b, 0, 0)),
        out_shape=jax.ShapeDtypeStruct((batch, nblk, LANES), BF16),
        compiler_params=_cparams(("parallel",)),
        name="nsa_compress",
    )(xg, w1e, pe_flat, w1, w2e)


def _nsa_cmp_kernel(q0_ref, q1_ref, q2_ref, kc_ref, vc_ref, ov_ref, o_ref, b_ref, *, n_sel, ns):
    i = pl.program_id(1)
    t = q0_ref.shape[0]
    ncp = kc_ref.shape[1]
    lane = _lane_iota()
    lo = lane < HALF
    kc = kc_ref[0]
    vc = vc_ref[0]
    tq = i * t + lax.broadcasted_iota(I32, (t, 1), 0)
    cmp_end = lax.broadcasted_iota(I32, (1, ncp), 1) * CMP_STRIDE + (CMP_BLOCK - 1)
    vis = cmp_end <= tq
    psum = [jnp.zeros((t, ncp), F32) for _ in range(B_KV_GROUPS)]
    for r, q_ref in enumerate((q0_ref, q1_ref, q2_ref)):
        q = q_ref[...]
        zero = jnp.zeros_like(q)
        outs = []
        for g in range(B_KV_GROUPS):
            qg = jnp.where(lo, q, zero) if g == 0 else jnp.where(lo, zero, q)
            s = jnp.where(vis, _dot_nt(qg, kc), NEG)
            m = jnp.max(s, axis=1, keepdims=True)
            m = jnp.where(m > 0.5 * NEG, m, 0.0)
            e = jnp.exp2(s - m)
            p = e / jnp.maximum(jnp.sum(e, axis=1, keepdims=True), 1e-30)
            psum[g] = psum[g] + p
            outs.append(jnp.dot(p.astype(BF16), vc, preferred_element_type=F32))
        o_ref[:, r * LANES:(r + 1) * LANES] = jnp.where(lo, outs[0], outs[1]).astype(o_ref.dtype)
    n_idx = lax.broadcasted_iota(I32, (ns, 1), 0)
    tcol = i * t + lax.broadcasted_iota(I32, (1, t), 1)
    qblk = tcol >> int(np.log2(SLC_BLOCK))
    forced = (n_idx == 0) | (n_idx == qblk) | (n_idx == qblk - 1)
    valid = n_idx <= qblk
    rows = []
    for g in (1, 0):
        imp = _dot_nt(ov_ref[...], psum[g], precision=HIGHEST)
        imp = jnp.where(forced, FORCE_SCORE, imp)
        imp = jnp.where(valid, imp, -jnp.inf)
        rank = _rank_desc(imp, ns)
        allowed = valid & (rank < n_sel)
        rows.append(jnp.where(allowed, 0.0, NEG))
        if ns < HALF:
            rows.append(jnp.zeros((HALF - ns, t), F32))
    b_ref[...] = jnp.concatenate(rows, axis=0).T.astype(b_ref.dtype)


def _nsa_cmp(proj, kc, vc, batch, seq):
    t = 256
    nq = seq // t
    ns = seq // SLC_BLOCK
    n_sel = min(SLC_TOPK, ns)
    ncp = seq // CMP_STRIDE
    cs = np.arange(ncp)[None, :] * CMP_STRIDE
    ss = np.arange(ns)[:, None] * SLC_BLOCK
    ov = ((cs < ss + SLC_BLOCK) & (cs + CMP_BLOCK > ss)).astype(np.float32)
    ov[:, ncp - 1] = 0.0
    n = batch * seq
    qspec = [pl.BlockSpec((t, LANES), (lambda b, i, r=r: (b * nq + i, T_BQ + r))) for r in range(B_REP)]
    return pl.pallas_call(
        functools.partial(_nsa_cmp_kernel, n_sel=n_sel, ns=ns),
        grid=(batch, nq),
        in_specs=qspec + [pl.BlockSpec((1, ncp, LANES), lambda b, i: (b, 0, 0)),
                          pl.BlockSpec((1, ncp, LANES), lambda b, i: (b, 0, 0)),
                          pl.BlockSpec((ns, ncp), lambda b, i: (0, 0))],
        out_specs=[pl.BlockSpec((t, B_REP * LANES), lambda b, i: (b * nq + i, 0)),
                   pl.BlockSpec((t, LANES), lambda b, i: (b * nq + i, 0))],
        out_shape=[jax.ShapeDtypeStruct((n, B_REP * LANES), BF16),
                   jax.ShapeDtypeStruct((n, LANES), BF16)],
        compiler_params=_cparams(("parallel", "parallel")),
        name="nsa_cmp_select",
    )(proj, proj, proj, kc, vc, jnp.asarray(ov))


def _merge_kernel(oa_ref, ocmp_ref, oslc_ref, owin_ref, bg_ref, oc0_ref, oc1_ref, oc2_ref,
                  l0_ref, l1_ref, l2_ref, mg_ref, x_ref, g1_ref, wa_ref, wb_ref, wc_ref, wo_ref,
                  eg_ref, lng_ref, lnb_ref, o_ref):
    d = D_MODEL
    sg = _sigmoid(bg_ref[...].astype(F32))
    sg_hi = sg.astype(BF16)
    sg_lo = (sg - sg_hi.astype(F32)).astype(BF16)
    ob = jnp.zeros(ocmp_ref.shape, F32)
    for br, ref in enumerate((ocmp_ref, oslc_ref, owin_ref)):
        gexp = (jnp.dot(sg_hi, eg_ref[br], preferred_element_type=F32)
                + jnp.dot(sg_lo, eg_ref[br], preferred_element_type=F32))
        ob = ob + gexp * ref[...].astype(F32)
    l0, l1, l2 = l0_ref[...], l1_ref[...], l2_ref[...]
    mx = jnp.maximum(jnp.maximum(l0, l1), l2)
    e0, e1, e2 = jnp.exp2(l0 - mx), jnp.exp2(l1 - mx), jnp.exp2(l2 - mx)
    den = e0 + e1 + e2
    oc = ((e0 / den) * oc0_ref[...].astype(F32) + (e1 / den) * oc1_ref[...].astype(F32)
          + (e2 / den) * oc2_ref[...].astype(F32))
    pa = jnp.dot(oa_ref[...], wa_ref[...], preferred_element_type=F32)
    pb = jnp.dot(ob.astype(BF16), wb_ref[...], preferred_element_type=F32)
    pc = jnp.dot(oc.astype(BF16), wc_ref[...], preferred_element_type=F32)
    merged = (_sigmoid_tanh(mg_ref[:, 0:d].astype(F32)) * pa
              + _sigmoid_tanh(mg_ref[:, d:2 * d].astype(F32)) * pb
              + _sigmoid_tanh(mg_ref[:, 2 * d:3 * d].astype(F32)) * pc)
    y = jnp.dot(merged.astype(BF16), wo_ref[...], preferred_element_type=F32)
    z = DEEPNORM_ALPHA * x_ref[...] + g1_ref[0, 0] * y
    o_ref[...] = _ln(z) * lng_ref[...] + lnb_ref[...]


def _gate_expand():
    eg = np.zeros((3, LANES, B_HEADS * HEAD_DIM), np.float32)
    for g in range(B_KV_GROUPS):
        for r in range(B_REP):
            for br in range(3):
                c0 = (r * B_KV_GROUPS + g) * HEAD_DIM
                eg[br, (g * B_REP + r) * 3 + br, c0:c0 + HEAD_DIM] = 1.0
    return jnp.asarray(eg)


def _merge(o_a, o_cmp, o_slc, o_win, proj, oc, lse, x2, mod4, wa, wb, wc, wo, lng, lnb, seq):
    n, d = x2.shape
    tm = 256
    per_b = seq // tm
    row = lambda w: pl.BlockSpec((tm, w), lambda i: (i, 0))
    full = lambda a: pl.BlockSpec(a.shape, lambda i: (0,) * a.ndim)
    eg = _gate_expand().astype(BF16)
    return pl.pallas_call(
        _merge_kernel,
        grid=(n // tm,),
        in_specs=[row(o_a.shape[1]), row(o_cmp.shape[1]), row(o_slc.shape[1]), row(o_win.shape[1]),
                  pl.BlockSpec((tm, LANES), lambda i: (i, T_BG)),
                  row(LANES), row(LANES), row(LANES), row(LANES), row(LANES), row(LANES),
                  pl.BlockSpec((tm, 3 * d), lambda i: (i, T_MG)),
                  row(d),
                  pl.BlockSpec((1, 1, 1, d), lambda i: (i // per_b, 2, 0, 0)),
                  full(wa), full(wb), full(wc), full(wo), full(eg), full(lng), full(lnb)],
        out_specs=row(d),
        out_shape=jax.ShapeDtypeStruct((n, d), F32),
        compiler_params=_cparams(("parallel",)),
        name="merge_out",
    )(o_a, o_cmp, o_slc, o_win, proj, oc[0], oc[1], oc[2], lse[0], lse[1], lse[2],
      proj, x2, mod4, wa, wb, wc, wo, eg, lng, lnb)


def _router_kernel(x_ref, sc_ref, sh_ref, rw_ref, rb_ref, meta_ref, gate_ref, tile_ref, cnt_ref, carry_sc):
    tm = x_ref.shape[0]
    ne = N_EXPERTS

    @pl.when(pl.program_id(0) == 0)
    def _():
        carry_sc[...] = jnp.zeros(carry_sc.shape, F32)

    h = _ln(x_ref[...]) * (1.0 + sc_ref[0, 0]) + sh_ref[0, 0]
    lg = _dot_nt(rw_ref[...], h, precision=HIGHEST)[:ne] + rb_ref[...][:ne, :1]
    e_idx = lax.broadcasted_iota(I32, (ne, 1), 0).astype(F32)
    onehots, vals, idxs = [], [], []
    for _ in range(TOP_K):
        m = jnp.max(lg, axis=0, keepdims=True)
        idx = jnp.min(jnp.where(lg == m, e_idx, float(ne)), axis=0, keepdims=True)
        oh = e_idx == idx
        onehots.append(oh)
        vals.append(m)
        idxs.append(idx)
        lg = jnp.where(oh, -jnp.inf, lg)
    es = [jnp.exp(v - vals[0]) for v in vals]
    den = es[0] + es[1] + es[2] + es[3]
    cnt = jnp.zeros((ne, tm), F32)
    for oh in onehots:
        cnt = cnt + jnp.where(oh, 1.0, 0.0)
    before = jnp.where(lax.broadcasted_iota(I32, (tm, tm), 0) < lax.broadcasted_iota(I32, (tm, tm), 1), 1.0, 0.0)
    earlier = jnp.dot(cnt.astype(BF16), before.astype(BF16), preferred_element_type=F32)
    tile_cnt = jnp.broadcast_to(jnp.sum(cnt, axis=1, keepdims=True), (ne, LANES))
    lower = jnp.where(lax.broadcasted_iota(I32, (ne, ne), 0) > lax.broadcasted_iota(I32, (ne, ne), 1), 1.0, 0.0)
    run_off = jnp.dot(lower.astype(BF16), tile_cnt.astype(BF16), preferred_element_type=F32)
    pos = earlier + run_off[:, :1]
    locals_ = [jnp.sum(jnp.where(oh, pos, 0.0), axis=0, keepdims=True) for oh in onehots]

    def token_major(rows):
        stack = jnp.concatenate(rows + [jnp.zeros((LANES - len(rows), tm), F32)], axis=0)
        return stack.T

    meta_ref[...] = token_major(idxs + locals_).astype(I32)
    gate_ref[...] = token_major([e / den for e in es])
    lane = _lane_iota()
    cols = jnp.where(lane == 0, carry_sc[...], jnp.where(lane == 1, tile_cnt, jnp.where(lane == 2, run_off, 0.0)))
    table = jnp.concatenate([cols, jnp.zeros((LANES - ne, LANES), F32)], axis=0).T
    tile_ref[...] = table[:SUBLANES].astype(I32)
    carry_sc[...] = carry_sc[...] + tile_cnt
    total = jnp.concatenate([carry_sc[...], jnp.zeros((LANES - ne, LANES), F32)], axis=0).T
    cnt_ref[...] = total[:SUBLANES].astype(I32)


def _router(x2, mod4, rw, rb, seq):
    n, d = x2.shape
    tm = TOK_TILE
    per_b = seq // tm
    rw_p = jnp.zeros((LANES, d), F32).at[:N_EXPERTS].set(rw.T)
    rb_p = jnp.zeros((LANES, LANES), F32).at[:N_EXPERTS].set(rb[:, None])
    return pl.pallas_call(
        _router_kernel,
        grid=(n // tm,),
        in_specs=[pl.BlockSpec((tm, d), lambda i: (i, 0)),
                  pl.BlockSpec((1, 1, 1, d), lambda i: (i // per_b, 4, 0, 0)),
                  pl.BlockSpec((1, 1, 1, d), lambda i: (i // per_b, 3, 0, 0)),
                  pl.BlockSpec((LANES, d), lambda i: (0, 0)),
                  pl.BlockSpec((LANES, LANES), lambda i: (0, 0))],
        out_specs=[pl.BlockSpec((tm, LANES), lambda i: (i, 0)),
                   pl.BlockSpec((tm, LANES), lambda i: (i, 0)),
                   pl.BlockSpec((SUBLANES, LANES), lambda i: (i, 0)),
                   pl.BlockSpec((SUBLANES, LANES), lambda i: (0, 0))],
        out_shape=[jax.ShapeDtypeStruct((n, LANES), I32),
                   jax.ShapeDtypeStruct((n, LANES), F32),
                   jax.ShapeDtypeStruct((n // tm * SUBLANES, LANES), I32),
                   jax.ShapeDtypeStruct((SUBLANES, LANES), I32)],
        scratch_shapes=[pltpu.VMEM((N_EXPERTS, LANES), F32)],
        compiler_params=_cparams(("arbitrary",)),
        name="router",
    )(x2, mod4, mod4, rw_p, rb_p)


def _wait_rows(src_like, dst_like, sem, n_rows):
    span = pl.ds(0, n_rows * ROW_CHUNKS)
    pltpu.make_async_copy(src_like.at[span], dst_like.at[span], sem).wait()


def _span_rows(ref, row, n):
    return ref.at[pl.ds(pl.multiple_of(row * ROW_CHUNKS, ROW_CHUNKS), n * ROW_CHUNKS)]


def _for_each_run(tab_ref, fn):
    for e in range(N_EXPERTS):
        n = tab_ref[1, e]

        @pl.when(n > 0)
        def _(e=e, n=n):
            fn(tab_ref[0, e], tab_ref[2, e], n, e % 2)


def _dispatch_kernel(tab_ref, fill_ref, x_ref, sc_ref, sh_ref, meta_ref, xs_hbm, xb0, xb1, zb, sem, fsem):
    i = pl.program_id(0)
    n_steps = pl.num_programs(0)
    tm = x_ref.shape[0]
    nrow = tm * TOP_K
    bufs = (xb0, xb1)

    @pl.when(i == 0)
    def _():
        zb[...] = jnp.zeros(zb.shape, F32)
        for j in range(N_FILL):
            n = fill_ref[1, j]

            @pl.when(n > 0)
            def _(j=j, n=n):
                fill = pltpu.make_async_copy(_span_rows(zb, 0, n), _span_rows(xs_hbm, fill_ref[0, j], n), fsem)
                fill.start()
                fill.wait()

    h = (_ln(x_ref[...]) * (1.0 + sc_ref[0, 0]) + sh_ref[0, 0]).astype(BF16)
    pos_t = meta_ref[...].astype(F32).T
    r = lax.broadcasted_iota(I32, (nrow, 1), 0)
    sel = jnp.zeros((nrow, tm), F32)
    for k in range(TOP_K):
        sel = sel + jnp.where(r == pos_t[TOP_K + k:TOP_K + k + 1, :].astype(I32), 1.0, 0.0)
    xt = jnp.dot(sel.astype(BF16), h, preferred_element_type=F32)

    for slot in range(2):
        @pl.when(i % 2 == slot)
        def _(slot=slot):
            buf = bufs[slot]

            @pl.when(i >= 2)
            def _():
                _wait_rows(buf, xs_hbm, sem.at[slot], nrow)

            for c in range(ROW_CHUNKS):
                buf[pl.ds(c, nrow, stride=ROW_CHUNKS), :] = xt[:, c * LANES:(c + 1) * LANES]
            _for_each_run(tab_ref, lambda srow, trow, n, prio: pltpu.make_async_copy(
                _span_rows(buf, trow, n), _span_rows(xs_hbm, srow, n), sem.at[slot]).start(priority=prio))

            @pl.when(i == n_steps - 1)
            def _():
                _wait_rows(buf, xs_hbm, sem.at[slot], nrow)

                @pl.when(i >= 1)
                def _():
                    _wait_rows(bufs[1 - slot], xs_hbm, sem.at[1 - slot], nrow)


def _dispatch(x2, mod4, meta, tab, fill_tab, n_rows, seq):
    n, d = x2.shape
    tm = TOK_TILE
    per_b = seq // tm
    return pl.pallas_call(
        _dispatch_kernel,
        grid=(n // tm,),
        in_specs=[pl.BlockSpec((SUBLANES, LANES), lambda i: (i, 0), memory_space=pltpu.SMEM),
                  pl.BlockSpec((SUBLANES, LANES), lambda i: (0, 0), memory_space=pltpu.SMEM),
                  pl.BlockSpec((tm, d), lambda i: (i, 0)),
                  pl.BlockSpec((1, 1, 1, d), lambda i: (i // per_b, 4, 0, 0)),
                  pl.BlockSpec((1, 1, 1, d), lambda i: (i // per_b, 3, 0, 0)),
                  pl.BlockSpec((tm, LANES), lambda i: (i, 0))],
        out_specs=pl.BlockSpec(memory_space=pl.ANY),
        out_shape=jax.ShapeDtypeStruct((n_rows * ROW_CHUNKS, LANES), F32),
        scratch_shapes=[pltpu.VMEM((tm * TOP_K * ROW_CHUNKS, LANES), F32),
                        pltpu.VMEM((tm * TOP_K * ROW_CHUNKS, LANES), F32),
                        pltpu.VMEM((EXPERT_ROWS * ROW_CHUNKS, LANES), F32),
                        pltpu.SemaphoreType.DMA((2,)),
                        pltpu.SemaphoreType.DMA],
        compiler_params=_cparams(("arbitrary",)),
        name="moe_dispatch",
    )(tab, fill_tab, x2, mod4, mod4, meta)


def _expert_kernel(be_ref, first_ref, used_ref, xs_ref, wgu_ref, bgu_ref, wd_ref, bd_ref, ys_ref,
                   x_sc, wgu_sc, wd_sc):
    del be_ref
    g = pl.program_id(0)

    @pl.when(g < used_ref[0])
    def _():
        _expert_block(g, first_ref, xs_ref, wgu_ref, bgu_ref, wd_ref, bd_ref, ys_ref, x_sc, wgu_sc, wd_sc)

    @pl.when(g >= used_ref[0])
    def _():
        ys_ref[...] = jnp.zeros(ys_ref.shape, ys_ref.dtype)


def _expert_block(g, first_ref, xs_ref, wgu_ref, bgu_ref, wd_ref, bd_ref, ys_ref, x_sc, wgu_sc, wd_sc):
    rows = x_sc.shape[0]
    grp = 2 * LANES

    @pl.when(first_ref[g] == 1)
    def _():
        r = lax.broadcasted_iota(I32, (grp, grp), 0)
        c = lax.broadcasted_iota(I32, (grp, grp), 1)
        src = jnp.where(c < LANES, 2 * c, 2 * (c - LANES) + 1)
        sel = jnp.where(r == src, 1.0, 0.0).astype(BF16)
        for j in range(wgu_sc.shape[1] // grp):
            blk = wgu_ref[0, :, j * grp:(j + 1) * grp].astype(BF16)
            wgu_sc[:, j * grp:(j + 1) * grp] = jnp.dot(blk, sel, preferred_element_type=F32).astype(BF16)
        wd_sc[...] = wd_ref[0].astype(BF16)

    for c in range(ROW_CHUNKS):
        x_sc[:, c * LANES:(c + 1) * LANES] = xs_ref[pl.ds(c, rows, stride=ROW_CHUNKS), :].astype(BF16)
    gu = jnp.dot(x_sc[...], wgu_sc[...], preferred_element_type=F32) + bgu_ref[0]
    acts = []
    for j in range(gu.shape[1] // grp):
        glu = jnp.minimum(gu[:, j * grp:j * grp + LANES], SWIGLU_LIMIT)
        lin = jnp.clip(gu[:, j * grp + LANES:(j + 1) * grp], -SWIGLU_LIMIT, SWIGLU_LIMIT)
        acts.append((glu * _sigmoid(SWIGLU_ALPHA * glu) * (lin + 1.0)).astype(BF16))
    act = jnp.concatenate(acts, axis=1)
    y = jnp.dot(act, wd_sc[...], preferred_element_type=F32) + bd_ref[0]
    for c in range(ROW_CHUNKS):
        ys_ref[pl.ds(c, rows, stride=ROW_CHUNKS), :] = y[:, c * LANES:(c + 1) * LANES]


def _experts(xs, block_exp, first, n_used, layer, w_gu, b_gu_grouped, w_dn, b_dn):
    d = D_MODEL
    n_blocks = block_exp.shape[0]
    rb = EXPERT_ROWS
    last = lambda g, nu: jnp.minimum(g, nu[0] - 1)
    wmap = lambda g, be, fi, nu: (layer, be[last(g, nu)], 0, 0)
    bmap = lambda g, be, fi, nu: (be[last(g, nu)], 0, 0)
    return pl.pallas_call(
        _expert_kernel,
        grid_spec=pltpu.PrefetchScalarGridSpec(
            num_scalar_prefetch=3,
            grid=(n_blocks,),
            in_specs=[pl.BlockSpec((rb * ROW_CHUNKS, LANES), lambda g, be, fi, nu: (last(g, nu), 0)),
                      pl.BlockSpec((None, 1, d, 2 * d), wmap),
                      pl.BlockSpec((1, 1, 2 * d), bmap),
                      pl.BlockSpec((None, 1, d, d), wmap),
                      pl.BlockSpec((1, 1, d), bmap)],
            out_specs=pl.BlockSpec((rb * ROW_CHUNKS, LANES), lambda g, be, fi, nu: (g, 0)),
            scratch_shapes=[pltpu.VMEM((rb, d), BF16),
                            pltpu.VMEM((d, 2 * d), BF16),
                            pltpu.VMEM((d, d), BF16)]),
        out_shape=jax.ShapeDtypeStruct(xs.shape, F32),
        compiler_params=_cparams(("arbitrary",)),
        name="moe_experts",
    )(block_exp, first, n_used, xs, w_gu, b_gu_grouped, w_dn, b_dn)


def _combine_kernel(tab_ref, tab_next_ref, meta_ref, gate_ref, x_ref, g2_ref, lng_ref, lnb_ref, ys_hbm, o_ref,
                    yb0, yb1, sem):
    i = pl.program_id(0)
    n_steps = pl.num_programs(0)
    tm = x_ref.shape[0]
    nrow = tm * TOP_K
    bufs = (yb0, yb1)

    def fetch(t_ref, slot):
        _for_each_run(t_ref, lambda srow, trow, n, prio: pltpu.make_async_copy(
            _span_rows(ys_hbm, srow, n), _span_rows(bufs[slot], trow, n), sem.at[slot]).start(priority=prio))

    @pl.when(i == 0)
    def _():
        fetch(tab_ref, 0)

    for slot in range(2):
        @pl.when(i % 2 == slot)
        def _(slot=slot):
            buf = bufs[slot]

            @pl.when(i + 1 < n_steps)
            def _():
                fetch(tab_next_ref, 1 - slot)

            _wait_rows(ys_hbm, buf, sem.at[slot], nrow)
            yt = jnp.concatenate([buf[pl.ds(c, nrow, stride=ROW_CHUNKS), :] for c in range(ROW_CHUNKS)],
                                 axis=1).astype(BF16)
            meta = meta_ref[...]
            gates = gate_ref[...]
            col = lax.broadcasted_iota(I32, (1, nrow), 1)
            w = jnp.zeros((tm, nrow), F32)
            for k in range(TOP_K):
                w = jnp.where(col == meta[:, TOP_K + k:TOP_K + k + 1], gates[:, k:k + 1], w)
            w_hi = w.astype(BF16)
            w_lo = (w - w_hi.astype(F32)).astype(BF16)
            y = (jnp.dot(w_hi, yt, preferred_element_type=F32) + jnp.dot(w_lo, yt, preferred_element_type=F32))
            z = DEEPNORM_ALPHA * x_ref[...] + g2_ref[0, 0] * y
            o_ref[...] = _ln(z) * lng_ref[...] + lnb_ref[...]


def _combine(ys, tab, meta, gates, x2, mod4, lng, lnb, seq):
    n, d = x2.shape
    tm = TOK_TILE
    per_b = seq // tm
    last = n // tm - 1
    return pl.pallas_call(
        _combine_kernel,
        grid=(n // tm,),
        in_specs=[pl.BlockSpec((SUBLANES, LANES), lambda i: (i, 0), memory_space=pltpu.SMEM),
                  pl.BlockSpec((SUBLANES, LANES), lambda i: (jnp.minimum(i + 1, last), 0),
                               memory_space=pltpu.SMEM),
                  pl.BlockSpec((tm, LANES), lambda i: (i, 0)),
                  pl.BlockSpec((tm, LANES), lambda i: (i, 0)),
                  pl.BlockSpec((tm, d), lambda i: (i, 0)),
                  pl.BlockSpec((1, 1, 1, d), lambda i: (i // per_b, 5, 0, 0)),
                  pl.BlockSpec((1, d), lambda i: (0, 0)),
                  pl.BlockSpec((1, d), lambda i: (0, 0)),
                  pl.BlockSpec(memory_space=pl.ANY)],
        out_specs=pl.BlockSpec((tm, d), lambda i: (i, 0)),
        out_shape=jax.ShapeDtypeStruct((n, d), F32),
        scratch_shapes=[pltpu.VMEM((tm * TOP_K * ROW_CHUNKS, LANES), F32),
                        pltpu.VMEM((tm * TOP_K * ROW_CHUNKS, LANES), F32),
                        pltpu.SemaphoreType.DMA((2,))],
        compiler_params=_cparams(("arbitrary",)),
        name="moe_combine",
    )(tab, tab, meta, gates, x2, mod4, lng, lnb, ys)


def _moe(x2, mod4, layer, rw, rb, w_gu, b_gu, w_dn, b_dn, lng, lnb, seq):
    n, d = x2.shape
    meta, gates, tile_meta, cnt = _router(x2, mod4, rw, rb, seq)
    counts = cnt[0, :N_EXPERTS]
    padded = -(-counts // EXPERT_ROWS) * EXPERT_ROWS
    pend = jnp.cumsum(padded)
    pstart = pend - padded
    n_blocks = -(-(n * TOP_K + N_EXPERTS * (EXPERT_ROWS - 1)) // EXPERT_ROWS)
    n_rows = n_blocks * EXPERT_ROWS
    block_row0 = jnp.arange(n_blocks, dtype=I32) * EXPERT_ROWS
    block_exp = jnp.minimum(jnp.sum((pend[None, :] <= block_row0[:, None]).astype(I32), axis=1), N_EXPERTS - 1)
    first = jnp.concatenate([jnp.ones((1,), I32), (block_exp[1:] != block_exp[:-1]).astype(I32)])
    n_used = (pend[-1:] // EXPERT_ROWS).astype(I32)
    n_tiles = n // TOK_TILE
    start_row = jnp.zeros((1, 1, LANES), I32).at[0, 0, :N_EXPERTS].set(pstart)
    sel_row0 = (jnp.arange(SUBLANES) == 0).astype(I32)[None, :, None]
    tab = (tile_meta.reshape(n_tiles, SUBLANES, LANES) + sel_row0 * start_row).reshape(n_tiles * SUBLANES, LANES)
    tail0 = pend[-1] + jnp.arange(N_FILL - N_EXPERTS, dtype=I32) * EXPERT_ROWS
    fill_start = jnp.concatenate([pstart + counts, tail0])
    fill_len = jnp.concatenate([padded - counts, jnp.clip(n_rows - tail0, 0, EXPERT_ROWS)])
    fill_tab = jnp.zeros((SUBLANES, LANES), I32).at[0, :N_FILL].set(fill_start).at[1, :N_FILL].set(fill_len)
    xs = _dispatch(x2, mod4, meta, tab, fill_tab, n_rows, seq)
    e = b_gu.shape[0]
    b_grouped = b_gu.reshape(e, d // LANES, LANES, 2).transpose(0, 1, 3, 2).reshape(e, 1, 2 * d)
    ys = _experts(xs, block_exp, first, n_used, layer, w_gu, b_grouped, w_dn, b_dn[:, None, :])
    return _combine(ys, tab, meta, gates, x2, mod4, lng.reshape(1, d), lnb.reshape(1, d), seq)


def _permute_w_in(w):
    d = w.shape[0]
    off = {}
    o = 0
    for name, wd in (("a_q", 256), ("a_k", 256), ("a_v", 256), ("b_q", 384), ("b_kc", 128), ("b_vc", 128),
                     ("b_ks", 128), ("b_vs", 128), ("b_kw", 128), ("b_vw", 128), ("b_gate", 18),
                     ("c_q", 384), ("c_k", 384), ("c_v", 384), ("merge_gate", 3 * D_MODEL)):
        off[name] = (o, o + wd)
        o += wd
    col = lambda name: w[:, off[name][0]:off[name][1]]
    scale = HEAD_DIM ** -0.5 * float(np.log2(np.e))
    b_q = col("b_q").reshape(d, B_KV_GROUPS, B_REP, HEAD_DIM).transpose(0, 2, 1, 3).reshape(d, B_HEADS * HEAD_DIM)
    gate_pad = jnp.zeros((d, LANES - B_HEADS * 3), w.dtype)
    parts = [col("merge_gate"),
             col("a_q") * scale, col("a_k"), b_q * scale, col("b_kc"), col("b_ks"), col("b_kw"),
             col("c_q") * scale, col("c_k"),
             col("a_v"), col("b_vc"), col("b_vs"), col("b_vw"), col("c_v"),
             col("b_gate"), gate_pad]
    return jnp.concatenate(parts, axis=1).astype(BF16)


def _dilate(arr, tile, batch, seq, dil):
    a = arr[:, tile * LANES:(tile + 1) * LANES].reshape(batch, seq // dil, dil, LANES)
    return a.transpose(0, 2, 1, 3).reshape(batch * seq, LANES)


def _undilate(arr, batch, seq, dil):
    a = arr.reshape(batch, dil, seq // dil, LANES)
    return a.transpose(0, 2, 1, 3).reshape(batch * seq, LANES)


def _mixer_layer(x2, mod4, cos_t, sin_t, batch, seq, w_in, cmp_w1_k, cmp_w2_k, cmp_pe_k,
                 cmp_w1_v, cmp_w2_v, cmp_pe_v, w_a, w_b, w_c, w_out, lng, lnb):
    d = D_MODEL
    proj = _inproj(x2, mod4, _permute_w_in(w_in), cos_t, sin_t, seq)

    bias_a = _moba_select(proj, _kmean(proj, batch, seq), batch, seq)
    o_a = _bb_attn(proj, bias_a, batch=batch, seq=seq, n_pairs=A_HEADS // 2,
                   q_tile=lambda p: T_AQ + p, k_tile=lambda p: T_AK + p, v_tile=lambda p: T_AV + p,
                   bias_tile=lambda p: p, blk=MOBA_BLOCK)

    kc = _compress(proj[:, T_BKC * LANES:(T_BKC + 1) * LANES], cmp_w1_k, cmp_w2_k, cmp_pe_k, batch, seq)
    vc = _compress(proj[:, T_BVC * LANES:(T_BVC + 1) * LANES], cmp_w1_v, cmp_w2_v, cmp_pe_v, batch, seq)
    o_cmp, bias_b = _nsa_cmp(proj, kc, vc, batch, seq)
    o_slc = _bb_attn(proj, bias_b, batch=batch, seq=seq, n_pairs=B_REP,
                     q_tile=lambda p: T_BQ + p, k_tile=lambda p: T_BKS, v_tile=lambda p: T_BVS,
                     bias_tile=lambda p: 0, blk=SLC_BLOCK)
    o_win, _ = _band_attn(proj, proj, proj, batch=batch, seq=seq, q_tiles=[T_BQ + r for r in range(B_REP)],
                          k_tile=T_BKW, v_tile=T_BVW, window=NSA_WINDOW, t=256, group=2)

    oc, lse = [], []
    for gi, (window, dil) in enumerate(DILATED_PAIRS):
        wlen = window // dil + 1
        if dil == 1:
            o, l = _band_attn(proj, proj, proj, batch=batch, seq=seq, q_tiles=[T_CQ + gi], k_tile=T_CK + gi,
                              v_tile=T_CV + gi, window=wlen, t=256, group=4)
        else:
            qd = _dilate(proj, T_CQ + gi, batch, seq, dil)
            kd = _dilate(proj, T_CK + gi, batch, seq, dil)
            vd = _dilate(proj, T_CV + gi, batch, seq, dil)
            o, l = _band_attn(qd, kd, vd, batch=batch * dil, seq=seq // dil, q_tiles=[0], k_tile=0, v_tile=0,
                              window=wlen, t=256, group=4)
            o, l = _undilate(o, batch, seq, dil), _undilate(l, batch, seq, dil)
        oc.append(o)
        lse.append(l)

    w_b_perm = w_b.reshape(B_KV_GROUPS, B_REP, HEAD_DIM, d).transpose(1, 0, 2, 3).reshape(B_HEADS * HEAD_DIM, d)
    return _merge(o_a, o_cmp, o_slc, o_win, proj, oc, lse, x2, mod4,
                  w_a.astype(BF16), w_b_perm.astype(BF16), w_c.astype(BF16), w_out.astype(BF16),
                  lng.reshape(1, d), lnb.reshape(1, d), seq)


def _rope_tables(positions):
    inv = ROPE_THETA ** (-jnp.arange(0, HEAD_DIM, 2, dtype=F32) / HEAD_DIM)
    ang = positions.astype(F32).reshape(-1, 1) * inv[None, :]
    cos, sin = jnp.cos(ang), jnp.sin(ang)
    return jnp.tile(cos, (1, 4)), jnp.tile(jnp.concatenate([-sin, sin], axis=1), (1, 2))


def kernel(x, c, positions, w_ada, b_ada, w_in, cmp_w1_k, cmp_w2_k, cmp_pe_k, cmp_w1_v, cmp_w2_v, cmp_pe_v, w_branch_a, w_branch_b, w_branch_c, w_out, ln1_g, ln1_b, router_w, router_b, w_gate_up, b_gate_up, w_down, b_down, ln2_g, ln2_b):
    batch, seq, d = x.shape
    depth = w_in.shape[0]
    cos_t, sin_t = _rope_tables(positions)
    mod = _ada(c, w_ada, b_ada)
    x2 = x.reshape(batch * seq, d)
    for l in range(depth):
        mod4 = mod[l].reshape(batch, N_ADA, 1, d)
        x2 = _mixer_layer(x2, mod4, cos_t, sin_t, batch, seq, w_in[l], cmp_w1_k[l], cmp_w2_k[l], cmp_pe_k[l],
                          cmp_w1_v[l], cmp_w2_v[l], cmp_pe_v[l], w_branch_a[l], w_branch_b[l],
                          w_branch_c[l], w_out[l], ln1_g[l], ln1_b[l])
        x2 = _moe(x2, mod4, l, router_w[l], router_b[l], w_gate_up, b_gate_up[l], w_down, b_down[l],
                  ln2_g[l], ln2_b[l], seq)
    return x2.reshape(batch, seq, d)
```

```python
import functools

import numpy as np
import jax
import jax.numpy as jnp
from jax import lax
from jax.experimental import pallas as pl
from jax.experimental.pallas import tpu as pltpu

F32 = jnp.float32
BF16 = jnp.bfloat16
I32 = jnp.int32
HIGHEST = lax.Precision.HIGHEST

D_MODEL = 1024
DEPTH = 2
HEAD_DIM = 64
ROPE_THETA = 10000.0
LN_EPS = 1e-5
DEEPNORM_ALPHA = (2 * DEPTH) ** 0.25
N_ADA = 6
A_HEADS = 4
MOBA_BLOCK = 256
MOBA_TOPK = 3
B_HEADS = 6
B_KV_GROUPS = 2
B_REP = 3
CMP_BLOCK = 32
CMP_STRIDE = 16
CMP_HIDDEN = 128
SLC_BLOCK = 64
SLC_TOPK = 16
NSA_WINDOW = 512
FORCE_SCORE = 1e6
DILATED_PAIRS = ((128, 1), (512, 4), (2048, 16))
N_EXPERTS = 32
TOP_K = 4
SWIGLU_ALPHA = 1.702
SWIGLU_LIMIT = 7.0

LANES = 128
SUBLANES = 8
HALF = LANES // 2
NEG = -1e30
ROW_CHUNKS = D_MODEL // LANES
VMEM_LIMIT = 56 * 1024 * 1024

T_MG = 0
T_ROPE0, T_ROPE1 = 24, 40
T_AQ, T_AK, T_BQ, T_BKC, T_BKS, T_BKW, T_CQ, T_CK = 24, 26, 28, 31, 32, 33, 34, 37
T_AV, T_BVC, T_BVS, T_BVW, T_CV, T_BG = 40, 42, 43, 44, 45, 48
N_PROJ_TILES = 49
PROJ_W = N_PROJ_TILES * LANES

EXPERT_ROWS = 512
TOK_TILE = 256
N_FILL = 2 * N_EXPERTS + 2


def _cparams(sem, **kw):
    return pltpu.CompilerParams(dimension_semantics=sem, vmem_limit_bytes=VMEM_LIMIT, **kw)


def _lane_iota(shape=(1, LANES)):
    return lax.broadcasted_iota(I32, shape, len(shape) - 1)


def _sigmoid(x):
    return 1.0 / (1.0 + jnp.exp(-x))


def _sigmoid_tanh(x):
    return 0.5 * jnp.tanh(0.5 * x) + 0.5


def _ln(x):
    mu = jnp.mean(x, axis=-1, keepdims=True)
    xc = x - mu
    var = jnp.mean(xc * xc, axis=-1, keepdims=True)
    return xc * lax.rsqrt(var + LN_EPS)


def _dot_nt(a, b, precision=None):
    return lax.dot_general(a, b, (((1,), (1,)), ((), ())), precision=precision,
                           preferred_element_type=F32)


def _ada_kernel(c_ref, w_ref, b_ref, o_ref):
    c = c_ref[...]
    cond = c * _sigmoid(c)
    o_ref[0] = jnp.dot(cond, w_ref[0], precision=HIGHEST, preferred_element_type=F32) + b_ref[0]


def _ada(c, w_ada, b_ada):
    depth, d, n = w_ada.shape
    b = c.shape[0]
    tn = 1536
    return pl.pallas_call(
        _ada_kernel,
        grid=(depth, n // tn),
        in_specs=[pl.BlockSpec((b, d), lambda l, j: (0, 0)),
                  pl.BlockSpec((1, d, tn), lambda l, j: (l, 0, j)),
                  pl.BlockSpec((1, 1, tn), lambda l, j: (l, 0, j))],
        out_specs=pl.BlockSpec((1, b, tn), lambda l, j: (l, 0, j)),
        out_shape=jax.ShapeDtypeStruct((depth, b, n), F32),
        compiler_params=_cparams(("parallel", "parallel")),
        name="ada",
    )(c, w_ada, b_ada.reshape(depth, 1, n))


def _inproj_kernel(x_ref, sc_ref, sh_ref, w_ref, cos_ref, sin_ref, o_ref, *, chunks):
    h = _ln(x_ref[...]) * (1.0 + sc_ref[0, 0]) + sh_ref[0, 0]
    hb = h.astype(BF16)
    first_half = (_lane_iota() & (HEAD_DIM - 1)) < (HEAD_DIM // 2)
    for c0, cw, rope in chunks:
        acc = jnp.dot(hb, w_ref[:, c0:c0 + cw], preferred_element_type=F32)
        if rope:
            cos = cos_ref[...]
            sin = sin_ref[...]
            for t in range(cw // LANES):
                a = acc[:, t * LANES:(t + 1) * LANES]
                rot = jnp.where(first_half, pltpu.roll(a, LANES - HEAD_DIM // 2, 1),
                                pltpu.roll(a, HEAD_DIM // 2, 1))
                o_ref[:, c0 + t * LANES:c0 + (t + 1) * LANES] = (a * cos + rot * sin).astype(o_ref.dtype)
        else:
            o_ref[:, c0:c0 + cw] = acc.astype(o_ref.dtype)


def _inproj_chunks():
    chunks = []
    for lo, hi, rope in ((0, T_ROPE0, False), (T_ROPE0, T_ROPE1, True), (T_ROPE1, N_PROJ_TILES, False)):
        c = lo * LANES
        while c < hi * LANES:
            cw = min(512, hi * LANES - c)
            chunks.append((c, cw, rope))
            c += cw
    return tuple(chunks)


def _inproj(x2, mod4, w_perm, cos_t, sin_t, seq):
    n, d = x2.shape
    tm = 512
    per_b = seq // tm
    return pl.pallas_call(
        functools.partial(_inproj_kernel, chunks=_inproj_chunks()),
        grid=(n // tm,),
        in_specs=[pl.BlockSpec((tm, d), lambda i: (i, 0)),
                  pl.BlockSpec((1, 1, 1, d), lambda i: (i // per_b, 1, 0, 0)),
                  pl.BlockSpec((1, 1, 1, d), lambda i: (i // per_b, 0, 0, 0)),
                  pl.BlockSpec((d, PROJ_W), lambda i: (0, 0)),
                  pl.BlockSpec((tm, LANES), lambda i: (i, 0)),
                  pl.BlockSpec((tm, LANES), lambda i: (i, 0))],
        out_specs=pl.BlockSpec((tm, PROJ_W), lambda i: (i, 0)),
        out_shape=jax.ShapeDtypeStruct((n, PROJ_W), BF16),
        compiler_params=_cparams(("parallel",)),
        name="inproj",
    )(x2, mod4, mod4, w_perm, cos_t, sin_t)


def _bb_attn_kernel(q_ref, k_ref, v_ref, b_ref, o_ref, qa_sc, s_sc, mp_sc, m_sc, acc_sc,
                    *, blk_shift, tq, tu, big):
    i = pl.program_id(2)
    n_diag = tq // tu
    lane = _lane_iota()
    lo = lane < HALF
    q = q_ref[...]
    b = b_ref[...]
    qa_sc[0] = jnp.where(lo, q, b)
    qa_sc[1] = jnp.where(lo, b, q)
    mp_sc[...] = jnp.full(mp_sc.shape, -jnp.inf, F32)
    acc_sc[...] = jnp.zeros(acc_sc.shape, F32)

    def scores(u0, n_units, diagonal_last):
        tk = n_units * tu
        start = pl.multiple_of(u0 * tu, tu)
        k = k_ref[pl.ds(start, tk), :]
        kblk = (start + lax.broadcasted_iota(I32, (tk, 1), 0)) >> blk_shift
        onehot = jnp.where(kblk == (lane & (HALF - 1)), 1.0, 0.0).astype(BF16)
        ka = (jnp.where(lo, k, onehot), jnp.where(lo, onehot, k))
        row = lax.broadcasted_iota(I32, (tq, 1), 0)
        col = lax.broadcasted_iota(I32, (1, tu), 1)
        for h in range(2):
            s = _dot_nt(qa_sc[h], ka[h])
            mp = mp_sc[h]
            for u in range(n_units):
                su = s[:, u * tu:(u + 1) * tu]
                d = u - (n_units - n_diag)
                if diagonal_last and d >= 0:
                    su = jnp.where(col + d * tu <= row, su, NEG)
                s_sc[h, u0 + u] = su
                for c in range(tu // LANES):
                    mp = jnp.maximum(mp, su[:, c * LANES:(c + 1) * LANES])
            mp_sc[h] = mp

    def weighted(u0, n_units):
        tk = n_units * tu
        start = pl.multiple_of(u0 * tu, tu)
        v = v_ref[pl.ds(start, tk), :]
        one = jnp.ones_like(v)
        vs = (jnp.where(lo, v, one), jnp.where(lo, one, v))
        for h in range(2):
            m_row = m_sc[h]
            ps = []
            for u in range(n_units):
                su = s_sc[h, u0 + u]
                for c in range(tu // LANES):
                    ps.append(jnp.exp2(su[:, c * LANES:(c + 1) * LANES] - m_row).astype(BF16))
            acc_sc[h] = acc_sc[h] + jnp.dot(jnp.concatenate(ps, axis=1), vs[h], preferred_element_type=F32)

    n_full = i * n_diag
    n_big = n_full // big
    n_tail = n_full - n_big * big + n_diag

    def run(fn):
        def big_body(t, c):
            fn(t * big, big, False) if fn is scores else fn(t * big, big)
            return c

        lax.fori_loop(0, n_big, big_body, 0)
        for n in range(n_diag, big + n_diag, n_diag):
            @pl.when(n_tail == n)
            def _(n=n):
                fn(n_big * big, n, True) if fn is scores else fn(n_big * big, n)

    run(scores)
    for h in range(2):
        m_sc[h] = jnp.broadcast_to(jnp.max(mp_sc[h], axis=1, keepdims=True), (tq, LANES))
    run(weighted)
    outs = [acc_sc[h] / pltpu.roll(acc_sc[h], HALF, 1) for h in range(2)]
    o_ref[...] = jnp.where(lo, outs[0], outs[1]).astype(o_ref.dtype)


def _bb_attn(proj, bias, *, batch, seq, n_pairs, q_tile, k_tile, v_tile, bias_tile, blk):
    t, tu = 512, 256
    nq = seq // t
    n = batch * seq
    kern = functools.partial(_bb_attn_kernel, blk_shift=int(np.log2(blk)), tq=t, tu=tu, big=4)
    return pl.pallas_call(
        kern,
        grid=(batch, n_pairs, nq),
        in_specs=[pl.BlockSpec((t, LANES), lambda b, p, i: (b * nq + i, q_tile(p))),
                  pl.BlockSpec((seq, LANES), lambda b, p, i: (b, k_tile(p))),
                  pl.BlockSpec((seq, LANES), lambda b, p, i: (b, v_tile(p))),
                  pl.BlockSpec((t, LANES), lambda b, p, i: (b * nq + i, bias_tile(p)))],
        out_specs=pl.BlockSpec((t, LANES), lambda b, p, i: (b * nq + i, p)),
        out_shape=jax.ShapeDtypeStruct((n, n_pairs * LANES), BF16),
        scratch_shapes=[pltpu.VMEM((2, t, LANES), BF16),
                        pltpu.VMEM((2, seq // tu, t, tu), F32),
                        pltpu.VMEM((2, t, LANES), F32),
                        pltpu.VMEM((2, t, LANES), F32),
                        pltpu.VMEM((2, t, LANES), F32)],
        compiler_params=_cparams(("parallel", "parallel", "parallel")),
        name="bb_attn_%d" % blk,
    )(proj, proj, proj, bias)


def _band_kernel(*refs, tq, kw, window, seq, n_q, group, tiles_per_seq):
    q_refs = refs[:n_q]
    k_ref, v_ref, o_ref, lse_ref = refs[n_q:n_q + 4]
    i = pl.program_id(1)
    lane = _lane_iota()
    lo = lane < HALF
    for g in range(group):
        if tiles_per_seq == 1:
            tile, base = 0, g * seq
        else:
            tile, base = i * group + g, 0
        start = jnp.clip(tile * tq - (kw - tq), 0, seq - kw)
        k = k_ref[pl.ds(pl.multiple_of(base + start, LANES), kw), :]
        v = v_ref[pl.ds(pl.multiple_of(base + start, LANES), kw), :]
        rel = (tile * tq + lax.broadcasted_iota(I32, (tq, 1), 0)) - (start + lax.broadcasted_iota(I32, (1, kw), 1))
        ok = (rel >= 0) & (rel < window)
        for qi in range(n_q):
            q = q_refs[qi][g * tq:(g + 1) * tq, :]
            zero = jnp.zeros_like(q)
            qs = (jnp.where(lo, q, zero), jnp.where(lo, zero, q))
            outs, lses = [], []
            for h in range(2):
                s = jnp.where(ok, _dot_nt(qs[h], k), NEG)
                m = jnp.max(s, axis=1, keepdims=True)
                p = jnp.exp2(s - m)
                l = jnp.sum(p, axis=1, keepdims=True)
                outs.append(jnp.dot(p.astype(BF16), v, preferred_element_type=F32) / l)
                lses.append(m + jnp.log2(l))
            rows, cols = slice(g * tq, (g + 1) * tq), slice(qi * LANES, (qi + 1) * LANES)
            o_ref[rows, cols] = jnp.where(lo, outs[0], outs[1]).astype(o_ref.dtype)
            lse_ref[rows, cols] = jnp.where(lo, lses[0], lses[1])


def _band_attn(q_arr, k_arr, v_arr, *, batch, seq, q_tiles, k_tile, v_tile, window, t, group):
    t = min(t, seq)
    tiles_per_seq = seq // t
    n = batch * seq
    n_q = len(q_tiles)
    if tiles_per_seq == 1:
        group = min(group, batch)
        grid = (batch // group, 1)
        kv_rows = group * seq
        row_blk = lambda b, i: b
    else:
        group = min(group, tiles_per_seq)
        grid = (batch, tiles_per_seq // group)
        kv_rows = seq
        row_blk = lambda b, i: b * (tiles_per_seq // group) + i
    kw = min(seq, t + -(-(window - 1) // LANES) * LANES)
    kern = functools.partial(_band_kernel, tq=t, kw=kw, window=window, seq=seq, n_q=n_q, group=group,
                             tiles_per_seq=tiles_per_seq)
    qspecs = [pl.BlockSpec((group * t, LANES), (lambda b, i, c=c: (row_blk(b, i), c))) for c in q_tiles]
    return pl.pallas_call(
        kern,
        grid=grid,
        in_specs=qspecs + [pl.BlockSpec((kv_rows, LANES), lambda b, i: (b, k_tile)),
                           pl.BlockSpec((kv_rows, LANES), lambda b, i: (b, v_tile))],
        out_specs=[pl.BlockSpec((group * t, n_q * LANES), lambda b, i: (row_blk(b, i), 0)),
                   pl.BlockSpec((group * t, n_q * LANES), lambda b, i: (row_blk(b, i), 0))],
        out_shape=[jax.ShapeDtypeStruct((n, n_q * LANES), BF16),
                   jax.ShapeDtypeStruct((n, n_q * LANES), F32)],
        compiler_params=_cparams(("parallel", "parallel")),
        name="band_attn_%d" % window,
    )(*([q_arr] * n_q), k_arr, v_arr)


def _kmean_kernel(k_ref, o_ref, *, nb):
    s = k_ref.shape[0]
    blk = lax.broadcasted_iota(I32, (nb, s), 1) >> int(np.log2(MOBA_BLOCK))
    avg = jnp.where(blk == lax.broadcasted_iota(I32, (nb, s), 0), 1.0 / MOBA_BLOCK, 0.0).astype(BF16)
    o_ref[0] = jnp.dot(avg, k_ref[...], preferred_element_type=F32)


def _kmean(proj, batch, seq):
    nb = seq // MOBA_BLOCK
    w = A_HEADS * HEAD_DIM
    return pl.pallas_call(
        functools.partial(_kmean_kernel, nb=nb),
        grid=(batch,),
        in_specs=[pl.BlockSpec((seq, w), lambda b: (b, T_AK * LANES // w))],
        out_specs=pl.BlockSpec((1, nb, w), lambda b: (b, 0, 0)),
        out_shape=jax.ShapeDtypeStruct((batch, nb, w), F32),
        compiler_params=_cparams(("parallel",)),
        name="moba_kmean",
    )(proj)


def _rank_desc(g, n):
    gs = min(SUBLANES, n)
    groups = [g[r0:r0 + gs, :] for r0 in range(0, n, gs)]
    ranks = [jnp.zeros(x.shape, I32) for x in groups]
    sub = lax.broadcasted_iota(I32, (gs, 1), 0)
    for m in range(n):
        c = g[m:m + 1, :]
        for gi, x in enumerate(groups):
            if gi * gs > m:
                beats = c >= x
            elif gi * gs + gs - 1 <= m:
                beats = c > x
            else:
                beats = (c > x) | ((c == x) & (sub + gi * gs > m))
            ranks[gi] = ranks[gi] + jnp.where(beats, 1, 0)
    return jnp.concatenate(ranks, axis=0)


def _moba_sel_kernel(*refs, nb, n_sel, n_pairs):
    q_refs = refs[:n_pairs]
    km_ref, b_ref = refs[n_pairs:]
    i = pl.program_id(1)
    t = b_ref.shape[0]
    lo = _lane_iota() < HALF
    pad = jnp.zeros((HALF - nb, LANES), F32)
    n_idx = lax.broadcasted_iota(I32, (nb, 1), 0)
    valid = n_idx < i
    for p in range(n_pairs):
        q = q_refs[p][...].astype(F32)
        km = km_ref[0, :, p * LANES:(p + 1) * LANES]
        zero = jnp.zeros_like(km)
        kmt = jnp.concatenate([jnp.where(lo, zero, km), pad, jnp.where(lo, km, zero), pad], axis=0)
        gt = _dot_nt(kmt, q, precision=HIGHEST)
        rows = []
        for r0 in (0, HALF):
            g = jnp.where(valid, gt[r0:r0 + nb, :], -jnp.inf)
            rank = _rank_desc(g, nb)
            allowed = (valid & (rank < n_sel)) | (n_idx == i)
            rows.append(jnp.where(allowed, 0.0, NEG))
            rows.append(jnp.zeros((HALF - nb, t), F32))
        b_ref[:, p * LANES:(p + 1) * LANES] = jnp.concatenate(rows, axis=0).T.astype(b_ref.dtype)


def _moba_select(proj, kmean, batch, seq):
    t = MOBA_BLOCK
    nq = seq // t
    nb = seq // MOBA_BLOCK
    n_sel = min(MOBA_TOPK, nb - 1)
    n_pairs = A_HEADS // 2
    qspecs = [pl.BlockSpec((t, LANES), (lambda b, i, p=p: (b * nq + i, T_AQ + p))) for p in range(n_pairs)]
    return pl.pallas_call(
        functools.partial(_moba_sel_kernel, nb=nb, n_sel=n_sel, n_pairs=n_pairs),
        grid=(batch, nq),
        in_specs=qspecs + [pl.BlockSpec((1, nb, n_pairs * LANES), lambda b, i: (b, 0, 0))],
        out_specs=pl.BlockSpec((t, n_pairs * LANES), lambda b, i: (b * nq + i, 0)),
        out_shape=jax.ShapeDtypeStruct((batch * seq, n_pairs * LANES), BF16),
        compiler_params=_cparams(("parallel", "parallel")),
        name="moba_select",
    )(*([proj] * n_pairs), kmean)


def _compress_kernel(x_ref, w1_ref, pe_ref, w1f_ref, w2_ref, o_ref):
    x = x_ref[0]
    nblk = x.shape[0]
    outs = []
    for g in range(B_KV_GROUPS):
        u = jnp.dot(x, w1_ref[g, 0], preferred_element_type=F32)
        v = jnp.dot(x, w1_ref[g, 1], preferred_element_type=F32)
        pe_h = jnp.dot(pe_ref[...], w1f_ref[...], precision=HIGHEST, preferred_element_type=F32)
        hid = u + pltpu.roll(v, nblk - 1, 0) + pe_h[:1]
        hid = hid * _sigmoid(hid)
        outs.append(jnp.dot(hid, w2_ref[g], precision=HIGHEST, preferred_element_type=F32))
    o = outs[0] + outs[1]
    rows = lax.broadcasted_iota(I32, (nblk, 1), 0)
    o_ref[0] = jnp.where(rows < nblk - 1, o, 0.0).astype(o_ref.dtype)


def _compress(xt, w1, w2, pe, batch, seq):
    nblk = seq // CMP_STRIDE
    xg = xt.reshape(batch, nblk, CMP_STRIDE * LANES)
    w1r = w1.reshape(2, CMP_STRIDE, HEAD_DIM, CMP_HIDDEN)
    w1e = jnp.zeros((B_KV_GROUPS, 2, CMP_STRIDE, B_KV_GROUPS, HEAD_DIM, CMP_HIDDEN), F32)
    for g in range(B_KV_GROUPS):
        w1e = w1e.at[g, :, :, g].set(w1r)
    w1e = w1e.reshape(B_KV_GROUPS, 2, CMP_STRIDE * LANES, CMP_HIDDEN).astype(BF16)
    pe_flat = jnp.broadcast_to(pe.reshape(1, CMP_BLOCK * HEAD_DIM), (SUBLANES, CMP_BLOCK * HEAD_DIM))
    w2e = jnp.zeros((B_KV_GROUPS, CMP_HIDDEN, LANES), F32)
    for g in range(B_KV_GROUPS):
        w2e = w2e.at[g, :, g * HEAD_DIM:(g + 1) * HEAD_DIM].set(w2)
    return pl.pallas_call(
        _compress_kernel,
        grid=(batch,),
        in_specs=[pl.BlockSpec((1, nblk, CMP_STRIDE * LANES), lambda b: (b, 0, 0)),
                  pl.BlockSpec(w1e.shape, lambda b: (0, 0, 0, 0)),
                  pl.BlockSpec(pe_flat.shape, lambda b: (0, 0)),
                  pl.BlockSpec(w1.shape, lambda b: (0, 0)),
                  pl.BlockSpec(w2e.shape, lambda b: (0, 0, 0))],
        out_specs=pl.BlockSpec((1, nblk, LANES), lambda b: (b, 0, 0)),
        out_shape=jax.ShapeDtypeStruct((batch, nblk, LANES), BF16),
        compiler_params=_cparams(("parallel",)),
        name="nsa_compress",
    )(xg, w1e, pe_flat, w1, w2e)


def _nsa_cmp_kernel(q0_ref, q1_ref, q2_ref, kc_ref, vc_ref, ov_ref, o_ref, b_ref, *, n_sel, ns):
    i = pl.program_id(1)
    t = q0_ref.shape[0]
    ncp = kc_ref.shape[1]
    lane = _lane_iota()
    lo = lane < HALF
    kc = kc_ref[0]
    vc = vc_ref[0]
    tq = i * t + lax.broadcasted_iota(I32, (t, 1), 0)
    cmp_end = lax.broadcasted_iota(I32, (1, ncp), 1) * CMP_STRIDE + (CMP_BLOCK - 1)
    vis = cmp_end <= tq
    psum = [jnp.zeros((t, ncp), F32) for _ in range(B_KV_GROUPS)]
    for r, q_ref in enumerate((q0_ref, q1_ref, q2_ref)):
        q = q_ref[...]
        zero = jnp.zeros_like(q)
        outs = []
        for g in range(B_KV_GROUPS):
            qg = jnp.where(lo, q, zero) if g == 0 else jnp.where(lo, zero, q)
            s = jnp.where(vis, _dot_nt(qg, kc), NEG)
            m = jnp.max(s, axis=1, keepdims=True)
            m = jnp.where(m > 0.5 * NEG, m, 0.0)
            e = jnp.exp2(s - m)
            p = e / jnp.maximum(jnp.sum(e, axis=1, keepdims=True), 1e-30)
            psum[g] = psum[g] + p
            outs.append(jnp.dot(p.astype(BF16), vc, preferred_element_type=F32))
        o_ref[:, r * LANES:(r + 1) * LANES] = jnp.where(lo, outs[0], outs[1]).astype(o_ref.dtype)
    n_idx = lax.broadcasted_iota(I32, (ns, 1), 0)
    tcol = i * t + lax.broadcasted_iota(I32, (1, t), 1)
    qblk = tcol >> int(np.log2(SLC_BLOCK))
    forced = (n_idx == 0) | (n_idx == qblk) | (n_idx == qblk - 1)
    valid = n_idx <= qblk
    rows = []
    for g in (1, 0):
        imp = _dot_nt(ov_ref[...], psum[g], precision=HIGHEST)
        imp = jnp.where(forced, FORCE_SCORE, imp)
        imp = jnp.where(valid, imp, -jnp.inf)
        rank = _rank_desc(imp, ns)
        allowed = valid & (rank < n_sel)
        rows.append(jnp.where(allowed, 0.0, NEG))
        if ns < HALF:
            rows.append(jnp.zeros((HALF - ns, t), F32))
    b_ref[...] = jnp.concatenate(rows, axis=0).T.astype(b_ref.dtype)


def _nsa_cmp(proj, kc, vc, batch, seq):
    t = 256
    nq = seq // t
    ns = seq // SLC_BLOCK
    n_sel = min(SLC_TOPK, ns)
    ncp = seq // CMP_STRIDE
    cs = np.arange(ncp)[None, :] * CMP_STRIDE
    ss = np.arange(ns)[:, None] * SLC_BLOCK
    ov = ((cs < ss + SLC_BLOCK) & (cs + CMP_BLOCK > ss)).astype(np.float32)
    ov[:, ncp - 1] = 0.0
    n = batch * seq
    qspec = [pl.BlockSpec((t, LANES), (lambda b, i, r=r: (b * nq + i, T_BQ + r))) for r in range(B_REP)]
    return pl.pallas_call(
        functools.partial(_nsa_cmp_kernel, n_sel=n_sel, ns=ns),
        grid=(batch, nq),
        in_specs=qspec + [pl.BlockSpec((1, ncp, LANES), lambda b, i: (b, 0, 0)),
                          pl.BlockSpec((1, ncp, LANES), lambda b, i: (b, 0, 0)),
                          pl.BlockSpec((ns, ncp), lambda b, i: (0, 0))],
        out_specs=[pl.BlockSpec((t, B_REP * LANES), lambda b, i: (b * nq + i, 0)),
                   pl.BlockSpec((t, LANES), lambda b, i: (b * nq + i, 0))],
        out_shape=[jax.ShapeDtypeStruct((n, B_REP * LANES), BF16),
                   jax.ShapeDtypeStruct((n, LANES), BF16)],
        compiler_params=_cparams(("parallel", "parallel")),
        name="nsa_cmp_select",
    )(proj, proj, proj, kc, vc, jnp.asarray(ov))


def _merge_kernel(oa_ref, ocmp_ref, oslc_ref, owin_ref, bg_ref, oc0_ref, oc1_ref, oc2_ref,
                  l0_ref, l1_ref, l2_ref, mg_ref, x_ref, g1_ref, wa_ref, wb_ref, wc_ref, wo_ref,
                  eg_ref, lng_ref, lnb_ref, o_ref):
    d = D_MODEL
    sg = _sigmoid(bg_ref[...].astype(F32))
    sg_hi = sg.astype(BF16)
    sg_lo = (sg - sg_hi.astype(F32)).astype(BF16)
    ob = jnp.zeros(ocmp_ref.shape, F32)
    for br, ref in enumerate((ocmp_ref, oslc_ref, owin_ref)):
        gexp = (jnp.dot(sg_hi, eg_ref[br], preferred_element_type=F32)
                + jnp.dot(sg_lo, eg_ref[br], preferred_element_type=F32))
        ob = ob + gexp * ref[...].astype(F32)
    l0, l1, l2 = l0_ref[...], l1_ref[...], l2_ref[...]
    mx = jnp.maximum(jnp.maximum(l0, l1), l2)
    e0, e1, e2 = jnp.exp2(l0 - mx), jnp.exp2(l1 - mx), jnp.exp2(l2 - mx)
    den = e0 + e1 + e2
    oc = ((e0 / den) * oc0_ref[...].astype(F32) + (e1 / den) * oc1_ref[...].astype(F32)
          + (e2 / den) * oc2_ref[...].astype(F32))
    pa = jnp.dot(oa_ref[...], wa_ref[...], preferred_element_type=F32)
    pb = jnp.dot(ob.astype(BF16), wb_ref[...], preferred_element_type=F32)
    pc = jnp.dot(oc.astype(BF16), wc_ref[...], preferred_element_type=F32)
    merged = (_sigmoid_tanh(mg_ref[:, 0:d].astype(F32)) * pa
              + _sigmoid_tanh(mg_ref[:, d:2 * d].astype(F32)) * pb
              + _sigmoid_tanh(mg_ref[:, 2 * d:3 * d].astype(F32)) * pc)
    y = jnp.dot(merged.astype(BF16), wo_ref[...], preferred_element_type=F32)
    z = DEEPNORM_ALPHA * x_ref[...] + g1_ref[0, 0] * y
    o_ref[...] = _ln(z) * lng_ref[...] + lnb_ref[...]


def _gate_expand():
    eg = np.zeros((3, LANES, B_HEADS * HEAD_DIM), np.float32)
    for g in range(B_KV_GROUPS):
        for r in range(B_REP):
            for br in range(3):
                c0 = (r * B_KV_GROUPS + g) * HEAD_DIM
                eg[br, (g * B_REP + r) * 3 + br, c0:c0 + HEAD_DIM] = 1.0
    return jnp.asarray(eg)


def _merge(o_a, o_cmp, o_slc, o_win, proj, oc, lse, x2, mod4, wa, wb, wc, wo, lng, lnb, seq):
    n, d = x2.shape
    tm = 256
    per_b = seq // tm
    row = lambda w: pl.BlockSpec((tm, w), lambda i: (i, 0))
    full = lambda a: pl.BlockSpec(a.shape, lambda i: (0,) * a.ndim)
    eg = _gate_expand().astype(BF16)
    return pl.pallas_call(
        _merge_kernel,
        grid=(n // tm,),
        in_specs=[row(o_a.shape[1]), row(o_cmp.shape[1]), row(o_slc.shape[1]), row(o_win.shape[1]),
                  pl.BlockSpec((tm, LANES), lambda i: (i, T_BG)),
                  row(LANES), row(LANES), row(LANES), row(LANES), row(LANES), row(LANES),
                  pl.BlockSpec((tm, 3 * d), lambda i: (i, T_MG)),
                  row(d),
                  pl.BlockSpec((1, 1, 1, d), lambda i: (i // per_b, 2, 0, 0)),
                  full(wa), full(wb), full(wc), full(wo), full(eg), full(lng), full(lnb)],
        out_specs=row(d),
        out_shape=jax.ShapeDtypeStruct((n, d), F32),
        compiler_params=_cparams(("parallel",)),
        name="merge_out",
    )(o_a, o_cmp, o_slc, o_win, proj, oc[0], oc[1], oc[2], lse[0], lse[1], lse[2],
      proj, x2, mod4, wa, wb, wc, wo, eg, lng, lnb)


def _router_kernel(x_ref, sc_ref, sh_ref, rw_ref, rb_ref, meta_ref, gate_ref, tile_ref, cnt_ref, carry_sc):
    tm = x_ref.shape[0]
    ne = N_EXPERTS

    @pl.when(pl.program_id(0) == 0)
    def _():
        carry_sc[...] = jnp.zeros(carry_sc.shape, F32)

    h = _ln(x_ref[...]) * (1.0 + sc_ref[0, 0]) + sh_ref[0, 0]
    lg = _dot_nt(rw_ref[...], h, precision=HIGHEST)[:ne] + rb_ref[...][:ne, :1]
    e_idx = lax.broadcasted_iota(I32, (ne, 1), 0).astype(F32)
    onehots, vals, idxs = [], [], []
    for _ in range(TOP_K):
        m = jnp.max(lg, axis=0, keepdims=True)
        idx = jnp.min(jnp.where(lg == m, e_idx, float(ne)), axis=0, keepdims=True)
        oh = e_idx == idx
        onehots.append(oh)
        vals.append(m)
        idxs.append(idx)
        lg = jnp.where(oh, -jnp.inf, lg)
    es = [jnp.exp(v - vals[0]) for v in vals]
    den = es[0] + es[1] + es[2] + es[3]
    cnt = jnp.zeros((ne, tm), F32)
    for oh in onehots:
        cnt = cnt + jnp.where(oh, 1.0, 0.0)
    before = jnp.where(lax.broadcasted_iota(I32, (tm, tm), 0) < lax.broadcasted_iota(I32, (tm, tm), 1), 1.0, 0.0)
    earlier = jnp.dot(cnt.astype(BF16), before.astype(BF16), preferred_element_type=F32)
    tile_cnt = jnp.broadcast_to(jnp.sum(cnt, axis=1, keepdims=True), (ne, LANES))
    lower = jnp.where(lax.broadcasted_iota(I32, (ne, ne), 0) > lax.broadcasted_iota(I32, (ne, ne), 1), 1.0, 0.0)
    run_off = jnp.dot(lower.astype(BF16), tile_cnt.astype(BF16), preferred_element_type=F32)
    pos = earlier + run_off[:, :1]
    locals_ = [jnp.sum(jnp.where(oh, pos, 0.0), axis=0, keepdims=True) for oh in onehots]

    def token_major(rows):
        stack = jnp.concatenate(rows + [jnp.zeros((LANES - len(rows), tm), F32)], axis=0)
        return stack.T

    meta_ref[...] = token_major(idxs + locals_).astype(I32)
    gate_ref[...] = token_major([e / den for e in es])
    lane = _lane_iota()
    cols = jnp.where(lane == 0, carry_sc[...], jnp.where(lane == 1, tile_cnt, jnp.where(lane == 2, run_off, 0.0)))
    table = jnp.concatenate([cols, jnp.zeros((LANES - ne, LANES), F32)], axis=0).T
    tile_ref[...] = table[:SUBLANES].astype(I32)
    carry_sc[...] = carry_sc[...] + tile_cnt
    total = jnp.concatenate([carry_sc[...], jnp.zeros((LANES - ne, LANES), F32)], axis=0).T
    cnt_ref[...] = total[:SUBLANES].astype(I32)


def _router(x2, mod4, rw, rb, seq):
    n, d = x2.shape
    tm = TOK_TILE
    per_b = seq // tm
    rw_p = jnp.zeros((LANES, d), F32).at[:N_EXPERTS].set(rw.T)
    rb_p = jnp.zeros((LANES, LANES), F32).at[:N_EXPERTS].set(rb[:, None])
    return pl.pallas_call(
        _router_kernel,
        grid=(n // tm,),
        in_specs=[pl.BlockSpec((tm, d), lambda i: (i, 0)),
                  pl.BlockSpec((1, 1, 1, d), lambda i: (i // per_b, 4, 0, 0)),
                  pl.BlockSpec((1, 1, 1, d), lambda i: (i // per_b, 3, 0, 0)),
                  pl.BlockSpec((LANES, d), lambda i: (0, 0)),
                  pl.BlockSpec((LANES, LANES), lambda i: (0, 0))],
        out_specs=[pl.BlockSpec((tm, LANES), lambda i: (i, 0)),
                   pl.BlockSpec((tm, LANES), lambda i: (i, 0)),
                   pl.BlockSpec((SUBLANES, LANES), lambda i: (i, 0)),
                   pl.BlockSpec((SUBLANES, LANES), lambda i: (0, 0))],
        out_shape=[jax.ShapeDtypeStruct((n, LANES), I32),
                   jax.ShapeDtypeStruct((n, LANES), F32),
                   jax.ShapeDtypeStruct((n // tm * SUBLANES, LANES), I32),
                   jax.ShapeDtypeStruct((SUBLANES, LANES), I32)],
        scratch_shapes=[pltpu.VMEM((N_EXPERTS, LANES), F32)],
        compiler_params=_cparams(("arbitrary",)),
        name="router",
    )(x2, mod4, mod4, rw_p, rb_p)


def _wait_rows(src_like, dst_like, sem, n_rows):
    span = pl.ds(0, n_rows * ROW_CHUNKS)
    pltpu.make_async_copy(src_like.at[span], dst_like.at[span], sem).wait()


def _span_rows(ref, row, n):
    return ref.at[pl.ds(pl.multiple_of(row * ROW_CHUNKS, ROW_CHUNKS), n * ROW_CHUNKS)]


def _for_each_run(tab_ref, fn):
    for e in range(N_EXPERTS):
        n = tab_ref[1, e]

        @pl.when(n > 0)
        def _(e=e, n=n):
            fn(tab_ref[0, e], tab_ref[2, e], n, e % 2)


def _dispatch_kernel(tab_ref, fill_ref, x_ref, sc_ref, sh_ref, meta_ref, xs_hbm, xb0, xb1, zb, sem, fsem):
    i = pl.program_id(0)
    n_steps = pl.num_programs(0)
    tm = x_ref.shape[0]
    nrow = tm * TOP_K
    bufs = (xb0, xb1)

    @pl.when(i == 0)
    def _():
        zb[...] = jnp.zeros(zb.shape, F32)
        for wait in (False, True):
            for j in range(N_FILL):
                n = fill_ref[1, j]

                @pl.when(n > 0)
                def _(j=j, n=n, wait=wait):
                    fill = pltpu.make_async_copy(_span_rows(zb, 0, n), _span_rows(xs_hbm, fill_ref[0, j], n), fsem)
                    fill.wait() if wait else fill.start()

    h = (_ln(x_ref[...]) * (1.0 + sc_ref[0, 0]) + sh_ref[0, 0]).astype(BF16)
    pos_t = meta_ref[...].astype(F32).T
    r = lax.broadcasted_iota(I32, (nrow, 1), 0)
    sel = jnp.zeros((nrow, tm), F32)
    for k in range(TOP_K):
        sel = sel + jnp.where(r == pos_t[TOP_K + k:TOP_K + k + 1, :].astype(I32), 1.0, 0.0)
    xt = jnp.dot(sel.astype(BF16), h, preferred_element_type=F32)

    for slot in range(2):
        @pl.when(i % 2 == slot)
        def _(slot=slot):
            buf = bufs[slot]

            @pl.when(i >= 2)
            def _():
                _wait_rows(buf, xs_hbm, sem.at[slot], nrow)

            for c in range(ROW_CHUNKS):
                buf[pl.ds(c, nrow, stride=ROW_CHUNKS), :] = xt[:, c * LANES:(c + 1) * LANES]
            _for_each_run(tab_ref, lambda srow, trow, n, prio: pltpu.make_async_copy(
                _span_rows(buf, trow, n), _span_rows(xs_hbm, srow, n), sem.at[slot]).start(priority=prio))

            @pl.when(i == n_steps - 1)
            def _():
                _wait_rows(buf, xs_hbm, sem.at[slot], nrow)

                @pl.when(i >= 1)
                def _():
                    _wait_rows(bufs[1 - slot], xs_hbm, sem.at[1 - slot], nrow)


def _dispatch(x2, mod4, meta, tab, fill_tab, n_rows, seq):
    n, d = x2.shape
    tm = TOK_TILE
    per_b = seq // tm
    return pl.pallas_call(
        _dispatch_kernel,
        grid=(n // tm,),
        in_specs=[pl.BlockSpec((SUBLANES, LANES), lambda i: (i, 0), memory_space=pltpu.SMEM),
                  pl.BlockSpec((SUBLANES, LANES), lambda i: (0, 0), memory_space=pltpu.SMEM),
                  pl.BlockSpec((tm, d), lambda i: (i, 0)),
                  pl.BlockSpec((1, 1, 1, d), lambda i: (i // per_b, 4, 0, 0)),
                  pl.BlockSpec((1, 1, 1, d), lambda i: (i // per_b, 3, 0, 0)),
                  pl.BlockSpec((tm, LANES), lambda i: (i, 0))],
        out_specs=pl.BlockSpec(memory_space=pl.ANY),
        out_shape=jax.ShapeDtypeStruct((n_rows * ROW_CHUNKS, LANES), F32),
        scratch_shapes=[pltpu.VMEM((tm * TOP_K * ROW_CHUNKS, LANES), F32),
                        pltpu.VMEM((tm * TOP_K * ROW_CHUNKS, LANES), F32),
                        pltpu.VMEM((EXPERT_ROWS * ROW_CHUNKS, LANES), F32),
                        pltpu.SemaphoreType.DMA((2,)),
                        pltpu.SemaphoreType.DMA],
        compiler_params=_cparams(("arbitrary",)),
        name="moe_dispatch",
    )(tab, fill_tab, x2, mod4, mod4, meta)


def _expert_kernel(be_ref, first_ref, used_ref, xs_ref, wgu_ref, bgu_ref, wd_ref, bd_ref, ys_ref,
                   x_sc, wgu_sc, wd_sc):
    del be_ref
    g = pl.program_id(0)

    @pl.when(g < used_ref[0])
    def _():
        _expert_block(g, first_ref, xs_ref, wgu_ref, bgu_ref, wd_ref, bd_ref, ys_ref, x_sc, wgu_sc, wd_sc)

    @pl.when(g >= used_ref[0])
    def _():
        ys_ref[...] = jnp.zeros(ys_ref.shape, ys_ref.dtype)


def _expert_block(g, first_ref, xs_ref, wgu_ref, bgu_ref, wd_ref, bd_ref, ys_ref, x_sc, wgu_sc, wd_sc):
    rows = x_sc.shape[0]
    grp = 2 * LANES

    @pl.when(first_ref[g] == 1)
    def _():
        r = lax.broadcasted_iota(I32, (grp, grp), 0)
        c = lax.broadcasted_iota(I32, (grp, grp), 1)
        src = jnp.where(c < LANES, 2 * c, 2 * (c - LANES) + 1)
        sel = jnp.where(r == src, 1.0, 0.0).astype(BF16)
        for j in range(wgu_sc.shape[1] // grp):
            blk = wgu_ref[0, :, j * grp:(j + 1) * grp].astype(BF16)
            wgu_sc[:, j * grp:(j + 1) * grp] = jnp.dot(blk, sel, preferred_element_type=F32).astype(BF16)
        wd_sc[...] = wd_ref[0].astype(BF16)

    for c in range(ROW_CHUNKS):
        x_sc[:, c * LANES:(c + 1) * LANES] = xs_ref[pl.ds(c, rows, stride=ROW_CHUNKS), :].astype(BF16)
    gu = jnp.dot(x_sc[...], wgu_sc[...], preferred_element_type=F32) + bgu_ref[0]
    acts = []
    for j in range(gu.shape[1] // grp):
        glu = jnp.minimum(gu[:, j * grp:j * grp + LANES], SWIGLU_LIMIT)
        lin = jnp.clip(gu[:, j * grp + LANES:(j + 1) * grp], -SWIGLU_LIMIT, SWIGLU_LIMIT)
        acts.append((glu * _sigmoid(SWIGLU_ALPHA * glu) * (lin + 1.0)).astype(BF16))
    act = jnp.concatenate(acts, axis=1)
    y = jnp.dot(act, wd_sc[...], preferred_element_type=F32) + bd_ref[0]
    for c in range(ROW_CHUNKS):
        ys_ref[pl.ds(c, rows, stride=ROW_CHUNKS), :] = y[:, c * LANES:(c + 1) * LANES]


def _experts(xs, block_exp, first, n_used, layer, w_gu, b_gu_grouped, w_dn, b_dn):
    d = D_MODEL
    n_blocks = block_exp.shape[0]
    rb = EXPERT_ROWS
    last = lambda g, nu: jnp.minimum(g, nu[0] - 1)
    wmap = lambda g, be, fi, nu: (layer, be[last(g, nu)], 0, 0)
    bmap = lambda g, be, fi, nu: (be[last(g, nu)], 0, 0)
    return pl.pallas_call(
        _expert_kernel,
        grid_spec=pltpu.PrefetchScalarGridSpec(
            num_scalar_prefetch=3,
            grid=(n_blocks,),
            in_specs=[pl.BlockSpec((rb * ROW_CHUNKS, LANES), lambda g, be, fi, nu: (last(g, nu), 0)),
                      pl.BlockSpec((None, 1, d, 2 * d), wmap),
                      pl.BlockSpec((1, 1, 2 * d), bmap),
                      pl.BlockSpec((None, 1, d, d), wmap),
                      pl.BlockSpec((1, 1, d), bmap)],
            out_specs=pl.BlockSpec((rb * ROW_CHUNKS, LANES), lambda g, be, fi, nu: (g, 0)),
            scratch_shapes=[pltpu.VMEM((rb, d), BF16),
                            pltpu.VMEM((d, 2 * d), BF16),
                            pltpu.VMEM((d, d), BF16)]),
        out_shape=jax.ShapeDtypeStruct(xs.shape, F32),
        compiler_params=_cparams(("arbitrary",)),
        name="moe_experts",
    )(block_exp, first, n_used, xs, w_gu, b_gu_grouped, w_dn, b_dn)


def _combine_kernel(tab_ref, tab_next_ref, meta_ref, gate_ref, x_ref, g2_ref, lng_ref, lnb_ref, ys_hbm, o_ref,
                    yb0, yb1, sem):
    i = pl.program_id(0)
    n_steps = pl.num_programs(0)
    tm = x_ref.shape[0]
    nrow = tm * TOP_K
    bufs = (yb0, yb1)

    def fetch(t_ref, slot):
        _for_each_run(t_ref, lambda srow, trow, n, prio: pltpu.make_async_copy(
            _span_rows(ys_hbm, srow, n), _span_rows(bufs[slot], trow, n), sem.at[slot]).start(priority=prio))

    @pl.when(i == 0)
    def _():
        fetch(tab_ref, 0)

    for slot in range(2):
        @pl.when(i % 2 == slot)
        def _(slot=slot):
            buf = bufs[slot]

            @pl.when(i + 1 < n_steps)
            def _():
                fetch(tab_next_ref, 1 - slot)

            _wait_rows(ys_hbm, buf, sem.at[slot], nrow)
            yt = jnp.concatenate([buf[pl.ds(c, nrow, stride=ROW_CHUNKS), :] for c in range(ROW_CHUNKS)],
                                 axis=1).astype(BF16)
            meta = meta_ref[...]
            gates = gate_ref[...]
            col = lax.broadcasted_iota(I32, (1, nrow), 1)
            w = jnp.zeros((tm, nrow), F32)
            for k in range(TOP_K):
                w = jnp.where(col == meta[:, TOP_K + k:TOP_K + k + 1], gates[:, k:k + 1], w)
            w_hi = w.astype(BF16)
            w_lo = (w - w_hi.astype(F32)).astype(BF16)
            y = (jnp.dot(w_hi, yt, preferred_element_type=F32) + jnp.dot(w_lo, yt, preferred_element_type=F32))
            z = DEEPNORM_ALPHA * x_ref[...] + g2_ref[0, 0] * y
            o_ref[...] = _ln(z) * lng_ref[...] + lnb_ref[...]


def _combine(ys, tab, meta, gates, x2, mod4, lng, lnb, seq):
    n, d = x2.shape
    tm = TOK_TILE
    per_b = seq // tm
    last = n // tm - 1
    return pl.pallas_call(
        _combine_kernel,
        grid=(n // tm,),
        in_specs=[pl.BlockSpec((SUBLANES, LANES), lambda i: (i, 0), memory_space=pltpu.SMEM),
                  pl.BlockSpec((SUBLANES, LANES), lambda i: (jnp.minimum(i + 1, last), 0),
                               memory_space=pltpu.SMEM),
                  pl.BlockSpec((tm, LANES), lambda i: (i, 0)),
                  pl.BlockSpec((tm, LANES), lambda i: (i, 0)),
                  pl.BlockSpec((tm, d), lambda i: (i, 0)),
                  pl.BlockSpec((1, 1, 1, d), lambda i: (i // per_b, 5, 0, 0)),
                  pl.BlockSpec((1, d), lambda i: (0, 0)),
                  pl.BlockSpec((1, d), lambda i: (0, 0)),
                  pl.BlockSpec(memory_space=pl.ANY)],
        out_specs=pl.BlockSpec((tm, d), lambda i: (i, 0)),
        out_shape=jax.ShapeDtypeStruct((n, d), F32),
        scratch_shapes=[pltpu.VMEM((tm * TOP_K * ROW_CHUNKS, LANES), F32),
                        pltpu.VMEM((tm * TOP_K * ROW_CHUNKS, LANES), F32),
                        pltpu.SemaphoreType.DMA((2,))],
        compiler_params=_cparams(("arbitrary",)),
        name="moe_combine",
    )(tab, tab, meta, gates, x2, mod4, lng, lnb, ys)


def _moe(x2, mod4, layer, rw, rb, w_gu, b_gu, w_dn, b_dn, lng, lnb, seq):
    n, d = x2.shape
    meta, gates, tile_meta, cnt = _router(x2, mod4, rw, rb, seq)
    counts = cnt[0, :N_EXPERTS]
    padded = -(-counts // EXPERT_ROWS) * EXPERT_ROWS
    pend = jnp.cumsum(padded)
    pstart = pend - padded
    n_blocks = -(-(n * TOP_K + N_EXPERTS * (EXPERT_ROWS - 1)) // EXPERT_ROWS)
    n_rows = n_blocks * EXPERT_ROWS
    block_row0 = jnp.arange(n_blocks, dtype=I32) * EXPERT_ROWS
    block_exp = jnp.minimum(jnp.sum((pend[None, :] <= block_row0[:, None]).astype(I32), axis=1), N_EXPERTS - 1)
    first = jnp.concatenate([jnp.ones((1,), I32), (block_exp[1:] != block_exp[:-1]).astype(I32)])
    n_used = (pend[-1:] // EXPERT_ROWS).astype(I32)
    n_tiles = n // TOK_TILE
    start_row = jnp.zeros((1, 1, LANES), I32).at[0, 0, :N_EXPERTS].set(pstart)
    sel_row0 = (jnp.arange(SUBLANES) == 0).astype(I32)[None, :, None]
    tab = (tile_meta.reshape(n_tiles, SUBLANES, LANES) + sel_row0 * start_row).reshape(n_tiles * SUBLANES, LANES)
    tail0 = pend[-1] + jnp.arange(N_FILL - N_EXPERTS, dtype=I32) * EXPERT_ROWS
    fill_start = jnp.concatenate([pstart + counts, tail0])
    fill_len = jnp.concatenate([padded - counts, jnp.clip(n_rows - tail0, 0, EXPERT_ROWS)])
    fill_tab = jnp.zeros((SUBLANES, LANES), I32).at[0, :N_FILL].set(fill_start).at[1, :N_FILL].set(fill_len)
    xs = _dispatch(x2, mod4, meta, tab, fill_tab, n_rows, seq)
    e = b_gu.shape[0]
    b_grouped = b_gu.reshape(e, d // LANES, LANES, 2).transpose(0, 1, 3, 2).reshape(e, 1, 2 * d)
    ys = _experts(xs, block_exp, first, n_used, layer, w_gu, b_grouped, w_dn, b_dn[:, None, :])
    return _combine(ys, tab, meta, gates, x2, mod4, lng.reshape(1, d), lnb.reshape(1, d), seq)


def _permute_w_in(w):
    d = w.shape[0]
    off = {}
    o = 0
    for name, wd in (("a_q", 256), ("a_k", 256), ("a_v", 256), ("b_q", 384), ("b_kc", 128), ("b_vc", 128),
                     ("b_ks", 128), ("b_vs", 128), ("b_kw", 128), ("b_vw", 128), ("b_gate", 18),
                     ("c_q", 384), ("c_k", 384), ("c_v", 384), ("merge_gate", 3 * D_MODEL)):
        off[name] = (o, o + wd)
        o += wd
    col = lambda name: w[:, off[name][0]:off[name][1]]
    scale = HEAD_DIM ** -0.5 * float(np.log2(np.e))
    b_q = col("b_q").reshape(d, B_KV_GROUPS, B_REP, HEAD_DIM).transpose(0, 2, 1, 3).reshape(d, B_HEADS * HEAD_DIM)
    gate_pad = jnp.zeros((d, LANES - B_HEADS * 3), w.dtype)
    parts = [col("merge_gate"),
             col("a_q") * scale, col("a_k"), b_q * scale, col("b_kc"), col("b_ks"), col("b_kw"),
             col("c_q") * scale, col("c_k"),
             col("a_v"), col("b_vc"), col("b_vs"), col("b_vw"), col("c_v"),
             col("b_gate"), gate_pad]
    return jnp.concatenate(parts, axis=1).astype(BF16)


def _dilate(arr, tile, batch, seq, dil):
    a = arr[:, tile * LANES:(tile + 1) * LANES].reshape(batch, seq // dil, dil, LANES)
    return a.transpose(0, 2, 1, 3).reshape(batch * seq, LANES)


def _undilate(arr, batch, seq, dil):
    a = arr.reshape(batch, dil, seq // dil, LANES)
    return a.transpose(0, 2, 1, 3).reshape(batch * seq, LANES)


def _mixer_layer(x2, mod4, cos_t, sin_t, batch, seq, w_in, cmp_w1_k, cmp_w2_k, cmp_pe_k,
                 cmp_w1_v, cmp_w2_v, cmp_pe_v, w_a, w_b, w_c, w_out, lng, lnb):
    d = D_MODEL
    proj = _inproj(x2, mod4, _permute_w_in(w_in), cos_t, sin_t, seq)

    bias_a = _moba_select(proj, _kmean(proj, batch, seq), batch, seq)
    o_a = _bb_attn(proj, bias_a, batch=batch, seq=seq, n_pairs=A_HEADS // 2,
                   q_tile=lambda p: T_AQ + p, k_tile=lambda p: T_AK + p, v_tile=lambda p: T_AV + p,
                   bias_tile=lambda p: p, blk=MOBA_BLOCK)

    kc = _compress(proj[:, T_BKC * LANES:(T_BKC + 1) * LANES], cmp_w1_k, cmp_w2_k, cmp_pe_k, batch, seq)
    vc = _compress(proj[:, T_BVC * LANES:(T_BVC + 1) * LANES], cmp_w1_v, cmp_w2_v, cmp_pe_v, batch, seq)
    o_cmp, bias_b = _nsa_cmp(proj, kc, vc, batch, seq)
    o_slc = _bb_attn(proj, bias_b, batch=batch, seq=seq, n_pairs=B_REP,
                     q_tile=lambda p: T_BQ + p, k_tile=lambda p: T_BKS, v_tile=lambda p: T_BVS,
                     bias_tile=lambda p: 0, blk=SLC_BLOCK)
    o_win, _ = _band_attn(proj, proj, proj, batch=batch, seq=seq, q_tiles=[T_BQ + r for r in range(B_REP)],
                          k_tile=T_BKW, v_tile=T_BVW, window=NSA_WINDOW, t=256, group=2)

    oc, lse = [], []
    for gi, (window, dil) in enumerate(DILATED_PAIRS):
        wlen = window // dil + 1
        if dil == 1:
            o, l = _band_attn(proj, proj, proj, batch=batch, seq=seq, q_tiles=[T_CQ + gi], k_tile=T_CK + gi,
                              v_tile=T_CV + gi, window=wlen, t=256, group=4)
        else:
            qd = _dilate(proj, T_CQ + gi, batch, seq, dil)
            kd = _dilate(proj, T_CK + gi, batch, seq, dil)
            vd = _dilate(proj, T_CV + gi, batch, seq, dil)
            o, l = _band_attn(qd, kd, vd, batch=batch * dil, seq=seq // dil, q_tiles=[0], k_tile=0, v_tile=0,
                              window=wlen, t=256, group=4)
            o, l = _undilate(o, batch, seq, dil), _undilate(l, batch, seq, dil)
        oc.append(o)
        lse.append(l)

    w_b_perm = w_b.reshape(B_KV_GROUPS, B_REP, HEAD_DIM, d).transpose(1, 0, 2, 3).reshape(B_HEADS * HEAD_DIM, d)
    return _merge(o_a, o_cmp, o_slc, o_win, proj, oc, lse, x2, mod4,
                  w_a.astype(BF16), w_b_perm.astype(BF16), w_c.astype(BF16), w_out.astype(BF16),
                  lng.reshape(1, d), lnb.reshape(1, d), seq)


def _rope_tables(positions):
    inv = ROPE_THETA ** (-jnp.arange(0, HEAD_DIM, 2, dtype=F32) / HEAD_DIM)
    ang = positions.astype(F32).reshape(-1, 1) * inv[None, :]
    cos, sin = jnp.cos(ang), jnp.sin(ang)
    return jnp.tile(cos, (1, 4)), jnp.tile(jnp.concatenate([-sin, sin], axis=1), (1, 2))


def kernel(x, c, positions, w_ada, b_ada, w_in, cmp_w1_k, cmp_w2_k, cmp_pe_k, cmp_w1_v, cmp_w2_v, cmp_pe_v, w_branch_a, w_branch_b, w_branch_c, w_out, ln1_g, ln1_b, router_w, router_b, w_gate_up, b_gate_up, w_down, b_down, ln2_g, ln2_b):
    batch, seq, d = x.shape
    depth = w_in.shape[0]
    cos_t, sin_t = _rope_tables(positions)
    mod = _ada(c, w_ada, b_ada)
    x2 = x.reshape(batch * seq, d)
    for l in range(depth):
        mod4 = mod[l].reshape(batch, N_ADA, 1, d)
        x2 = _mixer_layer(x2, mod4, cos_t, sin_t, batch, seq, w_in[l], cmp_w1_k[l], cmp_w2_k[l], cmp_pe_k[l],
                          cmp_w1_v[l], cmp_w2_v[l], cmp_pe_v[l], w_branch_a[l], w_branch_b[l],
                          w_branch_c[l], w_out[l], ln1_g[l], ln1_b[l])
        x2 = _moe(x2, mod4, l, router_w[l], router_b[l], w_gate_up, b_gate_up[l], w_down, b_down[l],
                  ln2_g[l], ln2_b[l], seq)
    return x2.reshape(batch, seq, d)
```
